```python
import jax, jax.numpy as jnp
from jax import lax
import numpy as np

D_MODEL = 1024
BATCH = 8
SEQ = 8192
DEPTH = 1

GRID_W = 64
CTX_LEN = 256
EPS = 1e-6

RET_HEADS = 4
RET_QK_DIM = 256
RET_V_DIM = 512
RET_CHUNK = 128
ATTN_HEADS = 8
ATTN_KV_HEADS = 2
ATTN_HEAD_DIM = 128
Q_BLOCK = 128
ROPE_THETA = 10000.0
N_EXPERTS = 32
TOP_K = 4
EXPERT_FF = 1024
SWIGLU_LIMIT = 7.0
SWIGLU_ALPHA = 1.702

RET_QK_W = RET_HEADS * RET_QK_DIM
RET_V_W = RET_HEADS * RET_V_DIM
ATTN_Q_W = ATTN_HEADS * ATTN_HEAD_DIM
ATTN_KV_W = ATTN_KV_HEADS * ATTN_HEAD_DIM
IN_SIZES = (RET_QK_W, RET_QK_W, RET_V_W, RET_V_W, ATTN_Q_W, ATTN_KV_W, ATTN_KV_W, D_MODEL, D_MODEL)
IN_WIDTH = sum(IN_SIZES)

kernel_name = 'hybrid_retention_gqa_moe_diffusion_block'


def rms_norm(x, gain=None):
    xf = x.astype(jnp.float32)
    y = xf * lax.rsqrt(jnp.mean(xf * xf, axis=-1, keepdims=True) + EPS)
    if gain is not None:
        y = y * gain.astype(jnp.float32)
    return y.astype(x.dtype)


def modulate(x, gain, shift, scale):
    return rms_norm(x, gain) * (1 + scale) + shift


def split_in_proj(p):
    idx = np.cumsum(IN_SIZES)[:-1].tolist()
    return jnp.split(p, idx, axis=-1)


def split_heads(t, n_heads):
    b, l, w = t.shape
    return t.reshape(b, l, n_heads, w // n_heads).transpose(0, 2, 1, 3)


def merge_heads(t):
    b, n, l, d = t.shape
    return t.transpose(0, 2, 1, 3).reshape(b, l, n * d)


def axial_rope(rows, head_dim):
    row_id = jnp.broadcast_to(jnp.arange(rows, dtype=jnp.float32)[:, None], (rows, GRID_W)).reshape(-1)
    col_id = jnp.broadcast_to(jnp.arange(GRID_W, dtype=jnp.float32)[None, :], (rows, GRID_W)).reshape(-1)
    n_freq = head_dim // 4
    inv_freq = ROPE_THETA ** (-jnp.arange(n_freq, dtype=jnp.float32) / n_freq)
    ang = jnp.stack([row_id[:, None] * inv_freq, col_id[:, None] * inv_freq], axis=1)
    return jnp.cos(ang), jnp.sin(ang)


def apply_rope(x, cos, sin):
    shp = x.shape
    xf = x.astype(jnp.float32).reshape(shp[:-1] + (2, 2, shp[-1] // 4))
    x1, x2 = xf[..., 0, :], xf[..., 1, :]
    o = jnp.stack([x1 * cos - x2 * sin, x2 * cos + x1 * sin], axis=-2)
    return o.reshape(shp).astype(x.dtype)


def retention_scan(q, k, v, log_gamma, s0, include_diag):
    b, h, l, dk = q.shape
    dv = v.shape[-1]
    n_chunks = l // RET_CHUNK
    pos = jnp.arange(RET_CHUNK, dtype=jnp.float32)
    diff = pos[:, None] - pos[None, :]
    mask = (diff >= 0) if include_diag else (diff > 0)
    lg = log_gamma.astype(jnp.float32)
    d_intra = jnp.where(mask[None], jnp.exp(lg[:, None, None] * jnp.maximum(diff, 0.0)[None]), 0.0)
    d_in = jnp.exp(lg[:, None] * (pos + 1.0))[None, :, :, None]
    d_out = jnp.exp(lg[:, None] * (RET_CHUNK - 1.0 - pos))[None, :, :, None]
    d_chunk = jnp.exp(lg * RET_CHUNK)[None, :, None, None]

    def to_chunks(t):
        return t.reshape(b, h, n_chunks, RET_CHUNK, t.shape[-1]).transpose(2, 0, 1, 3, 4)

    def step(s, xs):
        qc, kc, vc = (t.astype(jnp.float32) for t in xs)
        scores = jnp.einsum('bhid,bhjd->bhij', qc, kc) * d_intra
        o = jnp.einsum('bhij,bhje->bhie', scores, vc) + jnp.einsum('bhid,bhde->bhie', qc, s) * d_in
        s = s * d_chunk + jnp.einsum('bhjd,bhje->bhde', kc * d_out, vc)
        return s, o

    s_fin, o = lax.scan(step, s0, (to_chunks(q), to_chunks(k), to_chunks(v)))
    o = o.transpose(1, 2, 0, 3, 4).reshape(b, h, l, dv).astype(v.dtype)
    return o, s_fin


def bidir_retention(q, k, v, lg_f, lg_b, s_f0, s_b0):
    o_f, s_f = retention_scan(q, k, v, lg_f, s_f0, True)
    o_b, s_b = retention_scan(jnp.flip(q, 2), jnp.flip(k, 2), jnp.flip(v, 2), lg_b, s_b0, False)
    return o_f + jnp.flip(o_b, 2), s_f, s_b


def block_attention(q, k, v):
    b, kvh, g, lq, d = q.shape
    n_blk = lq // Q_BLOCK
    qb = jnp.moveaxis(q.reshape(b, kvh, g, n_blk, Q_BLOCK, d), 3, 0)
    scale = d ** -0.5

    def one_block(q_blk):
        s = jnp.einsum('bkgqd,bksd->bkgqs', q_blk, k, preferred_element_type=jnp.float32) * scale
        p = jax.nn.softmax(s, axis=-1)
        return jnp.einsum('bkgqs,bksd->bkgqd', p.astype(v.dtype), v,
                          preferred_element_type=jnp.float32).astype(v.dtype)

    o = lax.map(one_block, qb)
    return jnp.moveaxis(o, 0, 3).reshape(b, kvh * g, lq, d)


def group_q(t):
    b, h, l, d = t.shape
    return t.reshape(b, ATTN_KV_HEADS, h // ATTN_KV_HEADS, l, d)


def merge_branches(ret_o, ret_g, attn_o, gate_r, gate_a, w_ret_o, w_attn_o, w_out):
    ret_branch = (jax.nn.silu(ret_g) * merge_heads(rms_norm(ret_o))) @ w_ret_o
    attn_branch = merge_heads(attn_o) @ w_attn_o
    y = jax.nn.sigmoid(gate_r) * ret_branch + jax.nn.sigmoid(gate_a) * attn_branch
    return y @ w_out


def moe(h, w_router, b_router, w1, b1, w2, b2):
    shp = h.shape
    t = h.reshape(-1, shp[-1])
    logits = (t @ w_router).astype(jnp.float32) + b_router.astype(jnp.float32)
    top_v, top_i = lax.top_k(logits, TOP_K)
    wts = jax.nn.softmax(top_v, axis=-1)
    comb = jnp.sum(jax.nn.one_hot(top_i, N_EXPERTS, dtype=jnp.float32) * wts[..., None], axis=1)

    def expert(acc, xs):
        w1e, b1e, w2e, b2e, ce = xs
        a = t @ w1e + b1e
        gate = jnp.minimum(a[:, :EXPERT_FF], SWIGLU_LIMIT)
        up = jnp.clip(a[:, EXPERT_FF:], -SWIGLU_LIMIT, SWIGLU_LIMIT)
        out = (gate * jax.nn.sigmoid(SWIGLU_ALPHA * gate) * (up + 1)) @ w2e + b2e
        return acc + ce[:, None].astype(out.dtype) * out, None

    out, _ = lax.scan(expert, jnp.zeros_like(t), (w1, b1, w2, b2, comb.T))
    return out.reshape(shp)


def setup_inputs(seed: int = 0) -> dict:
    key = jax.random.key(seed)
    ks = jax.random.split(key, 24)
    f32 = jnp.float32

    def nrm(k, shape, scale):
        return jax.random.normal(k, shape, f32) * scale

    base = -np.log1p(-2.0 ** (-5.0 - np.arange(RET_HEADS)))
    raw = jnp.asarray(np.log(np.expm1(base)).astype(np.float32))
    return {
        'x': nrm(ks[0], (BATCH, SEQ, D_MODEL), 1.0),
        'c': nrm(ks[1], (BATCH, D_MODEL), 1.0),
        'ctx': nrm(ks[2], (BATCH, CTX_LEN, D_MODEL), 1.0),
        'c_ctx': nrm(ks[3], (D_MODEL,), 1.0),
        'norm1': 1.0 + nrm(ks[4], (DEPTH, D_MODEL), 0.01),
        'norm2': 1.0 + nrm(ks[5], (DEPTH, D_MODEL), 0.01),
        'w_ada': nrm(ks[6], (DEPTH, D_MODEL, 6 * D_MODEL), 0.5 * D_MODEL ** -0.5),
        'b_ada': nrm(ks[7], (DEPTH, 6 * D_MODEL), 0.01),
        'w_in': nrm(ks[8], (DEPTH, D_MODEL, IN_WIDTH), D_MODEL ** -0.5),
        'ret_decay_f': raw + nrm(ks[9], (DEPTH, RET_HEADS), 0.05),
        'ret_decay_b': raw + nrm(ks[10], (DEPTH, RET_HEADS), 0.05),
        'attn_q_norm': 1.0 + nrm(ks[11], (DEPTH, ATTN_HEAD_DIM), 0.01),
        'attn_k_norm': 1.0 + nrm(ks[12], (DEPTH, ATTN_HEAD_DIM), 0.01),
        'w_ret_o': nrm(ks[13], (DEPTH, RET_V_W, D_MODEL), RET_V_W ** -0.5),
        'w_attn_o': nrm(ks[14], (DEPTH, ATTN_Q_W, D_MODEL), ATTN_Q_W ** -0.5),
        'w_out': nrm(ks[15], (DEPTH, D_MODEL, D_MODEL), D_MODEL ** -0.5),
        'w_router': nrm(ks[16], (DEPTH, D_MODEL, N_EXPERTS), D_MODEL ** -0.5),
        'b_router': nrm(ks[17], (DEPTH, N_EXPERTS), 0.01),
        'w_exp_in': nrm(ks[18], (DEPTH, N_EXPERTS, D_MODEL, 2 * EXPERT_FF), D_MODEL ** -0.5),
        'b_exp_in': nrm(ks[19], (DEPTH, N_EXPERTS, 2 * EXPERT_FF), 0.01),
        'w_exp_out': nrm(ks[20], (DEPTH, N_EXPERTS, EXPERT_FF, D_MODEL), EXPERT_FF ** -0.5),
        'b_exp_out': nrm(ks[21], (DEPTH, N_EXPERTS, D_MODEL), 0.01),
    }


def reference(x, c, ctx, c_ctx, norm1, norm2, w_ada, b_ada, w_in, ret_decay_f, ret_decay_b,
              attn_q_norm, attn_k_norm, w_ret_o, w_attn_o, w_out, w_router, b_router,
              w_exp_in, b_exp_in, w_exp_out, b_exp_out):
    b = x.shape[0]
    rows = x.shape[1] // GRID_W
    rope_ret = axial_rope(rows, RET_QK_DIM)
    rope_attn = axial_rope(rows, ATTN_HEAD_DIM)
    k_scale = RET_QK_DIM ** -0.5
    zero_state = jnp.zeros((b, RET_HEADS, RET_QK_DIM, RET_V_DIM), jnp.float32)

    for i in range(DEPTH):
        mod = jax.nn.silu(c) @ w_ada[i] + b_ada[i]
        mod_c = jax.nn.silu(c_ctx) @ w_ada[i] + b_ada[i]
        sh1, sc1, g1, sh2, sc2, g2 = jnp.split(mod[:, None, :], 6, axis=-1)
        sh1c, sc1c, g1c, sh2c, sc2c, g2c = jnp.split(mod_c, 6, axis=-1)

        h = modulate(x, norm1[i], sh1, sc1)
        hc = modulate(ctx, norm1[i], sh1c, sc1c)
        qr, kr, vr, gr, qa, ka, va, gate_r, gate_a = split_in_proj(h @ w_in[i])
        qrc, krc, vrc, grc, qac, kac, vac, gate_rc, gate_ac = split_in_proj(hc @ w_in[i])

        lg_f = -jax.nn.softplus(ret_decay_f[i].astype(jnp.float32))
        lg_b = -jax.nn.softplus(ret_decay_b[i].astype(jnp.float32))
        o_rc, s_f, s_b = bidir_retention(split_heads(qrc, RET_HEADS), split_heads(krc, RET_HEADS) * k_scale,
                                         split_heads(vrc, RET_HEADS), lg_f, lg_b, zero_state, zero_state)
        q_r = apply_rope(split_heads(qr, RET_HEADS), *rope_ret)
        k_r = apply_rope(split_heads(kr, RET_HEADS), *rope_ret) * k_scale
        o_r, _, _ = bidir_retention(q_r, k_r, split_heads(vr, RET_HEADS), lg_f, lg_b, s_f, s_b)

        k_ac = rms_norm(split_heads(kac, ATTN_KV_HEADS), attn_k_norm[i])
        v_ac = split_heads(vac, ATTN_KV_HEADS)
        q_a = apply_rope(rms_norm(split_heads(qa, ATTN_HEADS), attn_q_norm[i]), *rope_attn)
        k_a = apply_rope(rms_norm(split_heads(ka, ATTN_KV_HEADS), attn_k_norm[i]), *rope_attn)
        v_a = split_heads(va, ATTN_KV_HEADS)
        o_a = block_attention(group_q(q_a), jnp.concatenate([k_ac, k_a], axis=2),
                              jnp.concatenate([v_ac, v_a], axis=2))

        x = x + g1 * merge_branches(o_r, gr, o_a, gate_r, gate_a, w_ret_o[i], w_attn_o[i], w_out[i])

        if i + 1 < DEPTH:
            q_ac = rms_norm(split_heads(qac, ATTN_HEADS), attn_q_norm[i])
            o_ac = block_attention(group_q(q_ac), k_ac, v_ac)
            ctx = ctx + g1c * merge_branches(o_rc, grc, o_ac, gate_rc, gate_ac, w_ret_o[i], w_attn_o[i], w_out[i])
            ctx = ctx + g2c * moe(modulate(ctx, norm2[i], sh2c, sc2c), w_router[i], b_router[i],
                                  w_exp_in[i], b_exp_in[i], w_exp_out[i], b_exp_out[i])

        x = x + g2 * moe(modulate(x, norm2[i], sh2, sc2), w_router[i], b_router[i],
                         w_exp_in[i], b_exp_in[i], w_exp_out[i], b_exp_out[i])
    return x
```

```python
import functools

import jax
import jax.numpy as jnp
import numpy as np
from jax import lax
from jax.experimental import pallas as pl
from jax.experimental.pallas import tpu as pltpu

F32 = jnp.float32
BF16 = jnp.bfloat16

D_MODEL = 1024
GRID_W = 64
EPS = 1e-6
RET_HEADS = 4
RET_QK_DIM = 256
RET_V_DIM = 512
ATTN_HEADS = 8
ATTN_KV_HEADS = 2
ATTN_GROUP = ATTN_HEADS // ATTN_KV_HEADS
ATTN_HEAD_DIM = 128
ROPE_THETA = 10000.0
N_EXPERTS = 32
TOP_K = 4
EXPERT_FF = 1024
SWIGLU_LIMIT = 7.0
SWIGLU_ALPHA = 1.702

RET_QK_W = RET_HEADS * RET_QK_DIM
RET_V_W = RET_HEADS * RET_V_DIM
ATTN_Q_W = ATTN_HEADS * ATTN_HEAD_DIM
ATTN_KV_W = ATTN_KV_HEADS * ATTN_HEAD_DIM
IN_SIZES = (RET_QK_W, RET_QK_W, RET_V_W, RET_V_W, ATTN_Q_W, ATTN_KV_W, ATTN_KV_W, D_MODEL, D_MODEL)
IN_WIDTH = sum(IN_SIZES)
COL_QR = 0
COL_KR = COL_QR + RET_QK_W
COL_VR = COL_KR + RET_QK_W
COL_GR = COL_VR + RET_V_W
COL_QA = COL_GR + RET_V_W
COL_GATE_R = COL_QA + ATTN_Q_W
COL_GATE_A = COL_GATE_R + D_MODEL
COL_KA = COL_GATE_A + D_MODEL
COL_VA = COL_KA + ATTN_KV_W

RET_CHUNK = 256
LANES = 128
VMEM_LIMIT = 56 * 1024 * 1024

ARB = pltpu.ARBITRARY


def _params(n_axes, **kw):
    return pltpu.CompilerParams(dimension_semantics=(ARB,) * n_axes, vmem_limit_bytes=VMEM_LIMIT, **kw)


def _largest_tile(n, cap, mult):
    best = None
    for t in range(mult, min(n, cap) + 1, mult):
        if n % t == 0:
            best = t
    assert best is not None, (n, cap, mult)
    return best


def _rms(x):
    return x * lax.rsqrt(jnp.mean(x * x, axis=-1, keepdims=True) + EPS)


def _rope_block(x, cos, sin, half):
    if 2 * half == LANES:
        swapped = pltpu.roll(x, half, 1)
    else:
        lane = lax.broadcasted_iota(jnp.int32, x.shape, 1)
        first = (lane % (2 * half)) < half
        swapped = jnp.where(first, pltpu.roll(x, LANES - half, 1), pltpu.roll(x, half, 1))
    return x * cos + swapped * sin


def _rope_tables(rows, head_dim):
    row_id = jnp.broadcast_to(jnp.arange(rows, dtype=F32)[:, None], (rows, GRID_W)).reshape(-1)
    col_id = jnp.broadcast_to(jnp.arange(GRID_W, dtype=F32)[None, :], (rows, GRID_W)).reshape(-1)
    n_freq = head_dim // 4
    inv_freq = ROPE_THETA ** (-jnp.arange(n_freq, dtype=F32) / n_freq)
    ang_r = row_id[:, None] * inv_freq
    ang_c = col_id[:, None] * inv_freq
    cos = jnp.concatenate([jnp.cos(ang_r), jnp.cos(ang_r), jnp.cos(ang_c), jnp.cos(ang_c)], axis=1)
    sin = jnp.concatenate([-jnp.sin(ang_r), jnp.sin(ang_r), -jnp.sin(ang_c), jnp.sin(ang_c)], axis=1)
    return cos, sin


def _ada_kernel(c_ref, w_ref, b_ref, o_ref):
    c = c_ref[...]
    s = c * jax.nn.sigmoid(c)
    o_ref[...] = jnp.dot(s, w_ref[...], preferred_element_type=F32,
                         precision=lax.Precision.HIGHEST) + b_ref[...]


def _ada(c_pad, w_ada, b_ada):
    rows = c_pad.shape[0]
    n = w_ada.shape[1]
    tn = _largest_tile(n, 1536, LANES)
    return pl.pallas_call(
        _ada_kernel,
        grid=(n // tn,),
        in_specs=[pl.BlockSpec((rows, D_MODEL), lambda j: (0, 0)),
                  pl.BlockSpec((D_MODEL, tn), lambda j: (0, j)),
                  pl.BlockSpec((1, tn), lambda j: (0, j))],
        out_specs=pl.BlockSpec((rows, tn), lambda j: (0, j)),
        out_shape=jax.ShapeDtypeStruct((rows, n), F32),
        compiler_params=_params(1),
        name="ada",
    )(c_pad, w_ada, b_ada.reshape(1, n))


def _inproj_kernel(x_ref, n_ref, sc_ref, sh_ref, w_ref, o_ref, h_sc):
    @pl.when(pl.program_id(1) == 0)
    def _():
        y = _rms(x_ref[...]) * n_ref[...]
        h_sc[...] = (y * (1.0 + sc_ref[0]) + sh_ref[0]).astype(BF16)

    o_ref[...] = jnp.dot(h_sc[...], w_ref[...], preferred_element_type=F32).astype(o_ref.dtype)


def _inproj(x2, norm, sc, sh, w, rows_per_batch):
    t = x2.shape[0]
    n = w.shape[1]
    tm = _largest_tile(rows_per_batch, 1024, 8)
    tn = _largest_tile(n, 2560, LANES)
    per_b = rows_per_batch // tm
    if sc.shape[0] == 1:
        mod_map = lambda i, j: (0, 0, 0)
    else:
        mod_map = lambda i, j: (i // per_b, 0, 0)
    return pl.pallas_call(
        _inproj_kernel,
        grid=(t // tm, n // tn),
        in_specs=[pl.BlockSpec((tm, D_MODEL), lambda i, j: (i, 0)),
                  pl.BlockSpec((1, D_MODEL), lambda i, j: (0, 0)),
                  pl.BlockSpec((1, 1, D_MODEL), mod_map),
                  pl.BlockSpec((1, 1, D_MODEL), mod_map),
                  pl.BlockSpec((D_MODEL, tn), lambda i, j: (0, j))],
        out_specs=pl.BlockSpec((tm, tn), lambda i, j: (i, j)),
        out_shape=jax.ShapeDtypeStruct((t, n), BF16),
        scratch_shapes=[pltpu.VMEM((tm, D_MODEL), BF16)],
        compiler_params=_params(2),
        name="inproj",
    )(x2, norm, sc, sh, w)


def _kprep_kernel(k_ref, g_ref, cos_ref, sin_ref, o_ref, *, use_rope):
    k = k_ref[...].astype(F32)
    for g in range(ATTN_KV_HEADS):
        sl = slice(g * ATTN_HEAD_DIM, (g + 1) * ATTN_HEAD_DIM)
        kh = _rms(k[:, sl]) * g_ref[...]
        if use_rope:
            kh = _rope_block(kh, cos_ref[...], sin_ref[...], ATTN_HEAD_DIM // 4)
        o_ref[:, sl] = kh.astype(BF16)


def _kprep(proj, gain, cos, sin, rows_per_batch, use_rope):
    t = proj.shape[0]
    tm = _largest_tile(rows_per_batch, 1024, 8)
    per_b = rows_per_batch // tm
    return pl.pallas_call(
        functools.partial(_kprep_kernel, use_rope=use_rope),
        grid=(t // tm,),
        in_specs=[pl.BlockSpec((tm, ATTN_KV_W), lambda i: (i, COL_KA // ATTN_KV_W)),
                  pl.BlockSpec((1, ATTN_HEAD_DIM), lambda i: (0, 0)),
                  pl.BlockSpec((tm, ATTN_HEAD_DIM), lambda i: (i % per_b, 0)),
                  pl.BlockSpec((tm, ATTN_HEAD_DIM), lambda i: (i % per_b, 0))],
        out_specs=pl.BlockSpec((tm, ATTN_KV_W), lambda i: (i, 0)),
        out_shape=jax.ShapeDtypeStruct((t, ATTN_KV_W), BF16),
        compiler_params=_params(1),
        name="kprep",
    )(proj, gain, cos, sin)


def _ret_kernel(dch_ref, q_ref, k_ref, v_ref, cos_ref, sin_ref, dmat_ref, din_ref, dout_ref, s0_ref,
                o_ref, sfin_ref, s_sc, *, use_rope, n_chunks):
    h = pl.program_id(1)
    c = pl.program_id(2)

    @pl.when(c == 0)
    def _():
        s_sc[...] = s0_ref[0, 0]

    q = q_ref[...].astype(F32)
    k = k_ref[...].astype(F32)
    if use_rope:
        cos = cos_ref[...]
        sin = sin_ref[...]
        q = jnp.concatenate([_rope_block(q[:, i * LANES:(i + 1) * LANES], cos[:, i * LANES:(i + 1) * LANES],
                                         sin[:, i * LANES:(i + 1) * LANES], RET_QK_DIM // 4)
                             for i in range(RET_QK_DIM // LANES)], axis=1)
        k = jnp.concatenate([_rope_block(k[:, i * LANES:(i + 1) * LANES], cos[:, i * LANES:(i + 1) * LANES],
                                         sin[:, i * LANES:(i + 1) * LANES], RET_QK_DIM // 4)
                             for i in range(RET_QK_DIM // LANES)], axis=1)
    v = v_ref[...]
    s = s_sc[...]
    scores = lax.dot_general(q.astype(BF16), k.astype(BF16), (((1,), (1,)), ((), ())),
                             preferred_element_type=F32) * dmat_ref[0]
    o = jnp.dot(scores.astype(BF16), v, preferred_element_type=F32)
    o = o + jnp.dot((q * din_ref[0]).astype(BF16), s.astype(BF16), preferred_element_type=F32)
    o_ref[...] = o.astype(o_ref.dtype)
    kd_t = (k * dout_ref[0]).T.astype(BF16)
    s_new = s * dch_ref[h] + jnp.dot(kd_t, v, preferred_element_type=F32)
    s_sc[...] = s_new

    @pl.when(c == n_chunks - 1)
    def _():
        sfin_ref[0, 0] = s_new


def _ret_tables(log_gamma, chunk, reverse, k_scale):
    pos = jnp.arange(chunk, dtype=F32)
    diff = pos[:, None] - pos[None, :]
    if reverse:
        diff = -diff
        mask = diff > 0
        p_in = chunk - pos
        p_out = pos
    else:
        mask = diff >= 0
        p_in = pos + 1.0
        p_out = chunk - 1.0 - pos
    lg = log_gamma.astype(F32)
    dmat = jnp.where(mask[None], jnp.exp(lg[:, None, None] * jnp.maximum(diff, 0.0)[None]), 0.0) * k_scale
    d_in = jnp.exp(lg[:, None] * p_in)
    d_out = jnp.exp(lg[:, None] * p_out) * k_scale
    d_in = jnp.broadcast_to(d_in[:, :, None], (RET_HEADS, chunk, RET_QK_DIM))
    d_out = jnp.broadcast_to(d_out[:, :, None], (RET_HEADS, chunk, RET_QK_DIM))
    d_chunk = jnp.exp(lg * chunk)
    return d_chunk, dmat, d_in, d_out


def _retention(proj, cos, sin, log_gamma, s0, batch, seq, reverse, use_rope):
    chunk = min(RET_CHUNK, seq)
    nc = seq // chunk
    k_scale = RET_QK_DIM ** -0.5
    d_chunk, dmat, d_in, d_out = _ret_tables(log_gamma, chunk, reverse, k_scale)
    if reverse:
        cidx = lambda c: nc - 1 - c
    else:
        cidx = lambda c: c
    row = lambda b, c: b * nc + cidx(c)
    return pl.pallas_call(
        functools.partial(_ret_kernel, use_rope=use_rope, n_chunks=nc),
        grid=(batch, RET_HEADS, nc),
        in_specs=[pl.BlockSpec(memory_space=pltpu.SMEM),
                  pl.BlockSpec((chunk, RET_QK_DIM), lambda b, h, c: (row(b, c), COL_QR // RET_QK_DIM + h)),
                  pl.BlockSpec((chunk, RET_QK_DIM), lambda b, h, c: (row(b, c), COL_KR // RET_QK_DIM + h)),
                  pl.BlockSpec((chunk, RET_V_DIM), lambda b, h, c: (row(b, c), COL_VR // RET_V_DIM + h)),
                  pl.BlockSpec((chunk, RET_QK_DIM), lambda b, h, c: (cidx(c), 0)),
                  pl.BlockSpec((chunk, RET_QK_DIM), lambda b, h, c: (cidx(c), 0)),
                  pl.BlockSpec((1, chunk, chunk), lambda b, h, c: (h, 0, 0)),
                  pl.BlockSpec((1, chunk, RET_QK_DIM), lambda b, h, c: (h, 0, 0)),
                  pl.BlockSpec((1, chunk, RET_QK_DIM), lambda b, h, c: (h, 0, 0)),
                  pl.BlockSpec((1, 1, RET_QK_DIM, RET_V_DIM), lambda b, h, c: (b, h, 0, 0))],
        out_specs=[pl.BlockSpec((chunk, RET_V_DIM), lambda b, h, c: (row(b, c), h)),
                   pl.BlockSpec((1, 1, RET_QK_DIM, RET_V_DIM), lambda b, h, c: (b, h, 0, 0))],
        out_shape=[jax.ShapeDtypeStruct((batch * seq, RET_V_W), BF16),
                   jax.ShapeDtypeStruct((batch, RET_HEADS, RET_QK_DIM, RET_V_DIM), F32)],
        scratch_shapes=[pltpu.VMEM((RET_QK_DIM, RET_V_DIM), F32)],
        compiler_params=_params(3),
        name="ret",
    )(d_chunk, proj, proj, proj, cos, sin, dmat, d_in, d_out, s0)


def _attn_kernel(q_ref, g_ref, cos_ref, sin_ref, k_ref, vt_ref, o_ref, qt_sc, m_sc, l_sc, acc_sc, *, tq, n_kv):
    j = pl.program_id(3)

    @pl.when(j == 0)
    def _():
        q = q_ref[...].astype(F32)
        scale = ATTN_HEAD_DIM ** -0.5
        for g in range(ATTN_GROUP):
            qh = _rms(q[:, g * ATTN_HEAD_DIM:(g + 1) * ATTN_HEAD_DIM]) * g_ref[...]
            qh = _rope_block(qh, cos_ref[...], sin_ref[...], ATTN_HEAD_DIM // 4) * scale
            qt_sc[:, g * tq:(g + 1) * tq] = qh.T.astype(BF16)
        m_sc[...] = jnp.full(m_sc.shape, -jnp.inf, F32)
        l_sc[...] = jnp.zeros(l_sc.shape, F32)
        acc_sc[...] = jnp.zeros(acc_sc.shape, F32)

    s_t = jnp.dot(k_ref[0], qt_sc[...], preferred_element_type=F32)
    m_prev = m_sc[...]
    m_new = jnp.maximum(m_prev, jnp.max(s_t, axis=0, keepdims=True))
    alpha = jnp.exp(m_prev - m_new)
    p = jnp.exp(s_t - m_new)
    l_sc[...] = alpha * l_sc[...] + jnp.sum(p, axis=0, keepdims=True)
    acc_sc[...] = alpha * acc_sc[...] + jnp.dot(vt_ref[0], p.astype(BF16), preferred_element_type=F32)
    m_sc[...] = m_new

    @pl.when(j == n_kv - 1)
    def _():
        o = acc_sc[...] / l_sc[...]
        for g in range(ATTN_GROUP):
            o_ref[:, g * ATTN_HEAD_DIM:(g + 1) * ATTN_HEAD_DIM] = o[:, g * tq:(g + 1) * tq].T.astype(o_ref.dtype)


def _attention(proj, gain, cos, sin, k_all, vt_all, batch, seq):
    lk = k_all.shape[1]
    tq = _largest_tile(seq, 256, 8)
    tk = _largest_tile(lk, 1536, LANES)
    nq = seq // tq
    nk = lk // tk
    qw = ATTN_GROUP * ATTN_HEAD_DIM
    return pl.pallas_call(
        functools.partial(_attn_kernel, tq=tq, n_kv=nk),
        grid=(batch, ATTN_KV_HEADS, nq, nk),
        in_specs=[pl.BlockSpec((tq, qw), lambda b, g, i, j: (b * nq + i, COL_QA // qw + g)),
                  pl.BlockSpec((1, ATTN_HEAD_DIM), lambda b, g, i, j: (0, 0)),
                  pl.BlockSpec((tq, ATTN_HEAD_DIM), lambda b, g, i, j: (i, 0)),
                  pl.BlockSpec((tq, ATTN_HEAD_DIM), lambda b, g, i, j: (i, 0)),
                  pl.BlockSpec((1, tk, ATTN_HEAD_DIM), lambda b, g, i, j: (b, j, g)),
                  pl.BlockSpec((1, ATTN_HEAD_DIM, tk), lambda b, g, i, j: (b, g, j))],
        out_specs=pl.BlockSpec((tq, qw), lambda b, g, i, j: (b * nq + i, g)),
        out_shape=jax.ShapeDtypeStruct((batch * seq, ATTN_Q_W), BF16),
        scratch_shapes=[pltpu.VMEM((ATTN_HEAD_DIM, ATTN_GROUP * tq), BF16),
                        pltpu.VMEM((1, ATTN_GROUP * tq), F32),
                        pltpu.VMEM((1, ATTN_GROUP * tq), F32),
                        pltpu.VMEM((ATTN_HEAD_DIM, ATTN_GROUP * tq), F32)],
        compiler_params=_params(4),
        name="attn",
    )(proj, gain, cos, sin, k_all, vt_all)


def _merge_kernel(x_ref, g1_ref, of_ref, ob_ref, gr_ref, ao_ref, gtr_ref, gta_ref,
                  wro_ref, wao_ref, wo_ref, o_ref):
    ro = of_ref[...].astype(F32) + ob_ref[...].astype(F32)
    gr = gr_ref[...].astype(F32)
    parts = []
    for h in range(RET_HEADS):
        sl = slice(h * RET_V_DIM, (h + 1) * RET_V_DIM)
        g = gr[:, sl]
        parts.append((g * jax.nn.sigmoid(g) * _rms(ro[:, sl])).astype(BF16))
    ret_in = jnp.concatenate(parts, axis=1)
    ret_branch = jnp.dot(ret_in, wro_ref[...], preferred_element_type=F32)
    attn_branch = jnp.dot(ao_ref[...], wao_ref[...], preferred_element_type=F32)
    y = (jax.nn.sigmoid(gtr_ref[...].astype(F32)) * ret_branch
         + jax.nn.sigmoid(gta_ref[...].astype(F32)) * attn_branch)
    y = jnp.dot(y.astype(BF16), wo_ref[...], preferred_element_type=F32)
    o_ref[...] = x_ref[...] + g1_ref[0] * y


def _merge(x2, g1, o_f, o_b, proj, attn_o, w_ret_o, w_attn_o, w_out, seq):
    t = x2.shape[0]
    tm = _largest_tile(seq, 256, 8)
    per_b = seq // tm
    full = lambda i: (0, 0)
    return pl.pallas_call(
        _merge_kernel,
        grid=(t // tm,),
        in_specs=[pl.BlockSpec((tm, D_MODEL), lambda i: (i, 0)),
                  pl.BlockSpec((1, 1, D_MODEL), lambda i: (i // per_b, 0, 0)),
                  pl.BlockSpec((tm, RET_V_W), lambda i: (i, 0)),
                  pl.BlockSpec((tm, RET_V_W), lambda i: (i, 0)),
                  pl.BlockSpec((tm, RET_V_W), lambda i: (i, COL_GR // RET_V_W)),
                  pl.BlockSpec((tm, ATTN_Q_W), lambda i: (i, 0)),
                  pl.BlockSpec((tm, D_MODEL), lambda i: (i, COL_GATE_R // D_MODEL)),
                  pl.BlockSpec((tm, D_MODEL), lambda i: (i, COL_GATE_A // D_MODEL)),
                  pl.BlockSpec((RET_V_W, D_MODEL), full),
                  pl.BlockSpec((ATTN_Q_W, D_MODEL), full),
                  pl.BlockSpec((D_MODEL, D_MODEL), full)],
        out_specs=pl.BlockSpec((tm, D_MODEL), lambda i: (i, 0)),
        out_shape=jax.ShapeDtypeStruct((t, D_MODEL), F32),
        compiler_params=_params(1),
        name="merge",
    )(x2, g1, o_f, o_b, proj, attn_o, proj, proj, w_ret_o, w_attn_o, w_out)


def _router_kernel(x_ref, n_ref, sc_ref, sh_ref, wr_ref, br_ref, h_ref, ti_ref, tw_ref):
    h = _rms(x_ref[...]) * n_ref[...] * (1.0 + sc_ref[0]) + sh_ref[0]
    h_ref[...] = h.astype(BF16)
    logits = lax.dot_general(wr_ref[...], h, (((1,), (1,)), ((), ())), preferred_element_type=F32,
                             precision=lax.Precision.HIGHEST) + br_ref[...]
    eid = lax.broadcasted_iota(jnp.int32, logits.shape, 0)
    vals = logits
    top_v = []
    top_i = []
    for _ in range(TOP_K):
        m = jnp.max(vals, axis=0, keepdims=True)
        idx = jnp.min(jnp.where(vals == m, eid, N_EXPERTS), axis=0, keepdims=True)
        top_v.append(m)
        top_i.append(idx)
        vals = jnp.where(eid == idx, -jnp.inf, vals)
    ex = [jnp.exp(v - top_v[0]) for v in top_v]
    denom = ex[0] + ex[1] + ex[2] + ex[3]
    ti_ref[...] = jnp.concatenate(top_i, axis=0)
    tw_ref[...] = jnp.concatenate([e / denom for e in ex], axis=0)


def _router(x1, norm, sc, sh, w_router_t, b_router, seq):
    t = x1.shape[0]
    tm = _largest_tile(seq, 1024, LANES)
    per_b = seq // tm
    mod_map = lambda i: (i // per_b, 0, 0)
    return pl.pallas_call(
        _router_kernel,
        grid=(t // tm,),
        in_specs=[pl.BlockSpec((tm, D_MODEL), lambda i: (i, 0)),
                  pl.BlockSpec((1, D_MODEL), lambda i: (0, 0)),
                  pl.BlockSpec((1, 1, D_MODEL), mod_map),
                  pl.BlockSpec((1, 1, D_MODEL), mod_map),
                  pl.BlockSpec((N_EXPERTS, D_MODEL), lambda i: (0, 0)),
                  pl.BlockSpec((N_EXPERTS, 1), lambda i: (0, 0))],
        out_specs=[pl.BlockSpec((tm, D_MODEL), lambda i: (i, 0)),
                   pl.BlockSpec((TOP_K, tm), lambda i: (0, i)),
                   pl.BlockSpec((TOP_K, tm), lambda i: (0, i))],
        out_shape=[jax.ShapeDtypeStruct((t, D_MODEL), BF16),
                   jax.ShapeDtypeStruct((TOP_K, t), jnp.int32),
                   jax.ShapeDtypeStruct((TOP_K, t), F32)],
        compiler_params=_params(1),
        name="router",
    )(x1, norm, sc, sh, w_router_t, b_router)


def _gather_kernel(idx_ref, src_ref, out_ref, sem, *, rows):
    base = pl.program_id(0) * rows

    def start(r, carry):
        pltpu.make_async_copy(src_ref.at[idx_ref[0, 0, r]], out_ref.at[base + r], sem).start()
        return carry

    lax.fori_loop(0, rows, start, 0)

    def wait(r, carry):
        pltpu.make_async_copy(src_ref.at[0], out_ref.at[base + r], sem).wait()
        return carry

    lax.fori_loop(0, rows, wait, 0)


def _gather_rows(src, idx):
    n, w = src.shape
    r = idx.shape[0]
    rows = _largest_tile(r, 2048, 8)
    sub = w // LANES
    out = pl.pallas_call(
        functools.partial(_gather_kernel, rows=rows),
        grid=(r // rows,),
        in_specs=[pl.BlockSpec((1, 1, rows), lambda i: (i, 0, 0), memory_space=pltpu.SMEM),
                  pl.BlockSpec(memory_space=pl.ANY)],
        out_specs=pl.BlockSpec(memory_space=pl.ANY),
        out_shape=jax.ShapeDtypeStruct((r, sub, LANES), src.dtype),
        scratch_shapes=[pltpu.SemaphoreType.DMA(())],
        compiler_params=_params(1, has_side_effects=True),
        name="gather",
    )(idx.reshape(r // rows, 1, rows), src.reshape(n, sub, LANES))
    return out.reshape(r, w)


def _ffn_kernel(te_ref, tv_ref, x_ref, w1_ref, b1_ref, w2_ref, b2_ref, o_ref):
    valid = tv_ref[pl.program_id(0)] != 0

    @pl.when(jnp.logical_not(valid))
    def _():
        o_ref[...] = jnp.zeros(o_ref.shape, o_ref.dtype)

    @pl.when(valid)
    def _():
        a = jnp.dot(x_ref[...], w1_ref[0], preferred_element_type=F32) + b1_ref[0]
        gate = jnp.minimum(a[:, :EXPERT_FF], SWIGLU_LIMIT)
        up = jnp.clip(a[:, EXPERT_FF:], -SWIGLU_LIMIT, SWIGLU_LIMIT)
        act = gate * jax.nn.sigmoid(SWIGLU_ALPHA * gate) * (up + 1.0)
        y = jnp.dot(act.astype(BF16), w2_ref[0], preferred_element_type=F32) + b2_ref[0]
        o_ref[...] = y.astype(o_ref.dtype)


def _ffn(xs, tile_expert, tile_valid, w1, b1, w2, b2, tg):
    p = xs.shape[0]
    grid_spec = pltpu.PrefetchScalarGridSpec(
        num_scalar_prefetch=2,
        grid=(p // tg,),
        in_specs=[pl.BlockSpec((tg, D_MODEL), lambda j, te, tv: (j, 0)),
                  pl.BlockSpec((1, D_MODEL, 2 * EXPERT_FF), lambda j, te, tv: (te[j], 0, 0)),
                  pl.BlockSpec((1, 1, 2 * EXPERT_FF), lambda j, te, tv: (te[j], 0, 0)),
                  pl.BlockSpec((1, EXPERT_FF, D_MODEL), lambda j, te, tv: (te[j], 0, 0)),
                  pl.BlockSpec((1, 1, D_MODEL), lambda j, te, tv: (te[j], 0, 0))],
        out_specs=pl.BlockSpec((tg, D_MODEL), lambda j, te, tv: (j, 0)),
    )
    return pl.pallas_call(
        _ffn_kernel,
        grid_spec=grid_spec,
        out_shape=jax.ShapeDtypeStruct((p, D_MODEL), BF16),
        compiler_params=_params(1),
        name="ffn",
    )(tile_expert, tile_valid, xs, w1, b1, w2, b2)


def _combine_kernel(x_ref, g2_ref, w_ref, y_ref, o_ref):
    w = w_ref[...]
    acc = w[:, 0:1] * y_ref[0].astype(F32)
    for k in range(1, TOP_K):
        acc = acc + w[:, k:k + 1] * y_ref[k].astype(F32)
    o_ref[...] = x_ref[...] + g2_ref[0] * acc


def _combine(x1, g2, w_tok, yk, seq):
    t = x1.shape[0]
    tm = _largest_tile(seq, 512, 8)
    per_b = seq // tm
    return pl.pallas_call(
        _combine_kernel,
        grid=(t // tm,),
        in_specs=[pl.BlockSpec((tm, D_MODEL), lambda i: (i, 0)),
                  pl.BlockSpec((1, 1, D_MODEL), lambda i: (i // per_b, 0, 0)),
                  pl.BlockSpec((tm, TOP_K), lambda i: (i, 0)),
                  pl.BlockSpec((TOP_K, tm, D_MODEL), lambda i: (0, i, 0))],
        out_specs=pl.BlockSpec((tm, D_MODEL), lambda i: (i, 0)),
        out_shape=jax.ShapeDtypeStruct((t, D_MODEL), F32),
        compiler_params=_params(1),
        name="combine",
    )(x1, g2, w_tok, yk)


def _plan(top_i, tg):
    t = top_i.shape[1]
    n_assign = TOP_K * t
    p = n_assign + N_EXPERTS * tg
    n_tiles = p // tg
    flat_e = top_i.reshape(-1)
    sorted_e, sorted_a = lax.sort_key_val(flat_e, jnp.arange(n_assign, dtype=jnp.int32))
    bounds = jnp.searchsorted(sorted_e, jnp.arange(N_EXPERTS + 1, dtype=jnp.int32), side="left").astype(jnp.int32)
    start = bounds[:-1]
    counts = bounds[1:] - start
    padded = ((counts + tg - 1) // tg) * tg
    off_end = jnp.cumsum(padded)
    off = off_end - padded
    dest_sorted = off[sorted_e] + jnp.arange(n_assign, dtype=jnp.int32) - start[sorted_e]
    dest = jnp.zeros((n_assign,), jnp.int32).at[sorted_a].set(dest_sorted)
    tile_start = jnp.arange(n_tiles, dtype=jnp.int32) * tg
    tile_valid = (tile_start < off_end[-1]).astype(jnp.int32)
    te = jnp.searchsorted(off_end, tile_start, side="right").astype(jnp.int32)
    last_e = jnp.searchsorted(off_end, off_end[-1] - 1, side="right").astype(jnp.int32)
    tile_expert = jnp.where(tile_valid != 0, te, last_e)
    slot = jnp.arange(p, dtype=jnp.int32)
    e_slot = jnp.repeat(tile_expert, tg)
    r = slot - off[e_slot]
    ok = (r < counts[e_slot]) & (jnp.repeat(tile_valid, tg) != 0)
    a_slot = sorted_a[jnp.clip(start[e_slot] + r, 0, n_assign - 1)]
    src_tok = jnp.where(ok, a_slot % t, 0).astype(jnp.int32)
    return src_tok, dest, tile_expert, tile_valid


def kernel(x, c, ctx, c_ctx, norm1, norm2, w_ada, b_ada, w_in, ret_decay_f, ret_decay_b, attn_q_norm, attn_k_norm,
           w_ret_o, w_attn_o, w_out, w_router, b_router, w_exp_in, b_exp_in, w_exp_out, b_exp_out):
    assert w_in.shape[0] == 1, "single-layer block"
    b, seq, d = x.shape
    n_ctx = ctx.shape[1]
    t = b * seq
    rows = seq // GRID_W

    idx = np.cumsum(IN_SIZES)[:-1].tolist()
    wq_r, wk_r, wv_r, wg_r, wq_a, wk_a, wv_a, wgt_r, wgt_a = jnp.split(w_in[0], idx, axis=-1)
    w_in_p = jnp.concatenate([wq_r, wk_r, wv_r, wg_r, wq_a, wgt_r, wgt_a, wk_a, wv_a], axis=-1).astype(BF16)
    w1 = w_exp_in[0].astype(BF16)
    w2 = w_exp_out[0].astype(BF16)
    b1 = b_exp_in[0].reshape(N_EXPERTS, 1, 2 * EXPERT_FF)
    b2 = b_exp_out[0].reshape(N_EXPERTS, 1, D_MODEL)

    pad = (-(b + 1)) % 8
    c_all = jnp.concatenate([c, c_ctx[None, :], jnp.zeros((pad, d), F32)], axis=0)
    mod = _ada(c_all, w_ada[0], b_ada[0])
    sh1, sc1, g1, sh2, sc2, g2 = [m.reshape(-1, 1, d) for m in jnp.split(mod, 6, axis=-1)]
    lat = lambda m: m[:b]
    cx = lambda m: m[b:b + 1]

    x2 = x.reshape(t, d)
    proj = _inproj(x2, norm1, lat(sc1), lat(sh1), w_in_p, seq)
    proj_c = _inproj(ctx.reshape(b * n_ctx, d), norm1, cx(sc1), cx(sh1), w_in_p, n_ctx)

    cos_r, sin_r = _rope_tables(rows, RET_QK_DIM)
    cos_a, sin_a = _rope_tables(rows, ATTN_HEAD_DIM)
    lg_f = -jax.nn.softplus(ret_decay_f[0].astype(F32))
    lg_b = -jax.nn.softplus(ret_decay_b[0].astype(F32))
    zero_state = jnp.zeros((b, RET_HEADS, RET_QK_DIM, RET_V_DIM), F32)
    cos_c = jnp.ones((n_ctx, RET_QK_DIM), F32)
    sin_c = jnp.zeros((n_ctx, RET_QK_DIM), F32)
    _, s_f = _retention(proj_c, cos_c, sin_c, lg_f, zero_state, b, n_ctx, False, False)
    _, s_b = _retention(proj_c, cos_c, sin_c, lg_b, zero_state, b, n_ctx, True, False)
    o_f, _ = _retention(proj, cos_r, sin_r, lg_f, s_f, b, seq, False, True)
    o_b, _ = _retention(proj, cos_r, sin_r, lg_b, s_b, b, seq, True, True)

    k_lat = _kprep(proj, attn_k_norm, cos_a, sin_a, seq, True)
    k_ctx = _kprep(proj_c, attn_k_norm, cos_a, sin_a, n_ctx, False)
    k_all = jnp.concatenate([k_ctx.reshape(b, n_ctx, ATTN_KV_W), k_lat.reshape(b, seq, ATTN_KV_W)], axis=1)
    v_all = jnp.concatenate([proj_c[:, COL_VA:COL_VA + ATTN_KV_W].reshape(b, n_ctx, ATTN_KV_W),
                             proj[:, COL_VA:COL_VA + ATTN_KV_W].reshape(b, seq, ATTN_KV_W)], axis=1)
    vt_all = jnp.swapaxes(v_all, 1, 2)
    attn_o = _attention(proj, attn_q_norm, cos_a, sin_a, k_all, vt_all, b, seq)

    x1 = _merge(x2, lat(g1), o_f, o_b, proj, attn_o, w_ret_o[0].astype(BF16), w_attn_o[0].astype(BF16),
                w_out[0].astype(BF16), seq)

    h2, top_i, top_w = _router(x1, norm2, lat(sc2), lat(sh2), w_router[0].T, b_router[0].reshape(N_EXPERTS, 1), seq)
    tg = _largest_tile(TOP_K * t, 512, 8)
    src_tok, dest, tile_expert, tile_valid = _plan(top_i, tg)
    xs = _gather_rows(h2, src_tok)
    ys = _ffn(xs, tile_expert, tile_valid, w1, b1, w2, b2, tg)
    yk = _gather_rows(ys, dest).reshape(TOP_K, t, d)
    out = _combine(x1, lat(g2), top_w.T, yk, seq)
    return out.reshape(b, seq, d)
```

```python
import functools

import jax
import jax.numpy as jnp
import numpy as np
from jax import lax
from jax.experimental import pallas as pl
from jax.experimental.pallas import tpu as pltpu
from jax.experimental.pallas import tpu_sc as plsc

F32 = jnp.float32
BF16 = jnp.bfloat16

D_MODEL = 1024
GRID_W = 64
EPS = 1e-6
RET_HEADS = 4
RET_QK_DIM = 256
RET_V_DIM = 512
ATTN_HEADS = 8
ATTN_KV_HEADS = 2
ATTN_GROUP = ATTN_HEADS // ATTN_KV_HEADS
ATTN_HEAD_DIM = 128
ROPE_THETA = 10000.0
N_EXPERTS = 32
TOP_K = 4
EXPERT_FF = 1024
SWIGLU_LIMIT = 7.0
SWIGLU_ALPHA = 1.702

RET_QK_W = RET_HEADS * RET_QK_DIM
RET_V_W = RET_HEADS * RET_V_DIM
ATTN_Q_W = ATTN_HEADS * ATTN_HEAD_DIM
ATTN_KV_W = ATTN_KV_HEADS * ATTN_HEAD_DIM
IN_SIZES = (RET_QK_W, RET_QK_W, RET_V_W, RET_V_W, ATTN_Q_W, ATTN_KV_W, ATTN_KV_W, D_MODEL, D_MODEL)
IN_WIDTH = sum(IN_SIZES)
COL_QR = 0
COL_KR = COL_QR + RET_QK_W
COL_VR = COL_KR + RET_QK_W
COL_GR = COL_VR + RET_V_W
COL_QA = COL_GR + RET_V_W
COL_GATE_R = COL_QA + ATTN_Q_W
COL_GATE_A = COL_GATE_R + D_MODEL
COL_KA = COL_GATE_A + D_MODEL
COL_VA = COL_KA + ATTN_KV_W

RET_CHUNK = 256
LANES = 128
SC_CORES = 2
SC_SUBCORES = 16
SC_GATHER_ROWS = 128
ONES_ROWS = 16
PACK_DTYPE = jnp.int32
PACK_W = D_MODEL // 2
VMEM_LIMIT = 56 * 1024 * 1024

ARB = pltpu.ARBITRARY


def _params(n_axes, **kw):
    return pltpu.CompilerParams(dimension_semantics=(ARB,) * n_axes, vmem_limit_bytes=VMEM_LIMIT, **kw)


def _largest_tile(n, cap, mult):
    best = None
    for t in range(mult, min(n, cap) + 1, mult):
        if n % t == 0:
            best = t
    assert best is not None, (n, cap, mult)
    return best


def _rms(x):
    return x * lax.rsqrt(jnp.mean(x * x, axis=-1, keepdims=True) + EPS)


def _pack_pairs(x):
    half = x.shape[1] // 2
    lo = lax.bitcast_convert_type(x[:, :half].astype(BF16).astype(F32), jnp.int32)
    hi = lax.bitcast_convert_type(x[:, half:].astype(BF16).astype(F32), jnp.int32)
    return lax.bitwise_or(lax.bitwise_and(hi, jnp.int32(-65536)), lax.shift_right_logical(lo, jnp.int32(16)))


def _unpack_pairs(w):
    lo = lax.bitcast_convert_type(lax.shift_left(w, jnp.int32(16)), F32)
    hi = lax.bitcast_convert_type(lax.bitwise_and(w, jnp.int32(-65536)), F32)
    return lo, hi


def _rope_block(x, cos, sin, half):
    if 2 * half == LANES:
        swapped = pltpu.roll(x, half, 1)
    else:
        lane = lax.broadcasted_iota(jnp.int32, x.shape, 1)
        first = (lane % (2 * half)) < half
        swapped = jnp.where(first, pltpu.roll(x, LANES - half, 1), pltpu.roll(x, half, 1))
    return x * cos + swapped * sin


def _rope_tables(rows, head_dim):
    row_id = jnp.broadcast_to(jnp.arange(rows, dtype=F32)[:, None], (rows, GRID_W)).reshape(-1)
    col_id = jnp.broadcast_to(jnp.arange(GRID_W, dtype=F32)[None, :], (rows, GRID_W)).reshape(-1)
    n_freq = head_dim // 4
    inv_freq = ROPE_THETA ** (-jnp.arange(n_freq, dtype=F32) / n_freq)
    ang_r = row_id[:, None] * inv_freq
    ang_c = col_id[:, None] * inv_freq
    cos = jnp.concatenate([jnp.cos(ang_r), jnp.cos(ang_r), jnp.cos(ang_c), jnp.cos(ang_c)], axis=1)
    sin = jnp.concatenate([-jnp.sin(ang_r), jnp.sin(ang_r), -jnp.sin(ang_c), jnp.sin(ang_c)], axis=1)
    return cos, sin


def _ada_kernel(c_ref, w_ref, b_ref, o_ref):
    c = c_ref[...]
    s = c * jax.nn.sigmoid(c)
    o_ref[...] = jnp.dot(s, w_ref[...], preferred_element_type=F32,
                         precision=lax.Precision.HIGHEST) + b_ref[...]


def _ada(c_pad, w_ada, b_ada):
    rows = c_pad.shape[0]
    n = w_ada.shape[1]
    tn = _largest_tile(n, 1536, LANES)
    return pl.pallas_call(
        _ada_kernel,
        grid=(n // tn,),
        in_specs=[pl.BlockSpec((rows, D_MODEL), lambda j: (0, 0)),
                  pl.BlockSpec((D_MODEL, tn), lambda j: (0, j)),
                  pl.BlockSpec((1, tn), lambda j: (0, j))],
        out_specs=pl.BlockSpec((rows, tn), lambda j: (0, j)),
        out_shape=jax.ShapeDtypeStruct((rows, n), F32),
        compiler_params=_params(1),
        name="ada",
    )(c_pad, w_ada, b_ada.reshape(1, n))


def _inproj_kernel(x_ref, n_ref, sc_ref, sh_ref, w_ref, o_ref, h_sc):
    @pl.when(pl.program_id(1) == 0)
    def _():
        y = _rms(x_ref[...]) * n_ref[...]
        h_sc[...] = (y * (1.0 + sc_ref[0]) + sh_ref[0]).astype(BF16)

    o_ref[...] = jnp.dot(h_sc[...], w_ref[...], preferred_element_type=F32).astype(o_ref.dtype)


def _inproj(x2, norm, sc, sh, w, rows_per_batch):
    t = x2.shape[0]
    n = w.shape[1]
    tm = _largest_tile(rows_per_batch, 1024, 8)
    tn = _largest_tile(n, 2560, LANES)
    per_b = rows_per_batch // tm
    if sc.shape[0] == 1:
        mod_map = lambda i, j: (0, 0, 0)
    else:
        mod_map = lambda i, j: (i // per_b, 0, 0)
    return pl.pallas_call(
        _inproj_kernel,
        grid=(t // tm, n // tn),
        in_specs=[pl.BlockSpec((tm, D_MODEL), lambda i, j: (i, 0)),
                  pl.BlockSpec((1, D_MODEL), lambda i, j: (0, 0)),
                  pl.BlockSpec((1, 1, D_MODEL), mod_map),
                  pl.BlockSpec((1, 1, D_MODEL), mod_map),
                  pl.BlockSpec((D_MODEL, tn), lambda i, j: (0, j))],
        out_specs=pl.BlockSpec((tm, tn), lambda i, j: (i, j)),
        out_shape=jax.ShapeDtypeStruct((t, n), BF16),
        scratch_shapes=[pltpu.VMEM((tm, D_MODEL), BF16)],
        compiler_params=_params(2),
        name="inproj",
    )(x2, norm, sc, sh, w)


def _kprep_kernel(k_ref, g_ref, cos_ref, sin_ref, o_ref, *, use_rope):
    k = k_ref[...].astype(F32)
    for g in range(ATTN_KV_HEADS):
        sl = slice(g * ATTN_HEAD_DIM, (g + 1) * ATTN_HEAD_DIM)
        kh = _rms(k[:, sl]) * g_ref[...]
        if use_rope:
            kh = _rope_block(kh, cos_ref[...], sin_ref[...], ATTN_HEAD_DIM // 4)
        o_ref[:, sl] = kh.astype(BF16)


def _kprep(proj, gain, cos, sin, rows_per_batch, use_rope):
    t = proj.shape[0]
    tm = _largest_tile(rows_per_batch, 1024, 8)
    per_b = rows_per_batch // tm
    return pl.pallas_call(
        functools.partial(_kprep_kernel, use_rope=use_rope),
        grid=(t // tm,),
        in_specs=[pl.BlockSpec((tm, ATTN_KV_W), lambda i: (i, COL_KA // ATTN_KV_W)),
                  pl.BlockSpec((1, ATTN_HEAD_DIM), lambda i: (0, 0)),
                  pl.BlockSpec((tm, ATTN_HEAD_DIM), lambda i: (i % per_b, 0)),
                  pl.BlockSpec((tm, ATTN_HEAD_DIM), lambda i: (i % per_b, 0))],
        out_specs=pl.BlockSpec((tm, ATTN_KV_W), lambda i: (i, 0)),
        out_shape=jax.ShapeDtypeStruct((t, ATTN_KV_W), BF16),
        compiler_params=_params(1),
        name="kprep",
    )(proj, gain, cos, sin)


def _ret_kernel(dch_ref, q_ref, k_ref, v_ref, cos_ref, sin_ref, dmat_ref, din_ref, dout_ref, s0_ref,
                o_ref, sfin_ref, s_sc, *, use_rope, n_chunks):
    h = pl.program_id(1)
    c = pl.program_id(2)

    @pl.when(c == 0)
    def _():
        s_sc[...] = s0_ref[0, 0]

    q = q_ref[...].astype(F32)
    k = k_ref[...].astype(F32)
    if use_rope:
        cos = cos_ref[...]
        sin = sin_ref[...]
        q = jnp.concatenate([_rope_block(q[:, i * LANES:(i + 1) * LANES], cos[:, i * LANES:(i + 1) * LANES],
                                         sin[:, i * LANES:(i + 1) * LANES], RET_QK_DIM // 4)
                             for i in range(RET_QK_DIM // LANES)], axis=1)
        k = jnp.concatenate([_rope_block(k[:, i * LANES:(i + 1) * LANES], cos[:, i * LANES:(i + 1) * LANES],
                                         sin[:, i * LANES:(i + 1) * LANES], RET_QK_DIM // 4)
                             for i in range(RET_QK_DIM // LANES)], axis=1)
    v = v_ref[...]
    s = s_sc[...]
    scores = lax.dot_general(q.astype(BF16), k.astype(BF16), (((1,), (1,)), ((), ())),
                             preferred_element_type=F32) * dmat_ref[0]
    o = jnp.dot(scores.astype(BF16), v, preferred_element_type=F32)
    o = o + jnp.dot((q * din_ref[0]).astype(BF16), s.astype(BF16), preferred_element_type=F32)
    o_ref[...] = o.astype(o_ref.dtype)
    kd_t = (k * dout_ref[0]).T.astype(BF16)
    s_new = s * dch_ref[h] + jnp.dot(kd_t, v, preferred_element_type=F32)
    s_sc[...] = s_new

    @pl.when(c == n_chunks - 1)
    def _():
        sfin_ref[0, 0] = s_new


def _ret_tables(log_gamma, chunk, reverse, k_scale):
    pos = jnp.arange(chunk, dtype=F32)
    diff = pos[:, None] - pos[None, :]
    if reverse:
        diff = -diff
        mask = diff > 0
        p_in = chunk - pos
        p_out = pos
    else:
        mask = diff >= 0
        p_in = pos + 1.0
        p_out = chunk - 1.0 - pos
    lg = log_gamma.astype(F32)
    dmat = jnp.where(mask[None], jnp.exp(lg[:, None, None] * jnp.maximum(diff, 0.0)[None]), 0.0) * k_scale
    d_in = jnp.exp(lg[:, None] * p_in)
    d_out = jnp.exp(lg[:, None] * p_out) * k_scale
    d_in = jnp.broadcast_to(d_in[:, :, None], (RET_HEADS, chunk, RET_QK_DIM))
    d_out = jnp.broadcast_to(d_out[:, :, None], (RET_HEADS, chunk, RET_QK_DIM))
    d_chunk = jnp.exp(lg * chunk)
    return d_chunk, dmat, d_in, d_out


def _retention(proj, cos, sin, log_gamma, s0, batch, seq, reverse, use_rope):
    chunk = min(RET_CHUNK, seq)
    nc = seq // chunk
    k_scale = RET_QK_DIM ** -0.5
    d_chunk, dmat, d_in, d_out = _ret_tables(log_gamma, chunk, reverse, k_scale)
    if reverse:
        cidx = lambda c: nc - 1 - c
    else:
        cidx = lambda c: c
    row = lambda b, c: b * nc + cidx(c)
    return pl.pallas_call(
        functools.partial(_ret_kernel, use_rope=use_rope, n_chunks=nc),
        grid=(batch, RET_HEADS, nc),
        in_specs=[pl.BlockSpec(memory_space=pltpu.SMEM),
                  pl.BlockSpec((chunk, RET_QK_DIM), lambda b, h, c: (row(b, c), COL_QR // RET_QK_DIM + h)),
                  pl.BlockSpec((chunk, RET_QK_DIM), lambda b, h, c: (row(b, c), COL_KR // RET_QK_DIM + h)),
                  pl.BlockSpec((chunk, RET_V_DIM), lambda b, h, c: (row(b, c), COL_VR // RET_V_DIM + h)),
                  pl.BlockSpec((chunk, RET_QK_DIM), lambda b, h, c: (cidx(c), 0)),
                  pl.BlockSpec((chunk, RET_QK_DIM), lambda b, h, c: (cidx(c), 0)),
                  pl.BlockSpec((1, chunk, chunk), lambda b, h, c: (h, 0, 0)),
                  pl.BlockSpec((1, chunk, RET_QK_DIM), lambda b, h, c: (h, 0, 0)),
                  pl.BlockSpec((1, chunk, RET_QK_DIM), lambda b, h, c: (h, 0, 0)),
                  pl.BlockSpec((1, 1, RET_QK_DIM, RET_V_DIM), lambda b, h, c: (b, h, 0, 0))],
        out_specs=[pl.BlockSpec((chunk, RET_V_DIM), lambda b, h, c: (row(b, c), h)),
                   pl.BlockSpec((1, 1, RET_QK_DIM, RET_V_DIM), lambda b, h, c: (b, h, 0, 0))],
        out_shape=[jax.ShapeDtypeStruct((batch * seq, RET_V_W), BF16),
                   jax.ShapeDtypeStruct((batch, RET_HEADS, RET_QK_DIM, RET_V_DIM), F32)],
        scratch_shapes=[pltpu.VMEM((RET_QK_DIM, RET_V_DIM), F32)],
        compiler_params=_params(3),
        name="ret",
    )(d_chunk, proj, proj, proj, cos, sin, dmat, d_in, d_out, s0)


def _attn_kernel(q_ref, g_ref, cos_ref, sin_ref, k_ref, vt_ref, o_ref, qt_sc, m_sc, acc_sc, *, tq, n_kv):
    j = pl.program_id(3)

    @pl.when(j == 0)
    def _():
        q = q_ref[...].astype(F32)
        scale = ATTN_HEAD_DIM ** -0.5 * np.log2(np.e)
        for g in range(ATTN_GROUP):
            qh = _rms(q[:, g * ATTN_HEAD_DIM:(g + 1) * ATTN_HEAD_DIM]) * g_ref[...]
            qh = _rope_block(qh, cos_ref[...], sin_ref[...], ATTN_HEAD_DIM // 4) * scale
            qt_sc[:, g * tq:(g + 1) * tq] = qh.T.astype(BF16)
        m_sc[...] = jnp.full(m_sc.shape, -jnp.inf, F32)
        acc_sc[...] = jnp.zeros(acc_sc.shape, F32)

    s_t = jnp.dot(k_ref[0], qt_sc[...], preferred_element_type=F32)
    m_prev = m_sc[...]
    m_new = jnp.maximum(m_prev, jnp.max(s_t, axis=0, keepdims=True))
    alpha = jnp.exp2(m_prev - m_new)
    p = jnp.exp2((s_t - m_new).astype(BF16))
    acc_sc[...] = alpha * acc_sc[...] + jnp.dot(vt_ref[0], p, preferred_element_type=F32)
    m_sc[...] = m_new

    @pl.when(j == n_kv - 1)
    def _():
        acc = acc_sc[...]
        o = acc[:ATTN_HEAD_DIM] / acc[ATTN_HEAD_DIM:ATTN_HEAD_DIM + 1]
        for g in range(ATTN_GROUP):
            o_ref[:, g * ATTN_HEAD_DIM:(g + 1) * ATTN_HEAD_DIM] = o[:, g * tq:(g + 1) * tq].T.astype(o_ref.dtype)


def _attention(proj, gain, cos, sin, k_all, vt_all, batch, seq):
    lk = k_all.shape[1]
    tq = _largest_tile(seq, 256, 8)
    tk = _largest_tile(lk, 1536, LANES)
    nq = seq // tq
    nk = lk // tk
    qw = ATTN_GROUP * ATTN_HEAD_DIM
    vrows = ATTN_HEAD_DIM + ONES_ROWS
    return pl.pallas_call(
        functools.partial(_attn_kernel, tq=tq, n_kv=nk),
        grid=(batch, ATTN_KV_HEADS, nq, nk),
        in_specs=[pl.BlockSpec((tq, qw), lambda b, g, i, j: (b * nq + i, COL_QA // qw + g)),
                  pl.BlockSpec((1, ATTN_HEAD_DIM), lambda b, g, i, j: (0, 0)),
                  pl.BlockSpec((tq, ATTN_HEAD_DIM), lambda b, g, i, j: (i, 0)),
                  pl.BlockSpec((tq, ATTN_HEAD_DIM), lambda b, g, i, j: (i, 0)),
                  pl.BlockSpec((1, tk, ATTN_HEAD_DIM), lambda b, g, i, j: (b, j, g)),
                  pl.BlockSpec((1, vrows, tk), lambda b, g, i, j: (b, g, j))],
        out_specs=pl.BlockSpec((tq, qw), lambda b, g, i, j: (b * nq + i, g)),
        out_shape=jax.ShapeDtypeStruct((batch * seq, ATTN_Q_W), BF16),
        scratch_shapes=[pltpu.VMEM((ATTN_HEAD_DIM, ATTN_GROUP * tq), BF16),
                        pltpu.VMEM((1, ATTN_GROUP * tq), F32),
                        pltpu.VMEM((vrows, ATTN_GROUP * tq), F32)],
        compiler_params=_params(4),
        name="attn",
    )(proj, gain, cos, sin, k_all, vt_all)


def _merge_kernel(x_ref, g1_ref, of_ref, ob_ref, gr_ref, ao_ref, gtr_ref, gta_ref,
                  wro_ref, wao_ref, wo_ref, o_ref):
    ro = of_ref[...].astype(F32) + ob_ref[...].astype(F32)
    gr = gr_ref[...].astype(F32)
    parts = []
    for h in range(RET_HEADS):
        sl = slice(h * RET_V_DIM, (h + 1) * RET_V_DIM)
        g = gr[:, sl]
        parts.append((g * jax.nn.sigmoid(g) * _rms(ro[:, sl])).astype(BF16))
    ret_in = jnp.concatenate(parts, axis=1)
    ret_branch = jnp.dot(ret_in, wro_ref[...], preferred_element_type=F32)
    attn_branch = jnp.dot(ao_ref[...], wao_ref[...], preferred_element_type=F32)
    y = (jax.nn.sigmoid(gtr_ref[...].astype(F32)) * ret_branch
         + jax.nn.sigmoid(gta_ref[...].astype(F32)) * attn_branch)
    y = jnp.dot(y.astype(BF16), wo_ref[...], preferred_element_type=F32)
    o_ref[...] = x_ref[...] + g1_ref[0] * y


def _merge(x2, g1, o_f, o_b, proj, attn_o, w_ret_o, w_attn_o, w_out, seq):
    t = x2.shape[0]
    tm = _largest_tile(seq, 256, 8)
    per_b = seq // tm
    full = lambda i: (0, 0)
    return pl.pallas_call(
        _merge_kernel,
        grid=(t // tm,),
        in_specs=[pl.BlockSpec((tm, D_MODEL), lambda i: (i, 0)),
                  pl.BlockSpec((1, 1, D_MODEL), lambda i: (i // per_b, 0, 0)),
                  pl.BlockSpec((tm, RET_V_W), lambda i: (i, 0)),
                  pl.BlockSpec((tm, RET_V_W), lambda i: (i, 0)),
                  pl.BlockSpec((tm, RET_V_W), lambda i: (i, COL_GR // RET_V_W)),
                  pl.BlockSpec((tm, ATTN_Q_W), lambda i: (i, 0)),
                  pl.BlockSpec((tm, D_MODEL), lambda i: (i, COL_GATE_R // D_MODEL)),
                  pl.BlockSpec((tm, D_MODEL), lambda i: (i, COL_GATE_A // D_MODEL)),
                  pl.BlockSpec((RET_V_W, D_MODEL), full),
                  pl.BlockSpec((ATTN_Q_W, D_MODEL), full),
                  pl.BlockSpec((D_MODEL, D_MODEL), full)],
        out_specs=pl.BlockSpec((tm, D_MODEL), lambda i: (i, 0)),
        out_shape=jax.ShapeDtypeStruct((t, D_MODEL), F32),
        compiler_params=_params(1),
        name="merge",
    )(x2, g1, o_f, o_b, proj, attn_o, proj, proj, w_ret_o, w_attn_o, w_out)


def _router_kernel(x_ref, n_ref, sc_ref, sh_ref, wr_ref, br_ref, h_ref, ti_ref, tw_ref):
    h = _rms(x_ref[...]) * n_ref[...] * (1.0 + sc_ref[0]) + sh_ref[0]
    h_ref[...] = _pack_pairs(h)
    logits = lax.dot_general(wr_ref[...], h, (((1,), (1,)), ((), ())), preferred_element_type=F32,
                             precision=lax.Precision.HIGHEST) + br_ref[...]
    eid = lax.broadcasted_iota(jnp.int32, logits.shape, 0)
    vals = logits
    top_v = []
    top_i = []
    for _ in range(TOP_K):
        m = jnp.max(vals, axis=0, keepdims=True)
        idx = jnp.min(jnp.where(vals == m, eid, N_EXPERTS), axis=0, keepdims=True)
        top_v.append(m)
        top_i.append(idx)
        vals = jnp.where(eid == idx, -jnp.inf, vals)
    ex = [jnp.exp(v - top_v[0]) for v in top_v]
    denom = ex[0] + ex[1] + ex[2] + ex[3]
    ti_ref[...] = jnp.concatenate(top_i, axis=0)
    tw_ref[...] = jnp.concatenate([e / denom for e in ex], axis=0)


def _router(x1, norm, sc, sh, w_router_t, b_router, seq):
    t = x1.shape[0]
    tm = _largest_tile(seq, 1024, LANES)
    per_b = seq // tm
    mod_map = lambda i: (i // per_b, 0, 0)
    return pl.pallas_call(
        _router_kernel,
        grid=(t // tm,),
        in_specs=[pl.BlockSpec((tm, D_MODEL), lambda i: (i, 0)),
                  pl.BlockSpec((1, D_MODEL), lambda i: (0, 0)),
                  pl.BlockSpec((1, 1, D_MODEL), mod_map),
                  pl.BlockSpec((1, 1, D_MODEL), mod_map),
                  pl.BlockSpec((N_EXPERTS, D_MODEL), lambda i: (0, 0)),
                  pl.BlockSpec((N_EXPERTS, 1), lambda i: (0, 0))],
        out_specs=[pl.BlockSpec((tm, PACK_W), lambda i: (i, 0)),
                   pl.BlockSpec((TOP_K, tm), lambda i: (0, i)),
                   pl.BlockSpec((TOP_K, tm), lambda i: (0, i))],
        out_shape=[jax.ShapeDtypeStruct((t, PACK_W), PACK_DTYPE),
                   jax.ShapeDtypeStruct((TOP_K, t), jnp.int32),
                   jax.ShapeDtypeStruct((TOP_K, t), F32)],
        compiler_params=_params(1),
        name="router",
    )(x1, norm, sc, sh, w_router_t, b_router)


def _gather_rows(src, idx):
    n, w = src.shape
    r = idx.shape[0]
    workers = SC_CORES * SC_SUBCORES
    per_w = r // workers
    n_win = per_w // SC_GATHER_ROWS
    assert per_w * workers == r and n_win * SC_GATHER_ROWS == per_w, (r, workers, SC_GATHER_ROWS)
    mesh = plsc.VectorSubcoreMesh(core_axis_name="c", subcore_axis_name="s")

    @functools.partial(
        pl.kernel, mesh=mesh, out_type=jax.ShapeDtypeStruct((r, w), src.dtype),
        scratch_types=[pltpu.VMEM((SC_GATHER_ROWS,), jnp.int32),
                       pltpu.VMEM((SC_GATHER_ROWS, w), src.dtype),
                       pltpu.SemaphoreType.DMA])
    def gather(src_hbm, idx_hbm, out_hbm, idx_v, rows_v, sem):
        wid = lax.axis_index("s") * SC_CORES + lax.axis_index("c")
        base = wid * per_w

        @pl.loop(0, n_win)
        def _(win):
            off = base + win * SC_GATHER_ROWS
            pltpu.sync_copy(idx_hbm.at[pl.ds(off, SC_GATHER_ROWS)], idx_v)
            pltpu.async_copy(src_hbm.at[idx_v], rows_v, sem).wait()
            pltpu.sync_copy(rows_v, out_hbm.at[pl.ds(off, SC_GATHER_ROWS)])

    return gather(src, idx)


def _ffn_kernel(te_ref, tv_ref, x_ref, w1_ref, b1_ref, w2_ref, b2_ref, o_ref):
    valid = tv_ref[pl.program_id(0)] != 0

    @pl.when(jnp.logical_not(valid))
    def _():
        o_ref[...] = jnp.zeros(o_ref.shape, o_ref.dtype)

    @pl.when(valid)
    def _():
        lo, hi = _unpack_pairs(x_ref[...])
        x = jnp.concatenate([lo, hi], axis=1).astype(BF16)
        a = jnp.dot(x, w1_ref[0], preferred_element_type=F32) + b1_ref[0]
        gate = jnp.minimum(a[:, :EXPERT_FF], SWIGLU_LIMIT)
        up = jnp.clip(a[:, EXPERT_FF:], -SWIGLU_LIMIT, SWIGLU_LIMIT)
        act = gate * jax.nn.sigmoid(SWIGLU_ALPHA * gate) * (up + 1.0)
        y = jnp.dot(act.astype(BF16), w2_ref[0], preferred_element_type=F32) + b2_ref[0]
        o_ref[...] = _pack_pairs(y)


def _ffn(xs, tile_expert, tile_valid, w1, b1, w2, b2, tg):
    p = xs.shape[0]
    grid_spec = pltpu.PrefetchScalarGridSpec(
        num_scalar_prefetch=2,
        grid=(p // tg,),
        in_specs=[pl.BlockSpec((tg, PACK_W), lambda j, te, tv: (j, 0)),
                  pl.BlockSpec((1, D_MODEL, 2 * EXPERT_FF), lambda j, te, tv: (te[j], 0, 0)),
                  pl.BlockSpec((1, 1, 2 * EXPERT_FF), lambda j, te, tv: (te[j], 0, 0)),
                  pl.BlockSpec((1, EXPERT_FF, D_MODEL), lambda j, te, tv: (te[j], 0, 0)),
                  pl.BlockSpec((1, 1, D_MODEL), lambda j, te, tv: (te[j], 0, 0))],
        out_specs=pl.BlockSpec((tg, PACK_W), lambda j, te, tv: (j, 0)),
    )
    return pl.pallas_call(
        _ffn_kernel,
        grid_spec=grid_spec,
        out_shape=jax.ShapeDtypeStruct((p, PACK_W), PACK_DTYPE),
        compiler_params=_params(1),
        name="ffn",
    )(tile_expert, tile_valid, xs, w1, b1, w2, b2)


def _combine_kernel(x_ref, g2_ref, w_ref, y_ref, o_ref):
    w = w_ref[...]
    acc_lo = None
    for k in range(TOP_K):
        lo, hi = _unpack_pairs(y_ref[k])
        wk = w[:, k:k + 1]
        acc_lo = wk * lo if acc_lo is None else acc_lo + wk * lo
        acc_hi = wk * hi if k == 0 else acc_hi + wk * hi
    acc = jnp.concatenate([acc_lo, acc_hi], axis=1)
    o_ref[...] = x_ref[...] + g2_ref[0] * acc


def _combine(x1, g2, w_tok, yk, seq):
    t = x1.shape[0]
    tm = _largest_tile(seq, 512, 8)
    per_b = seq // tm
    return pl.pallas_call(
        _combine_kernel,
        grid=(t // tm,),
        in_specs=[pl.BlockSpec((tm, D_MODEL), lambda i: (i, 0)),
                  pl.BlockSpec((1, 1, D_MODEL), lambda i: (i // per_b, 0, 0)),
                  pl.BlockSpec((tm, TOP_K), lambda i: (i, 0)),
                  pl.BlockSpec((TOP_K, tm, PACK_W), lambda i: (0, i, 0))],
        out_specs=pl.BlockSpec((tm, D_MODEL), lambda i: (i, 0)),
        out_shape=jax.ShapeDtypeStruct((t, D_MODEL), F32),
        compiler_params=_params(1),
        name="combine",
    )(x1, g2, w_tok, yk)


def _plan(top_i, tg):
    t = top_i.shape[1]
    n_assign = TOP_K * t
    p = n_assign + N_EXPERTS * tg
    n_tiles = p // tg
    flat_e = top_i.reshape(-1)
    sorted_e, sorted_a = lax.sort_key_val(flat_e, jnp.arange(n_assign, dtype=jnp.int32))
    experts = jnp.arange(N_EXPERTS + 1, dtype=jnp.int32)
    bounds = jnp.sum((sorted_e[None, :] < experts[:, None]).astype(jnp.int32), axis=1)
    start = bounds[:-1]
    counts = bounds[1:] - start
    padded = ((counts + tg - 1) // tg) * tg
    off_end = jnp.cumsum(padded)
    off = off_end - padded
    dest_sorted = off[sorted_e] + jnp.arange(n_assign, dtype=jnp.int32) - start[sorted_e]
    _, dest = lax.sort_key_val(sorted_a, dest_sorted)
    tile_start = jnp.arange(n_tiles, dtype=jnp.int32) * tg
    tile_valid = (tile_start < off_end[-1]).astype(jnp.int32)
    te = jnp.sum((tile_start[:, None] >= off_end[None, :]).astype(jnp.int32), axis=1)
    last_e = jnp.sum(((off_end[-1] - 1) >= off_end).astype(jnp.int32))
    tile_expert = jnp.where(tile_valid != 0, te, last_e)
    slot = jnp.arange(p, dtype=jnp.int32)
    e_slot = jnp.repeat(tile_expert, tg)
    r = slot - off[e_slot]
    ok = (r < counts[e_slot]) & (jnp.repeat(tile_valid, tg) != 0)
    a_slot = sorted_a[jnp.clip(start[e_slot] + r, 0, n_assign - 1)]
    src_tok = jnp.where(ok, a_slot % t, 0).astype(jnp.int32)
    return src_tok, dest, tile_expert, tile_valid


def kernel(x, c, ctx, c_ctx, norm1, norm2, w_ada, b_ada, w_in, ret_decay_f, ret_decay_b, attn_q_norm, attn_k_norm,
           w_ret_o, w_attn_o, w_out, w_router, b_router, w_exp_in, b_exp_in, w_exp_out, b_exp_out):
    assert w_in.shape[0] == 1, "single-layer block"
    b, seq, d = x.shape
    n_ctx = ctx.shape[1]
    t = b * seq
    rows = seq // GRID_W

    idx = np.cumsum(IN_SIZES)[:-1].tolist()
    wq_r, wk_r, wv_r, wg_r, wq_a, wk_a, wv_a, wgt_r, wgt_a = jnp.split(w_in[0], idx, axis=-1)
    w_in_p = jnp.concatenate([wq_r, wk_r, wv_r, wg_r, wq_a, wgt_r, wgt_a, wk_a, wv_a], axis=-1).astype(BF16)
    w1 = w_exp_in[0].astype(BF16)
    w2 = w_exp_out[0].astype(BF16)
    b1 = b_exp_in[0].reshape(N_EXPERTS, 1, 2 * EXPERT_FF)
    b2 = b_exp_out[0].reshape(N_EXPERTS, 1, D_MODEL)

    pad = (-(b + 1)) % 8
    c_all = jnp.concatenate([c, c_ctx[None, :], jnp.zeros((pad, d), F32)], axis=0)
    mod = _ada(c_all, w_ada[0], b_ada[0])
    sh1, sc1, g1, sh2, sc2, g2 = [m.reshape(-1, 1, d) for m in jnp.split(mod, 6, axis=-1)]
    lat = lambda m: m[:b]
    cx = lambda m: m[b:b + 1]

    x2 = x.reshape(t, d)
    proj = _inproj(x2, norm1, lat(sc1), lat(sh1), w_in_p, seq)
    proj_c = _inproj(ctx.reshape(b * n_ctx, d), norm1, cx(sc1), cx(sh1), w_in_p, n_ctx)

    cos_r, sin_r = _rope_tables(rows, RET_QK_DIM)
    cos_a, sin_a = _rope_tables(rows, ATTN_HEAD_DIM)
    lg_f = -jax.nn.softplus(ret_decay_f[0].astype(F32))
    lg_b = -jax.nn.softplus(ret_decay_b[0].astype(F32))
    zero_state = jnp.zeros((b, RET_HEADS, RET_QK_DIM, RET_V_DIM), F32)
    cos_c = jnp.ones((n_ctx, RET_QK_DIM), F32)
    sin_c = jnp.zeros((n_ctx, RET_QK_DIM), F32)
    _, s_f = _retention(proj_c, cos_c, sin_c, lg_f, zero_state, b, n_ctx, False, False)
    _, s_b = _retention(proj_c, cos_c, sin_c, lg_b, zero_state, b, n_ctx, True, False)
    o_f, _ = _retention(proj, cos_r, sin_r, lg_f, s_f, b, seq, False, True)
    o_b, _ = _retention(proj, cos_r, sin_r, lg_b, s_b, b, seq, True, True)

    k_lat = _kprep(proj, attn_k_norm, cos_a, sin_a, seq, True)
    k_ctx = _kprep(proj_c, attn_k_norm, cos_a, sin_a, n_ctx, False)
    k_all = jnp.concatenate([k_ctx.reshape(b, n_ctx, ATTN_KV_W), k_lat.reshape(b, seq, ATTN_KV_W)], axis=1)
    v_all = jnp.concatenate([proj_c[:, COL_VA:COL_VA + ATTN_KV_W].reshape(b, n_ctx, ATTN_KV_W),
                             proj[:, COL_VA:COL_VA + ATTN_KV_W].reshape(b, seq, ATTN_KV_W)], axis=1)
    lk = n_ctx + seq
    vt = jnp.swapaxes(v_all.reshape(b, lk, ATTN_KV_HEADS, ATTN_HEAD_DIM), 1, 3).swapaxes(1, 2)
    vt_all = jnp.concatenate([vt, jnp.ones((b, ATTN_KV_HEADS, ONES_ROWS, lk), vt.dtype)], axis=2)
    vt_all = vt_all.reshape(b, ATTN_KV_HEADS * (ATTN_HEAD_DIM + ONES_ROWS), lk)
    attn_o = _attention(proj, attn_q_norm, cos_a, sin_a, k_all, vt_all, b, seq)

    x1 = _merge(x2, lat(g1), o_f, o_b, proj, attn_o, w_ret_o[0].astype(BF16), w_attn_o[0].astype(BF16),
                w_out[0].astype(BF16), seq)

    h2, top_i, top_w = _router(x1, norm2, lat(sc2), lat(sh2), w_router[0].T, b_router[0].reshape(N_EXPERTS, 1), seq)
    tg = _largest_tile(TOP_K * t, 512, 8)
    src_tok, dest, tile_expert, tile_valid = _plan(top_i, tg)
    xs = _gather_rows(h2, src_tok)
    ys = _ffn(xs, tile_expert, tile_valid, w1, b1, w2, b2, tg)
    yk = _gather_rows(ys, dest).reshape(TOP_K, t, PACK_W)
    out = _combine(x1, lat(g2), top_w.T, yk, seq)
    return out.reshape(b, seq, d)
```

```python
import functools

import jax
import jax.numpy as jnp
import numpy as np
from jax import lax
from jax.experimental import pallas as pl
from jax.experimental.pallas import tpu as pltpu
from jax.experimental.pallas import tpu_sc as plsc

F32 = jnp.float32
BF16 = jnp.bfloat16

D_MODEL = 1024
GRID_W = 64
EPS = 1e-6
RET_HEADS = 4
RET_QK_DIM = 256
RET_V_DIM = 512
ATTN_HEADS = 8
ATTN_KV_HEADS = 2
ATTN_GROUP = ATTN_HEADS // ATTN_KV_HEADS
ATTN_HEAD_DIM = 128
ROPE_THETA = 10000.0
N_EXPERTS = 32
TOP_K = 4
EXPERT_FF = 1024
SWIGLU_LIMIT = 7.0
SWIGLU_ALPHA = 1.702

RET_QK_W = RET_HEADS * RET_QK_DIM
RET_V_W = RET_HEADS * RET_V_DIM
ATTN_Q_W = ATTN_HEADS * ATTN_HEAD_DIM
ATTN_KV_W = ATTN_KV_HEADS * ATTN_HEAD_DIM
IN_SIZES = (RET_QK_W, RET_QK_W, RET_V_W, RET_V_W, ATTN_Q_W, ATTN_KV_W, ATTN_KV_W, D_MODEL, D_MODEL)
IN_WIDTH = sum(IN_SIZES)
COL_QR = 0
COL_KR = COL_QR + RET_QK_W
COL_VR = COL_KR + RET_QK_W
COL_GR = COL_VR + RET_V_W
COL_QA = COL_GR + RET_V_W
COL_GATE_R = COL_QA + ATTN_Q_W
COL_GATE_A = COL_GATE_R + D_MODEL
COL_KA = COL_GATE_A + D_MODEL
COL_VA = COL_KA + ATTN_KV_W

RET_CHUNK = 256
LANES = 128
SC_CORES = 2
SC_SUBCORES = 16
SC_GATHER_ROWS = 128
MXU_DEPTH = 256
ATTN_KEY_CHUNK = 768
ATTN_Q_TILE = 512
ONES_ROWS = 16
PACK_DTYPE = jnp.int32
PACK_W = D_MODEL // 2
TILE_PAD, TILE_BODY, TILE_FIRST = 0, 1, 2
VMEM_LIMIT = 56 * 1024 * 1024

ARB = pltpu.ARBITRARY


def _params(n_axes, **kw):
    return pltpu.CompilerParams(dimension_semantics=(ARB,) * n_axes, vmem_limit_bytes=VMEM_LIMIT, **kw)


def _largest_tile(n, cap, mult):
    best = None
    for t in range(mult, min(n, cap) + 1, mult):
        if n % t == 0:
            best = t
    assert best is not None, (n, cap, mult)
    return best


def _rms(x):
    return x * lax.rsqrt(jnp.mean(x * x, axis=-1, keepdims=True) + EPS)


def _pack_pairs(x):
    half = x.shape[1] // 2
    lo = lax.bitcast_convert_type(x[:, :half].astype(BF16).astype(F32), jnp.int32)
    hi = lax.bitcast_convert_type(x[:, half:].astype(BF16).astype(F32), jnp.int32)
    return lax.bitwise_or(lax.bitwise_and(hi, jnp.int32(-65536)), lax.shift_right_logical(lo, jnp.int32(16)))


def _unpack_pairs(w):
    lo = lax.bitcast_convert_type(lax.shift_left(w, jnp.int32(16)), F32)
    hi = lax.bitcast_convert_type(lax.bitwise_and(w, jnp.int32(-65536)), F32)
    return lo, hi


def _rope_block(x, cos, sin, half):
    if 2 * half == LANES:
        swapped = pltpu.roll(x, half, 1)
    else:
        lane = lax.broadcasted_iota(jnp.int32, x.shape, 1)
        first = (lane % (2 * half)) < half
        swapped = jnp.where(first, pltpu.roll(x, LANES - half, 1), pltpu.roll(x, half, 1))
    return x * cos + swapped * sin


def _rope_tables(rows, head_dim):
    row_id = jnp.broadcast_to(jnp.arange(rows, dtype=F32)[:, None], (rows, GRID_W)).reshape(-1)
    col_id = jnp.broadcast_to(jnp.arange(GRID_W, dtype=F32)[None, :], (rows, GRID_W)).reshape(-1)
    n_freq = head_dim // 4
    inv_freq = ROPE_THETA ** (-jnp.arange(n_freq, dtype=F32) / n_freq)
    ang_r = row_id[:, None] * inv_freq
    ang_c = col_id[:, None] * inv_freq
    cos = jnp.concatenate([jnp.cos(ang_r), jnp.cos(ang_r), jnp.cos(ang_c), jnp.cos(ang_c)], axis=1)
    sin = jnp.concatenate([-jnp.sin(ang_r), jnp.sin(ang_r), -jnp.sin(ang_c), jnp.sin(ang_c)], axis=1)
    return cos, sin


def _ada_kernel(c_ref, w_ref, b_ref, o_ref):
    c = c_ref[...]
    s = c * jax.nn.sigmoid(c)
    o_ref[...] = jnp.dot(s, w_ref[...], preferred_element_type=F32,
                         precision=lax.Precision.HIGHEST) + b_ref[...]


def _ada(c_pad, w_ada, b_ada):
    rows = c_pad.shape[0]
    n = w_ada.shape[1]
    tn = _largest_tile(n, 1536, LANES)
    return pl.pallas_call(
        _ada_kernel,
        grid=(n // tn,),
        in_specs=[pl.BlockSpec((rows, D_MODEL), lambda j: (0, 0)),
                  pl.BlockSpec((D_MODEL, tn), lambda j: (0, j)),
                  pl.BlockSpec((1, tn), lambda j: (0, j))],
        out_specs=pl.BlockSpec((rows, tn), lambda j: (0, j)),
        out_shape=jax.ShapeDtypeStruct((rows, n), F32),
        compiler_params=_params(1),
        name="ada",
    )(c_pad, w_ada, b_ada.reshape(1, n))


def _inproj_kernel(x_ref, n_ref, sc_ref, sh_ref, w_ref, o_ref, w_sc):
    @pl.when(pl.program_id(1) == 0)
    def _():
        w_sc[...] = w_ref[...].astype(BF16)

    y = _rms(x_ref[...]) * n_ref[...]
    h = (y * (1.0 + sc_ref[0]) + sh_ref[0]).astype(BF16)
    o_ref[...] = jnp.dot(h, w_sc[...], preferred_element_type=F32).astype(o_ref.dtype)


def _inproj(x2, norm, sc, sh, w, rows_per_batch):
    t = x2.shape[0]
    n = w.shape[1]
    tm = _largest_tile(rows_per_batch, 512, 8)
    tn = _largest_tile(n, 2560, LANES)
    per_b = rows_per_batch // tm
    if sc.shape[0] == 1:
        mod_map = lambda j, i: (0, 0, 0)
    else:
        mod_map = lambda j, i: (i // per_b, 0, 0)
    return pl.pallas_call(
        _inproj_kernel,
        grid=(n // tn, t // tm),
        in_specs=[pl.BlockSpec((tm, D_MODEL), lambda j, i: (i, 0)),
                  pl.BlockSpec((1, D_MODEL), lambda j, i: (0, 0)),
                  pl.BlockSpec((1, 1, D_MODEL), mod_map),
                  pl.BlockSpec((1, 1, D_MODEL), mod_map),
                  pl.BlockSpec((D_MODEL, tn), lambda j, i: (0, j))],
        out_specs=pl.BlockSpec((tm, tn), lambda j, i: (i, j)),
        out_shape=jax.ShapeDtypeStruct((t, n), BF16),
        scratch_shapes=[pltpu.VMEM((D_MODEL, tn), BF16)],
        compiler_params=_params(2),
        name="inproj",
    )(x2, norm, sc, sh, w)


def _kprep_kernel(k_ref, g_ref, cos_ref, sin_ref, o_ref, *, use_rope):
    k = k_ref[...].astype(F32)
    for g in range(ATTN_KV_HEADS):
        sl = slice(g * ATTN_HEAD_DIM, (g + 1) * ATTN_HEAD_DIM)
        kh = _rms(k[:, sl]) * g_ref[...]
        if use_rope:
            kh = _rope_block(kh, cos_ref[...], sin_ref[...], ATTN_HEAD_DIM // 4)
        o_ref[:, sl] = kh.astype(BF16)


def _kprep(proj, gain, cos, sin, rows_per_batch, use_rope):
    t = proj.shape[0]
    tm = _largest_tile(rows_per_batch, 1024, 8)
    per_b = rows_per_batch // tm
    return pl.pallas_call(
        functools.partial(_kprep_kernel, use_rope=use_rope),
        grid=(t // tm,),
        in_specs=[pl.BlockSpec((tm, ATTN_KV_W), lambda i: (i, COL_KA // ATTN_KV_W)),
                  pl.BlockSpec((1, ATTN_HEAD_DIM), lambda i: (0, 0)),
                  pl.BlockSpec((tm, ATTN_HEAD_DIM), lambda i: (i % per_b, 0)),
                  pl.BlockSpec((tm, ATTN_HEAD_DIM), lambda i: (i % per_b, 0))],
        out_specs=pl.BlockSpec((tm, ATTN_KV_W), lambda i: (i, 0)),
        out_shape=jax.ShapeDtypeStruct((t, ATTN_KV_W), BF16),
        compiler_params=_params(1),
        name="kprep",
    )(proj, gain, cos, sin)


def _ret_kernel(dch_ref, qf_ref, kf_ref, vf_ref, cosf_ref, sinf_ref, qb_ref, kb_ref, vb_ref, cosb_ref, sinb_ref,
                dmat_ref, din_ref, dout_ref, s0_ref, of_ref, ob_ref, sfin_ref, s_sc, *, use_rope, n_chunks):
    h = pl.program_id(1)
    c = pl.program_id(2)

    @pl.when(c == 0)
    def _():
        s_sc[...] = s0_ref[:, 0, 0]

    def rope(x, cos, sin):
        return jnp.concatenate([_rope_block(x[:, i * LANES:(i + 1) * LANES], cos[:, i * LANES:(i + 1) * LANES],
                                            sin[:, i * LANES:(i + 1) * LANES], RET_QK_DIM // 4)
                                for i in range(RET_QK_DIM // LANES)], axis=1)

    sides = ((qf_ref, kf_ref, vf_ref, cosf_ref, sinf_ref), (qb_ref, kb_ref, vb_ref, cosb_ref, sinb_ref))
    qs, ks, vs, ss = [], [], [], []
    for d, (q_ref, k_ref, v_ref, cos_ref, sin_ref) in enumerate(sides):
        q = q_ref[...].astype(F32)
        k = k_ref[...].astype(F32)
        if use_rope:
            q = rope(q, cos_ref[...], sin_ref[...])
            k = rope(k, cos_ref[...], sin_ref[...])
        qs.append(q)
        ks.append(k)
        vs.append(v_ref[...])
        ss.append(s_sc[d])
    scores = [lax.dot_general(qs[d].astype(BF16), ks[d].astype(BF16), (((1,), (1,)), ((), ())),
                              preferred_element_type=F32) * dmat_ref[d, 0] for d in range(2)]
    inter = [jnp.dot((qs[d] * din_ref[d, 0]).astype(BF16), ss[d].astype(BF16), preferred_element_type=F32)
             for d in range(2)]
    intra = [jnp.dot(scores[d].astype(BF16), vs[d], preferred_element_type=F32) for d in range(2)]
    of_ref[...] = (intra[0] + inter[0]).astype(of_ref.dtype)
    ob_ref[...] = (intra[1] + inter[1]).astype(ob_ref.dtype)
    s_new = [ss[d] * dch_ref[d * RET_HEADS + h]
             + jnp.dot((ks[d] * dout_ref[d, 0]).T.astype(BF16), vs[d], preferred_element_type=F32) for d in range(2)]
    s_sc[0] = s_new[0]
    s_sc[1] = s_new[1]

    @pl.when(c == n_chunks - 1)
    def _():
        sfin_ref[0, 0, 0] = s_new[0]
        sfin_ref[1, 0, 0] = s_new[1]


def _ret_tables(log_gamma, chunk, reverse, k_scale):
    pos = jnp.arange(chunk, dtype=F32)
    diff = pos[:, None] - pos[None, :]
    if reverse:
        diff = -diff
        mask = diff > 0
        p_in = chunk - pos
        p_out = pos
    else:
        mask = diff >= 0
        p_in = pos + 1.0
        p_out = chunk - 1.0 - pos
    lg = log_gamma.astype(F32)
    dmat = jnp.where(mask[None], jnp.exp(lg[:, None, None] * jnp.maximum(diff, 0.0)[None]), 0.0) * k_scale
    d_in = jnp.exp(lg[:, None] * p_in)
    d_out = jnp.exp(lg[:, None] * p_out) * k_scale
    d_in = jnp.broadcast_to(d_in[:, :, None], (RET_HEADS, chunk, RET_QK_DIM))
    d_out = jnp.broadcast_to(d_out[:, :, None], (RET_HEADS, chunk, RET_QK_DIM))
    d_chunk = jnp.exp(lg * chunk)
    return d_chunk, dmat, d_in, d_out


def _retention(proj, cos, sin, lg_f, lg_b, s0, batch, seq, use_rope):
    chunk = min(RET_CHUNK, seq)
    nc = seq // chunk
    k_scale = RET_QK_DIM ** -0.5
    tabs = [_ret_tables(lg_f, chunk, False, k_scale), _ret_tables(lg_b, chunk, True, k_scale)]
    d_chunk, dmat, d_in, d_out = [jnp.stack([tabs[0][i], tabs[1][i]]) for i in range(4)]
    fwd = lambda c: c
    bwd = lambda c: nc - 1 - c

    def side(cidx):
        row = lambda b, c: b * nc + cidx(c)
        return [pl.BlockSpec((chunk, RET_QK_DIM), lambda b, h, c: (row(b, c), COL_QR // RET_QK_DIM + h)),
                pl.BlockSpec((chunk, RET_QK_DIM), lambda b, h, c: (row(b, c), COL_KR // RET_QK_DIM + h)),
                pl.BlockSpec((chunk, RET_V_DIM), lambda b, h, c: (row(b, c), COL_VR // RET_V_DIM + h)),
                pl.BlockSpec((chunk, RET_QK_DIM), lambda b, h, c: (cidx(c), 0)),
                pl.BlockSpec((chunk, RET_QK_DIM), lambda b, h, c: (cidx(c), 0))]

    state_spec = pl.BlockSpec((2, 1, 1, RET_QK_DIM, RET_V_DIM), lambda b, h, c: (0, b, h, 0, 0))
    return pl.pallas_call(
        functools.partial(_ret_kernel, use_rope=use_rope, n_chunks=nc),
        grid=(batch, RET_HEADS, nc),
        in_specs=[pl.BlockSpec(memory_space=pltpu.SMEM)] + side(fwd) + side(bwd) + [
            pl.BlockSpec((2, 1, chunk, chunk), lambda b, h, c: (0, h, 0, 0)),
            pl.BlockSpec((2, 1, chunk, RET_QK_DIM), lambda b, h, c: (0, h, 0, 0)),
            pl.BlockSpec((2, 1, chunk, RET_QK_DIM), lambda b, h, c: (0, h, 0, 0)),
            state_spec],
        out_specs=[pl.BlockSpec((chunk, RET_V_DIM), lambda b, h, c: (b * nc + c, h)),
                   pl.BlockSpec((chunk, RET_V_DIM), lambda b, h, c: (b * nc + nc - 1 - c, h)),
                   state_spec],
        out_shape=[jax.ShapeDtypeStruct((batch * seq, RET_V_W), BF16),
                   jax.ShapeDtypeStruct((batch * seq, RET_V_W), BF16),
                   jax.ShapeDtypeStruct((2, batch, RET_HEADS, RET_QK_DIM, RET_V_DIM), F32)],
        scratch_shapes=[pltpu.VMEM((2, RET_QK_DIM, RET_V_DIM), F32)],
        compiler_params=_params(3),
        name="ret",
    )(d_chunk.reshape(-1), proj, proj, proj, cos, sin, proj, proj, proj, cos, sin, dmat, d_in, d_out, s0)


def _attn_kernel(q_ref, g_ref, cos_ref, sin_ref, k_ref, vt_ref, o_ref, qt_sc, m_sc, acc_sc,
                 s0_sc, s1_sc, p0_sc, p1_sc, a0_sc, a1_sc, *, tq, kc, n_chunks):
    s_bufs = (s0_sc, s1_sc)
    p_bufs = (p0_sc, p1_sc)
    a_bufs = (a0_sc, a1_sc)

    q = q_ref[...].astype(F32)
    scale = ATTN_HEAD_DIM ** -0.5 * np.log2(np.e)
    for g in range(ATTN_GROUP):
        qh = _rms(q[:, g * ATTN_HEAD_DIM:(g + 1) * ATTN_HEAD_DIM]) * g_ref[...]
        qh = _rope_block(qh, cos_ref[...], sin_ref[...], ATTN_HEAD_DIM // 4) * scale
        qt_sc[:, g * tq:(g + 1) * tq] = qh.T.astype(BF16)
    m_sc[...] = jnp.full(m_sc.shape, -jnp.inf, F32)
    acc_sc[...] = jnp.zeros(acc_sc.shape, F32)

    def stage_s(c, slot):
        off = pl.multiple_of(c * kc, kc)
        s_bufs[slot][...] = jnp.dot(k_ref[0, pl.ds(off, kc), :], qt_sc[...], preferred_element_type=F32)

    def stage_f(slot):
        s_t = s_bufs[slot][...]
        m_prev = m_sc[...]
        m_new = jnp.maximum(m_prev, jnp.max(s_t, axis=0, keepdims=True))
        m_sc[...] = m_new
        a_bufs[slot][...] = jnp.exp2(m_prev - m_new)
        p_bufs[slot][...] = jnp.exp2((s_t - m_new).astype(BF16))

    def stage_a(c, slot):
        off = pl.multiple_of(c * kc, kc)
        acc_sc[...] = a_bufs[slot][...] * acc_sc[...] + jnp.dot(
            vt_ref[0, :, pl.ds(off, kc)], p_bufs[slot][...], preferred_element_type=F32)

    def tick(t, parity):
        stage_s(t, parity)
        stage_f(1 - parity)
        stage_a(t - 2, parity)

    n = n_chunks
    stage_s(0, 0)
    if n > 1:
        stage_s(1, 1)
    stage_f(0)
    first = 2
    if n > 2 and (n - 2) % 2 == 1:
        tick(2, 0)
        first = 3
    n_pairs = (n - first) // 2 if n > first else 0
    if n_pairs > 0:
        def pair(u, carry):
            t = first + 2 * u
            tick(t, first % 2)
            tick(t + 1, 1 - first % 2)
            return carry
        lax.fori_loop(0, n_pairs, pair, 0)
    if n > 1:
        stage_f((n - 1) % 2)
        stage_a(n - 2, (n - 2) % 2)
    stage_a(n - 1, (n - 1) % 2)

    acc = acc_sc[...]
    o = acc[:ATTN_HEAD_DIM] / acc[ATTN_HEAD_DIM:ATTN_HEAD_DIM + 1]
    for g in range(ATTN_GROUP):
        o_ref[:, g * ATTN_HEAD_DIM:(g + 1) * ATTN_HEAD_DIM] = o[:, g * tq:(g + 1) * tq].T.astype(o_ref.dtype)


def _attention(proj, gain, cos, sin, k_all, vt_all, batch, seq):
    lk = k_all.shape[1]
    kc = _largest_tile(lk, ATTN_KEY_CHUNK, MXU_DEPTH)
    tq = _largest_tile(seq, ATTN_Q_TILE, 8)
    nq = seq // tq
    qw = ATTN_GROUP * ATTN_HEAD_DIM
    cols = ATTN_GROUP * tq
    vrows = ATTN_HEAD_DIM + ONES_ROWS
    return pl.pallas_call(
        functools.partial(_attn_kernel, tq=tq, kc=kc, n_chunks=lk // kc),
        grid=(batch, ATTN_KV_HEADS, nq),
        in_specs=[pl.BlockSpec((tq, qw), lambda b, g, i: (b * nq + i, COL_QA // qw + g)),
                  pl.BlockSpec((1, ATTN_HEAD_DIM), lambda b, g, i: (0, 0)),
                  pl.BlockSpec((tq, ATTN_HEAD_DIM), lambda b, g, i: (i, 0)),
                  pl.BlockSpec((tq, ATTN_HEAD_DIM), lambda b, g, i: (i, 0)),
                  pl.BlockSpec((1, lk, ATTN_HEAD_DIM), lambda b, g, i: (b, 0, g)),
                  pl.BlockSpec((1, vrows, lk), lambda b, g, i: (b, g, 0))],
        out_specs=pl.BlockSpec((tq, qw), lambda b, g, i: (b * nq + i, g)),
        out_shape=jax.ShapeDtypeStruct((batch * seq, ATTN_Q_W), BF16),
        scratch_shapes=[pltpu.VMEM((ATTN_HEAD_DIM, cols), BF16),
                        pltpu.VMEM((1, cols), F32),
                        pltpu.VMEM((vrows, cols), F32),
                        pltpu.VMEM((kc, cols), F32),
                        pltpu.VMEM((kc, cols), F32),
                        pltpu.VMEM((kc, cols), BF16),
                        pltpu.VMEM((kc, cols), BF16),
                        pltpu.VMEM((1, cols), F32),
                        pltpu.VMEM((1, cols), F32)],
        compiler_params=_params(3),
        name="attn",
    )(proj, gain, cos, sin, k_all, vt_all)


def _merge_kernel(x_ref, g1_ref, of_ref, ob_ref, gr_ref, ao_ref, gtr_ref, gta_ref,
                  wro_ref, wao_ref, wo_ref, o_ref):
    ro = of_ref[...].astype(F32) + ob_ref[...].astype(F32)
    gr = gr_ref[...].astype(F32)
    parts = []
    for h in range(RET_HEADS):
        sl = slice(h * RET_V_DIM, (h + 1) * RET_V_DIM)
        g = gr[:, sl]
        parts.append((g * jax.nn.sigmoid(g) * _rms(ro[:, sl])).astype(BF16))
    ret_in = jnp.concatenate(parts, axis=1)
    ret_branch = jnp.dot(ret_in, wro_ref[...], preferred_element_type=F32)
    attn_branch = jnp.dot(ao_ref[...], wao_ref[...], preferred_element_type=F32)
    y = (jax.nn.sigmoid(gtr_ref[...].astype(F32)) * ret_branch
         + jax.nn.sigmoid(gta_ref[...].astype(F32)) * attn_branch)
    y = jnp.dot(y.astype(BF16), wo_ref[...], preferred_element_type=F32)
    o_ref[...] = x_ref[...] + g1_ref[0] * y


def _merge(x2, g1, o_f, o_b, proj, attn_o, w_ret_o, w_attn_o, w_out, seq):
    t = x2.shape[0]
    tm = _largest_tile(seq, 256, 8)
    per_b = seq // tm
    full = lambda i: (0, 0)
    return pl.pallas_call(
        _merge_kernel,
        grid=(t // tm,),
        in_specs=[pl.BlockSpec((tm, D_MODEL), lambda i: (i, 0)),
                  pl.BlockSpec((1, 1, D_MODEL), lambda i: (i // per_b, 0, 0)),
                  pl.BlockSpec((tm, RET_V_W), lambda i: (i, 0)),
                  pl.BlockSpec((tm, RET_V_W), lambda i: (i, 0)),
                  pl.BlockSpec((tm, RET_V_W), lambda i: (i, COL_GR // RET_V_W)),
                  pl.BlockSpec((tm, ATTN_Q_W), lambda i: (i, 0)),
                  pl.BlockSpec((tm, D_MODEL), lambda i: (i, COL_GATE_R // D_MODEL)),
                  pl.BlockSpec((tm, D_MODEL), lambda i: (i, COL_GATE_A // D_MODEL)),
                  pl.BlockSpec((RET_V_W, D_MODEL), full),
                  pl.BlockSpec((ATTN_Q_W, D_MODEL), full),
                  pl.BlockSpec((D_MODEL, D_MODEL), full)],
        out_specs=pl.BlockSpec((tm, D_MODEL), lambda i: (i, 0)),
        out_shape=jax.ShapeDtypeStruct((t, D_MODEL), F32),
        compiler_params=_params(1),
        name="merge",
    )(x2, g1, o_f, o_b, proj, attn_o, proj, proj, w_ret_o, w_attn_o, w_out)


def _router_kernel(x_ref, n_ref, sc_ref, sh_ref, wr_ref, br_ref, h_ref, ti_ref, tw_ref):
    h = _rms(x_ref[...]) * n_ref[...] * (1.0 + sc_ref[0]) + sh_ref[0]
    h_ref[...] = _pack_pairs(h)
    logits = lax.dot_general(wr_ref[...], h, (((1,), (1,)), ((), ())), preferred_element_type=F32,
                             precision=lax.Precision.HIGHEST) + br_ref[...]
    eid = lax.broadcasted_iota(jnp.int32, logits.shape, 0)
    vals = logits
    top_v = []
    top_i = []
    for _ in range(TOP_K):
        m = jnp.max(vals, axis=0, keepdims=True)
        idx = jnp.min(jnp.where(vals == m, eid, N_EXPERTS), axis=0, keepdims=True)
        top_v.append(m)
        top_i.append(idx)
        vals = jnp.where(eid == idx, -jnp.inf, vals)
    ex = [jnp.exp(v - top_v[0]) for v in top_v]
    denom = ex[0] + ex[1] + ex[2] + ex[3]
    ti_ref[...] = jnp.concatenate(top_i, axis=0)
    tw_ref[...] = jnp.concatenate([e / denom for e in ex], axis=0)


def _router(x1, norm, sc, sh, w_router_t, b_router, seq):
    t = x1.shape[0]
    tm = _largest_tile(seq, 1024, LANES)
    per_b = seq // tm
    mod_map = lambda i: (i // per_b, 0, 0)
    return pl.pallas_call(
        _router_kernel,
        grid=(t // tm,),
        in_specs=[pl.BlockSpec((tm, D_MODEL), lambda i: (i, 0)),
                  pl.BlockSpec((1, D_MODEL), lambda i: (0, 0)),
                  pl.BlockSpec((1, 1, D_MODEL), mod_map),
                  pl.BlockSpec((1, 1, D_MODEL), mod_map),
                  pl.BlockSpec((N_EXPERTS, D_MODEL), lambda i: (0, 0)),
                  pl.BlockSpec((N_EXPERTS, 1), lambda i: (0, 0))],
        out_specs=[pl.BlockSpec((tm, PACK_W), lambda i: (i, 0)),
                   pl.BlockSpec((TOP_K, tm), lambda i: (0, i)),
                   pl.BlockSpec((TOP_K, tm), lambda i: (0, i))],
        out_shape=[jax.ShapeDtypeStruct((t, PACK_W), PACK_DTYPE),
                   jax.ShapeDtypeStruct((TOP_K, t), jnp.int32),
                   jax.ShapeDtypeStruct((TOP_K, t), F32)],
        compiler_params=_params(1),
        name="router",
    )(x1, norm, sc, sh, w_router_t, b_router)


def _gather_rows(src, idx):
    n, w = src.shape
    r = idx.shape[0]
    workers = SC_CORES * SC_SUBCORES
    per_w = r // workers
    n_win = per_w // SC_GATHER_ROWS
    assert per_w * workers == r and n_win * SC_GATHER_ROWS == per_w, (r, workers, SC_GATHER_ROWS)
    mesh = plsc.VectorSubcoreMesh(core_axis_name="c", subcore_axis_name="s")

    @functools.partial(
        pl.kernel, mesh=mesh, out_type=jax.ShapeDtypeStruct((r, w), src.dtype),
        scratch_types=[pltpu.VMEM((SC_GATHER_ROWS,), jnp.int32),
                       pltpu.VMEM((SC_GATHER_ROWS, w), src.dtype),
                       pltpu.SemaphoreType.DMA])
    def gather(src_hbm, idx_hbm, out_hbm, idx_v, rows_v, sem):
        wid = lax.axis_index("s") * SC_CORES + lax.axis_index("c")
        base = wid * per_w

        @pl.loop(0, n_win)
        def _(win):
            off = base + win * SC_GATHER_ROWS
            pltpu.sync_copy(idx_hbm.at[pl.ds(off, SC_GATHER_ROWS)], idx_v)
            pltpu.async_copy(src_hbm.at[idx_v], rows_v, sem).wait()
            pltpu.sync_copy(rows_v, out_hbm.at[pl.ds(off, SC_GATHER_ROWS)])

    return gather(src, idx)


def _ffn_kernel(te_ref, tf_ref, x_ref, w1_ref, b1_ref, w2_ref, b2_ref, o_ref, w1_sc, w2_sc):
    flag = tf_ref[pl.program_id(0)]

    @pl.when(flag == TILE_FIRST)
    def _():
        w1_sc[...] = w1_ref[0].astype(BF16)
        w2_sc[...] = w2_ref[0].astype(BF16)

    @pl.when(flag == TILE_PAD)
    def _():
        o_ref[...] = jnp.zeros(o_ref.shape, o_ref.dtype)

    @pl.when(flag != TILE_PAD)
    def _():
        lo, hi = _unpack_pairs(x_ref[...])
        x = jnp.concatenate([lo, hi], axis=1).astype(BF16)
        a = jnp.dot(x, w1_sc[...], preferred_element_type=F32) + b1_ref[0]
        gate = jnp.minimum(a[:, :EXPERT_FF], SWIGLU_LIMIT)
        up = jnp.clip(a[:, EXPERT_FF:], -SWIGLU_LIMIT, SWIGLU_LIMIT)
        act = gate * jax.nn.sigmoid(SWIGLU_ALPHA * gate) * (up + 1.0)
        y = jnp.dot(act.astype(BF16), w2_sc[...], preferred_element_type=F32) + b2_ref[0]
        o_ref[...] = _pack_pairs(y)


def _ffn(xs, tile_expert, tile_flag, w1, b1, w2, b2, tg):
    p = xs.shape[0]
    grid_spec = pltpu.PrefetchScalarGridSpec(
        num_scalar_prefetch=2,
        grid=(p // tg,),
        in_specs=[pl.BlockSpec((tg, PACK_W), lambda j, te, tv: (j, 0)),
                  pl.BlockSpec((1, D_MODEL, 2 * EXPERT_FF), lambda j, te, tv: (te[j], 0, 0)),
                  pl.BlockSpec((1, 1, 2 * EXPERT_FF), lambda j, te, tv: (te[j], 0, 0)),
                  pl.BlockSpec((1, EXPERT_FF, D_MODEL), lambda j, te, tv: (te[j], 0, 0)),
                  pl.BlockSpec((1, 1, D_MODEL), lambda j, te, tv: (te[j], 0, 0))],
        out_specs=pl.BlockSpec((tg, PACK_W), lambda j, te, tv: (j, 0)),
        scratch_shapes=[pltpu.VMEM((D_MODEL, 2 * EXPERT_FF), BF16),
                        pltpu.VMEM((EXPERT_FF, D_MODEL), BF16)],
    )
    return pl.pallas_call(
        _ffn_kernel,
        grid_spec=grid_spec,
        out_shape=jax.ShapeDtypeStruct((p, PACK_W), PACK_DTYPE),
        compiler_params=_params(1),
        name="ffn",
    )(tile_expert, tile_flag, xs, w1, b1, w2, b2)


def _combine_kernel(x_ref, g2_ref, w_ref, y_ref, o_ref):
    w = w_ref[...]
    acc_lo = None
    for k in range(TOP_K):
        lo, hi = _unpack_pairs(y_ref[k])
        wk = w[:, k:k + 1]
        acc_lo = wk * lo if acc_lo is None else acc_lo + wk * lo
        acc_hi = wk * hi if k == 0 else acc_hi + wk * hi
    acc = jnp.concatenate([acc_lo, acc_hi], axis=1)
    o_ref[...] = x_ref[...] + g2_ref[0] * acc


def _combine(x1, g2, w_tok, yk, seq):
    t = x1.shape[0]
    tm = _largest_tile(seq, 512, 8)
    per_b = seq // tm
    return pl.pallas_call(
        _combine_kernel,
        grid=(t // tm,),
        in_specs=[pl.BlockSpec((tm, D_MODEL), lambda i: (i, 0)),
                  pl.BlockSpec((1, 1, D_MODEL), lambda i: (i // per_b, 0, 0)),
                  pl.BlockSpec((tm, TOP_K), lambda i: (i, 0)),
                  pl.BlockSpec((TOP_K, tm, PACK_W), lambda i: (0, i, 0))],
        out_specs=pl.BlockSpec((tm, D_MODEL), lambda i: (i, 0)),
        out_shape=jax.ShapeDtypeStruct((t, D_MODEL), F32),
        compiler_params=_params(1),
        name="combine",
    )(x1, g2, w_tok, yk)


def _plan(top_i, tg):
    t = top_i.shape[1]
    n_assign = TOP_K * t
    p = n_assign + N_EXPERTS * tg
    n_tiles = p // tg
    flat_e = top_i.reshape(-1)
    sorted_e, sorted_a = lax.sort_key_val(flat_e, jnp.arange(n_assign, dtype=jnp.int32))
    experts = jnp.arange(N_EXPERTS + 1, dtype=jnp.int32)
    bounds = jnp.sum((sorted_e[None, :] < experts[:, None]).astype(jnp.int32), axis=1)
    start = bounds[:-1]
    counts = bounds[1:] - start
    padded = ((counts + tg - 1) // tg) * tg
    off_end = jnp.cumsum(padded)
    off = off_end - padded
    dest_sorted = off[sorted_e] + jnp.arange(n_assign, dtype=jnp.int32) - start[sorted_e]
    _, dest = lax.sort_key_val(sorted_a, dest_sorted)
    tile_start = jnp.arange(n_tiles, dtype=jnp.int32) * tg
    tile_valid = (tile_start < off_end[-1]).astype(jnp.int32)
    te = jnp.sum((tile_start[:, None] >= off_end[None, :]).astype(jnp.int32), axis=1)
    last_e = jnp.sum(((off_end[-1] - 1) >= off_end).astype(jnp.int32))
    tile_expert = jnp.where(tile_valid != 0, te, last_e)
    slot = jnp.arange(p, dtype=jnp.int32)
    e_slot = jnp.repeat(tile_expert, tg)
    r = slot - off[e_slot]
    ok = (r < counts[e_slot]) & (jnp.repeat(tile_valid, tg) != 0)
    a_slot = sorted_a[jnp.clip(start[e_slot] + r, 0, n_assign - 1)]
    src_tok = jnp.where(ok, a_slot % t, 0).astype(jnp.int32)
    changed = jnp.concatenate([jnp.ones((1,), jnp.bool_), tile_expert[1:] != tile_expert[:-1]])
    tile_flag = jnp.where(tile_valid != 0, jnp.where(changed, TILE_FIRST, TILE_BODY), TILE_PAD).astype(jnp.int32)
    return src_tok, dest, tile_expert, tile_flag


def kernel(x, c, ctx, c_ctx, norm1, norm2, w_ada, b_ada, w_in, ret_decay_f, ret_decay_b, attn_q_norm, attn_k_norm,
           w_ret_o, w_attn_o, w_out, w_router, b_router, w_exp_in, b_exp_in, w_exp_out, b_exp_out):
    assert w_in.shape[0] == 1, "single-layer block"
    b, seq, d = x.shape
    n_ctx = ctx.shape[1]
    t = b * seq
    rows = seq // GRID_W

    idx = np.cumsum(IN_SIZES)[:-1].tolist()
    wq_r, wk_r, wv_r, wg_r, wq_a, wk_a, wv_a, wgt_r, wgt_a = jnp.split(w_in[0], idx, axis=-1)
    w_in_p = jnp.concatenate([wq_r, wk_r, wv_r, wg_r, wq_a, wgt_r, wgt_a, wk_a, wv_a], axis=-1)
    w1 = w_exp_in[0]
    w2 = w_exp_out[0]
    b1 = b_exp_in[0].reshape(N_EXPERTS, 1, 2 * EXPERT_FF)
    b2 = b_exp_out[0].reshape(N_EXPERTS, 1, D_MODEL)

    pad = (-(b + 1)) % 8
    c_all = jnp.concatenate([c, c_ctx[None, :], jnp.zeros((pad, d), F32)], axis=0)
    mod = _ada(c_all, w_ada[0], b_ada[0])
    sh1, sc1, g1, sh2, sc2, g2 = [m.reshape(-1, 1, d) for m in jnp.split(mod, 6, axis=-1)]
    lat = lambda m: m[:b]
    cx = lambda m: m[b:b + 1]

    x2 = x.reshape(t, d)
    proj = _inproj(x2, norm1, lat(sc1), lat(sh1), w_in_p, seq)
    proj_c = _inproj(ctx.reshape(b * n_ctx, d), norm1, cx(sc1), cx(sh1), w_in_p, n_ctx)

    cos_r, sin_r = _rope_tables(rows, RET_QK_DIM)
    cos_a, sin_a = _rope_tables(rows, ATTN_HEAD_DIM)
    lg_f = -jax.nn.softplus(ret_decay_f[0].astype(F32))
    lg_b = -jax.nn.softplus(ret_decay_b[0].astype(F32))
    zero_state = jnp.zeros((2, b, RET_HEADS, RET_QK_DIM, RET_V_DIM), F32)
    cos_c = jnp.ones((n_ctx, RET_QK_DIM), F32)
    sin_c = jnp.zeros((n_ctx, RET_QK_DIM), F32)
    _, _, s_ctx = _retention(proj_c, cos_c, sin_c, lg_f, lg_b, zero_state, b, n_ctx, False)
    o_f, o_b, _ = _retention(proj, cos_r, sin_r, lg_f, lg_b, s_ctx, b, seq, True)

    k_lat = _kprep(proj, attn_k_norm, cos_a, sin_a, seq, True)
    k_ctx = _kprep(proj_c, attn_k_norm, cos_a, sin_a, n_ctx, False)
    k_all = jnp.concatenate([k_ctx.reshape(b, n_ctx, ATTN_KV_W), k_lat.reshape(b, seq, ATTN_KV_W)], axis=1)
    v_all = jnp.concatenate([proj_c[:, COL_VA:COL_VA + ATTN_KV_W].reshape(b, n_ctx, ATTN_KV_W),
                             proj[:, COL_VA:COL_VA + ATTN_KV_W].reshape(b, seq, ATTN_KV_W)], axis=1)
    lk = n_ctx + seq
    vt = jnp.swapaxes(v_all.reshape(b, lk, ATTN_KV_HEADS, ATTN_HEAD_DIM), 1, 3).swapaxes(1, 2)
    vt_all = jnp.concatenate([vt, jnp.ones((b, ATTN_KV_HEADS, ONES_ROWS, lk), vt.dtype)], axis=2)
    vt_all = vt_all.reshape(b, ATTN_KV_HEADS * (ATTN_HEAD_DIM + ONES_ROWS), lk)
    attn_o = _attention(proj, attn_q_norm, cos_a, sin_a, k_all, vt_all, b, seq)

    x1 = _merge(x2, lat(g1), o_f, o_b, proj, attn_o, w_ret_o[0].astype(BF16), w_attn_o[0].astype(BF16),
                w_out[0].astype(BF16), seq)

    h2, top_i, top_w = _router(x1, norm2, lat(sc2), lat(sh2), w_router[0].T, b_router[0].reshape(N_EXPERTS, 1), seq)
    tg = _largest_tile(TOP_K * t, 512, 8)
    src_tok, dest, tile_expert, tile_flag = _plan(top_i, tg)
    xs = _gather_rows(h2, src_tok)
    ys = _ffn(xs, tile_expert, tile_flag, w1, b1, w2, b2, tg)
    yk = _gather_rows(ys, dest).reshape(TOP_K, t, PACK_W)
    out = _combine(x1, lat(g2), top_w.T, yk, seq)
    return out.reshape(b, seq, d)
```

```python
import functools

import jax
import jax.numpy as jnp
import numpy as np
from jax import lax
from jax.experimental import pallas as pl
from jax.experimental.pallas import tpu as pltpu
from jax.experimental.pallas import tpu_sc as plsc

F32 = jnp.float32
BF16 = jnp.bfloat16

D_MODEL = 1024
GRID_W = 64
EPS = 1e-6
RET_HEADS = 4
RET_QK_DIM = 256
RET_V_DIM = 512
ATTN_HEADS = 8
ATTN_KV_HEADS = 2
ATTN_GROUP = ATTN_HEADS // ATTN_KV_HEADS
ATTN_HEAD_DIM = 128
ROPE_THETA = 10000.0
N_EXPERTS = 32
TOP_K = 4
EXPERT_FF = 1024
SWIGLU_LIMIT = 7.0
SWIGLU_ALPHA = 1.702

RET_QK_W = RET_HEADS * RET_QK_DIM
RET_V_W = RET_HEADS * RET_V_DIM
ATTN_Q_W = ATTN_HEADS * ATTN_HEAD_DIM
ATTN_KV_W = ATTN_KV_HEADS * ATTN_HEAD_DIM
IN_SIZES = (RET_QK_W, RET_QK_W, RET_V_W, RET_V_W, ATTN_Q_W, ATTN_KV_W, ATTN_KV_W, D_MODEL, D_MODEL)
IN_WIDTH = sum(IN_SIZES)
COL_QR = 0
COL_KR = COL_QR + RET_QK_W
COL_VR = COL_KR + RET_QK_W
COL_GR = COL_VR + RET_V_W
COL_QA = COL_GR + RET_V_W
COL_GATE_R = COL_QA + ATTN_Q_W
COL_GATE_A = COL_GATE_R + D_MODEL
COL_KA = COL_GATE_A + D_MODEL
COL_VA = COL_KA + ATTN_KV_W

RET_CHUNK = 256
RET_HEADS_PER_STEP = 2
LANES = 128
SC_CORES = 2
SC_SUBCORES = 16
SC_GATHER_ROWS = 128
MXU_DEPTH = 256
ATTN_KEY_CHUNK = 768
ATTN_Q_TILE = 512
ONES_ROWS = 16
PACK_DTYPE = jnp.int32
PACK_W = D_MODEL // 2
MOE_GROUPS = 2
TILE_PAD, TILE_BODY, TILE_FIRST = 0, 1, 2
VMEM_LIMIT = 56 * 1024 * 1024

ARB = pltpu.ARBITRARY


def _params(n_axes, **kw):
    return pltpu.CompilerParams(dimension_semantics=(ARB,) * n_axes, vmem_limit_bytes=VMEM_LIMIT, **kw)


def _largest_tile(n, cap, mult):
    best = None
    for t in range(mult, min(n, cap) + 1, mult):
        if n % t == 0:
            best = t
    assert best is not None, (n, cap, mult)
    return best


def _rms(x):
    return x * lax.rsqrt(jnp.mean(x * x, axis=-1, keepdims=True) + EPS)


def _pack_pairs(x):
    half = x.shape[1] // 2
    lo = lax.bitcast_convert_type(x[:, :half].astype(BF16).astype(F32), jnp.int32)
    hi = lax.bitcast_convert_type(x[:, half:].astype(BF16).astype(F32), jnp.int32)
    return lax.bitwise_or(lax.bitwise_and(hi, jnp.int32(-65536)), lax.shift_right_logical(lo, jnp.int32(16)))


def _unpack_pairs(w):
    lo = lax.bitcast_convert_type(lax.shift_left(w, jnp.int32(16)), F32)
    hi = lax.bitcast_convert_type(lax.bitwise_and(w, jnp.int32(-65536)), F32)
    return lo, hi


def _rope_block(x, cos, sin, half):
    if 2 * half == LANES:
        swapped = pltpu.roll(x, half, 1)
    else:
        lane = lax.broadcasted_iota(jnp.int32, x.shape, 1)
        first = (lane % (2 * half)) < half
        swapped = jnp.where(first, pltpu.roll(x, LANES - half, 1), pltpu.roll(x, half, 1))
    return x * cos + swapped * sin


def _rope_tables(rows, head_dim):
    row_id = jnp.broadcast_to(jnp.arange(rows, dtype=F32)[:, None], (rows, GRID_W)).reshape(-1)
    col_id = jnp.broadcast_to(jnp.arange(GRID_W, dtype=F32)[None, :], (rows, GRID_W)).reshape(-1)
    n_freq = head_dim // 4
    inv_freq = ROPE_THETA ** (-jnp.arange(n_freq, dtype=F32) / n_freq)
    ang_r = row_id[:, None] * inv_freq
    ang_c = col_id[:, None] * inv_freq
    cos = jnp.concatenate([jnp.cos(ang_r), jnp.cos(ang_r), jnp.cos(ang_c), jnp.cos(ang_c)], axis=1)
    sin = jnp.concatenate([-jnp.sin(ang_r), jnp.sin(ang_r), -jnp.sin(ang_c), jnp.sin(ang_c)], axis=1)
    return cos, sin


def _ada_kernel(c_ref, w_ref, b_ref, o_ref):
    c = c_ref[...]
    s = c * jax.nn.sigmoid(c)
    o_ref[...] = jnp.dot(s, w_ref[...], preferred_element_type=F32,
                         precision=lax.Precision.HIGHEST) + b_ref[...]


def _ada(c_pad, w_ada, b_ada):
    rows = c_pad.shape[0]
    n = w_ada.shape[1]
    tn = _largest_tile(n, 1536, LANES)
    return pl.pallas_call(
        _ada_kernel,
        grid=(n // tn,),
        in_specs=[pl.BlockSpec((rows, D_MODEL), lambda j: (0, 0)),
                  pl.BlockSpec((D_MODEL, tn), lambda j: (0, j)),
                  pl.BlockSpec((1, tn), lambda j: (0, j))],
        out_specs=pl.BlockSpec((rows, tn), lambda j: (0, j)),
        out_shape=jax.ShapeDtypeStruct((rows, n), F32),
        compiler_params=_params(1),
        name="ada",
    )(c_pad, w_ada, b_ada.reshape(1, n))


def _inproj_kernel(x_ref, n_ref, sc_ref, sh_ref, w_ref, o_ref, w_sc):
    @pl.when(pl.program_id(1) == 0)
    def _():
        w_sc[...] = w_ref[...].astype(BF16)

    y = _rms(x_ref[...]) * n_ref[...]
    h = (y * (1.0 + sc_ref[0]) + sh_ref[0]).astype(BF16)
    o_ref[...] = jnp.dot(h, w_sc[...], preferred_element_type=F32).astype(o_ref.dtype)


def _inproj(x2, norm, sc, sh, w, rows_per_batch):
    t = x2.shape[0]
    n = w.shape[1]
    tm = _largest_tile(rows_per_batch, 512, 8)
    tn = _largest_tile(n, 2560, LANES)
    per_b = rows_per_batch // tm
    if sc.shape[0] == 1:
        mod_map = lambda j, i: (0, 0, 0)
    else:
        mod_map = lambda j, i: (i // per_b, 0, 0)
    return pl.pallas_call(
        _inproj_kernel,
        grid=(n // tn, t // tm),
        in_specs=[pl.BlockSpec((tm, D_MODEL), lambda j, i: (i, 0)),
                  pl.BlockSpec((1, D_MODEL), lambda j, i: (0, 0)),
                  pl.BlockSpec((1, 1, D_MODEL), mod_map),
                  pl.BlockSpec((1, 1, D_MODEL), mod_map),
                  pl.BlockSpec((D_MODEL, tn), lambda j, i: (0, j))],
        out_specs=pl.BlockSpec((tm, tn), lambda j, i: (i, j)),
        out_shape=jax.ShapeDtypeStruct((t, n), BF16),
        scratch_shapes=[pltpu.VMEM((D_MODEL, tn), BF16)],
        compiler_params=_params(2),
        name="inproj",
    )(x2, norm, sc, sh, w)


def _kprep_kernel(k_ref, g_ref, cos_ref, sin_ref, o_ref, *, use_rope):
    k = k_ref[...].astype(F32)
    for g in range(ATTN_KV_HEADS):
        sl = slice(g * ATTN_HEAD_DIM, (g + 1) * ATTN_HEAD_DIM)
        kh = _rms(k[:, sl]) * g_ref[...]
        if use_rope:
            kh = _rope_block(kh, cos_ref[...], sin_ref[...], ATTN_HEAD_DIM // 4)
        o_ref[:, sl] = kh.astype(BF16)


def _kprep(proj, gain, cos, sin, rows_per_batch, use_rope):
    t = proj.shape[0]
    tm = _largest_tile(rows_per_batch, 1024, 8)
    per_b = rows_per_batch // tm
    return pl.pallas_call(
        functools.partial(_kprep_kernel, use_rope=use_rope),
        grid=(t // tm,),
        in_specs=[pl.BlockSpec((tm, ATTN_KV_W), lambda i: (i, COL_KA // ATTN_KV_W)),
                  pl.BlockSpec((1, ATTN_HEAD_DIM), lambda i: (0, 0)),
                  pl.BlockSpec((tm, ATTN_HEAD_DIM), lambda i: (i % per_b, 0)),
                  pl.BlockSpec((tm, ATTN_HEAD_DIM), lambda i: (i % per_b, 0))],
        out_specs=pl.BlockSpec((tm, ATTN_KV_W), lambda i: (i, 0)),
        out_shape=jax.ShapeDtypeStruct((t, ATTN_KV_W), BF16),
        compiler_params=_params(1),
        name="kprep",
    )(proj, gain, cos, sin)


def _ret_kernel(dch_ref, qf_ref, kf_ref, vf_ref, cosf_ref, sinf_ref, qb_ref, kb_ref, vb_ref, cosb_ref, sinb_ref,
                dmat_ref, din_ref, dout_ref, s0_ref, of_ref, ob_ref, sfin_ref, s_sc, *, use_rope, n_chunks, heads):
    hb = pl.program_id(1)
    c = pl.program_id(2)

    @pl.when(c == 0)
    def _():
        s_sc[...] = s0_ref[:, 0]

    def rope(x, cos, sin):
        return jnp.concatenate([_rope_block(x[:, i * LANES:(i + 1) * LANES], cos[:, i * LANES:(i + 1) * LANES],
                                            sin[:, i * LANES:(i + 1) * LANES], RET_QK_DIM // 4)
                                for i in range(RET_QK_DIM // LANES)], axis=1)

    sides = ((qf_ref, kf_ref, vf_ref, cosf_ref, sinf_ref), (qb_ref, kb_ref, vb_ref, cosb_ref, sinb_ref))
    chains = [(d, j) for d in range(2) for j in range(heads)]
    qs, ks, vs, ss = {}, {}, {}, {}
    for d, j in chains:
        q_ref, k_ref, v_ref, cos_ref, sin_ref = sides[d]
        q = q_ref[:, j * RET_QK_DIM:(j + 1) * RET_QK_DIM].astype(F32)
        k = k_ref[:, j * RET_QK_DIM:(j + 1) * RET_QK_DIM].astype(F32)
        if use_rope:
            q = rope(q, cos_ref[...], sin_ref[...])
            k = rope(k, cos_ref[...], sin_ref[...])
        qs[d, j], ks[d, j] = q, k
        vs[d, j] = v_ref[:, j * RET_V_DIM:(j + 1) * RET_V_DIM]
        ss[d, j] = s_sc[d, j]
    scores = {ch: lax.dot_general(qs[ch].astype(BF16), ks[ch].astype(BF16), (((1,), (1,)), ((), ())),
                                  preferred_element_type=F32) * dmat_ref[ch[0], ch[1]] for ch in chains}
    inter = {ch: jnp.dot((qs[ch] * din_ref[ch[0], ch[1]]).astype(BF16), ss[ch].astype(BF16),
                         preferred_element_type=F32) for ch in chains}
    intra = {ch: jnp.dot(scores[ch].astype(BF16), vs[ch], preferred_element_type=F32) for ch in chains}
    o_refs = (of_ref, ob_ref)
    for d, j in chains:
        o_refs[d][:, j * RET_V_DIM:(j + 1) * RET_V_DIM] = (intra[d, j] + inter[d, j]).astype(of_ref.dtype)
    s_new = {ch: ss[ch] * dch_ref[ch[0] * RET_HEADS + hb * heads + ch[1]]
             + jnp.dot((ks[ch] * dout_ref[ch[0], ch[1]]).T.astype(BF16), vs[ch], preferred_element_type=F32)
             for ch in chains}
    for d, j in chains:
        s_sc[d, j] = s_new[d, j]

    @pl.when(c == n_chunks - 1)
    def _():
        for d, j in chains:
            sfin_ref[d, 0, j] = s_new[d, j]


def _ret_tables(log_gamma, chunk, reverse, k_scale):
    pos = jnp.arange(chunk, dtype=F32)
    diff = pos[:, None] - pos[None, :]
    if reverse:
        diff = -diff
        mask = diff > 0
        p_in = chunk - pos
        p_out = pos
    else:
        mask = diff >= 0
        p_in = pos + 1.0
        p_out = chunk - 1.0 - pos
    lg = log_gamma.astype(F32)
    dmat = jnp.where(mask[None], jnp.exp(lg[:, None, None] * jnp.maximum(diff, 0.0)[None]), 0.0) * k_scale
    d_in = jnp.exp(lg[:, None] * p_in)
    d_out = jnp.exp(lg[:, None] * p_out) * k_scale
    d_in = jnp.broadcast_to(d_in[:, :, None], (RET_HEADS, chunk, RET_QK_DIM))
    d_out = jnp.broadcast_to(d_out[:, :, None], (RET_HEADS, chunk, RET_QK_DIM))
    d_chunk = jnp.exp(lg * chunk)
    return d_chunk, dmat, d_in, d_out


def _retention(proj, cos, sin, lg_f, lg_b, s0, batch, seq, use_rope):
    chunk = min(RET_CHUNK, seq)
    nc = seq // chunk
    hs = RET_HEADS_PER_STEP
    qw, vw = hs * RET_QK_DIM, hs * RET_V_DIM
    k_scale = RET_QK_DIM ** -0.5
    tabs = [_ret_tables(lg_f, chunk, False, k_scale), _ret_tables(lg_b, chunk, True, k_scale)]
    d_chunk, dmat, d_in, d_out = [jnp.stack([tabs[0][i], tabs[1][i]]) for i in range(4)]
    fwd = lambda c: c
    bwd = lambda c: nc - 1 - c

    def side(cidx):
        row = lambda b, c: b * nc + cidx(c)
        return [pl.BlockSpec((chunk, qw), lambda b, h, c: (row(b, c), COL_QR // qw + h)),
                pl.BlockSpec((chunk, qw), lambda b, h, c: (row(b, c), COL_KR // qw + h)),
                pl.BlockSpec((chunk, vw), lambda b, h, c: (row(b, c), COL_VR // vw + h)),
                pl.BlockSpec((chunk, RET_QK_DIM), lambda b, h, c: (cidx(c), 0)),
                pl.BlockSpec((chunk, RET_QK_DIM), lambda b, h, c: (cidx(c), 0))]

    state_spec = pl.BlockSpec((2, 1, hs, RET_QK_DIM, RET_V_DIM), lambda b, h, c: (0, b, h, 0, 0))
    return pl.pallas_call(
        functools.partial(_ret_kernel, use_rope=use_rope, n_chunks=nc, heads=hs),
        grid=(batch, RET_HEADS // hs, nc),
        in_specs=[pl.BlockSpec(memory_space=pltpu.SMEM)] + side(fwd) + side(bwd) + [
            pl.BlockSpec((2, hs, chunk, chunk), lambda b, h, c: (0, h, 0, 0)),
            pl.BlockSpec((2, hs, chunk, RET_QK_DIM), lambda b, h, c: (0, h, 0, 0)),
            pl.BlockSpec((2, hs, chunk, RET_QK_DIM), lambda b, h, c: (0, h, 0, 0)),
            state_spec],
        out_specs=[pl.BlockSpec((chunk, vw), lambda b, h, c: (b * nc + c, h)),
                   pl.BlockSpec((chunk, vw), lambda b, h, c: (b * nc + nc - 1 - c, h)),
                   state_spec],
        out_shape=[jax.ShapeDtypeStruct((batch * seq, RET_V_W), BF16),
                   jax.ShapeDtypeStruct((batch * seq, RET_V_W), BF16),
                   jax.ShapeDtypeStruct((2, batch, RET_HEADS, RET_QK_DIM, RET_V_DIM), F32)],
        scratch_shapes=[pltpu.VMEM((2, hs, RET_QK_DIM, RET_V_DIM), F32)],
        compiler_params=_params(3),
        name="ret",
    )(d_chunk.reshape(-1), proj, proj, proj, cos, sin, proj, proj, proj, cos, sin, dmat, d_in, d_out, s0)


def _attn_kernel(q_ref, g_ref, cos_ref, sin_ref, k_ref, vt_ref, o_ref, qt_sc, m_sc, acc_sc,
                 s0_sc, s1_sc, p0_sc, p1_sc, a0_sc, a1_sc, *, tq, kc, n_chunks):
    s_bufs = (s0_sc, s1_sc)
    p_bufs = (p0_sc, p1_sc)
    a_bufs = (a0_sc, a1_sc)

    q = q_ref[...].astype(F32)
    scale = ATTN_HEAD_DIM ** -0.5 * np.log2(np.e)
    for g in range(ATTN_GROUP):
        qh = _rms(q[:, g * ATTN_HEAD_DIM:(g + 1) * ATTN_HEAD_DIM]) * g_ref[...]
        qh = _rope_block(qh, cos_ref[...], sin_ref[...], ATTN_HEAD_DIM // 4) * scale
        qt_sc[:, g * tq:(g + 1) * tq] = qh.T.astype(BF16)
    m_sc[...] = jnp.full(m_sc.shape, -jnp.inf, F32)
    acc_sc[...] = jnp.zeros(acc_sc.shape, F32)

    def stage_s(c, slot):
        off = pl.multiple_of(c * kc, kc)
        s_bufs[slot][...] = jnp.dot(k_ref[0, pl.ds(off, kc), :], qt_sc[...], preferred_element_type=F32)

    def stage_f(slot):
        s_t = s_bufs[slot][...]
        m_prev = m_sc[...]
        m_new = jnp.maximum(m_prev, jnp.max(s_t, axis=0, keepdims=True))
        m_sc[...] = m_new
        a_bufs[slot][...] = jnp.exp2(m_prev - m_new)
        p_bufs[slot][...] = jnp.exp2((s_t - m_new).astype(BF16))

    def stage_a(c, slot):
        off = pl.multiple_of(c * kc, kc)
        acc_sc[...] = a_bufs[slot][...] * acc_sc[...] + jnp.dot(
            vt_ref[0, :, pl.ds(off, kc)], p_bufs[slot][...], preferred_element_type=F32)

    def tick(t, parity):
        stage_s(t, parity)
        stage_f(1 - parity)
        stage_a(t - 2, parity)

    n = n_chunks
    stage_s(0, 0)
    if n > 1:
        stage_s(1, 1)
    stage_f(0)
    first = 2
    if n > 2 and (n - 2) % 2 == 1:
        tick(2, 0)
        first = 3
    n_pairs = (n - first) // 2 if n > first else 0
    if n_pairs > 0:
        def pair(u, carry):
            t = first + 2 * u
            tick(t, first % 2)
            tick(t + 1, 1 - first % 2)
            return carry
        lax.fori_loop(0, n_pairs, pair, 0)
    if n > 1:
        stage_f((n - 1) % 2)
        stage_a(n - 2, (n - 2) % 2)
    stage_a(n - 1, (n - 1) % 2)

    acc = acc_sc[...]
    o = acc[:ATTN_HEAD_DIM] / acc[ATTN_HEAD_DIM:ATTN_HEAD_DIM + 1]
    for g in range(ATTN_GROUP):
        o_ref[:, g * ATTN_HEAD_DIM:(g + 1) * ATTN_HEAD_DIM] = o[:, g * tq:(g + 1) * tq].T.astype(o_ref.dtype)


def _attention(proj, gain, cos, sin, k_all, vt_all, batch, seq):
    lk = k_all.shape[1]
    kc = _largest_tile(lk, ATTN_KEY_CHUNK, MXU_DEPTH)
    tq = _largest_tile(seq, ATTN_Q_TILE, 8)
    nq = seq // tq
    qw = ATTN_GROUP * ATTN_HEAD_DIM
    cols = ATTN_GROUP * tq
    vrows = ATTN_HEAD_DIM + ONES_ROWS
    return pl.pallas_call(
        functools.partial(_attn_kernel, tq=tq, kc=kc, n_chunks=lk // kc),
        grid=(batch, ATTN_KV_HEADS, nq),
        in_specs=[pl.BlockSpec((tq, qw), lambda b, g, i: (b * nq + i, COL_QA // qw + g)),
                  pl.BlockSpec((1, ATTN_HEAD_DIM), lambda b, g, i: (0, 0)),
                  pl.BlockSpec((tq, ATTN_HEAD_DIM), lambda b, g, i: (i, 0)),
                  pl.BlockSpec((tq, ATTN_HEAD_DIM), lambda b, g, i: (i, 0)),
                  pl.BlockSpec((1, lk, ATTN_HEAD_DIM), lambda b, g, i: (b, 0, g)),
                  pl.BlockSpec((1, vrows, lk), lambda b, g, i: (b, g, 0))],
        out_specs=pl.BlockSpec((tq, qw), lambda b, g, i: (b * nq + i, g)),
        out_shape=jax.ShapeDtypeStruct((batch * seq, ATTN_Q_W), BF16),
        scratch_shapes=[pltpu.VMEM((ATTN_HEAD_DIM, cols), BF16),
                        pltpu.VMEM((1, cols), F32),
                        pltpu.VMEM((vrows, cols), F32),
                        pltpu.VMEM((kc, cols), F32),
                        pltpu.VMEM((kc, cols), F32),
                        pltpu.VMEM((kc, cols), BF16),
                        pltpu.VMEM((kc, cols), BF16),
                        pltpu.VMEM((1, cols), F32),
                        pltpu.VMEM((1, cols), F32)],
        compiler_params=_params(3),
        name="attn",
    )(proj, gain, cos, sin, k_all, vt_all)


def _merge_kernel(x_ref, g1_ref, of_ref, ob_ref, gr_ref, ao_ref, gtr_ref, gta_ref,
                  wro_ref, wao_ref, wo_ref, o_ref):
    ro = of_ref[...].astype(F32) + ob_ref[...].astype(F32)
    gr = gr_ref[...].astype(F32)
    parts = []
    for h in range(RET_HEADS):
        sl = slice(h * RET_V_DIM, (h + 1) * RET_V_DIM)
        g = gr[:, sl]
        parts.append((g * jax.nn.sigmoid(g) * _rms(ro[:, sl])).astype(BF16))
    ret_in = jnp.concatenate(parts, axis=1)
    ret_branch = jnp.dot(ret_in, wro_ref[...], preferred_element_type=F32)
    attn_branch = jnp.dot(ao_ref[...], wao_ref[...], preferred_element_type=F32)
    y = (jax.nn.sigmoid(gtr_ref[...].astype(F32)) * ret_branch
         + jax.nn.sigmoid(gta_ref[...].astype(F32)) * attn_branch)
    y = jnp.dot(y.astype(BF16), wo_ref[...], preferred_element_type=F32)
    o_ref[...] = x_ref[...] + g1_ref[0] * y


def _merge(x2, g1, o_f, o_b, proj, attn_o, w_ret_o, w_attn_o, w_out, seq):
    t = x2.shape[0]
    tm = _largest_tile(seq, 256, 8)
    per_b = seq // tm
    full = lambda i: (0, 0)
    return pl.pallas_call(
        _merge_kernel,
        grid=(t // tm,),
        in_specs=[pl.BlockSpec((tm, D_MODEL), lambda i: (i, 0)),
                  pl.BlockSpec((1, 1, D_MODEL), lambda i: (i // per_b, 0, 0)),
                  pl.BlockSpec((tm, RET_V_W), lambda i: (i, 0)),
                  pl.BlockSpec((tm, RET_V_W), lambda i: (i, 0)),
                  pl.BlockSpec((tm, RET_V_W), lambda i: (i, COL_GR // RET_V_W)),
                  pl.BlockSpec((tm, ATTN_Q_W), lambda i: (i, 0)),
                  pl.BlockSpec((tm, D_MODEL), lambda i: (i, COL_GATE_R // D_MODEL)),
                  pl.BlockSpec((tm, D_MODEL), lambda i: (i, COL_GATE_A // D_MODEL)),
                  pl.BlockSpec((RET_V_W, D_MODEL), full),
                  pl.BlockSpec((ATTN_Q_W, D_MODEL), full),
                  pl.BlockSpec((D_MODEL, D_MODEL), full)],
        out_specs=pl.BlockSpec((tm, D_MODEL), lambda i: (i, 0)),
        out_shape=jax.ShapeDtypeStruct((t, D_MODEL), F32),
        compiler_params=_params(1),
        name="merge",
    )(x2, g1, o_f, o_b, proj, attn_o, proj, proj, w_ret_o, w_attn_o, w_out)


def _router_kernel(x_ref, n_ref, sc_ref, sh_ref, wr_ref, br_ref, h_ref, ti_ref, tw_ref):
    h = _rms(x_ref[...]) * n_ref[...] * (1.0 + sc_ref[0]) + sh_ref[0]
    h_ref[...] = _pack_pairs(h)
    logits = lax.dot_general(wr_ref[...], h, (((1,), (1,)), ((), ())), preferred_element_type=F32,
                             precision=lax.Precision.HIGHEST) + br_ref[...]
    eid = lax.broadcasted_iota(jnp.int32, logits.shape, 0)
    vals = logits
    top_v = []
    top_i = []
    for _ in range(TOP_K):
        m = jnp.max(vals, axis=0, keepdims=True)
        idx = jnp.min(jnp.where(vals == m, eid, N_EXPERTS), axis=0, keepdims=True)
        top_v.append(m)
        top_i.append(idx)
        vals = jnp.where(eid == idx, -jnp.inf, vals)
    ex = [jnp.exp(v - top_v[0]) for v in top_v]
    denom = ex[0] + ex[1] + ex[2] + ex[3]
    ti_ref[...] = jnp.concatenate(top_i, axis=0)
    tw_ref[...] = jnp.concatenate([e / denom for e in ex], axis=0)


def _router(x1, norm, sc, sh, w_router_t, b_router, seq):
    t = x1.shape[0]
    tm = _largest_tile(seq, 1024, LANES)
    per_b = seq // tm
    mod_map = lambda i: (i // per_b, 0, 0)
    return pl.pallas_call(
        _router_kernel,
        grid=(t // tm,),
        in_specs=[pl.BlockSpec((tm, D_MODEL), lambda i: (i, 0)),
                  pl.BlockSpec((1, D_MODEL), lambda i: (0, 0)),
                  pl.BlockSpec((1, 1, D_MODEL), mod_map),
                  pl.BlockSpec((1, 1, D_MODEL), mod_map),
                  pl.BlockSpec((N_EXPERTS, D_MODEL), lambda i: (0, 0)),
                  pl.BlockSpec((N_EXPERTS, 1), lambda i: (0, 0))],
        out_specs=[pl.BlockSpec((tm, PACK_W), lambda i: (i, 0)),
                   pl.BlockSpec((TOP_K, tm), lambda i: (0, i)),
                   pl.BlockSpec((TOP_K, tm), lambda i: (0, i))],
        out_shape=[jax.ShapeDtypeStruct((t, PACK_W), PACK_DTYPE),
                   jax.ShapeDtypeStruct((TOP_K, t), jnp.int32),
                   jax.ShapeDtypeStruct((TOP_K, t), F32)],
        compiler_params=_params(1),
        name="router",
    )(x1, norm, sc, sh, w_router_t, b_router)


def _gather_rows(src, idx):
    n, w = src.shape
    r = idx.shape[0]
    workers = SC_CORES * SC_SUBCORES
    per_w = r // workers
    n_win = per_w // SC_GATHER_ROWS
    assert per_w * workers == r and n_win * SC_GATHER_ROWS == per_w, (r, workers, SC_GATHER_ROWS)
    mesh = plsc.VectorSubcoreMesh(core_axis_name="c", subcore_axis_name="s")

    @functools.partial(
        pl.kernel, mesh=mesh, out_type=jax.ShapeDtypeStruct((r, w), src.dtype),
        scratch_types=[pltpu.VMEM((SC_GATHER_ROWS,), jnp.int32),
                       pltpu.VMEM((SC_GATHER_ROWS, w), src.dtype),
                       pltpu.SemaphoreType.DMA])
    def gather(src_hbm, idx_hbm, out_hbm, idx_v, rows_v, sem):
        wid = lax.axis_index("s") * SC_CORES + lax.axis_index("c")
        base = wid * per_w

        @pl.loop(0, n_win)
        def _(win):
            off = base + win * SC_GATHER_ROWS
            pltpu.sync_copy(idx_hbm.at[pl.ds(off, SC_GATHER_ROWS)], idx_v)
            pltpu.async_copy(src_hbm.at[idx_v], rows_v, sem).wait()
            pltpu.sync_copy(rows_v, out_hbm.at[pl.ds(off, SC_GATHER_ROWS)])

    return gather(src, idx)


def _ffn_kernel(te_ref, tf_ref, x_ref, w1_ref, b1_ref, w2_ref, b2_ref, o_ref, w1_sc, w2_sc):
    flag = tf_ref[pl.program_id(0)]

    @pl.when(flag == TILE_FIRST)
    def _():
        w1_sc[...] = w1_ref[0].astype(BF16)
        w2_sc[...] = w2_ref[0].astype(BF16)

    @pl.when(flag == TILE_PAD)
    def _():
        o_ref[...] = jnp.zeros(o_ref.shape, o_ref.dtype)

    @pl.when(flag != TILE_PAD)
    def _():
        lo, hi = _unpack_pairs(x_ref[...])
        x = jnp.concatenate([lo, hi], axis=1).astype(BF16)
        a = jnp.dot(x, w1_sc[...], preferred_element_type=F32) + b1_ref[0]
        gate = jnp.minimum(a[:, :EXPERT_FF], SWIGLU_LIMIT)
        up = jnp.clip(a[:, EXPERT_FF:], -SWIGLU_LIMIT, SWIGLU_LIMIT)
        act = gate * jax.nn.sigmoid(SWIGLU_ALPHA * gate) * (up + 1.0)
        y = jnp.dot(act.astype(BF16), w2_sc[...], preferred_element_type=F32) + b2_ref[0]
        o_ref[...] = _pack_pairs(y)


def _ffn(xs, tile_expert, tile_flag, w1, b1, w2, b2, tg):
    p = xs.shape[0]
    grid_spec = pltpu.PrefetchScalarGridSpec(
        num_scalar_prefetch=2,
        grid=(p // tg,),
        in_specs=[pl.BlockSpec((tg, PACK_W), lambda j, te, tv: (j, 0)),
                  pl.BlockSpec((1, D_MODEL, 2 * EXPERT_FF), lambda j, te, tv: (te[j], 0, 0)),
                  pl.BlockSpec((1, 1, 2 * EXPERT_FF), lambda j, te, tv: (te[j], 0, 0)),
                  pl.BlockSpec((1, EXPERT_FF, D_MODEL), lambda j, te, tv: (te[j], 0, 0)),
                  pl.BlockSpec((1, 1, D_MODEL), lambda j, te, tv: (te[j], 0, 0))],
        out_specs=pl.BlockSpec((tg, PACK_W), lambda j, te, tv: (j, 0)),
        scratch_shapes=[pltpu.VMEM((D_MODEL, 2 * EXPERT_FF), BF16),
                        pltpu.VMEM((EXPERT_FF, D_MODEL), BF16)],
    )
    return pl.pallas_call(
        _ffn_kernel,
        grid_spec=grid_spec,
        out_shape=jax.ShapeDtypeStruct((p, PACK_W), PACK_DTYPE),
        compiler_params=_params(1),
        name="ffn",
    )(tile_expert, tile_flag, xs, w1, b1, w2, b2)


def _combine_kernel(x_ref, g2_ref, w_ref, y_ref, o_ref):
    w = w_ref[...]
    acc_lo = None
    for k in range(TOP_K):
        lo, hi = _unpack_pairs(y_ref[k])
        wk = w[:, k:k + 1]
        acc_lo = wk * lo if acc_lo is None else acc_lo + wk * lo
        acc_hi = wk * hi if k == 0 else acc_hi + wk * hi
    acc = jnp.concatenate([acc_lo, acc_hi], axis=1)
    o_ref[...] = x_ref[...] + g2_ref[0] * acc


def _combine_into_kernel(x_ref, g2_ref, w_ref, y_ref, prev_ref, o_ref):
    del prev_ref
    _combine_kernel(x_ref, g2_ref, w_ref, y_ref, o_ref)


def _combine(x1, g2, w_tok, yk, seq, first_row, prev):
    t = x1.shape[0]
    rows = yk.shape[1]
    tm = _largest_tile(seq, 512, 8)
    per_b = seq // tm
    i0 = first_row // tm
    in_specs = [pl.BlockSpec((tm, D_MODEL), lambda i: (i0 + i, 0)),
                pl.BlockSpec((1, 1, D_MODEL), lambda i: ((i0 + i) // per_b, 0, 0)),
                pl.BlockSpec((tm, TOP_K), lambda i: (i0 + i, 0)),
                pl.BlockSpec((TOP_K, tm, PACK_W), lambda i: (0, i, 0))]
    args = [x1, g2, w_tok, yk]
    if prev is not None:
        in_specs.append(pl.BlockSpec(memory_space=pl.ANY))
        args.append(prev)
    return pl.pallas_call(
        _combine_kernel if prev is None else _combine_into_kernel,
        grid=(rows // tm,),
        in_specs=in_specs,
        out_specs=pl.BlockSpec((tm, D_MODEL), lambda i: (i0 + i, 0)),
        out_shape=jax.ShapeDtypeStruct((t, D_MODEL), F32),
        input_output_aliases={} if prev is None else {len(args) - 1: 0},
        compiler_params=_params(1),
        name="combine",
    )(*args)


def _plan(top_i, tg):
    t = top_i.shape[1]
    n_assign = TOP_K * t
    p = n_assign + N_EXPERTS * tg
    n_tiles = p // tg
    flat_e = top_i.reshape(-1)
    sorted_e, sorted_a = lax.sort_key_val(flat_e, jnp.arange(n_assign, dtype=jnp.int32))
    experts = jnp.arange(N_EXPERTS + 1, dtype=jnp.int32)
    bounds = jnp.sum((sorted_e[None, :] < experts[:, None]).astype(jnp.int32), axis=1)
    start = bounds[:-1]
    counts = bounds[1:] - start
    padded = ((counts + tg - 1) // tg) * tg
    off_end = jnp.cumsum(padded)
    off = off_end - padded
    dest_sorted = off[sorted_e] + jnp.arange(n_assign, dtype=jnp.int32) - start[sorted_e]
    _, dest = lax.sort_key_val(sorted_a, dest_sorted)
    tile_start = jnp.arange(n_tiles, dtype=jnp.int32) * tg
    tile_valid = (tile_start < off_end[-1]).astype(jnp.int32)
    te = jnp.sum((tile_start[:, None] >= off_end[None, :]).astype(jnp.int32), axis=1)
    last_e = jnp.sum(((off_end[-1] - 1) >= off_end).astype(jnp.int32))
    tile_expert = jnp.where(tile_valid != 0, te, last_e)
    slot = jnp.arange(p, dtype=jnp.int32)
    e_slot = jnp.repeat(tile_expert, tg)
    r = slot - off[e_slot]
    ok = (r < counts[e_slot]) & (jnp.repeat(tile_valid, tg) != 0)
    a_slot = sorted_a[jnp.clip(start[e_slot] + r, 0, n_assign - 1)]
    src_tok = jnp.where(ok, a_slot % t, slot % t).astype(jnp.int32)
    changed = jnp.concatenate([jnp.ones((1,), jnp.bool_), tile_expert[1:] != tile_expert[:-1]])
    tile_flag = jnp.where(tile_valid != 0, jnp.where(changed, TILE_FIRST, TILE_BODY), TILE_PAD).astype(jnp.int32)
    return src_tok, dest, tile_expert, tile_flag


def kernel(x, c, ctx, c_ctx, norm1, norm2, w_ada, b_ada, w_in, ret_decay_f, ret_decay_b, attn_q_norm, attn_k_norm,
           w_ret_o, w_attn_o, w_out, w_router, b_router, w_exp_in, b_exp_in, w_exp_out, b_exp_out):
    assert w_in.shape[0] == 1, "single-layer block"
    b, seq, d = x.shape
    n_ctx = ctx.shape[1]
    t = b * seq
    rows = seq // GRID_W

    idx = np.cumsum(IN_SIZES)[:-1].tolist()
    wq_r, wk_r, wv_r, wg_r, wq_a, wk_a, wv_a, wgt_r, wgt_a = jnp.split(w_in[0], idx, axis=-1)
    w_in_p = jnp.concatenate([wq_r, wk_r, wv_r, wg_r, wq_a, wgt_r, wgt_a, wk_a, wv_a], axis=-1)
    w1 = w_exp_in[0]
    w2 = w_exp_out[0]
    b1 = b_exp_in[0].reshape(N_EXPERTS, 1, 2 * EXPERT_FF)
    b2 = b_exp_out[0].reshape(N_EXPERTS, 1, D_MODEL)

    pad = (-(b + 1)) % 8
    c_all = jnp.concatenate([c, c_ctx[None, :], jnp.zeros((pad, d), F32)], axis=0)
    mod = _ada(c_all, w_ada[0], b_ada[0])
    sh1, sc1, g1, sh2, sc2, g2 = [m.reshape(-1, 1, d) for m in jnp.split(mod, 6, axis=-1)]
    lat = lambda m: m[:b]
    cx = lambda m: m[b:b + 1]

    x2 = x.reshape(t, d)
    proj = _inproj(x2, norm1, lat(sc1), lat(sh1), w_in_p, seq)
    proj_c = _inproj(ctx.reshape(b * n_ctx, d), norm1, cx(sc1), cx(sh1), w_in_p, n_ctx)

    cos_r, sin_r = _rope_tables(rows, RET_QK_DIM)
    cos_a, sin_a = _rope_tables(rows, ATTN_HEAD_DIM)
    lg_f = -jax.nn.softplus(ret_decay_f[0].astype(F32))
    lg_b = -jax.nn.softplus(ret_decay_b[0].astype(F32))
    zero_state = jnp.zeros((2, b, RET_HEADS, RET_QK_DIM, RET_V_DIM), F32)
    cos_c = jnp.ones((n_ctx, RET_QK_DIM), F32)
    sin_c = jnp.zeros((n_ctx, RET_QK_DIM), F32)
    _, _, s_ctx = _retention(proj_c, cos_c, sin_c, lg_f, lg_b, zero_state, b, n_ctx, False)
    o_f, o_b, _ = _retention(proj, cos_r, sin_r, lg_f, lg_b, s_ctx, b, seq, True)

    k_lat = _kprep(proj, attn_k_norm, cos_a, sin_a, seq, True)
    k_ctx = _kprep(proj_c, attn_k_norm, cos_a, sin_a, n_ctx, False)
    k_all = jnp.concatenate([k_ctx.reshape(b, n_ctx, ATTN_KV_W), k_lat.reshape(b, seq, ATTN_KV_W)], axis=1)
    v_all = jnp.concatenate([proj_c[:, COL_VA:COL_VA + ATTN_KV_W].reshape(b, n_ctx, ATTN_KV_W),
                             proj[:, COL_VA:COL_VA + ATTN_KV_W].reshape(b, seq, ATTN_KV_W)], axis=1)
    lk = n_ctx + seq
    vt = jnp.swapaxes(v_all.reshape(b, lk, ATTN_KV_HEADS, ATTN_HEAD_DIM), 1, 3).swapaxes(1, 2)
    vt_all = jnp.concatenate([vt, jnp.ones((b, ATTN_KV_HEADS, ONES_ROWS, lk), vt.dtype)], axis=2)
    vt_all = vt_all.reshape(b, ATTN_KV_HEADS * (ATTN_HEAD_DIM + ONES_ROWS), lk)
    attn_o = _attention(proj, attn_q_norm, cos_a, sin_a, k_all, vt_all, b, seq)

    x1 = _merge(x2, lat(g1), o_f, o_b, proj, attn_o, w_ret_o[0].astype(BF16), w_attn_o[0].astype(BF16),
                w_out[0].astype(BF16), seq)

    h2, top_i, top_w = _router(x1, norm2, lat(sc2), lat(sh2), w_router[0].T, b_router[0].reshape(N_EXPERTS, 1), seq)
    n_groups = MOE_GROUPS if b % MOE_GROUPS == 0 else 1
    tgrp = t // n_groups
    tg = _largest_tile(TOP_K * tgrp, 512, 8)
    w_tok = top_w.T
    out = None
    for grp in range(n_groups):
        first = grp * tgrp
        src_tok, dest, tile_expert, tile_flag = _plan(top_i[:, first:first + tgrp], tg)
        xs = _gather_rows(h2, src_tok + first)
        ys = _ffn(xs, tile_expert, tile_flag, w1, b1, w2, b2, tg)
        yk = _gather_rows(ys, dest).reshape(TOP_K, tgrp, PACK_W)
        out = _combine(x1, lat(g2), w_tok, yk, seq, first, out)
    return out.reshape(b, seq, d)
```

```python
import functools

import jax
import jax.numpy as jnp
import numpy as np
from jax import lax
from jax.experimental import pallas as pl
from jax.experimental.pallas import tpu as pltpu
from jax.experimental.pallas import tpu_sc as plsc

F32 = jnp.float32
BF16 = jnp.bfloat16

D_MODEL = 1024
GRID_W = 64
EPS = 1e-6
RET_HEADS = 4
RET_QK_DIM = 256
RET_V_DIM = 512
ATTN_HEADS = 8
ATTN_KV_HEADS = 2
ATTN_GROUP = ATTN_HEADS // ATTN_KV_HEADS
ATTN_HEAD_DIM = 128
ROPE_THETA = 10000.0
N_EXPERTS = 32
TOP_K = 4
EXPERT_FF = 1024
SWIGLU_LIMIT = 7.0
SWIGLU_ALPHA = 1.702

RET_QK_W = RET_HEADS * RET_QK_DIM
RET_V_W = RET_HEADS * RET_V_DIM
ATTN_Q_W = ATTN_HEADS * ATTN_HEAD_DIM
ATTN_KV_W = ATTN_KV_HEADS * ATTN_HEAD_DIM
IN_SIZES = (RET_QK_W, RET_QK_W, RET_V_W, RET_V_W, ATTN_Q_W, ATTN_KV_W, ATTN_KV_W, D_MODEL, D_MODEL)
IN_WIDTH = sum(IN_SIZES)
COL_QR = 0
COL_KR = COL_QR + RET_QK_W
COL_VR = COL_KR + RET_QK_W
COL_GR = COL_VR + RET_V_W
COL_QA = COL_GR + RET_V_W
COL_GATE_R = COL_QA + ATTN_Q_W
COL_GATE_A = COL_GATE_R + D_MODEL
COL_KA = COL_GATE_A + D_MODEL
COL_VA = COL_KA + ATTN_KV_W

RET_CHUNK = 256
RET_HEADS_PER_STEP = 2
LANES = 128
SC_CORES = 2
SC_SUBCORES = 16
SC_GATHER_ROWS = 128
MXU_DEPTH = 256
ATTN_KEY_CHUNK = 768
ATTN_Q_TILE = 512
ATTN_EXP_SLAB = 32
ONES_ROWS = 16
PACK_DTYPE = jnp.int32
PACK_W = D_MODEL // 2
MOE_GROUPS = 2
TILE_PAD, TILE_BODY, TILE_FIRST = 0, 1, 2
VMEM_LIMIT = 56 * 1024 * 1024

ARB = pltpu.ARBITRARY


def _params(n_axes, **kw):
    return pltpu.CompilerParams(dimension_semantics=(ARB,) * n_axes, vmem_limit_bytes=VMEM_LIMIT, **kw)


def _largest_tile(n, cap, mult):
    best = None
    for t in range(mult, min(n, cap) + 1, mult):
        if n % t == 0:
            best = t
    assert best is not None, (n, cap, mult)
    return best


def _rms(x):
    return x * lax.rsqrt(jnp.mean(x * x, axis=-1, keepdims=True) + EPS)


def _pack_pairs(x):
    half = x.shape[1] // 2
    lo = lax.bitcast_convert_type(x[:, :half].astype(BF16).astype(F32), jnp.int32)
    hi = lax.bitcast_convert_type(x[:, half:].astype(BF16).astype(F32), jnp.int32)
    return lax.bitwise_or(lax.bitwise_and(hi, jnp.int32(-65536)), lax.shift_right_logical(lo, jnp.int32(16)))


def _unpack_pairs(w):
    lo = lax.bitcast_convert_type(lax.shift_left(w, jnp.int32(16)), F32)
    hi = lax.bitcast_convert_type(lax.bitwise_and(w, jnp.int32(-65536)), F32)
    return lo, hi


def _rope_block(x, cos, sin, half):
    if 2 * half == LANES:
        swapped = pltpu.roll(x, half, 1)
    else:
        lane = lax.broadcasted_iota(jnp.int32, x.shape, 1)
        first = (lane % (2 * half)) < half
        swapped = jnp.where(first, pltpu.roll(x, LANES - half, 1), pltpu.roll(x, half, 1))
    return x * cos + swapped * sin


def _rope_tables(rows, head_dim):
    row_id = jnp.broadcast_to(jnp.arange(rows, dtype=F32)[:, None], (rows, GRID_W)).reshape(-1)
    col_id = jnp.broadcast_to(jnp.arange(GRID_W, dtype=F32)[None, :], (rows, GRID_W)).reshape(-1)
    n_freq = head_dim // 4
    inv_freq = ROPE_THETA ** (-jnp.arange(n_freq, dtype=F32) / n_freq)
    ang_r = row_id[:, None] * inv_freq
    ang_c = col_id[:, None] * inv_freq
    cos = jnp.concatenate([jnp.cos(ang_r), jnp.cos(ang_r), jnp.cos(ang_c), jnp.cos(ang_c)], axis=1)
    sin = jnp.concatenate([-jnp.sin(ang_r), jnp.sin(ang_r), -jnp.sin(ang_c), jnp.sin(ang_c)], axis=1)
    return cos, sin


def _ada_kernel(c_ref, w_ref, b_ref, o_ref):
    c = c_ref[...]
    s = c * jax.nn.sigmoid(c)
    o_ref[...] = jnp.dot(s, w_ref[...], preferred_element_type=F32,
                         precision=lax.Precision.HIGHEST) + b_ref[...]


def _ada(c_pad, w_ada, b_ada):
    rows = c_pad.shape[0]
    n = w_ada.shape[1]
    tn = _largest_tile(n, 1536, LANES)
    return pl.pallas_call(
        _ada_kernel,
        grid=(n // tn,),
        in_specs=[pl.BlockSpec((rows, D_MODEL), lambda j: (0, 0)),
                  pl.BlockSpec((D_MODEL, tn), lambda j: (0, j)),
                  pl.BlockSpec((1, tn), lambda j: (0, j))],
        out_specs=pl.BlockSpec((rows, tn), lambda j: (0, j)),
        out_shape=jax.ShapeDtypeStruct((rows, n), F32),
        compiler_params=_params(1),
        name="ada",
    )(c_pad, w_ada, b_ada.reshape(1, n))


def _inproj_kernel(x_ref, n_ref, sc_ref, sh_ref, w_ref, o_ref, w_sc):
    @pl.when(pl.program_id(1) == 0)
    def _():
        w_sc[...] = w_ref[...].astype(BF16)

    y = _rms(x_ref[...]) * n_ref[...]
    h = (y * (1.0 + sc_ref[0]) + sh_ref[0]).astype(BF16)
    o_ref[...] = jnp.dot(h, w_sc[...], preferred_element_type=F32).astype(o_ref.dtype)


def _inproj(x2, norm, sc, sh, w, rows_per_batch):
    t = x2.shape[0]
    n = w.shape[1]
    tm = _largest_tile(rows_per_batch, 512, 8)
    tn = _largest_tile(n, 2560, LANES)
    per_b = rows_per_batch // tm
    if sc.shape[0] == 1:
        mod_map = lambda j, i: (0, 0, 0)
    else:
        mod_map = lambda j, i: (i // per_b, 0, 0)
    return pl.pallas_call(
        _inproj_kernel,
        grid=(n // tn, t // tm),
        in_specs=[pl.BlockSpec((tm, D_MODEL), lambda j, i: (i, 0)),
                  pl.BlockSpec((1, D_MODEL), lambda j, i: (0, 0)),
                  pl.BlockSpec((1, 1, D_MODEL), mod_map),
                  pl.BlockSpec((1, 1, D_MODEL), mod_map),
                  pl.BlockSpec((D_MODEL, tn), lambda j, i: (0, j))],
        out_specs=pl.BlockSpec((tm, tn), lambda j, i: (i, j)),
        out_shape=jax.ShapeDtypeStruct((t, n), BF16),
        scratch_shapes=[pltpu.VMEM((D_MODEL, tn), BF16)],
        compiler_params=_params(2),
        name="inproj",
    )(x2, norm, sc, sh, w)


def _kprep_kernel(k_ref, g_ref, cos_ref, sin_ref, o_ref, *, use_rope):
    k = k_ref[...].astype(F32)
    for g in range(ATTN_KV_HEADS):
        sl = slice(g * ATTN_HEAD_DIM, (g + 1) * ATTN_HEAD_DIM)
        kh = _rms(k[:, sl]) * g_ref[...]
        if use_rope:
            kh = _rope_block(kh, cos_ref[...], sin_ref[...], ATTN_HEAD_DIM // 4)
        o_ref[:, sl] = kh.astype(BF16)


def _kprep(proj, gain, cos, sin, rows_per_batch, use_rope):
    t = proj.shape[0]
    tm = _largest_tile(rows_per_batch, 1024, 8)
    per_b = rows_per_batch // tm
    return pl.pallas_call(
        functools.partial(_kprep_kernel, use_rope=use_rope),
        grid=(t // tm,),
        in_specs=[pl.BlockSpec((tm, ATTN_KV_W), lambda i: (i, COL_KA // ATTN_KV_W)),
                  pl.BlockSpec((1, ATTN_HEAD_DIM), lambda i: (0, 0)),
                  pl.BlockSpec((tm, ATTN_HEAD_DIM), lambda i: (i % per_b, 0)),
                  pl.BlockSpec((tm, ATTN_HEAD_DIM), lambda i: (i % per_b, 0))],
        out_specs=pl.BlockSpec((tm, ATTN_KV_W), lambda i: (i, 0)),
        out_shape=jax.ShapeDtypeStruct((t, ATTN_KV_W), BF16),
        compiler_params=_params(1),
        name="kprep",
    )(proj, gain, cos, sin)


def _ret_kernel(dch_ref, qf_ref, kf_ref, vf_ref, cosf_ref, sinf_ref, qb_ref, kb_ref, vb_ref, cosb_ref, sinb_ref,
                dmat_ref, din_ref, dout_ref, s0_ref, of_ref, ob_ref, sfin_ref, s_sc, *, use_rope, n_chunks, heads):
    hb = pl.program_id(1)
    c = pl.program_id(2)

    @pl.when(c == 0)
    def _():
        s_sc[...] = s0_ref[:, 0]

    def rope(x, cos, sin):
        return jnp.concatenate([_rope_block(x[:, i * LANES:(i + 1) * LANES], cos[:, i * LANES:(i + 1) * LANES],
                                            sin[:, i * LANES:(i + 1) * LANES], RET_QK_DIM // 4)
                                for i in range(RET_QK_DIM // LANES)], axis=1)

    sides = ((qf_ref, kf_ref, vf_ref, cosf_ref, sinf_ref), (qb_ref, kb_ref, vb_ref, cosb_ref, sinb_ref))
    chains = [(d, j) for d in range(2) for j in range(heads)]
    qs, ks, vs, ss = {}, {}, {}, {}
    for d, j in chains:
        q_ref, k_ref, v_ref, cos_ref, sin_ref = sides[d]
        q = q_ref[:, j * RET_QK_DIM:(j + 1) * RET_QK_DIM].astype(F32)
        k = k_ref[:, j * RET_QK_DIM:(j + 1) * RET_QK_DIM].astype(F32)
        if use_rope:
            q = rope(q, cos_ref[...], sin_ref[...])
            k = rope(k, cos_ref[...], sin_ref[...])
        qs[d, j], ks[d, j] = q, k
        vs[d, j] = v_ref[:, j * RET_V_DIM:(j + 1) * RET_V_DIM]
        ss[d, j] = s_sc[d, j]
    scores = {ch: lax.dot_general(qs[ch].astype(BF16), ks[ch].astype(BF16), (((1,), (1,)), ((), ())),
                                  preferred_element_type=F32) * dmat_ref[ch[0], ch[1]] for ch in chains}
    inter = {ch: jnp.dot((qs[ch] * din_ref[ch[0], ch[1]]).astype(BF16), ss[ch].astype(BF16),
                         preferred_element_type=F32) for ch in chains}
    intra = {ch: jnp.dot(scores[ch].astype(BF16), vs[ch], preferred_element_type=F32) for ch in chains}
    o_refs = (of_ref, ob_ref)
    for d, j in chains:
        o_refs[d][:, j * RET_V_DIM:(j + 1) * RET_V_DIM] = (intra[d, j] + inter[d, j]).astype(of_ref.dtype)
    s_new = {ch: ss[ch] * dch_ref[ch[0] * RET_HEADS + hb * heads + ch[1]]
             + jnp.dot((ks[ch] * dout_ref[ch[0], ch[1]]).T.astype(BF16), vs[ch], preferred_element_type=F32)
             for ch in chains}
    for d, j in chains:
        s_sc[d, j] = s_new[d, j]

    @pl.when(c == n_chunks - 1)
    def _():
        for d, j in chains:
            sfin_ref[d, 0, j] = s_new[d, j]


def _ret_tables(log_gamma, chunk, reverse, k_scale):
    pos = jnp.arange(chunk, dtype=F32)
    diff = pos[:, None] - pos[None, :]
    if reverse:
        diff = -diff
        mask = diff > 0
        p_in = chunk - pos
        p_out = pos
    else:
        mask = diff >= 0
        p_in = pos + 1.0
        p_out = chunk - 1.0 - pos
    lg = log_gamma.astype(F32)
    dmat = jnp.where(mask[None], jnp.exp(lg[:, None, None] * jnp.maximum(diff, 0.0)[None]), 0.0) * k_scale
    d_in = jnp.exp(lg[:, None] * p_in)
    d_out = jnp.exp(lg[:, None] * p_out) * k_scale
    d_in = jnp.broadcast_to(d_in[:, :, None], (RET_HEADS, chunk, RET_QK_DIM))
    d_out = jnp.broadcast_to(d_out[:, :, None], (RET_HEADS, chunk, RET_QK_DIM))
    d_chunk = jnp.exp(lg * chunk)
    return d_chunk, dmat, d_in, d_out


def _retention(proj, cos, sin, lg_f, lg_b, s0, batch, seq, use_rope):
    chunk = min(RET_CHUNK, seq)
    nc = seq // chunk
    hs = RET_HEADS_PER_STEP
    qw, vw = hs * RET_QK_DIM, hs * RET_V_DIM
    k_scale = RET_QK_DIM ** -0.5
    tabs = [_ret_tables(lg_f, chunk, False, k_scale), _ret_tables(lg_b, chunk, True, k_scale)]
    d_chunk, dmat, d_in, d_out = [jnp.stack([tabs[0][i], tabs[1][i]]) for i in range(4)]
    fwd = lambda c: c
    bwd = lambda c: nc - 1 - c

    def side(cidx):
        row = lambda b, c: b * nc + cidx(c)
        return [pl.BlockSpec((chunk, qw), lambda b, h, c: (row(b, c), COL_QR // qw + h)),
                pl.BlockSpec((chunk, qw), lambda b, h, c: (row(b, c), COL_KR // qw + h)),
                pl.BlockSpec((chunk, vw), lambda b, h, c: (row(b, c), COL_VR // vw + h)),
                pl.BlockSpec((chunk, RET_QK_DIM), lambda b, h, c: (cidx(c), 0)),
                pl.BlockSpec((chunk, RET_QK_DIM), lambda b, h, c: (cidx(c), 0))]

    state_spec = pl.BlockSpec((2, 1, hs, RET_QK_DIM, RET_V_DIM), lambda b, h, c: (0, b, h, 0, 0))
    return pl.pallas_call(
        functools.partial(_ret_kernel, use_rope=use_rope, n_chunks=nc, heads=hs),
        grid=(batch, RET_HEADS // hs, nc),
        in_specs=[pl.BlockSpec(memory_space=pltpu.SMEM)] + side(fwd) + side(bwd) + [
            pl.BlockSpec((2, hs, chunk, chunk), lambda b, h, c: (0, h, 0, 0)),
            pl.BlockSpec((2, hs, chunk, RET_QK_DIM), lambda b, h, c: (0, h, 0, 0)),
            pl.BlockSpec((2, hs, chunk, RET_QK_DIM), lambda b, h, c: (0, h, 0, 0)),
            state_spec],
        out_specs=[pl.BlockSpec((chunk, vw), lambda b, h, c: (b * nc + c, h)),
                   pl.BlockSpec((chunk, vw), lambda b, h, c: (b * nc + nc - 1 - c, h)),
                   state_spec],
        out_shape=[jax.ShapeDtypeStruct((batch * seq, RET_V_W), BF16),
                   jax.ShapeDtypeStruct((batch * seq, RET_V_W), BF16),
                   jax.ShapeDtypeStruct((2, batch, RET_HEADS, RET_QK_DIM, RET_V_DIM), F32)],
        scratch_shapes=[pltpu.VMEM((2, hs, RET_QK_DIM, RET_V_DIM), F32)],
        compiler_params=_params(3),
        name="ret",
    )(d_chunk.reshape(-1), proj, proj, proj, cos, sin, proj, proj, proj, cos, sin, dmat, d_in, d_out, s0)


def _attn_kernel(q_ref, g_ref, cos_ref, sin_ref, k_ref, vt_ref, o_ref, qt_sc, m_sc, acc_sc,
                 s0_sc, s1_sc, p0_sc, p1_sc, a0_sc, a1_sc, *, tq, kc, n_chunks):
    s_bufs = (s0_sc, s1_sc)
    p_bufs = (p0_sc, p1_sc)
    a_bufs = (a0_sc, a1_sc)

    q = q_ref[...].astype(F32)
    scale = ATTN_HEAD_DIM ** -0.5 * np.log2(np.e)
    for g in range(ATTN_GROUP):
        qh = _rms(q[:, g * ATTN_HEAD_DIM:(g + 1) * ATTN_HEAD_DIM]) * g_ref[...]
        qh = _rope_block(qh, cos_ref[...], sin_ref[...], ATTN_HEAD_DIM // 4) * scale
        qt_sc[:, g * tq:(g + 1) * tq] = qh.T.astype(BF16)
    m_sc[...] = jnp.full(m_sc.shape, -jnp.inf, F32)
    acc_sc[...] = jnp.zeros(acc_sc.shape, F32)

    def stage_s(c, slot):
        off = pl.multiple_of(c * kc, kc)
        s_bufs[slot][...] = jnp.dot(k_ref[0, pl.ds(off, kc), :], qt_sc[...], preferred_element_type=F32)

    def stage_f(slot):
        m_prev = m_sc[...]
        m_new = jnp.maximum(m_prev, jnp.max(s_bufs[slot][...], axis=0, keepdims=True))
        m_sc[...] = m_new
        a_bufs[slot][...] = jnp.exp2(m_prev - m_new)
        for r in range(0, kc, ATTN_EXP_SLAB):
            p_bufs[slot][r:r + ATTN_EXP_SLAB, :] = jnp.exp2(
                (s_bufs[slot][r:r + ATTN_EXP_SLAB, :] - m_new).astype(BF16))

    def stage_a(c, slot):
        off = pl.multiple_of(c * kc, kc)
        acc_sc[...] = a_bufs[slot][...] * acc_sc[...] + jnp.dot(
            vt_ref[0, :, pl.ds(off, kc)], p_bufs[slot][...], preferred_element_type=F32)

    def tick(t, parity):
        stage_s(t, parity)
        stage_f(1 - parity)
        stage_a(t - 2, parity)

    n = n_chunks
    stage_s(0, 0)
    if n > 1:
        stage_s(1, 1)
    stage_f(0)
    first = 2
    if n > 2 and (n - 2) % 2 == 1:
        tick(2, 0)
        first = 3
    n_pairs = (n - first) // 2 if n > first else 0
    if n_pairs > 0:
        def pair(u, carry):
            t = first + 2 * u
            tick(t, first % 2)
            tick(t + 1, 1 - first % 2)
            return carry
        lax.fori_loop(0, n_pairs, pair, 0)
    if n > 1:
        stage_f((n - 1) % 2)
        stage_a(n - 2, (n - 2) % 2)
    stage_a(n - 1, (n - 1) % 2)

    acc = acc_sc[...]
    o = acc[:ATTN_HEAD_DIM] / acc[ATTN_HEAD_DIM:ATTN_HEAD_DIM + 1]
    for g in range(ATTN_GROUP):
        o_ref[:, g * ATTN_HEAD_DIM:(g + 1) * ATTN_HEAD_DIM] = o[:, g * tq:(g + 1) * tq].T.astype(o_ref.dtype)


def _attention(proj, gain, cos, sin, k_all, vt_all, batch, seq):
    lk = k_all.shape[1]
    kc = _largest_tile(lk, ATTN_KEY_CHUNK, MXU_DEPTH)
    tq = _largest_tile(seq, ATTN_Q_TILE, 8)
    nq = seq // tq
    qw = ATTN_GROUP * ATTN_HEAD_DIM
    cols = ATTN_GROUP * tq
    vrows = ATTN_HEAD_DIM + ONES_ROWS
    return pl.pallas_call(
        functools.partial(_attn_kernel, tq=tq, kc=kc, n_chunks=lk // kc),
        grid=(batch, ATTN_KV_HEADS, nq),
        in_specs=[pl.BlockSpec((tq, qw), lambda b, g, i: (b * nq + i, COL_QA // qw + g)),
                  pl.BlockSpec((1, ATTN_HEAD_DIM), lambda b, g, i: (0, 0)),
                  pl.BlockSpec((tq, ATTN_HEAD_DIM), lambda b, g, i: (i, 0)),
                  pl.BlockSpec((tq, ATTN_HEAD_DIM), lambda b, g, i: (i, 0)),
                  pl.BlockSpec((1, lk, ATTN_HEAD_DIM), lambda b, g, i: (b, 0, g)),
                  pl.BlockSpec((1, vrows, lk), lambda b, g, i: (b, g, 0))],
        out_specs=pl.BlockSpec((tq, qw), lambda b, g, i: (b * nq + i, g)),
        out_shape=jax.ShapeDtypeStruct((batch * seq, ATTN_Q_W), BF16),
        scratch_shapes=[pltpu.VMEM((ATTN_HEAD_DIM, cols), BF16),
                        pltpu.VMEM((1, cols), F32),
                        pltpu.VMEM((vrows, cols), F32),
                        pltpu.VMEM((kc, cols), F32),
                        pltpu.VMEM((kc, cols), F32),
                        pltpu.VMEM((kc, cols), BF16),
                        pltpu.VMEM((kc, cols), BF16),
                        pltpu.VMEM((1, cols), F32),
                        pltpu.VMEM((1, cols), F32)],
        compiler_params=_params(3),
        name="attn",
    )(proj, gain, cos, sin, k_all, vt_all)


def _merge_kernel(x_ref, g1_ref, of_ref, ob_ref, gr_ref, ao_ref, gtr_ref, gta_ref,
                  wro_ref, wao_ref, wo_ref, o_ref):
    ro = of_ref[...].astype(F32) + ob_ref[...].astype(F32)
    gr = gr_ref[...].astype(F32)
    parts = []
    for h in range(RET_HEADS):
        sl = slice(h * RET_V_DIM, (h + 1) * RET_V_DIM)
        g = gr[:, sl]
        parts.append((g * jax.nn.sigmoid(g) * _rms(ro[:, sl])).astype(BF16))
    ret_in = jnp.concatenate(parts, axis=1)
    ret_branch = jnp.dot(ret_in, wro_ref[...], preferred_element_type=F32)
    attn_branch = jnp.dot(ao_ref[...], wao_ref[...], preferred_element_type=F32)
    y = (jax.nn.sigmoid(gtr_ref[...].astype(F32)) * ret_branch
         + jax.nn.sigmoid(gta_ref[...].astype(F32)) * attn_branch)
    y = jnp.dot(y.astype(BF16), wo_ref[...], preferred_element_type=F32)
    o_ref[...] = x_ref[...] + g1_ref[0] * y


def _merge(x2, g1, o_f, o_b, proj, attn_o, w_ret_o, w_attn_o, w_out, seq):
    t = x2.shape[0]
    tm = _largest_tile(seq, 256, 8)
    per_b = seq // tm
    full = lambda i: (0, 0)
    return pl.pallas_call(
        _merge_kernel,
        grid=(t // tm,),
        in_specs=[pl.BlockSpec((tm, D_MODEL), lambda i: (i, 0)),
                  pl.BlockSpec((1, 1, D_MODEL), lambda i: (i // per_b, 0, 0)),
                  pl.BlockSpec((tm, RET_V_W), lambda i: (i, 0)),
                  pl.BlockSpec((tm, RET_V_W), lambda i: (i, 0)),
                  pl.BlockSpec((tm, RET_V_W), lambda i: (i, COL_GR // RET_V_W)),
                  pl.BlockSpec((tm, ATTN_Q_W), lambda i: (i, 0)),
                  pl.BlockSpec((tm, D_MODEL), lambda i: (i, COL_GATE_R // D_MODEL)),
                  pl.BlockSpec((tm, D_MODEL), lambda i: (i, COL_GATE_A // D_MODEL)),
                  pl.BlockSpec((RET_V_W, D_MODEL), full),
                  pl.BlockSpec((ATTN_Q_W, D_MODEL), full),
                  pl.BlockSpec((D_MODEL, D_MODEL), full)],
        out_specs=pl.BlockSpec((tm, D_MODEL), lambda i: (i, 0)),
        out_shape=jax.ShapeDtypeStruct((t, D_MODEL), F32),
        compiler_params=_params(1),
        name="merge",
    )(x2, g1, o_f, o_b, proj, attn_o, proj, proj, w_ret_o, w_attn_o, w_out)


def _router_kernel(x_ref, n_ref, sc_ref, sh_ref, wr_ref, br_ref, h_ref, ti_ref, tw_ref):
    h = _rms(x_ref[...]) * n_ref[...] * (1.0 + sc_ref[0]) + sh_ref[0]
    h_ref[...] = _pack_pairs(h)
    logits = lax.dot_general(wr_ref[...], h, (((1,), (1,)), ((), ())), preferred_element_type=F32,
                             precision=lax.Precision.HIGHEST) + br_ref[...]
    eid = lax.broadcasted_iota(jnp.int32, logits.shape, 0)
    vals = logits
    top_v = []
    top_i = []
    for _ in range(TOP_K):
        m = jnp.max(vals, axis=0, keepdims=True)
        idx = jnp.min(jnp.where(vals == m, eid, N_EXPERTS), axis=0, keepdims=True)
        top_v.append(m)
        top_i.append(idx)
        vals = jnp.where(eid == idx, -jnp.inf, vals)
    ex = [jnp.exp(v - top_v[0]) for v in top_v]
    denom = ex[0] + ex[1] + ex[2] + ex[3]
    ti_ref[...] = jnp.concatenate(top_i, axis=0)
    tw_ref[...] = jnp.concatenate([e / denom for e in ex], axis=0)


def _router(x1, norm, sc, sh, w_router_t, b_router, seq):
    t = x1.shape[0]
    tm = _largest_tile(seq, 1024, LANES)
    per_b = seq // tm
    mod_map = lambda i: (i // per_b, 0, 0)
    return pl.pallas_call(
        _router_kernel,
        grid=(t // tm,),
        in_specs=[pl.BlockSpec((tm, D_MODEL), lambda i: (i, 0)),
                  pl.BlockSpec((1, D_MODEL), lambda i: (0, 0)),
                  pl.BlockSpec((1, 1, D_MODEL), mod_map),
                  pl.BlockSpec((1, 1, D_MODEL), mod_map),
                  pl.BlockSpec((N_EXPERTS, D_MODEL), lambda i: (0, 0)),
                  pl.BlockSpec((N_EXPERTS, 1), lambda i: (0, 0))],
        out_specs=[pl.BlockSpec((tm, PACK_W), lambda i: (i, 0)),
                   pl.BlockSpec((TOP_K, tm), lambda i: (0, i)),
                   pl.BlockSpec((TOP_K, tm), lambda i: (0, i))],
        out_shape=[jax.ShapeDtypeStruct((t, PACK_W), PACK_DTYPE),
                   jax.ShapeDtypeStruct((TOP_K, t), jnp.int32),
                   jax.ShapeDtypeStruct((TOP_K, t), F32)],
        compiler_params=_params(1),
        name="router",
    )(x1, norm, sc, sh, w_router_t, b_router)


def _gather_rows(src, idx):
    n, w = src.shape
    r = idx.shape[0]
    workers = SC_CORES * SC_SUBCORES
    per_w = r // workers
    n_win = per_w // SC_GATHER_ROWS
    assert per_w * workers == r and n_win * SC_GATHER_ROWS == per_w, (r, workers, SC_GATHER_ROWS)
    mesh = plsc.VectorSubcoreMesh(core_axis_name="c", subcore_axis_name="s")

    @functools.partial(
        pl.kernel, mesh=mesh, out_type=jax.ShapeDtypeStruct((r, w), src.dtype),
        scratch_types=[pltpu.VMEM((SC_GATHER_ROWS,), jnp.int32),
                       pltpu.VMEM((SC_GATHER_ROWS, w), src.dtype),
                       pltpu.SemaphoreType.DMA])
    def gather(src_hbm, idx_hbm, out_hbm, idx_v, rows_v, sem):
        wid = lax.axis_index("s") * SC_CORES + lax.axis_index("c")
        base = wid * per_w

        @pl.loop(0, n_win)
        def _(win):
            off = base + win * SC_GATHER_ROWS
            pltpu.sync_copy(idx_hbm.at[pl.ds(off, SC_GATHER_ROWS)], idx_v)
            pltpu.async_copy(src_hbm.at[idx_v], rows_v, sem).wait()
            pltpu.sync_copy(rows_v, out_hbm.at[pl.ds(off, SC_GATHER_ROWS)])

    return gather(src, idx)


def _ffn_kernel(te_ref, tf_ref, x_ref, w1_ref, b1_ref, w2_ref, b2_ref, o_ref, w1_sc, w2_sc):
    flag = tf_ref[pl.program_id(0)]

    @pl.when(flag == TILE_FIRST)
    def _():
        w1_sc[...] = w1_ref[0].astype(BF16)
        w2_sc[...] = w2_ref[0].astype(BF16)

    @pl.when(flag == TILE_PAD)
    def _():
        o_ref[...] = jnp.zeros(o_ref.shape, o_ref.dtype)

    @pl.when(flag != TILE_PAD)
    def _():
        lo, hi = _unpack_pairs(x_ref[...])
        x = jnp.concatenate([lo, hi], axis=1).astype(BF16)
        a = jnp.dot(x, w1_sc[...], preferred_element_type=F32) + b1_ref[0]
        gate = jnp.minimum(a[:, :EXPERT_FF], SWIGLU_LIMIT)
        up = jnp.clip(a[:, EXPERT_FF:], -SWIGLU_LIMIT, SWIGLU_LIMIT)
        act = gate * jax.nn.sigmoid(SWIGLU_ALPHA * gate) * (up + 1.0)
        y = jnp.dot(act.astype(BF16), w2_sc[...], preferred_element_type=F32) + b2_ref[0]
        o_ref[...] = _pack_pairs(y)


def _ffn(xs, tile_expert, tile_flag, w1, b1, w2, b2, tg):
    p = xs.shape[0]
    grid_spec = pltpu.PrefetchScalarGridSpec(
        num_scalar_prefetch=2,
        grid=(p // tg,),
        in_specs=[pl.BlockSpec((tg, PACK_W), lambda j, te, tv: (j, 0)),
                  pl.BlockSpec((1, D_MODEL, 2 * EXPERT_FF), lambda j, te, tv: (te[j], 0, 0)),
                  pl.BlockSpec((1, 1, 2 * EXPERT_FF), lambda j, te, tv: (te[j], 0, 0)),
                  pl.BlockSpec((1, EXPERT_FF, D_MODEL), lambda j, te, tv: (te[j], 0, 0)),
                  pl.BlockSpec((1, 1, D_MODEL), lambda j, te, tv: (te[j], 0, 0))],
        out_specs=pl.BlockSpec((tg, PACK_W), lambda j, te, tv: (j, 0)),
        scratch_shapes=[pltpu.VMEM((D_MODEL, 2 * EXPERT_FF), BF16),
                        pltpu.VMEM((EXPERT_FF, D_MODEL), BF16)],
    )
    return pl.pallas_call(
        _ffn_kernel,
        grid_spec=grid_spec,
        out_shape=jax.ShapeDtypeStruct((p, PACK_W), PACK_DTYPE),
        compiler_params=_params(1),
        name="ffn",
    )(tile_expert, tile_flag, xs, w1, b1, w2, b2)


def _combine_kernel(x_ref, g2_ref, w_ref, y_ref, o_ref):
    w = w_ref[...]
    acc_lo = None
    for k in range(TOP_K):
        lo, hi = _unpack_pairs(y_ref[k])
        wk = w[:, k:k + 1]
        acc_lo = wk * lo if acc_lo is None else acc_lo + wk * lo
        acc_hi = wk * hi if k == 0 else acc_hi + wk * hi
    acc = jnp.concatenate([acc_lo, acc_hi], axis=1)
    o_ref[...] = x_ref[...] + g2_ref[0] * acc


def _combine_into_kernel(x_ref, g2_ref, w_ref, y_ref, prev_ref, o_ref):
    del prev_ref
    _combine_kernel(x_ref, g2_ref, w_ref, y_ref, o_ref)


def _combine(x1, g2, w_tok, yk, seq, first_row, prev):
    t = x1.shape[0]
    rows = yk.shape[1]
    tm = _largest_tile(seq, 512, 8)
    per_b = seq // tm
    i0 = first_row // tm
    in_specs = [pl.BlockSpec((tm, D_MODEL), lambda i: (i0 + i, 0)),
                pl.BlockSpec((1, 1, D_MODEL), lambda i: ((i0 + i) // per_b, 0, 0)),
                pl.BlockSpec((tm, TOP_K), lambda i: (i0 + i, 0)),
                pl.BlockSpec((TOP_K, tm, PACK_W), lambda i: (0, i, 0))]
    args = [x1, g2, w_tok, yk]
    if prev is not None:
        in_specs.append(pl.BlockSpec(memory_space=pl.ANY))
        args.append(prev)
    return pl.pallas_call(
        _combine_kernel if prev is None else _combine_into_kernel,
        grid=(rows // tm,),
        in_specs=in_specs,
        out_specs=pl.BlockSpec((tm, D_MODEL), lambda i: (i0 + i, 0)),
        out_shape=jax.ShapeDtypeStruct((t, D_MODEL), F32),
        input_output_aliases={} if prev is None else {len(args) - 1: 0},
        compiler_params=_params(1),
        name="combine",
    )(*args)


def _plan(top_i, tg):
    t = top_i.shape[1]
    n_assign = TOP_K * t
    p = n_assign + N_EXPERTS * tg
    n_tiles = p // tg
    flat_e = top_i.reshape(-1)
    sorted_e, sorted_a = lax.sort_key_val(flat_e, jnp.arange(n_assign, dtype=jnp.int32))
    experts = jnp.arange(N_EXPERTS + 1, dtype=jnp.int32)
    bounds = jnp.sum((sorted_e[None, :] < experts[:, None]).astype(jnp.int32), axis=1)
    start = bounds[:-1]
    counts = bounds[1:] - start
    padded = ((counts + tg - 1) // tg) * tg
    off_end = jnp.cumsum(padded)
    off = off_end - padded
    dest_sorted = off[sorted_e] + jnp.arange(n_assign, dtype=jnp.int32) - start[sorted_e]
    _, dest = lax.sort_key_val(sorted_a, dest_sorted)
    tile_start = jnp.arange(n_tiles, dtype=jnp.int32) * tg
    tile_valid = (tile_start < off_end[-1]).astype(jnp.int32)
    te = jnp.sum((tile_start[:, None] >= off_end[None, :]).astype(jnp.int32), axis=1)
    last_e = jnp.sum(((off_end[-1] - 1) >= off_end).astype(jnp.int32))
    tile_expert = jnp.where(tile_valid != 0, te, last_e)
    slot = jnp.arange(p, dtype=jnp.int32)
    e_slot = jnp.repeat(tile_expert, tg)
    r = slot - off[e_slot]
    ok = (r < counts[e_slot]) & (jnp.repeat(tile_valid, tg) != 0)
    a_slot = sorted_a[jnp.clip(start[e_slot] + r, 0, n_assign - 1)]
    src_tok = jnp.where(ok, a_slot % t, slot % t).astype(jnp.int32)
    changed = jnp.concatenate([jnp.ones((1,), jnp.bool_), tile_expert[1:] != tile_expert[:-1]])
    tile_flag = jnp.where(tile_valid != 0, jnp.where(changed, TILE_FIRST, TILE_BODY), TILE_PAD).astype(jnp.int32)
    return src_tok, dest, tile_expert, tile_flag


def kernel(x, c, ctx, c_ctx, norm1, norm2, w_ada, b_ada, w_in, ret_decay_f, ret_decay_b, attn_q_norm, attn_k_norm,
           w_ret_o, w_attn_o, w_out, w_router, b_router, w_exp_in, b_exp_in, w_exp_out, b_exp_out):
    assert w_in.shape[0] == 1, "single-layer block"
    b, seq, d = x.shape
    n_ctx = ctx.shape[1]
    t = b * seq
    rows = seq // GRID_W

    idx = np.cumsum(IN_SIZES)[:-1].tolist()
    wq_r, wk_r, wv_r, wg_r, wq_a, wk_a, wv_a, wgt_r, wgt_a = jnp.split(w_in[0], idx, axis=-1)
    w_in_p = jnp.concatenate([wq_r, wk_r, wv_r, wg_r, wq_a, wgt_r, wgt_a, wk_a, wv_a], axis=-1)
    w1 = w_exp_in[0]
    w2 = w_exp_out[0]
    b1 = b_exp_in[0].reshape(N_EXPERTS, 1, 2 * EXPERT_FF)
    b2 = b_exp_out[0].reshape(N_EXPERTS, 1, D_MODEL)

    pad = (-(b + 1)) % 8
    c_all = jnp.concatenate([c, c_ctx[None, :], jnp.zeros((pad, d), F32)], axis=0)
    mod = _ada(c_all, w_ada[0], b_ada[0])
    sh1, sc1, g1, sh2, sc2, g2 = [m.reshape(-1, 1, d) for m in jnp.split(mod, 6, axis=-1)]
    lat = lambda m: m[:b]
    cx = lambda m: m[b:b + 1]

    x2 = x.reshape(t, d)
    proj = _inproj(x2, norm1, lat(sc1), lat(sh1), w_in_p, seq)
    proj_c = _inproj(ctx.reshape(b * n_ctx, d), norm1, cx(sc1), cx(sh1), w_in_p, n_ctx)

    cos_r, sin_r = _rope_tables(rows, RET_QK_DIM)
    cos_a, sin_a = _rope_tables(rows, ATTN_HEAD_DIM)
    lg_f = -jax.nn.softplus(ret_decay_f[0].astype(F32))
    lg_b = -jax.nn.softplus(ret_decay_b[0].astype(F32))
    zero_state = jnp.zeros((2, b, RET_HEADS, RET_QK_DIM, RET_V_DIM), F32)
    cos_c = jnp.ones((n_ctx, RET_QK_DIM), F32)
    sin_c = jnp.zeros((n_ctx, RET_QK_DIM), F32)
    _, _, s_ctx = _retention(proj_c, cos_c, sin_c, lg_f, lg_b, zero_state, b, n_ctx, False)
    o_f, o_b, _ = _retention(proj, cos_r, sin_r, lg_f, lg_b, s_ctx, b, seq, True)

    k_lat = _kprep(proj, attn_k_norm, cos_a, sin_a, seq, True)
    k_ctx = _kprep(proj_c, attn_k_norm, cos_a, sin_a, n_ctx, False)
    k_all = jnp.concatenate([k_ctx.reshape(b, n_ctx, ATTN_KV_W), k_lat.reshape(b, seq, ATTN_KV_W)], axis=1)
    v_all = jnp.concatenate([proj_c[:, COL_VA:COL_VA + ATTN_KV_W].reshape(b, n_ctx, ATTN_KV_W),
                             proj[:, COL_VA:COL_VA + ATTN_KV_W].reshape(b, seq, ATTN_KV_W)], axis=1)
    lk = n_ctx + seq
    vt = jnp.swapaxes(v_all.reshape(b, lk, ATTN_KV_HEADS, ATTN_HEAD_DIM), 1, 3).swapaxes(1, 2)
    vt_all = jnp.concatenate([vt, jnp.ones((b, ATTN_KV_HEADS, ONES_ROWS, lk), vt.dtype)], axis=2)
    vt_all = vt_all.reshape(b, ATTN_KV_HEADS * (ATTN_HEAD_DIM + ONES_ROWS), lk)
    attn_o = _attention(proj, attn_q_norm, cos_a, sin_a, k_all, vt_all, b, seq)

    x1 = _merge(x2, lat(g1), o_f, o_b, proj, attn_o, w_ret_o[0].astype(BF16), w_attn_o[0].astype(BF16),
                w_out[0].astype(BF16), seq)

    h2, top_i, top_w = _router(x1, norm2, lat(sc2), lat(sh2), w_router[0].T, b_router[0].reshape(N_EXPERTS, 1), seq)
    n_groups = MOE_GROUPS if b % MOE_GROUPS == 0 else 1
    tgrp = t // n_groups
    tg = _largest_tile(TOP_K * tgrp, 512, 8)
    w_tok = top_w.T
    out = None
    for grp in range(n_groups):
        first = grp * tgrp
        src_tok, dest, tile_expert, tile_flag = _plan(top_i[:, first:first + tgrp], tg)
        xs = _gather_rows(h2, src_tok + first)
        ys = _ffn(xs, tile_expert, tile_flag, w1, b1, w2, b2, tg)
        yk = _gather_rows(ys, dest).reshape(TOP_K, tgrp, PACK_W)
        out = _combine(x1, lat(g2), w_tok, yk, seq, first, out)
    return out.reshape(b, seq, d)
```

```python
import functools

import jax
import jax.numpy as jnp
import numpy as np
from jax import lax
from jax.experimental import pallas as pl
from jax.experimental.pallas import tpu as pltpu
from jax.experimental.pallas import tpu_sc as plsc

F32 = jnp.float32
BF16 = jnp.bfloat16

D_MODEL = 1024
GRID_W = 64
EPS = 1e-6
RET_HEADS = 4
RET_QK_DIM = 256
RET_V_DIM = 512
ATTN_HEADS = 8
ATTN_KV_HEADS = 2
ATTN_GROUP = ATTN_HEADS // ATTN_KV_HEADS
ATTN_HEAD_DIM = 128
ROPE_THETA = 10000.0
N_EXPERTS = 32
TOP_K = 4
EXPERT_FF = 1024
SWIGLU_LIMIT = 7.0
SWIGLU_ALPHA = 1.702

RET_QK_W = RET_HEADS * RET_QK_DIM
RET_V_W = RET_HEADS * RET_V_DIM
ATTN_Q_W = ATTN_HEADS * ATTN_HEAD_DIM
ATTN_KV_W = ATTN_KV_HEADS * ATTN_HEAD_DIM
IN_SIZES = (RET_QK_W, RET_QK_W, RET_V_W, RET_V_W, ATTN_Q_W, ATTN_KV_W, ATTN_KV_W, D_MODEL, D_MODEL)
IN_WIDTH = sum(IN_SIZES)
COL_QR = 0
COL_KR = COL_QR + RET_QK_W
COL_VR = COL_KR + RET_QK_W
COL_GR = COL_VR + RET_V_W
COL_QA = COL_GR + RET_V_W
COL_GATE_R = COL_QA + ATTN_Q_W
COL_GATE_A = COL_GATE_R + D_MODEL
COL_KA = COL_GATE_A + D_MODEL
COL_VA = COL_KA + ATTN_KV_W

RET_CHUNK = 256
RET_HEADS_PER_STEP = 2
LANES = 128
SC_CORES = 2
SC_SUBCORES = 16
SC_GATHER_ROWS = 128
MXU_DEPTH = 256
ATTN_KEY_CHUNK = 768
ATTN_Q_TILE = 512
ATTN_EXP_SLAB = 32
ONES_ROWS = 16
PACK_DTYPE = jnp.int32
PACK_W = D_MODEL // 2
MOE_GROUPS = 2
TILE_PAD, TILE_BODY, TILE_FIRST = 0, 1, 2
VMEM_LIMIT = 56 * 1024 * 1024

ARB = pltpu.ARBITRARY


def _params(n_axes, **kw):
    return pltpu.CompilerParams(dimension_semantics=(ARB,) * n_axes, vmem_limit_bytes=VMEM_LIMIT, **kw)


def _largest_tile(n, cap, mult):
    best = None
    for t in range(mult, min(n, cap) + 1, mult):
        if n % t == 0:
            best = t
    assert best is not None, (n, cap, mult)
    return best


def _rms(x):
    return x * lax.rsqrt(jnp.mean(x * x, axis=-1, keepdims=True) + EPS)


def _pack_pairs(x):
    half = x.shape[1] // 2
    lo = lax.bitcast_convert_type(x[:, :half].astype(BF16).astype(F32), jnp.int32)
    hi = lax.bitcast_convert_type(x[:, half:].astype(BF16).astype(F32), jnp.int32)
    return lax.bitwise_or(lax.bitwise_and(hi, jnp.int32(-65536)), lax.shift_right_logical(lo, jnp.int32(16)))


def _unpack_pairs(w):
    lo = lax.bitcast_convert_type(lax.shift_left(w, jnp.int32(16)), F32)
    hi = lax.bitcast_convert_type(lax.bitwise_and(w, jnp.int32(-65536)), F32)
    return lo, hi


def _rope_block(x, cos, sin, half):
    if 2 * half == LANES:
        swapped = pltpu.roll(x, half, 1)
    else:
        lane = lax.broadcasted_iota(jnp.int32, x.shape, 1)
        first = (lane % (2 * half)) < half
        swapped = jnp.where(first, pltpu.roll(x, LANES - half, 1), pltpu.roll(x, half, 1))
    return x * cos + swapped * sin


def _rope_tables(rows, head_dim):
    row_id = jnp.broadcast_to(jnp.arange(rows, dtype=F32)[:, None], (rows, GRID_W)).reshape(-1)
    col_id = jnp.broadcast_to(jnp.arange(GRID_W, dtype=F32)[None, :], (rows, GRID_W)).reshape(-1)
    n_freq = head_dim // 4
    inv_freq = ROPE_THETA ** (-jnp.arange(n_freq, dtype=F32) / n_freq)
    ang_r = row_id[:, None] * inv_freq
    ang_c = col_id[:, None] * inv_freq
    cos = jnp.concatenate([jnp.cos(ang_r), jnp.cos(ang_r), jnp.cos(ang_c), jnp.cos(ang_c)], axis=1)
    sin = jnp.concatenate([-jnp.sin(ang_r), jnp.sin(ang_r), -jnp.sin(ang_c), jnp.sin(ang_c)], axis=1)
    return cos, sin


def _ada_kernel(c_ref, w_ref, b_ref, o_ref):
    c = c_ref[...]
    s = c * jax.nn.sigmoid(c)
    o_ref[...] = jnp.dot(s, w_ref[...], preferred_element_type=F32,
                         precision=lax.Precision.HIGHEST) + b_ref[...]


def _ada(c_pad, w_ada, b_ada):
    rows = c_pad.shape[0]
    n = w_ada.shape[1]
    tn = _largest_tile(n, 1536, LANES)
    return pl.pallas_call(
        _ada_kernel,
        grid=(n // tn,),
        in_specs=[pl.BlockSpec((rows, D_MODEL), lambda j: (0, 0)),
                  pl.BlockSpec((D_MODEL, tn), lambda j: (0, j)),
                  pl.BlockSpec((1, tn), lambda j: (0, j))],
        out_specs=pl.BlockSpec((rows, tn), lambda j: (0, j)),
        out_shape=jax.ShapeDtypeStruct((rows, n), F32),
        compiler_params=_params(1),
        name="ada",
    )(c_pad, w_ada, b_ada.reshape(1, n))


def _inproj_kernel(x_ref, n_ref, sc_ref, sh_ref, w_ref, o_ref, w_sc):
    @pl.when(pl.program_id(1) == 0)
    def _():
        w_sc[...] = w_ref[...].astype(BF16)

    y = _rms(x_ref[...]) * n_ref[...]
    h = (y * (1.0 + sc_ref[0]) + sh_ref[0]).astype(BF16)
    o_ref[...] = jnp.dot(h, w_sc[...], preferred_element_type=F32).astype(o_ref.dtype)


def _inproj(x2, norm, sc, sh, w, rows_per_batch):
    t = x2.shape[0]
    n = w.shape[1]
    tm = _largest_tile(rows_per_batch, 512, 8)
    tn = _largest_tile(n, 2560, LANES)
    per_b = rows_per_batch // tm
    if sc.shape[0] == 1:
        mod_map = lambda j, i: (0, 0, 0)
    else:
        mod_map = lambda j, i: (i // per_b, 0, 0)
    return pl.pallas_call(
        _inproj_kernel,
        grid=(n // tn, t // tm),
        in_specs=[pl.BlockSpec((tm, D_MODEL), lambda j, i: (i, 0)),
                  pl.BlockSpec((1, D_MODEL), lambda j, i: (0, 0)),
                  pl.BlockSpec((1, 1, D_MODEL), mod_map),
                  pl.BlockSpec((1, 1, D_MODEL), mod_map),
                  pl.BlockSpec((D_MODEL, tn), lambda j, i: (0, j))],
        out_specs=pl.BlockSpec((tm, tn), lambda j, i: (i, j)),
        out_shape=jax.ShapeDtypeStruct((t, n), BF16),
        scratch_shapes=[pltpu.VMEM((D_MODEL, tn), BF16)],
        compiler_params=_params(2),
        name="inproj",
    )(x2, norm, sc, sh, w)


def _kprep_kernel(k_ref, g_ref, cos_ref, sin_ref, o_ref, *, use_rope):
    k = k_ref[...].astype(F32)
    for g in range(ATTN_KV_HEADS):
        sl = slice(g * ATTN_HEAD_DIM, (g + 1) * ATTN_HEAD_DIM)
        kh = _rms(k[:, sl]) * g_ref[...]
        if use_rope:
            kh = _rope_block(kh, cos_ref[...], sin_ref[...], ATTN_HEAD_DIM // 4)
        o_ref[:, sl] = kh.astype(BF16)


def _kprep(proj, gain, cos, sin, rows_per_batch, use_rope):
    t = proj.shape[0]
    tm = _largest_tile(rows_per_batch, 1024, 8)
    per_b = rows_per_batch // tm
    return pl.pallas_call(
        functools.partial(_kprep_kernel, use_rope=use_rope),
        grid=(t // tm,),
        in_specs=[pl.BlockSpec((tm, ATTN_KV_W), lambda i: (i, COL_KA // ATTN_KV_W)),
                  pl.BlockSpec((1, ATTN_HEAD_DIM), lambda i: (0, 0)),
                  pl.BlockSpec((tm, ATTN_HEAD_DIM), lambda i: (i % per_b, 0)),
                  pl.BlockSpec((tm, ATTN_HEAD_DIM), lambda i: (i % per_b, 0))],
        out_specs=pl.BlockSpec((tm, ATTN_KV_W), lambda i: (i, 0)),
        out_shape=jax.ShapeDtypeStruct((t, ATTN_KV_W), BF16),
        compiler_params=_params(1),
        name="kprep",
    )(proj, gain, cos, sin)


def _ret_kernel(dch_ref, qf_ref, kf_ref, vf_ref, cosf_ref, sinf_ref, qb_ref, kb_ref, vb_ref, cosb_ref, sinb_ref,
                dmat_ref, din_ref, dout_ref, s0_ref, of_ref, ob_ref, sfin_ref, s_sc, *, use_rope, n_chunks, heads):
    hb = pl.program_id(1)
    c = pl.program_id(2)

    @pl.when(c == 0)
    def _():
        s_sc[...] = s0_ref[:, 0]

    def rope(x, cos, sin):
        return jnp.concatenate([_rope_block(x[:, i * LANES:(i + 1) * LANES], cos[:, i * LANES:(i + 1) * LANES],
                                            sin[:, i * LANES:(i + 1) * LANES], RET_QK_DIM // 4)
                                for i in range(RET_QK_DIM // LANES)], axis=1)

    sides = ((qf_ref, kf_ref, vf_ref, cosf_ref, sinf_ref), (qb_ref, kb_ref, vb_ref, cosb_ref, sinb_ref))
    chains = [(d, j) for d in range(2) for j in range(heads)]
    qs, ks, vs, ss = {}, {}, {}, {}
    for d, j in chains:
        q_ref, k_ref, v_ref, cos_ref, sin_ref = sides[d]
        q = q_ref[:, j * RET_QK_DIM:(j + 1) * RET_QK_DIM].astype(F32)
        k = k_ref[:, j * RET_QK_DIM:(j + 1) * RET_QK_DIM].astype(F32)
        if use_rope:
            q = rope(q, cos_ref[...], sin_ref[...])
            k = rope(k, cos_ref[...], sin_ref[...])
        qs[d, j], ks[d, j] = q, k
        vs[d, j] = v_ref[:, j * RET_V_DIM:(j + 1) * RET_V_DIM]
        ss[d, j] = s_sc[d, j]
    scores = {ch: lax.dot_general(qs[ch].astype(BF16), ks[ch].astype(BF16), (((1,), (1,)), ((), ())),
                                  preferred_element_type=F32) * dmat_ref[ch[0], ch[1]] for ch in chains}
    inter = {ch: jnp.dot((qs[ch] * din_ref[ch[0], ch[1]]).astype(BF16), ss[ch].astype(BF16),
                         preferred_element_type=F32) for ch in chains}
    intra = {ch: jnp.dot(scores[ch].astype(BF16), vs[ch], preferred_element_type=F32) for ch in chains}
    o_refs = (of_ref, ob_ref)
    for d, j in chains:
        o_refs[d][:, j * RET_V_DIM:(j + 1) * RET_V_DIM] = (intra[d, j] + inter[d, j]).astype(of_ref.dtype)
    s_new = {ch: ss[ch] * dch_ref[ch[0] * RET_HEADS + hb * heads + ch[1]]
             + jnp.dot((ks[ch] * dout_ref[ch[0], ch[1]]).T.astype(BF16), vs[ch], preferred_element_type=F32)
             for ch in chains}
    for d, j in chains:
        s_sc[d, j] = s_new[d, j]

    @pl.when(c == n_chunks - 1)
    def _():
        for d, j in chains:
            sfin_ref[d, 0, j] = s_new[d, j]


def _ret_tables(log_gamma, chunk, reverse, k_scale):
    pos = jnp.arange(chunk, dtype=F32)
    diff = pos[:, None] - pos[None, :]
    if reverse:
        diff = -diff
        mask = diff > 0
        p_in = chunk - pos
        p_out = pos
    else:
        mask = diff >= 0
        p_in = pos + 1.0
        p_out = chunk - 1.0 - pos
    lg = log_gamma.astype(F32)
    dmat = jnp.where(mask[None], jnp.exp(lg[:, None, None] * jnp.maximum(diff, 0.0)[None]), 0.0) * k_scale
    d_in = jnp.exp(lg[:, None] * p_in)
    d_out = jnp.exp(lg[:, None] * p_out) * k_scale
    d_in = jnp.broadcast_to(d_in[:, :, None], (RET_HEADS, chunk, RET_QK_DIM))
    d_out = jnp.broadcast_to(d_out[:, :, None], (RET_HEADS, chunk, RET_QK_DIM))
    d_chunk = jnp.exp(lg * chunk)
    return d_chunk, dmat, d_in, d_out


def _retention(proj, cos, sin, lg_f, lg_b, s0, batch, seq, use_rope):
    chunk = min(RET_CHUNK, seq)
    nc = seq // chunk
    hs = RET_HEADS_PER_STEP
    qw, vw = hs * RET_QK_DIM, hs * RET_V_DIM
    k_scale = RET_QK_DIM ** -0.5
    tabs = [_ret_tables(lg_f, chunk, False, k_scale), _ret_tables(lg_b, chunk, True, k_scale)]
    d_chunk, dmat, d_in, d_out = [jnp.stack([tabs[0][i], tabs[1][i]]) for i in range(4)]
    fwd = lambda c: c
    bwd = lambda c: nc - 1 - c

    def side(cidx):
        row = lambda b, c: b * nc + cidx(c)
        return [pl.BlockSpec((chunk, qw), lambda b, h, c: (row(b, c), COL_QR // qw + h)),
                pl.BlockSpec((chunk, qw), lambda b, h, c: (row(b, c), COL_KR // qw + h)),
                pl.BlockSpec((chunk, vw), lambda b, h, c: (row(b, c), COL_VR // vw + h)),
                pl.BlockSpec((chunk, RET_QK_DIM), lambda b, h, c: (cidx(c), 0)),
                pl.BlockSpec((chunk, RET_QK_DIM), lambda b, h, c: (cidx(c), 0))]

    state_spec = pl.BlockSpec((2, 1, hs, RET_QK_DIM, RET_V_DIM), lambda b, h, c: (0, b, h, 0, 0))
    return pl.pallas_call(
        functools.partial(_ret_kernel, use_rope=use_rope, n_chunks=nc, heads=hs),
        grid=(batch, RET_HEADS // hs, nc),
        in_specs=[pl.BlockSpec(memory_space=pltpu.SMEM)] + side(fwd) + side(bwd) + [
            pl.BlockSpec((2, hs, chunk, chunk), lambda b, h, c: (0, h, 0, 0)),
            pl.BlockSpec((2, hs, chunk, RET_QK_DIM), lambda b, h, c: (0, h, 0, 0)),
            pl.BlockSpec((2, hs, chunk, RET_QK_DIM), lambda b, h, c: (0, h, 0, 0)),
            state_spec],
        out_specs=[pl.BlockSpec((chunk, vw), lambda b, h, c: (b * nc + c, h)),
                   pl.BlockSpec((chunk, vw), lambda b, h, c: (b * nc + nc - 1 - c, h)),
                   state_spec],
        out_shape=[jax.ShapeDtypeStruct((batch * seq, RET_V_W), BF16),
                   jax.ShapeDtypeStruct((batch * seq, RET_V_W), BF16),
                   jax.ShapeDtypeStruct((2, batch, RET_HEADS, RET_QK_DIM, RET_V_DIM), F32)],
        scratch_shapes=[pltpu.VMEM((2, hs, RET_QK_DIM, RET_V_DIM), F32)],
        compiler_params=_params(3),
        name="ret",
    )(d_chunk.reshape(-1), proj, proj, proj, cos, sin, proj, proj, proj, cos, sin, dmat, d_in, d_out, s0)


def _attn_kernel(q_ref, g_ref, cos_ref, sin_ref, k_ref, vt_ref, o_ref, qt_sc, m_sc, acc_sc,
                 s0_sc, s1_sc, p0_sc, p1_sc, a0_sc, a1_sc, *, tq, kc, n_chunks):
    s_bufs = (s0_sc, s1_sc)
    p_bufs = (p0_sc, p1_sc)
    a_bufs = (a0_sc, a1_sc)

    q = q_ref[...].astype(F32)
    scale = ATTN_HEAD_DIM ** -0.5 * np.log2(np.e)
    for g in range(ATTN_GROUP):
        qh = _rms(q[:, g * ATTN_HEAD_DIM:(g + 1) * ATTN_HEAD_DIM]) * g_ref[...]
        qh = _rope_block(qh, cos_ref[...], sin_ref[...], ATTN_HEAD_DIM // 4) * scale
        qt_sc[:, g * tq:(g + 1) * tq] = qh.T.astype(BF16)
    m_sc[...] = jnp.full(m_sc.shape, -jnp.inf, F32)
    acc_sc[...] = jnp.zeros(acc_sc.shape, F32)

    def stage_s(c, slot):
        off = pl.multiple_of(c * kc, kc)
        s_bufs[slot][...] = jnp.dot(k_ref[0, pl.ds(off, kc), :], qt_sc[...], preferred_element_type=F32)

    def stage_f(slot):
        m_prev = m_sc[...]
        m_new = jnp.maximum(m_prev, jnp.max(s_bufs[slot][...], axis=0, keepdims=True))
        m_sc[...] = m_new
        a_bufs[slot][...] = jnp.exp2(m_prev - m_new)
        for r in range(0, kc, ATTN_EXP_SLAB):
            p_bufs[slot][r:r + ATTN_EXP_SLAB, :] = jnp.exp2(
                s_bufs[slot][r:r + ATTN_EXP_SLAB, :] - m_new).astype(BF16)

    def stage_a(c, slot):
        off = pl.multiple_of(c * kc, kc)
        acc_sc[...] = a_bufs[slot][...] * acc_sc[...] + jnp.dot(
            vt_ref[0, :, pl.ds(off, kc)], p_bufs[slot][...], preferred_element_type=F32)

    def tick(t, parity):
        stage_s(t, parity)
        stage_f(1 - parity)
        stage_a(t - 2, parity)

    n = n_chunks
    stage_s(0, 0)
    if n > 1:
        stage_s(1, 1)
    stage_f(0)
    first = 2
    if n > 2 and (n - 2) % 2 == 1:
        tick(2, 0)
        first = 3
    n_pairs = (n - first) // 2 if n > first else 0
    if n_pairs > 0:
        def pair(u, carry):
            t = first + 2 * u
            tick(t, first % 2)
            tick(t + 1, 1 - first % 2)
            return carry
        lax.fori_loop(0, n_pairs, pair, 0)
    if n > 1:
        stage_f((n - 1) % 2)
        stage_a(n - 2, (n - 2) % 2)
    stage_a(n - 1, (n - 1) % 2)

    acc = acc_sc[...]
    o = acc[:ATTN_HEAD_DIM] / acc[ATTN_HEAD_DIM:ATTN_HEAD_DIM + 1]
    for g in range(ATTN_GROUP):
        o_ref[:, g * ATTN_HEAD_DIM:(g + 1) * ATTN_HEAD_DIM] = o[:, g * tq:(g + 1) * tq].T.astype(o_ref.dtype)


def _attention(proj, gain, cos, sin, k_all, vt_all, batch, seq):
    lk = k_all.shape[1]
    kc = _largest_tile(lk, ATTN_KEY_CHUNK, MXU_DEPTH)
    tq = _largest_tile(seq, ATTN_Q_TILE, 8)
    nq = seq // tq
    qw = ATTN_GROUP * ATTN_HEAD_DIM
    cols = ATTN_GROUP * tq
    vrows = ATTN_HEAD_DIM + ONES_ROWS
    return pl.pallas_call(
        functools.partial(_attn_kernel, tq=tq, kc=kc, n_chunks=lk // kc),
        grid=(batch, ATTN_KV_HEADS, nq),
        in_specs=[pl.BlockSpec((tq, qw), lambda b, g, i: (b * nq + i, COL_QA // qw + g)),
                  pl.BlockSpec((1, ATTN_HEAD_DIM), lambda b, g, i: (0, 0)),
                  pl.BlockSpec((tq, ATTN_HEAD_DIM), lambda b, g, i: (i, 0)),
                  pl.BlockSpec((tq, ATTN_HEAD_DIM), lambda b, g, i: (i, 0)),
                  pl.BlockSpec((1, lk, ATTN_HEAD_DIM), lambda b, g, i: (b, 0, g)),
                  pl.BlockSpec((1, vrows, lk), lambda b, g, i: (b, g, 0))],
        out_specs=pl.BlockSpec((tq, qw), lambda b, g, i: (b * nq + i, g)),
        out_shape=jax.ShapeDtypeStruct((batch * seq, ATTN_Q_W), BF16),
        scratch_shapes=[pltpu.VMEM((ATTN_HEAD_DIM, cols), BF16),
                        pltpu.VMEM((1, cols), F32),
                        pltpu.VMEM((vrows, cols), F32),
                        pltpu.VMEM((kc, cols), F32),
                        pltpu.VMEM((kc, cols), F32),
                        pltpu.VMEM((kc, cols), BF16),
                        pltpu.VMEM((kc, cols), BF16),
                        pltpu.VMEM((1, cols), F32),
                        pltpu.VMEM((1, cols), F32)],
        compiler_params=_params(3),
        name="attn",
    )(proj, gain, cos, sin, k_all, vt_all)


def _merge_kernel(x_ref, g1_ref, of_ref, ob_ref, gr_ref, ao_ref, gtr_ref, gta_ref,
                  wro_ref, wao_ref, wo_ref, o_ref):
    ro = of_ref[...].astype(F32) + ob_ref[...].astype(F32)
    gr = gr_ref[...].astype(F32)
    parts = []
    for h in range(RET_HEADS):
        sl = slice(h * RET_V_DIM, (h + 1) * RET_V_DIM)
        g = gr[:, sl]
        parts.append((g * jax.nn.sigmoid(g) * _rms(ro[:, sl])).astype(BF16))
    ret_in = jnp.concatenate(parts, axis=1)
    ret_branch = jnp.dot(ret_in, wro_ref[...], preferred_element_type=F32)
    attn_branch = jnp.dot(ao_ref[...], wao_ref[...], preferred_element_type=F32)
    y = (jax.nn.sigmoid(gtr_ref[...].astype(F32)) * ret_branch
         + jax.nn.sigmoid(gta_ref[...].astype(F32)) * attn_branch)
    y = jnp.dot(y.astype(BF16), wo_ref[...], preferred_element_type=F32)
    o_ref[...] = x_ref[...] + g1_ref[0] * y


def _merge(x2, g1, o_f, o_b, proj, attn_o, w_ret_o, w_attn_o, w_out, seq):
    t = x2.shape[0]
    tm = _largest_tile(seq, 256, 8)
    per_b = seq // tm
    full = lambda i: (0, 0)
    return pl.pallas_call(
        _merge_kernel,
        grid=(t // tm,),
        in_specs=[pl.BlockSpec((tm, D_MODEL), lambda i: (i, 0)),
                  pl.BlockSpec((1, 1, D_MODEL), lambda i: (i // per_b, 0, 0)),
                  pl.BlockSpec((tm, RET_V_W), lambda i: (i, 0)),
                  pl.BlockSpec((tm, RET_V_W), lambda i: (i, 0)),
                  pl.BlockSpec((tm, RET_V_W), lambda i: (i, COL_GR // RET_V_W)),
                  pl.BlockSpec((tm, ATTN_Q_W), lambda i: (i, 0)),
                  pl.BlockSpec((tm, D_MODEL), lambda i: (i, COL_GATE_R // D_MODEL)),
                  pl.BlockSpec((tm, D_MODEL), lambda i: (i, COL_GATE_A // D_MODEL)),
                  pl.BlockSpec((RET_V_W, D_MODEL), full),
                  pl.BlockSpec((ATTN_Q_W, D_MODEL), full),
                  pl.BlockSpec((D_MODEL, D_MODEL), full)],
        out_specs=pl.BlockSpec((tm, D_MODEL), lambda i: (i, 0)),
        out_shape=jax.ShapeDtypeStruct((t, D_MODEL), F32),
        compiler_params=_params(1),
        name="merge",
    )(x2, g1, o_f, o_b, proj, attn_o, proj, proj, w_ret_o, w_attn_o, w_out)


def _router_kernel(x_ref, n_ref, sc_ref, sh_ref, wr_ref, br_ref, h_ref, ti_ref, tw_ref):
    h = _rms(x_ref[...]) * n_ref[...] * (1.0 + sc_ref[0]) + sh_ref[0]
    h_ref[...] = _pack_pairs(h)
    logits = lax.dot_general(wr_ref[...], h, (((1,), (1,)), ((), ())), preferred_element_type=F32,
                             precision=lax.Precision.HIGHEST) + br_ref[...]
    eid = lax.broadcasted_iota(jnp.int32, logits.shape, 0)
    vals = logits
    top_v = []
    top_i = []
    for _ in range(TOP_K):
        m = jnp.max(vals, axis=0, keepdims=True)
        idx = jnp.min(jnp.where(vals == m, eid, N_EXPERTS), axis=0, keepdims=True)
        top_v.append(m)
        top_i.append(idx)
        vals = jnp.where(eid == idx, -jnp.inf, vals)
    ex = [jnp.exp(v - top_v[0]) for v in top_v]
    denom = ex[0] + ex[1] + ex[2] + ex[3]
    ti_ref[...] = jnp.concatenate(top_i, axis=0)
    tw_ref[...] = jnp.concatenate([e / denom for e in ex], axis=0)


def _router(x1, norm, sc, sh, w_router_t, b_router, seq):
    t = x1.shape[0]
    tm = _largest_tile(seq, 1024, LANES)
    per_b = seq // tm
    mod_map = lambda i: (i // per_b, 0, 0)
    return pl.pallas_call(
        _router_kernel,
        grid=(t // tm,),
        in_specs=[pl.BlockSpec((tm, D_MODEL), lambda i: (i, 0)),
                  pl.BlockSpec((1, D_MODEL), lambda i: (0, 0)),
                  pl.BlockSpec((1, 1, D_MODEL), mod_map),
                  pl.BlockSpec((1, 1, D_MODEL), mod_map),
                  pl.BlockSpec((N_EXPERTS, D_MODEL), lambda i: (0, 0)),
                  pl.BlockSpec((N_EXPERTS, 1), lambda i: (0, 0))],
        out_specs=[pl.BlockSpec((tm, PACK_W), lambda i: (i, 0)),
                   pl.BlockSpec((TOP_K, tm), lambda i: (0, i)),
                   pl.BlockSpec((TOP_K, tm), lambda i: (0, i))],
        out_shape=[jax.ShapeDtypeStruct((t, PACK_W), PACK_DTYPE),
                   jax.ShapeDtypeStruct((TOP_K, t), jnp.int32),
                   jax.ShapeDtypeStruct((TOP_K, t), F32)],
        compiler_params=_params(1),
        name="router",
    )(x1, norm, sc, sh, w_router_t, b_router)


def _gather_rows(src, idx):
    n, w = src.shape
    r = idx.shape[0]
    workers = SC_CORES * SC_SUBCORES
    per_w = r // workers
    n_win = per_w // SC_GATHER_ROWS
    assert per_w * workers == r and n_win * SC_GATHER_ROWS == per_w, (r, workers, SC_GATHER_ROWS)
    mesh = plsc.VectorSubcoreMesh(core_axis_name="c", subcore_axis_name="s")

    @functools.partial(
        pl.kernel, mesh=mesh, out_type=jax.ShapeDtypeStruct((r, w), src.dtype),
        scratch_types=[pltpu.VMEM((SC_GATHER_ROWS,), jnp.int32),
                       pltpu.VMEM((SC_GATHER_ROWS, w), src.dtype),
                       pltpu.SemaphoreType.DMA])
    def gather(src_hbm, idx_hbm, out_hbm, idx_v, rows_v, sem):
        wid = lax.axis_index("s") * SC_CORES + lax.axis_index("c")
        base = wid * per_w

        @pl.loop(0, n_win)
        def _(win):
            off = base + win * SC_GATHER_ROWS
            pltpu.sync_copy(idx_hbm.at[pl.ds(off, SC_GATHER_ROWS)], idx_v)
            pltpu.async_copy(src_hbm.at[idx_v], rows_v, sem).wait()
            pltpu.sync_copy(rows_v, out_hbm.at[pl.ds(off, SC_GATHER_ROWS)])

    return gather(src, idx)


def _ffn_kernel(te_ref, tf_ref, x_ref, w1_ref, b1_ref, w2_ref, b2_ref, o_ref, w1_sc, w2_sc):
    flag = tf_ref[pl.program_id(0)]

    @pl.when(flag == TILE_FIRST)
    def _():
        w1_sc[...] = w1_ref[0].astype(BF16)
        w2_sc[...] = w2_ref[0].astype(BF16)

    @pl.when(flag == TILE_PAD)
    def _():
        o_ref[...] = jnp.zeros(o_ref.shape, o_ref.dtype)

    @pl.when(flag != TILE_PAD)
    def _():
        lo, hi = _unpack_pairs(x_ref[...])
        x = jnp.concatenate([lo, hi], axis=1).astype(BF16)
        a = jnp.dot(x, w1_sc[...], preferred_element_type=F32) + b1_ref[0]
        gate = jnp.minimum(a[:, :EXPERT_FF], SWIGLU_LIMIT)
        up = jnp.clip(a[:, EXPERT_FF:], -SWIGLU_LIMIT, SWIGLU_LIMIT)
        act = gate * jax.nn.sigmoid(SWIGLU_ALPHA * gate) * (up + 1.0)
        y = jnp.dot(act.astype(BF16), w2_sc[...], preferred_element_type=F32) + b2_ref[0]
        o_ref[...] = _pack_pairs(y)


def _ffn(xs, tile_expert, tile_flag, w1, b1, w2, b2, tg):
    p = xs.shape[0]
    grid_spec = pltpu.PrefetchScalarGridSpec(
        num_scalar_prefetch=2,
        grid=(p // tg,),
        in_specs=[pl.BlockSpec((tg, PACK_W), lambda j, te, tv: (j, 0)),
                  pl.BlockSpec((1, D_MODEL, 2 * EXPERT_FF), lambda j, te, tv: (te[j], 0, 0)),
                  pl.BlockSpec((1, 1, 2 * EXPERT_FF), lambda j, te, tv: (te[j], 0, 0)),
                  pl.BlockSpec((1, EXPERT_FF, D_MODEL), lambda j, te, tv: (te[j], 0, 0)),
                  pl.BlockSpec((1, 1, D_MODEL), lambda j, te, tv: (te[j], 0, 0))],
        out_specs=pl.BlockSpec((tg, PACK_W), lambda j, te, tv: (j, 0)),
        scratch_shapes=[pltpu.VMEM((D_MODEL, 2 * EXPERT_FF), BF16),
                        pltpu.VMEM((EXPERT_FF, D_MODEL), BF16)],
    )
    return pl.pallas_call(
        _ffn_kernel,
        grid_spec=grid_spec,
        out_shape=jax.ShapeDtypeStruct((p, PACK_W), PACK_DTYPE),
        compiler_params=_params(1),
        name="ffn",
    )(tile_expert, tile_flag, xs, w1, b1, w2, b2)


def _combine_kernel(x_ref, g2_ref, w_ref, y_ref, o_ref):
    w = w_ref[...]
    acc_lo = None
    for k in range(TOP_K):
        lo, hi = _unpack_pairs(y_ref[k])
        wk = w[:, k:k + 1]
        acc_lo = wk * lo if acc_lo is None else acc_lo + wk * lo
        acc_hi = wk * hi if k == 0 else acc_hi + wk * hi
    acc = jnp.concatenate([acc_lo, acc_hi], axis=1)
    o_ref[...] = x_ref[...] + g2_ref[0] * acc


def _combine_into_kernel(x_ref, g2_ref, w_ref, y_ref, prev_ref, o_ref):
    del prev_ref
    _combine_kernel(x_ref, g2_ref, w_ref, y_ref, o_ref)


def _combine(x1, g2, w_tok, yk, seq, first_row, prev):
    t = x1.shape[0]
    rows = yk.shape[1]
    tm = _largest_tile(seq, 512, 8)
    per_b = seq // tm
    i0 = first_row // tm
    in_specs = [pl.BlockSpec((tm, D_MODEL), lambda i: (i0 + i, 0)),
                pl.BlockSpec((1, 1, D_MODEL), lambda i: ((i0 + i) // per_b, 0, 0)),
                pl.BlockSpec((tm, TOP_K), lambda i: (i0 + i, 0)),
                pl.BlockSpec((TOP_K, tm, PACK_W), lambda i: (0, i, 0))]
    args = [x1, g2, w_tok, yk]
    if prev is not None:
        in_specs.append(pl.BlockSpec(memory_space=pl.ANY))
        args.append(prev)
    return pl.pallas_call(
        _combine_kernel if prev is None else _combine_into_kernel,
        grid=(rows // tm,),
        in_specs=in_specs,
        out_specs=pl.BlockSpec((tm, D_MODEL), lambda i: (i0 + i, 0)),
        out_shape=jax.ShapeDtypeStruct((t, D_MODEL), F32),
        input_output_aliases={} if prev is None else {len(args) - 1: 0},
        compiler_params=_params(1),
        name="combine",
    )(*args)


def _plan(top_i, tg):
    t = top_i.shape[1]
    n_assign = TOP_K * t
    p = n_assign + N_EXPERTS * tg
    n_tiles = p // tg
    flat_e = top_i.reshape(-1)
    sorted_e, sorted_a = lax.sort_key_val(flat_e, jnp.arange(n_assign, dtype=jnp.int32))
    experts = jnp.arange(N_EXPERTS + 1, dtype=jnp.int32)
    bounds = jnp.sum((sorted_e[None, :] < experts[:, None]).astype(jnp.int32), axis=1)
    start = bounds[:-1]
    counts = bounds[1:] - start
    padded = ((counts + tg - 1) // tg) * tg
    off_end = jnp.cumsum(padded)
    off = off_end - padded
    dest_sorted = off[sorted_e] + jnp.arange(n_assign, dtype=jnp.int32) - start[sorted_e]
    _, dest = lax.sort_key_val(sorted_a, dest_sorted)
    tile_start = jnp.arange(n_tiles, dtype=jnp.int32) * tg
    tile_valid = (tile_start < off_end[-1]).astype(jnp.int32)
    te = jnp.sum((tile_start[:, None] >= off_end[None, :]).astype(jnp.int32), axis=1)
    last_e = jnp.sum(((off_end[-1] - 1) >= off_end).astype(jnp.int32))
    tile_expert = jnp.where(tile_valid != 0, te, last_e)
    slot = jnp.arange(p, dtype=jnp.int32)
    e_slot = jnp.repeat(tile_expert, tg)
    r = slot - off[e_slot]
    ok = (r < counts[e_slot]) & (jnp.repeat(tile_valid, tg) != 0)
    a_slot = sorted_a[jnp.clip(start[e_slot] + r, 0, n_assign - 1)]
    src_tok = jnp.where(ok, a_slot % t, slot % t).astype(jnp.int32)
    changed = jnp.concatenate([jnp.ones((1,), jnp.bool_), tile_expert[1:] != tile_expert[:-1]])
    tile_flag = jnp.where(tile_valid != 0, jnp.where(changed, TILE_FIRST, TILE_BODY), TILE_PAD).astype(jnp.int32)
    return src_tok, dest, tile_expert, tile_flag


def kernel(x, c, ctx, c_ctx, norm1, norm2, w_ada, b_ada, w_in, ret_decay_f, ret_decay_b, attn_q_norm, attn_k_norm,
           w_ret_o, w_attn_o, w_out, w_router, b_router, w_exp_in, b_exp_in, w_exp_out, b_exp_out):
    assert w_in.shape[0] == 1, "single-layer block"
    b, seq, d = x.shape
    n_ctx = ctx.shape[1]
    t = b * seq
    rows = seq // GRID_W

    idx = np.cumsum(IN_SIZES)[:-1].tolist()
    wq_r, wk_r, wv_r, wg_r, wq_a, wk_a, wv_a, wgt_r, wgt_a = jnp.split(w_in[0], idx, axis=-1)
    w_in_p = jnp.concatenate([wq_r, wk_r, wv_r, wg_r, wq_a, wgt_r, wgt_a, wk_a, wv_a], axis=-1)
    w1 = w_exp_in[0]
    w2 = w_exp_out[0]
    b1 = b_exp_in[0].reshape(N_EXPERTS, 1, 2 * EXPERT_FF)
    b2 = b_exp_out[0].reshape(N_EXPERTS, 1, D_MODEL)

    pad = (-(b + 1)) % 8
    c_all = jnp.concatenate([c, c_ctx[None, :], jnp.zeros((pad, d), F32)], axis=0)
    mod = _ada(c_all, w_ada[0], b_ada[0])
    sh1, sc1, g1, sh2, sc2, g2 = [m.reshape(-1, 1, d) for m in jnp.split(mod, 6, axis=-1)]
    lat = lambda m: m[:b]
    cx = lambda m: m[b:b + 1]

    x2 = x.reshape(t, d)
    proj = _inproj(x2, norm1, lat(sc1), lat(sh1), w_in_p, seq)
    proj_c = _inproj(ctx.reshape(b * n_ctx, d), norm1, cx(sc1), cx(sh1), w_in_p, n_ctx)

    cos_r, sin_r = _rope_tables(rows, RET_QK_DIM)
    cos_a, sin_a = _rope_tables(rows, ATTN_HEAD_DIM)
    lg_f = -jax.nn.softplus(ret_decay_f[0].astype(F32))
    lg_b = -jax.nn.softplus(ret_decay_b[0].astype(F32))
    zero_state = jnp.zeros((2, b, RET_HEADS, RET_QK_DIM, RET_V_DIM), F32)
    cos_c = jnp.ones((n_ctx, RET_QK_DIM), F32)
    sin_c = jnp.zeros((n_ctx, RET_QK_DIM), F32)
    _, _, s_ctx = _retention(proj_c, cos_c, sin_c, lg_f, lg_b, zero_state, b, n_ctx, False)
    o_f, o_b, _ = _retention(proj, cos_r, sin_r, lg_f, lg_b, s_ctx, b, seq, True)

    k_lat = _kprep(proj, attn_k_norm, cos_a, sin_a, seq, True)
    k_ctx = _kprep(proj_c, attn_k_norm, cos_a, sin_a, n_ctx, False)
    k_all = jnp.concatenate([k_ctx.reshape(b, n_ctx, ATTN_KV_W), k_lat.reshape(b, seq, ATTN_KV_W)], axis=1)
    v_all = jnp.concatenate([proj_c[:, COL_VA:COL_VA + ATTN_KV_W].reshape(b, n_ctx, ATTN_KV_W),
                             proj[:, COL_VA:COL_VA + ATTN_KV_W].reshape(b, seq, ATTN_KV_W)], axis=1)
    lk = n_ctx + seq
    vt = jnp.swapaxes(v_all.reshape(b, lk, ATTN_KV_HEADS, ATTN_HEAD_DIM), 1, 3).swapaxes(1, 2)
    vt_all = jnp.concatenate([vt, jnp.ones((b, ATTN_KV_HEADS, ONES_ROWS, lk), vt.dtype)], axis=2)
    vt_all = vt_all.reshape(b, ATTN_KV_HEADS * (ATTN_HEAD_DIM + ONES_ROWS), lk)
    attn_o = _attention(proj, attn_q_norm, cos_a, sin_a, k_all, vt_all, b, seq)

    x1 = _merge(x2, lat(g1), o_f, o_b, proj, attn_o, w_ret_o[0].astype(BF16), w_attn_o[0].astype(BF16),
                w_out[0].astype(BF16), seq)

    h2, top_i, top_w = _router(x1, norm2, lat(sc2), lat(sh2), w_router[0].T, b_router[0].reshape(N_EXPERTS, 1), seq)
    n_groups = MOE_GROUPS if b % MOE_GROUPS == 0 else 1
    tgrp = t // n_groups
    tg = _largest_tile(TOP_K * tgrp, 512, 8)
    w_tok = top_w.T
    out = None
    for grp in range(n_groups):
        first = grp * tgrp
        src_tok, dest, tile_expert, tile_flag = _plan(top_i[:, first:first + tgrp], tg)
        xs = _gather_rows(h2, src_tok + first)
        ys = _ffn(xs, tile_expert, tile_flag, w1, b1, w2, b2, tg)
        yk = _gather_rows(ys, dest).reshape(TOP_K, tgrp, PACK_W)
        out = _combine(x1, lat(g2), w_tok, yk, seq, first, out)
    return out.reshape(b, seq, d)
```

```python
import functools

import jax
import jax.numpy as jnp
import numpy as np
from jax import lax
from jax.experimental import pallas as pl
from jax.experimental.pallas import tpu as pltpu
from jax.experimental.pallas import tpu_sc as plsc

F32 = jnp.float32
BF16 = jnp.bfloat16

D_MODEL = 1024
GRID_W = 64
EPS = 1e-6
RET_HEADS = 4
RET_QK_DIM = 256
RET_V_DIM = 512
ATTN_HEADS = 8
ATTN_KV_HEADS = 2
ATTN_GROUP = ATTN_HEADS // ATTN_KV_HEADS
ATTN_HEAD_DIM = 128
ROPE_THETA = 10000.0
N_EXPERTS = 32
TOP_K = 4
EXPERT_FF = 1024
SWIGLU_LIMIT = 7.0
SWIGLU_ALPHA = 1.702

RET_QK_W = RET_HEADS * RET_QK_DIM
RET_V_W = RET_HEADS * RET_V_DIM
ATTN_Q_W = ATTN_HEADS * ATTN_HEAD_DIM
ATTN_KV_W = ATTN_KV_HEADS * ATTN_HEAD_DIM
IN_SIZES = (RET_QK_W, RET_QK_W, RET_V_W, RET_V_W, ATTN_Q_W, ATTN_KV_W, ATTN_KV_W, D_MODEL, D_MODEL)
IN_WIDTH = sum(IN_SIZES)
COL_QR = 0
COL_KR = COL_QR + RET_QK_W
COL_VR = COL_KR + RET_QK_W
COL_GR = COL_VR + RET_V_W
COL_QA = COL_GR + RET_V_W
COL_GATE_R = COL_QA + ATTN_Q_W
COL_GATE_A = COL_GATE_R + D_MODEL
COL_KA = COL_GATE_A + D_MODEL
COL_VA = COL_KA + ATTN_KV_W

RET_CHUNK = 256
RET_HEADS_PER_STEP = 2
LANES = 128
SC_CORES = 2
SC_SUBCORES = 16
SC_GATHER_ROWS = 128
MXU_DEPTH = 256
ATTN_KEY_CHUNK = 768
ATTN_Q_TILE = 512
ATTN_EXP_SLAB = 32
ONES_ROWS = 16
PACK_DTYPE = jnp.int32
PACK_W = D_MODEL // 2
MOE_GROUPS = 2
PLAN_BLOCK = 512
TILE_PAD, TILE_BODY, TILE_FIRST = 0, 1, 2
VMEM_LIMIT = 56 * 1024 * 1024

ARB = pltpu.ARBITRARY


def _params(n_axes, **kw):
    return pltpu.CompilerParams(dimension_semantics=(ARB,) * n_axes, vmem_limit_bytes=VMEM_LIMIT, **kw)


def _largest_tile(n, cap, mult):
    best = None
    for t in range(mult, min(n, cap) + 1, mult):
        if n % t == 0:
            best = t
    assert best is not None, (n, cap, mult)
    return best


def _rms(x):
    return x * lax.rsqrt(jnp.mean(x * x, axis=-1, keepdims=True) + EPS)


def _pack_pairs(x):
    half = x.shape[1] // 2
    lo = lax.bitcast_convert_type(x[:, :half].astype(BF16).astype(F32), jnp.int32)
    hi = lax.bitcast_convert_type(x[:, half:].astype(BF16).astype(F32), jnp.int32)
    return lax.bitwise_or(lax.bitwise_and(hi, jnp.int32(-65536)), lax.shift_right_logical(lo, jnp.int32(16)))


def _unpack_pairs(w):
    lo = lax.bitcast_convert_type(lax.shift_left(w, jnp.int32(16)), F32)
    hi = lax.bitcast_convert_type(lax.bitwise_and(w, jnp.int32(-65536)), F32)
    return lo, hi


def _rope_block(x, cos, sin, half):
    if 2 * half == LANES:
        swapped = pltpu.roll(x, half, 1)
    else:
        lane = lax.broadcasted_iota(jnp.int32, x.shape, 1)
        first = (lane % (2 * half)) < half
        swapped = jnp.where(first, pltpu.roll(x, LANES - half, 1), pltpu.roll(x, half, 1))
    return x * cos + swapped * sin


def _rope_tables(rows, head_dim):
    row_id = jnp.broadcast_to(jnp.arange(rows, dtype=F32)[:, None], (rows, GRID_W)).reshape(-1)
    col_id = jnp.broadcast_to(jnp.arange(GRID_W, dtype=F32)[None, :], (rows, GRID_W)).reshape(-1)
    n_freq = head_dim // 4
    inv_freq = ROPE_THETA ** (-jnp.arange(n_freq, dtype=F32) / n_freq)
    ang_r = row_id[:, None] * inv_freq
    ang_c = col_id[:, None] * inv_freq
    cos = jnp.concatenate([jnp.cos(ang_r), jnp.cos(ang_r), jnp.cos(ang_c), jnp.cos(ang_c)], axis=1)
    sin = jnp.concatenate([-jnp.sin(ang_r), jnp.sin(ang_r), -jnp.sin(ang_c), jnp.sin(ang_c)], axis=1)
    return cos, sin


def _ada_kernel(c_ref, w_ref, b_ref, o_ref):
    c = c_ref[...]
    s = c * jax.nn.sigmoid(c)
    o_ref[...] = jnp.dot(s, w_ref[...], preferred_element_type=F32,
                         precision=lax.Precision.HIGHEST) + b_ref[...]


def _ada(c_pad, w_ada, b_ada):
    rows = c_pad.shape[0]
    n = w_ada.shape[1]
    tn = _largest_tile(n, 1536, LANES)
    return pl.pallas_call(
        _ada_kernel,
        grid=(n // tn,),
        in_specs=[pl.BlockSpec((rows, D_MODEL), lambda j: (0, 0)),
                  pl.BlockSpec((D_MODEL, tn), lambda j: (0, j)),
                  pl.BlockSpec((1, tn), lambda j: (0, j))],
        out_specs=pl.BlockSpec((rows, tn), lambda j: (0, j)),
        out_shape=jax.ShapeDtypeStruct((rows, n), F32),
        compiler_params=_params(1),
        name="ada",
    )(c_pad, w_ada, b_ada.reshape(1, n))


def _inproj_kernel(x_ref, n_ref, sc_ref, sh_ref, w_ref, o_ref, w_sc):
    @pl.when(pl.program_id(1) == 0)
    def _():
        w_sc[...] = w_ref[...].astype(BF16)

    y = _rms(x_ref[...]) * n_ref[...]
    h = (y * (1.0 + sc_ref[0]) + sh_ref[0]).astype(BF16)
    o_ref[...] = jnp.dot(h, w_sc[...], preferred_element_type=F32).astype(o_ref.dtype)


def _inproj(x2, norm, sc, sh, w, rows_per_batch):
    t = x2.shape[0]
    n = w.shape[1]
    tm = _largest_tile(rows_per_batch, 512, 8)
    tn = _largest_tile(n, 2560, LANES)
    per_b = rows_per_batch // tm
    if sc.shape[0] == 1:
        mod_map = lambda j, i: (0, 0, 0)
    else:
        mod_map = lambda j, i: (i // per_b, 0, 0)
    return pl.pallas_call(
        _inproj_kernel,
        grid=(n // tn, t // tm),
        in_specs=[pl.BlockSpec((tm, D_MODEL), lambda j, i: (i, 0)),
                  pl.BlockSpec((1, D_MODEL), lambda j, i: (0, 0)),
                  pl.BlockSpec((1, 1, D_MODEL), mod_map),
                  pl.BlockSpec((1, 1, D_MODEL), mod_map),
                  pl.BlockSpec((D_MODEL, tn), lambda j, i: (0, j))],
        out_specs=pl.BlockSpec((tm, tn), lambda j, i: (i, j)),
        out_shape=jax.ShapeDtypeStruct((t, n), BF16),
        scratch_shapes=[pltpu.VMEM((D_MODEL, tn), BF16)],
        compiler_params=_params(2),
        name="inproj",
    )(x2, norm, sc, sh, w)


def _kprep_kernel(k_ref, g_ref, cos_ref, sin_ref, o_ref, *, use_rope):
    k = k_ref[...].astype(F32)
    for g in range(ATTN_KV_HEADS):
        sl = slice(g * ATTN_HEAD_DIM, (g + 1) * ATTN_HEAD_DIM)
        kh = _rms(k[:, sl]) * g_ref[...]
        if use_rope:
            kh = _rope_block(kh, cos_ref[...], sin_ref[...], ATTN_HEAD_DIM // 4)
        o_ref[:, sl] = kh.astype(BF16)


def _kprep(proj, gain, cos, sin, rows_per_batch, use_rope):
    t = proj.shape[0]
    tm = _largest_tile(rows_per_batch, 1024, 8)
    per_b = rows_per_batch // tm
    return pl.pallas_call(
        functools.partial(_kprep_kernel, use_rope=use_rope),
        grid=(t // tm,),
        in_specs=[pl.BlockSpec((tm, ATTN_KV_W), lambda i: (i, COL_KA // ATTN_KV_W)),
                  pl.BlockSpec((1, ATTN_HEAD_DIM), lambda i: (0, 0)),
                  pl.BlockSpec((tm, ATTN_HEAD_DIM), lambda i: (i % per_b, 0)),
                  pl.BlockSpec((tm, ATTN_HEAD_DIM), lambda i: (i % per_b, 0))],
        out_specs=pl.BlockSpec((tm, ATTN_KV_W), lambda i: (i, 0)),
        out_shape=jax.ShapeDtypeStruct((t, ATTN_KV_W), BF16),
        compiler_params=_params(1),
        name="kprep",
    )(proj, gain, cos, sin)


def _ret_kernel(dch_ref, qf_ref, kf_ref, vf_ref, cosf_ref, sinf_ref, qb_ref, kb_ref, vb_ref, cosb_ref, sinb_ref,
                dmat_ref, din_ref, dout_ref, s0_ref, of_ref, ob_ref, sfin_ref, s_sc, *, use_rope, n_chunks, heads):
    hb = pl.program_id(1)
    c = pl.program_id(2)

    @pl.when(c == 0)
    def _():
        s_sc[...] = s0_ref[:, 0]

    def rope(x, cos, sin):
        return jnp.concatenate([_rope_block(x[:, i * LANES:(i + 1) * LANES], cos[:, i * LANES:(i + 1) * LANES],
                                            sin[:, i * LANES:(i + 1) * LANES], RET_QK_DIM // 4)
                                for i in range(RET_QK_DIM // LANES)], axis=1)

    sides = ((qf_ref, kf_ref, vf_ref, cosf_ref, sinf_ref), (qb_ref, kb_ref, vb_ref, cosb_ref, sinb_ref))
    chains = [(d, j) for d in range(2) for j in range(heads)]
    qs, ks, vs, ss = {}, {}, {}, {}
    for d, j in chains:
        q_ref, k_ref, v_ref, cos_ref, sin_ref = sides[d]
        q = q_ref[:, j * RET_QK_DIM:(j + 1) * RET_QK_DIM].astype(F32)
        k = k_ref[:, j * RET_QK_DIM:(j + 1) * RET_QK_DIM].astype(F32)
        if use_rope:
            q = rope(q, cos_ref[...], sin_ref[...])
            k = rope(k, cos_ref[...], sin_ref[...])
        qs[d, j], ks[d, j] = q, k
        vs[d, j] = v_ref[:, j * RET_V_DIM:(j + 1) * RET_V_DIM]
        ss[d, j] = s_sc[d, j]
    scores = {ch: lax.dot_general(qs[ch].astype(BF16), ks[ch].astype(BF16), (((1,), (1,)), ((), ())),
                                  preferred_element_type=F32) * dmat_ref[ch[0], ch[1]] for ch in chains}
    inter = {ch: jnp.dot((qs[ch] * din_ref[ch[0], ch[1]]).astype(BF16), ss[ch].astype(BF16),
                         preferred_element_type=F32) for ch in chains}
    intra = {ch: jnp.dot(scores[ch].astype(BF16), vs[ch], preferred_element_type=F32) for ch in chains}
    o_refs = (of_ref, ob_ref)
    for d, j in chains:
        o_refs[d][:, j * RET_V_DIM:(j + 1) * RET_V_DIM] = (intra[d, j] + inter[d, j]).astype(of_ref.dtype)
    s_new = {ch: ss[ch] * dch_ref[ch[0] * RET_HEADS + hb * heads + ch[1]]
             + jnp.dot((ks[ch] * dout_ref[ch[0], ch[1]]).T.astype(BF16), vs[ch], preferred_element_type=F32)
             for ch in chains}
    for d, j in chains:
        s_sc[d, j] = s_new[d, j]

    @pl.when(c == n_chunks - 1)
    def _():
        for d, j in chains:
            sfin_ref[d, 0, j] = s_new[d, j]


def _ret_tables(log_gamma, chunk, reverse, k_scale):
    pos = jnp.arange(chunk, dtype=F32)
    diff = pos[:, None] - pos[None, :]
    if reverse:
        diff = -diff
        mask = diff > 0
        p_in = chunk - pos
        p_out = pos
    else:
        mask = diff >= 0
        p_in = pos + 1.0
        p_out = chunk - 1.0 - pos
    lg = log_gamma.astype(F32)
    dmat = jnp.where(mask[None], jnp.exp(lg[:, None, None] * jnp.maximum(diff, 0.0)[None]), 0.0) * k_scale
    d_in = jnp.exp(lg[:, None] * p_in)
    d_out = jnp.exp(lg[:, None] * p_out) * k_scale
    d_in = jnp.broadcast_to(d_in[:, :, None], (RET_HEADS, chunk, RET_QK_DIM))
    d_out = jnp.broadcast_to(d_out[:, :, None], (RET_HEADS, chunk, RET_QK_DIM))
    d_chunk = jnp.exp(lg * chunk)
    return d_chunk, dmat, d_in, d_out


def _retention(proj, cos, sin, lg_f, lg_b, s0, batch, seq, use_rope):
    chunk = min(RET_CHUNK, seq)
    nc = seq // chunk
    hs = RET_HEADS_PER_STEP
    qw, vw = hs * RET_QK_DIM, hs * RET_V_DIM
    k_scale = RET_QK_DIM ** -0.5
    tabs = [_ret_tables(lg_f, chunk, False, k_scale), _ret_tables(lg_b, chunk, True, k_scale)]
    d_chunk, dmat, d_in, d_out = [jnp.stack([tabs[0][i], tabs[1][i]]) for i in range(4)]
    fwd = lambda c: c
    bwd = lambda c: nc - 1 - c

    def side(cidx):
        row = lambda b, c: b * nc + cidx(c)
        return [pl.BlockSpec((chunk, qw), lambda b, h, c: (row(b, c), COL_QR // qw + h)),
                pl.BlockSpec((chunk, qw), lambda b, h, c: (row(b, c), COL_KR // qw + h)),
                pl.BlockSpec((chunk, vw), lambda b, h, c: (row(b, c), COL_VR // vw + h)),
                pl.BlockSpec((chunk, RET_QK_DIM), lambda b, h, c: (cidx(c), 0)),
                pl.BlockSpec((chunk, RET_QK_DIM), lambda b, h, c: (cidx(c), 0))]

    state_spec = pl.BlockSpec((2, 1, hs, RET_QK_DIM, RET_V_DIM), lambda b, h, c: (0, b, h, 0, 0))
    return pl.pallas_call(
        functools.partial(_ret_kernel, use_rope=use_rope, n_chunks=nc, heads=hs),
        grid=(batch, RET_HEADS // hs, nc),
        in_specs=[pl.BlockSpec(memory_space=pltpu.SMEM)] + side(fwd) + side(bwd) + [
            pl.BlockSpec((2, hs, chunk, chunk), lambda b, h, c: (0, h, 0, 0)),
            pl.BlockSpec((2, hs, chunk, RET_QK_DIM), lambda b, h, c: (0, h, 0, 0)),
            pl.BlockSpec((2, hs, chunk, RET_QK_DIM), lambda b, h, c: (0, h, 0, 0)),
            state_spec],
        out_specs=[pl.BlockSpec((chunk, vw), lambda b, h, c: (b * nc + c, h)),
                   pl.BlockSpec((chunk, vw), lambda b, h, c: (b * nc + nc - 1 - c, h)),
                   state_spec],
        out_shape=[jax.ShapeDtypeStruct((batch * seq, RET_V_W), BF16),
                   jax.ShapeDtypeStruct((batch * seq, RET_V_W), BF16),
                   jax.ShapeDtypeStruct((2, batch, RET_HEADS, RET_QK_DIM, RET_V_DIM), F32)],
        scratch_shapes=[pltpu.VMEM((2, hs, RET_QK_DIM, RET_V_DIM), F32)],
        compiler_params=_params(3),
        name="ret",
    )(d_chunk.reshape(-1), proj, proj, proj, cos, sin, proj, proj, proj, cos, sin, dmat, d_in, d_out, s0)


def _attn_kernel(q_ref, g_ref, cos_ref, sin_ref, k_ref, vt_ref, o_ref, qt_sc, m_sc, acc_sc,
                 s0_sc, s1_sc, p0_sc, p1_sc, a0_sc, a1_sc, *, tq, kc, n_chunks):
    s_bufs = (s0_sc, s1_sc)
    p_bufs = (p0_sc, p1_sc)
    a_bufs = (a0_sc, a1_sc)

    q = q_ref[...].astype(F32)
    scale = ATTN_HEAD_DIM ** -0.5 * np.log2(np.e)
    for g in range(ATTN_GROUP):
        qh = _rms(q[:, g * ATTN_HEAD_DIM:(g + 1) * ATTN_HEAD_DIM]) * g_ref[...]
        qh = _rope_block(qh, cos_ref[...], sin_ref[...], ATTN_HEAD_DIM // 4) * scale
        qt_sc[:, g * tq:(g + 1) * tq] = qh.T.astype(BF16)
    m_sc[...] = jnp.full(m_sc.shape, -jnp.inf, F32)
    acc_sc[...] = jnp.zeros(acc_sc.shape, F32)

    def stage_s(c, slot):
        off = pl.multiple_of(c * kc, kc)
        s_bufs[slot][...] = jnp.dot(k_ref[0, pl.ds(off, kc), :], qt_sc[...], preferred_element_type=F32)

    def stage_f(slot):
        m_prev = m_sc[...]
        m_new = jnp.maximum(m_prev, jnp.max(s_bufs[slot][...], axis=0, keepdims=True))
        m_sc[...] = m_new
        a_bufs[slot][...] = jnp.exp2(m_prev - m_new)
        for r in range(0, kc, ATTN_EXP_SLAB):
            p_bufs[slot][r:r + ATTN_EXP_SLAB, :] = jnp.exp2(
                s_bufs[slot][r:r + ATTN_EXP_SLAB, :] - m_new).astype(BF16)

    def stage_a(c, slot):
        off = pl.multiple_of(c * kc, kc)
        acc_sc[...] = a_bufs[slot][...] * acc_sc[...] + jnp.dot(
            vt_ref[0, :, pl.ds(off, kc)], p_bufs[slot][...], preferred_element_type=F32)

    def tick(t, parity):
        stage_s(t, parity)
        stage_f(1 - parity)
        stage_a(t - 2, parity)

    n = n_chunks
    stage_s(0, 0)
    if n > 1:
        stage_s(1, 1)
    stage_f(0)
    first = 2
    if n > 2 and (n - 2) % 2 == 1:
        tick(2, 0)
        first = 3
    n_pairs = (n - first) // 2 if n > first else 0
    if n_pairs > 0:
        def pair(u, carry):
            t = first + 2 * u
            tick(t, first % 2)
            tick(t + 1, 1 - first % 2)
            return carry
        lax.fori_loop(0, n_pairs, pair, 0)
    if n > 1:
        stage_f((n - 1) % 2)
        stage_a(n - 2, (n - 2) % 2)
    stage_a(n - 1, (n - 1) % 2)

    acc = acc_sc[...]
    o = acc[:ATTN_HEAD_DIM] / acc[ATTN_HEAD_DIM:ATTN_HEAD_DIM + 1]
    for g in range(ATTN_GROUP):
        o_ref[:, g * ATTN_HEAD_DIM:(g + 1) * ATTN_HEAD_DIM] = o[:, g * tq:(g + 1) * tq].T.astype(o_ref.dtype)


def _attention(proj, gain, cos, sin, k_all, vt_all, batch, seq):
    lk = k_all.shape[1]
    kc = _largest_tile(lk, ATTN_KEY_CHUNK, MXU_DEPTH)
    tq = _largest_tile(seq, ATTN_Q_TILE, 8)
    nq = seq // tq
    qw = ATTN_GROUP * ATTN_HEAD_DIM
    cols = ATTN_GROUP * tq
    vrows = ATTN_HEAD_DIM + ONES_ROWS
    return pl.pallas_call(
        functools.partial(_attn_kernel, tq=tq, kc=kc, n_chunks=lk // kc),
        grid=(batch, ATTN_KV_HEADS, nq),
        in_specs=[pl.BlockSpec((tq, qw), lambda b, g, i: (b * nq + i, COL_QA // qw + g)),
                  pl.BlockSpec((1, ATTN_HEAD_DIM), lambda b, g, i: (0, 0)),
                  pl.BlockSpec((tq, ATTN_HEAD_DIM), lambda b, g, i: (i, 0)),
                  pl.BlockSpec((tq, ATTN_HEAD_DIM), lambda b, g, i: (i, 0)),
                  pl.BlockSpec((1, lk, ATTN_HEAD_DIM), lambda b, g, i: (b, 0, g)),
                  pl.BlockSpec((1, vrows, lk), lambda b, g, i: (b, g, 0))],
        out_specs=pl.BlockSpec((tq, qw), lambda b, g, i: (b * nq + i, g)),
        out_shape=jax.ShapeDtypeStruct((batch * seq, ATTN_Q_W), BF16),
        scratch_shapes=[pltpu.VMEM((ATTN_HEAD_DIM, cols), BF16),
                        pltpu.VMEM((1, cols), F32),
                        pltpu.VMEM((vrows, cols), F32),
                        pltpu.VMEM((kc, cols), F32),
                        pltpu.VMEM((kc, cols), F32),
                        pltpu.VMEM((kc, cols), BF16),
                        pltpu.VMEM((kc, cols), BF16),
                        pltpu.VMEM((1, cols), F32),
                        pltpu.VMEM((1, cols), F32)],
        compiler_params=_params(3),
        name="attn",
    )(proj, gain, cos, sin, k_all, vt_all)


def _merge_kernel(x_ref, g1_ref, of_ref, ob_ref, gr_ref, ao_ref, gtr_ref, gta_ref,
                  wro_ref, wao_ref, wo_ref, o_ref):
    ro = of_ref[...].astype(F32) + ob_ref[...].astype(F32)
    gr = gr_ref[...].astype(F32)
    parts = []
    for h in range(RET_HEADS):
        sl = slice(h * RET_V_DIM, (h + 1) * RET_V_DIM)
        g = gr[:, sl]
        parts.append((g * jax.nn.sigmoid(g) * _rms(ro[:, sl])).astype(BF16))
    ret_in = jnp.concatenate(parts, axis=1)
    ret_branch = jnp.dot(ret_in, wro_ref[...], preferred_element_type=F32)
    attn_branch = jnp.dot(ao_ref[...], wao_ref[...], preferred_element_type=F32)
    y = (jax.nn.sigmoid(gtr_ref[...].astype(F32)) * ret_branch
         + jax.nn.sigmoid(gta_ref[...].astype(F32)) * attn_branch)
    y = jnp.dot(y.astype(BF16), wo_ref[...], preferred_element_type=F32)
    o_ref[...] = x_ref[...] + g1_ref[0] * y


def _merge(x2, g1, o_f, o_b, proj, attn_o, w_ret_o, w_attn_o, w_out, seq):
    t = x2.shape[0]
    tm = _largest_tile(seq, 256, 8)
    per_b = seq // tm
    full = lambda i: (0, 0)
    return pl.pallas_call(
        _merge_kernel,
        grid=(t // tm,),
        in_specs=[pl.BlockSpec((tm, D_MODEL), lambda i: (i, 0)),
                  pl.BlockSpec((1, 1, D_MODEL), lambda i: (i // per_b, 0, 0)),
                  pl.BlockSpec((tm, RET_V_W), lambda i: (i, 0)),
                  pl.BlockSpec((tm, RET_V_W), lambda i: (i, 0)),
                  pl.BlockSpec((tm, RET_V_W), lambda i: (i, COL_GR // RET_V_W)),
                  pl.BlockSpec((tm, ATTN_Q_W), lambda i: (i, 0)),
                  pl.BlockSpec((tm, D_MODEL), lambda i: (i, COL_GATE_R // D_MODEL)),
                  pl.BlockSpec((tm, D_MODEL), lambda i: (i, COL_GATE_A // D_MODEL)),
                  pl.BlockSpec((RET_V_W, D_MODEL), full),
                  pl.BlockSpec((ATTN_Q_W, D_MODEL), full),
                  pl.BlockSpec((D_MODEL, D_MODEL), full)],
        out_specs=pl.BlockSpec((tm, D_MODEL), lambda i: (i, 0)),
        out_shape=jax.ShapeDtypeStruct((t, D_MODEL), F32),
        compiler_params=_params(1),
        name="merge",
    )(x2, g1, o_f, o_b, proj, attn_o, proj, proj, w_ret_o, w_attn_o, w_out)


def _router_kernel(x_ref, n_ref, sc_ref, sh_ref, wr_ref, br_ref, h_ref, ti_ref, tw_ref):
    h = _rms(x_ref[...]) * n_ref[...] * (1.0 + sc_ref[0]) + sh_ref[0]
    h_ref[...] = _pack_pairs(h)
    nt = (((1,), (1,)), ((), ()))
    w = wr_ref[...]
    h_hi = h.astype(BF16)
    h_lo = (h - h_hi.astype(F32)).astype(BF16)
    w_hi = w.astype(BF16)
    w_lo = (w - w_hi.astype(F32)).astype(BF16)
    logits = (lax.dot_general(w_hi, h_hi, nt, preferred_element_type=F32)
              + lax.dot_general(w_lo, h_hi, nt, preferred_element_type=F32)
              + lax.dot_general(w_hi, h_lo, nt, preferred_element_type=F32)) + br_ref[...]
    eid = lax.broadcasted_iota(jnp.int32, logits.shape, 0)
    vals = logits
    top_v = []
    top_i = []
    for _ in range(TOP_K):
        m = jnp.max(vals, axis=0, keepdims=True)
        idx = jnp.min(jnp.where(vals == m, eid, N_EXPERTS), axis=0, keepdims=True)
        top_v.append(m)
        top_i.append(idx)
        vals = jnp.where(eid == idx, -jnp.inf, vals)
    ex = [jnp.exp(v - top_v[0]) for v in top_v]
    denom = ex[0] + ex[1] + ex[2] + ex[3]
    ti_ref[...] = jnp.concatenate(top_i, axis=0)
    tw_ref[...] = jnp.concatenate([e / denom for e in ex], axis=0)


def _router(x1, norm, sc, sh, w_router_t, b_router, seq):
    t = x1.shape[0]
    tm = _largest_tile(seq, 1024, LANES)
    per_b = seq // tm
    mod_map = lambda i: (i // per_b, 0, 0)
    return pl.pallas_call(
        _router_kernel,
        grid=(t // tm,),
        in_specs=[pl.BlockSpec((tm, D_MODEL), lambda i: (i, 0)),
                  pl.BlockSpec((1, D_MODEL), lambda i: (0, 0)),
                  pl.BlockSpec((1, 1, D_MODEL), mod_map),
                  pl.BlockSpec((1, 1, D_MODEL), mod_map),
                  pl.BlockSpec((N_EXPERTS, D_MODEL), lambda i: (0, 0)),
                  pl.BlockSpec((N_EXPERTS, 1), lambda i: (0, 0))],
        out_specs=[pl.BlockSpec((tm, PACK_W), lambda i: (i, 0)),
                   pl.BlockSpec((TOP_K, tm), lambda i: (0, i)),
                   pl.BlockSpec((TOP_K, tm), lambda i: (0, i))],
        out_shape=[jax.ShapeDtypeStruct((t, PACK_W), PACK_DTYPE),
                   jax.ShapeDtypeStruct((TOP_K, t), jnp.int32),
                   jax.ShapeDtypeStruct((TOP_K, t), F32)],
        compiler_params=_params(1),
        name="router",
    )(x1, norm, sc, sh, w_router_t, b_router)


def _gather_rows(src, idx):
    n, w = src.shape
    r = idx.shape[0]
    workers = SC_CORES * SC_SUBCORES
    per_w = r // workers
    n_win = per_w // SC_GATHER_ROWS
    assert per_w * workers == r and n_win * SC_GATHER_ROWS == per_w, (r, workers, SC_GATHER_ROWS)
    mesh = plsc.VectorSubcoreMesh(core_axis_name="c", subcore_axis_name="s")

    @functools.partial(
        pl.kernel, mesh=mesh, out_type=jax.ShapeDtypeStruct((r, w), src.dtype),
        scratch_types=[pltpu.VMEM((SC_GATHER_ROWS,), jnp.int32),
                       pltpu.VMEM((SC_GATHER_ROWS, w), src.dtype),
                       pltpu.SemaphoreType.DMA])
    def gather(src_hbm, idx_hbm, out_hbm, idx_v, rows_v, sem):
        wid = lax.axis_index("s") * SC_CORES + lax.axis_index("c")
        base = wid * per_w

        @pl.loop(0, n_win)
        def _(win):
            off = base + win * SC_GATHER_ROWS
            pltpu.sync_copy(idx_hbm.at[pl.ds(off, SC_GATHER_ROWS)], idx_v)
            pltpu.async_copy(src_hbm.at[idx_v], rows_v, sem).wait()
            pltpu.sync_copy(rows_v, out_hbm.at[pl.ds(off, SC_GATHER_ROWS)])

    return gather(src, idx)


def _scatter_rows(src, dest, n_out, first, group_rows):
    n, w = src.shape
    n_assign = dest.shape[0]
    workers = SC_CORES * SC_SUBCORES
    per_w = n_assign // workers
    n_win = per_w // SC_GATHER_ROWS
    assert per_w * workers == n_assign and n_win * SC_GATHER_ROWS == per_w and group_rows % per_w == 0
    mesh = plsc.VectorSubcoreMesh(core_axis_name="c", subcore_axis_name="s")

    @functools.partial(
        pl.kernel, mesh=mesh, out_type=jax.ShapeDtypeStruct((n_out, w), src.dtype),
        scratch_types=[pltpu.VMEM((SC_GATHER_ROWS,), jnp.int32),
                       pltpu.VMEM((SC_GATHER_ROWS, w), src.dtype),
                       pltpu.SemaphoreType.DMA])
    def scatter(src_hbm, dest_hbm, out_hbm, idx_v, rows_v, sem):
        wid = lax.axis_index("s") * SC_CORES + lax.axis_index("c")
        base = wid * per_w
        row_base = first + lax.rem(base, group_rows)

        @pl.loop(0, n_win)
        def _(win):
            pltpu.sync_copy(dest_hbm.at[pl.ds(base + win * SC_GATHER_ROWS, SC_GATHER_ROWS)], idx_v)
            pltpu.sync_copy(src_hbm.at[pl.ds(row_base + win * SC_GATHER_ROWS, SC_GATHER_ROWS)], rows_v)
            pltpu.async_copy(rows_v, out_hbm.at[idx_v], sem).wait()

    return scatter(src, dest)


def _ffn_kernel(te_ref, tf_ref, tr_ref, x_ref, w1_ref, b1_ref, w2_ref, b2_ref, o_ref, w1_sc, w2_sc):
    flag = tf_ref[pl.program_id(0)]
    n_rows = tr_ref[pl.program_id(0)]

    @pl.when(flag == TILE_FIRST)
    def _():
        w1_sc[...] = w1_ref[0].astype(BF16)
        w2_sc[...] = w2_ref[0].astype(BF16)

    @pl.when(flag == TILE_PAD)
    def _():
        o_ref[...] = jnp.zeros(o_ref.shape, o_ref.dtype)

    @pl.when(flag != TILE_PAD)
    def _():
        xw = x_ref[...]
        rid = lax.broadcasted_iota(jnp.int32, xw.shape, 0)
        lo, hi = _unpack_pairs(jnp.where(rid < n_rows, xw, jnp.zeros_like(xw)))
        x = jnp.concatenate([lo, hi], axis=1).astype(BF16)
        a = jnp.dot(x, w1_sc[...], preferred_element_type=F32) + b1_ref[0]
        gate = jnp.minimum(a[:, :EXPERT_FF], SWIGLU_LIMIT)
        up = jnp.clip(a[:, EXPERT_FF:], -SWIGLU_LIMIT, SWIGLU_LIMIT)
        act = gate * jax.nn.sigmoid(SWIGLU_ALPHA * gate) * (up + 1.0)
        y = jnp.dot(act.astype(BF16), w2_sc[...], preferred_element_type=F32) + b2_ref[0]
        o_ref[...] = _pack_pairs(y)


def _ffn(xs, tile_expert, tile_flag, tile_rows, w1, b1, w2, b2, tg):
    p = xs.shape[0]
    grid_spec = pltpu.PrefetchScalarGridSpec(
        num_scalar_prefetch=3,
        grid=(p // tg,),
        in_specs=[pl.BlockSpec((tg, PACK_W), lambda j, te, tf, tr: (j, 0)),
                  pl.BlockSpec((1, D_MODEL, 2 * EXPERT_FF), lambda j, te, tf, tr: (te[j], 0, 0)),
                  pl.BlockSpec((1, 1, 2 * EXPERT_FF), lambda j, te, tf, tr: (te[j], 0, 0)),
                  pl.BlockSpec((1, EXPERT_FF, D_MODEL), lambda j, te, tf, tr: (te[j], 0, 0)),
                  pl.BlockSpec((1, 1, D_MODEL), lambda j, te, tf, tr: (te[j], 0, 0))],
        out_specs=pl.BlockSpec((tg, PACK_W), lambda j, te, tf, tr: (j, 0)),
        scratch_shapes=[pltpu.VMEM((D_MODEL, 2 * EXPERT_FF), BF16),
                        pltpu.VMEM((EXPERT_FF, D_MODEL), BF16)],
    )
    return pl.pallas_call(
        _ffn_kernel,
        grid_spec=grid_spec,
        out_shape=jax.ShapeDtypeStruct((p, PACK_W), PACK_DTYPE),
        compiler_params=_params(1),
        name="ffn",
    )(tile_expert, tile_flag, tile_rows, xs, w1, b1, w2, b2)


def _combine_kernel(x_ref, g2_ref, w_ref, y_ref, o_ref):
    w = w_ref[...]
    acc_lo = None
    for k in range(TOP_K):
        lo, hi = _unpack_pairs(y_ref[k])
        wk = w[:, k:k + 1]
        acc_lo = wk * lo if acc_lo is None else acc_lo + wk * lo
        acc_hi = wk * hi if k == 0 else acc_hi + wk * hi
    acc = jnp.concatenate([acc_lo, acc_hi], axis=1)
    o_ref[...] = x_ref[...] + g2_ref[0] * acc


def _combine_into_kernel(x_ref, g2_ref, w_ref, y_ref, prev_ref, o_ref):
    del prev_ref
    _combine_kernel(x_ref, g2_ref, w_ref, y_ref, o_ref)


def _combine(x1, g2, w_tok, yk, seq, first_row, prev):
    t = x1.shape[0]
    rows = yk.shape[1]
    tm = _largest_tile(seq, 512, 8)
    per_b = seq // tm
    i0 = first_row // tm
    in_specs = [pl.BlockSpec((tm, D_MODEL), lambda i: (i0 + i, 0)),
                pl.BlockSpec((1, 1, D_MODEL), lambda i: ((i0 + i) // per_b, 0, 0)),
                pl.BlockSpec((tm, TOP_K), lambda i: (i0 + i, 0)),
                pl.BlockSpec((TOP_K, tm, PACK_W), lambda i: (0, i, 0))]
    args = [x1, g2, w_tok, yk]
    if prev is not None:
        in_specs.append(pl.BlockSpec(memory_space=pl.ANY))
        args.append(prev)
    return pl.pallas_call(
        _combine_kernel if prev is None else _combine_into_kernel,
        grid=(rows // tm,),
        in_specs=in_specs,
        out_specs=pl.BlockSpec((tm, D_MODEL), lambda i: (i0 + i, 0)),
        out_shape=jax.ShapeDtypeStruct((t, D_MODEL), F32),
        input_output_aliases={} if prev is None else {len(args) - 1: 0},
        compiler_params=_params(1),
        name="combine",
    )(*args)


def _count_kernel(ti_ref, cnt_ref):
    @pl.when(pl.program_id(0) == 0)
    def _():
        cnt_ref[...] = jnp.zeros(cnt_ref.shape, F32)

    bt = ti_ref.shape[1]
    eid = lax.broadcasted_iota(jnp.int32, (N_EXPERTS, bt), 0)
    acc = jnp.zeros(cnt_ref.shape, F32)
    for k in range(TOP_K):
        m = (eid == ti_ref[k:k + 1, :]).astype(F32)
        for c in range(bt // LANES):
            acc = acc + m[:, c * LANES:(c + 1) * LANES]
    cnt_ref[...] += acc


def _rank_kernel(ti_ref, off_ref, tri_ref, dest_ref, run_sc):
    @pl.when(pl.program_id(0) == 0)
    def _():
        run_sc[...] = off_ref[...] - 1.0

    bt = ti_ref.shape[1]
    eid = lax.broadcasted_iota(jnp.int32, (N_EXPERTS, bt), 0)
    run = run_sc[...]
    for k in range(TOP_K):
        m = eid == ti_ref[k:k + 1, :]
        pre = jnp.dot(jnp.where(m, 1.0, 0.0).astype(BF16), tri_ref[...], preferred_element_type=F32)
        slot = jnp.sum(jnp.where(m, pre + run, 0.0), axis=0, keepdims=True)
        dest_ref[k:k + 1, :] = slot.astype(jnp.int32)
        run = run + pre[:, bt - 1:bt]
    run_sc[...] = run


def _plan(top_i, tg):
    t = top_i.shape[1]
    n_assign = TOP_K * t
    p = n_assign + N_EXPERTS * tg
    n_tiles = p // tg
    bt = _largest_tile(t, PLAN_BLOCK, LANES)
    cnt = pl.pallas_call(
        _count_kernel,
        grid=(t // bt,),
        in_specs=[pl.BlockSpec((TOP_K, bt), lambda i: (0, i))],
        out_specs=pl.BlockSpec((N_EXPERTS, LANES), lambda i: (0, 0)),
        out_shape=jax.ShapeDtypeStruct((N_EXPERTS, LANES), F32),
        compiler_params=_params(1),
        name="count",
    )(top_i)
    counts = jnp.sum(cnt, axis=1).astype(jnp.int32)
    padded = ((counts + tg - 1) // tg) * tg
    off_end = jnp.cumsum(padded)
    off = off_end - padded
    tri = (jnp.arange(bt)[:, None] <= jnp.arange(bt)[None, :]).astype(BF16)
    dest = pl.pallas_call(
        _rank_kernel,
        grid=(t // bt,),
        in_specs=[pl.BlockSpec((TOP_K, bt), lambda i: (0, i)),
                  pl.BlockSpec((N_EXPERTS, 1), lambda i: (0, 0)),
                  pl.BlockSpec((bt, bt), lambda i: (0, 0))],
        out_specs=pl.BlockSpec((TOP_K, bt), lambda i: (0, i)),
        out_shape=jax.ShapeDtypeStruct((TOP_K, t), jnp.int32),
        scratch_shapes=[pltpu.VMEM((N_EXPERTS, 1), F32)],
        compiler_params=_params(1),
        name="rank",
    )(top_i, off.astype(F32).reshape(N_EXPERTS, 1), tri)
    tile_start = jnp.arange(n_tiles, dtype=jnp.int32) * tg
    tile_valid = tile_start < off_end[-1]
    te = jnp.sum((tile_start[:, None] >= off_end[None, :]).astype(jnp.int32), axis=1)
    last_e = jnp.sum(((off_end[-1] - 1) >= off_end).astype(jnp.int32))
    tile_expert = jnp.where(tile_valid, te, last_e)
    tile_rows = jnp.where(tile_valid, jnp.clip(counts[tile_expert] - (tile_start - off[tile_expert]), 0, tg), 0)
    changed = jnp.concatenate([jnp.ones((1,), jnp.bool_), tile_expert[1:] != tile_expert[:-1]])
    tile_flag = jnp.where(tile_valid, jnp.where(changed, TILE_FIRST, TILE_BODY), TILE_PAD).astype(jnp.int32)
    return dest.reshape(-1), tile_expert, tile_flag, tile_rows.astype(jnp.int32)


def kernel(x, c, ctx, c_ctx, norm1, norm2, w_ada, b_ada, w_in, ret_decay_f, ret_decay_b, attn_q_norm, attn_k_norm,
           w_ret_o, w_attn_o, w_out, w_router, b_router, w_exp_in, b_exp_in, w_exp_out, b_exp_out):
    assert w_in.shape[0] == 1, "single-layer block"
    b, seq, d = x.shape
    n_ctx = ctx.shape[1]
    t = b * seq
    rows = seq // GRID_W

    idx = np.cumsum(IN_SIZES)[:-1].tolist()
    wq_r, wk_r, wv_r, wg_r, wq_a, wk_a, wv_a, wgt_r, wgt_a = jnp.split(w_in[0], idx, axis=-1)
    w_in_p = jnp.concatenate([wq_r, wk_r, wv_r, wg_r, wq_a, wgt_r, wgt_a, wk_a, wv_a], axis=-1)
    w1 = w_exp_in[0]
    w2 = w_exp_out[0]
    b1 = b_exp_in[0].reshape(N_EXPERTS, 1, 2 * EXPERT_FF)
    b2 = b_exp_out[0].reshape(N_EXPERTS, 1, D_MODEL)

    pad = (-(b + 1)) % 8
    c_all = jnp.concatenate([c, c_ctx[None, :], jnp.zeros((pad, d), F32)], axis=0)
    mod = _ada(c_all, w_ada[0], b_ada[0])
    sh1, sc1, g1, sh2, sc2, g2 = [m.reshape(-1, 1, d) for m in jnp.split(mod, 6, axis=-1)]
    lat = lambda m: m[:b]
    cx = lambda m: m[b:b + 1]

    x2 = x.reshape(t, d)
    proj = _inproj(x2, norm1, lat(sc1), lat(sh1), w_in_p, seq)
    proj_c = _inproj(ctx.reshape(b * n_ctx, d), norm1, cx(sc1), cx(sh1), w_in_p, n_ctx)

    cos_r, sin_r = _rope_tables(rows, RET_QK_DIM)
    cos_a, sin_a = _rope_tables(rows, ATTN_HEAD_DIM)
    lg_f = -jax.nn.softplus(ret_decay_f[0].astype(F32))
    lg_b = -jax.nn.softplus(ret_decay_b[0].astype(F32))
    zero_state = jnp.zeros((2, b, RET_HEADS, RET_QK_DIM, RET_V_DIM), F32)
    cos_c = jnp.ones((n_ctx, RET_QK_DIM), F32)
    sin_c = jnp.zeros((n_ctx, RET_QK_DIM), F32)
    _, _, s_ctx = _retention(proj_c, cos_c, sin_c, lg_f, lg_b, zero_state, b, n_ctx, False)
    o_f, o_b, _ = _retention(proj, cos_r, sin_r, lg_f, lg_b, s_ctx, b, seq, True)

    k_lat = _kprep(proj, attn_k_norm, cos_a, sin_a, seq, True)
    k_ctx = _kprep(proj_c, attn_k_norm, cos_a, sin_a, n_ctx, False)
    k_all = jnp.concatenate([k_ctx.reshape(b, n_ctx, ATTN_KV_W), k_lat.reshape(b, seq, ATTN_KV_W)], axis=1)
    v_all = jnp.concatenate([proj_c[:, COL_VA:COL_VA + ATTN_KV_W].reshape(b, n_ctx, ATTN_KV_W),
                             proj[:, COL_VA:COL_VA + ATTN_KV_W].reshape(b, seq, ATTN_KV_W)], axis=1)
    lk = n_ctx + seq
    vt = jnp.swapaxes(v_all.reshape(b, lk, ATTN_KV_HEADS, ATTN_HEAD_DIM), 1, 3).swapaxes(1, 2)
    vt_all = jnp.concatenate([vt, jnp.ones((b, ATTN_KV_HEADS, ONES_ROWS, lk), vt.dtype)], axis=2)
    vt_all = vt_all.reshape(b, ATTN_KV_HEADS * (ATTN_HEAD_DIM + ONES_ROWS), lk)
    attn_o = _attention(proj, attn_q_norm, cos_a, sin_a, k_all, vt_all, b, seq)

    x1 = _merge(x2, lat(g1), o_f, o_b, proj, attn_o, w_ret_o[0].astype(BF16), w_attn_o[0].astype(BF16),
                w_out[0].astype(BF16), seq)

    h2, top_i, top_w = _router(x1, norm2, lat(sc2), lat(sh2), w_router[0].T, b_router[0].reshape(N_EXPERTS, 1), seq)
    n_groups = MOE_GROUPS if b % MOE_GROUPS == 0 else 1
    tgrp = t // n_groups
    tg = _largest_tile(TOP_K * tgrp, 512, 8)
    w_tok = top_w.T
    out = None
    for grp in range(n_groups):
        first = grp * tgrp
        dest, tile_expert, tile_flag, tile_rows = _plan(top_i[:, first:first + tgrp], tg)
        xs = _scatter_rows(h2, dest, TOP_K * tgrp + N_EXPERTS * tg, first, tgrp)
        ys = _ffn(xs, tile_expert, tile_flag, tile_rows, w1, b1, w2, b2, tg)
        yk = _gather_rows(ys, dest).reshape(TOP_K, tgrp, PACK_W)
        out = _combine(x1, lat(g2), w_tok, yk, seq, first, out)
    return out.reshape(b, seq, d)
```

```python
import functools

import jax
import jax.numpy as jnp
import numpy as np
from jax import lax
from jax.experimental import pallas as pl
from jax.experimental.pallas import tpu as pltpu
from jax.experimental.pallas import tpu_sc as plsc

F32 = jnp.float32
BF16 = jnp.bfloat16

D_MODEL = 1024
GRID_W = 64
EPS = 1e-6
RET_HEADS = 4
RET_QK_DIM = 256
RET_V_DIM = 512
ATTN_HEADS = 8
ATTN_KV_HEADS = 2
ATTN_GROUP = ATTN_HEADS // ATTN_KV_HEADS
ATTN_HEAD_DIM = 128
ROPE_THETA = 10000.0
N_EXPERTS = 32
TOP_K = 4
EXPERT_FF = 1024
SWIGLU_LIMIT = 7.0
SWIGLU_ALPHA = 1.702

RET_QK_W = RET_HEADS * RET_QK_DIM
RET_V_W = RET_HEADS * RET_V_DIM
ATTN_Q_W = ATTN_HEADS * ATTN_HEAD_DIM
ATTN_KV_W = ATTN_KV_HEADS * ATTN_HEAD_DIM
IN_SIZES = (RET_QK_W, RET_QK_W, RET_V_W, RET_V_W, ATTN_Q_W, ATTN_KV_W, ATTN_KV_W, D_MODEL, D_MODEL)
IN_WIDTH = sum(IN_SIZES)
COL_QR = 0
COL_KR = COL_QR + RET_QK_W
COL_VR = COL_KR + RET_QK_W
COL_GR = COL_VR + RET_V_W
COL_QA = COL_GR + RET_V_W
COL_GATE_R = COL_QA + ATTN_Q_W
COL_GATE_A = COL_GATE_R + D_MODEL
COL_KA = COL_GATE_A + D_MODEL
COL_VA = COL_KA + ATTN_KV_W

RET_CHUNK = 256
RET_HEADS_PER_STEP = 2
LANES = 128
SC_CORES = 2
SC_SUBCORES = 16
SC_GATHER_ROWS = 128
MXU_DEPTH = 256
ATTN_KEY_CHUNK = 768
ATTN_Q_TILE = 512
ATTN_EXP_SLAB = 32
ONES_ROWS = 16
KV_PREP_ROWS = 256
PACK_DTYPE = jnp.int32
PACK_W = D_MODEL // 2
MOE_GROUPS = 2
PLAN_BLOCK = 512
TILE_PAD, TILE_BODY, TILE_FIRST = 0, 1, 2
VMEM_LIMIT = 56 * 1024 * 1024

ARB = pltpu.ARBITRARY


def _params(n_axes, **kw):
    return pltpu.CompilerParams(dimension_semantics=(ARB,) * n_axes, vmem_limit_bytes=VMEM_LIMIT, **kw)


def _largest_tile(n, cap, mult):
    best = None
    for t in range(mult, min(n, cap) + 1, mult):
        if n % t == 0:
            best = t
    assert best is not None, (n, cap, mult)
    return best


def _rms(x):
    return x * lax.rsqrt(jnp.mean(x * x, axis=-1, keepdims=True) + EPS)


def _pack_pairs(x):
    half = x.shape[1] // 2
    lo = lax.bitcast_convert_type(x[:, :half].astype(BF16).astype(F32), jnp.int32)
    hi = lax.bitcast_convert_type(x[:, half:].astype(BF16).astype(F32), jnp.int32)
    return lax.bitwise_or(lax.bitwise_and(hi, jnp.int32(-65536)), lax.shift_right_logical(lo, jnp.int32(16)))


def _unpack_pairs(w):
    lo = lax.bitcast_convert_type(lax.shift_left(w, jnp.int32(16)), F32)
    hi = lax.bitcast_convert_type(lax.bitwise_and(w, jnp.int32(-65536)), F32)
    return lo, hi


def _rope_block(x, cos, sin, half):
    if 2 * half == LANES:
        swapped = pltpu.roll(x, half, 1)
    else:
        lane = lax.broadcasted_iota(jnp.int32, x.shape, 1)
        first = (lane % (2 * half)) < half
        swapped = jnp.where(first, pltpu.roll(x, LANES - half, 1), pltpu.roll(x, half, 1))
    return x * cos + swapped * sin


def _rope_tables(rows, head_dim):
    n_freq = head_dim // 4
    inv_freq = ROPE_THETA ** (-jnp.arange(n_freq, dtype=F32) / n_freq)
    ang_r = jnp.arange(rows, dtype=F32)[:, None] * inv_freq
    ang_c = jnp.arange(GRID_W, dtype=F32)[:, None] * inv_freq
    per_row = lambda a: jnp.broadcast_to(a[:, None, :], (rows, GRID_W, n_freq)).reshape(rows * GRID_W, n_freq)
    per_col = lambda a: jnp.broadcast_to(a[None, :, :], (rows, GRID_W, n_freq)).reshape(rows * GRID_W, n_freq)
    cos_r, sin_r = per_row(jnp.cos(ang_r)), per_row(jnp.sin(ang_r))
    cos_c, sin_c = per_col(jnp.cos(ang_c)), per_col(jnp.sin(ang_c))
    cos = jnp.concatenate([cos_r, cos_r, cos_c, cos_c], axis=1)
    sin = jnp.concatenate([-sin_r, sin_r, -sin_c, sin_c], axis=1)
    return cos, sin


def _ada_kernel(c_ref, w_ref, b_ref, o_ref):
    c = c_ref[...]
    s = c * jax.nn.sigmoid(c)
    o_ref[...] = jnp.dot(s, w_ref[...], preferred_element_type=F32,
                         precision=lax.Precision.HIGHEST) + b_ref[...]


def _ada(c_pad, w_ada, b_ada):
    rows = c_pad.shape[0]
    n = w_ada.shape[1]
    tn = _largest_tile(n, 1536, LANES)
    return pl.pallas_call(
        _ada_kernel,
        grid=(n // tn,),
        in_specs=[pl.BlockSpec((rows, D_MODEL), lambda j: (0, 0)),
                  pl.BlockSpec((D_MODEL, tn), lambda j: (0, j)),
                  pl.BlockSpec((1, tn), lambda j: (0, j))],
        out_specs=pl.BlockSpec((rows, tn), lambda j: (0, j)),
        out_shape=jax.ShapeDtypeStruct((rows, n), F32),
        compiler_params=_params(1),
        name="ada",
    )(c_pad, w_ada, b_ada.reshape(1, n))


def _inproj_kernel(x_ref, n_ref, sc_ref, sh_ref, w_ref, o_ref, w_sc):
    @pl.when(pl.program_id(1) == 0)
    def _():
        w_sc[...] = w_ref[...].astype(BF16)

    y = _rms(x_ref[...]) * n_ref[...]
    h = (y * (1.0 + sc_ref[0]) + sh_ref[0]).astype(BF16)
    o_ref[...] = jnp.dot(h, w_sc[...], preferred_element_type=F32).astype(o_ref.dtype)


def _inproj(x2, norm, sc, sh, w, rows_per_batch):
    t = x2.shape[0]
    n = w.shape[1]
    tm = _largest_tile(rows_per_batch, 512, 8)
    tn = _largest_tile(n, 2560, LANES)
    per_b = rows_per_batch // tm
    if sc.shape[0] == 1:
        mod_map = lambda j, i: (0, 0, 0)
    else:
        mod_map = lambda j, i: (i // per_b, 0, 0)
    return pl.pallas_call(
        _inproj_kernel,
        grid=(n // tn, t // tm),
        in_specs=[pl.BlockSpec((tm, D_MODEL), lambda j, i: (i, 0)),
                  pl.BlockSpec((1, D_MODEL), lambda j, i: (0, 0)),
                  pl.BlockSpec((1, 1, D_MODEL), mod_map),
                  pl.BlockSpec((1, 1, D_MODEL), mod_map),
                  pl.BlockSpec((D_MODEL, tn), lambda j, i: (0, j))],
        out_specs=pl.BlockSpec((tm, tn), lambda j, i: (i, j)),
        out_shape=jax.ShapeDtypeStruct((t, n), BF16),
        scratch_shapes=[pltpu.VMEM((D_MODEL, tn), BF16)],
        compiler_params=_params(2),
        name="inproj",
    )(x2, norm, sc, sh, w)


def _kvprep_kernel(kc_ref, vc_ref, kl_ref, vl_ref, g_ref, cos_ref, sin_ref, k_ref, vt_ref, *, ctx_blocks):
    j = pl.program_id(1)
    vrows = ATTN_HEAD_DIM + ONES_ROWS

    def emit(kin_ref, vin_ref, use_rope):
        k = kin_ref[...].astype(F32)
        v = vin_ref[...].astype(F32)
        for g in range(ATTN_KV_HEADS):
            sl = slice(g * ATTN_HEAD_DIM, (g + 1) * ATTN_HEAD_DIM)
            kh = _rms(k[:, sl]) * g_ref[...]
            if use_rope:
                kh = _rope_block(kh, cos_ref[...], sin_ref[...], ATTN_HEAD_DIM // 4)
            k_ref[0, :, sl] = kh.astype(BF16)
            vt_ref[0, g * vrows:g * vrows + ATTN_HEAD_DIM, :] = v[:, sl].T.astype(BF16)
            vt_ref[0, g * vrows + ATTN_HEAD_DIM:(g + 1) * vrows, :] = jnp.ones((ONES_ROWS, v.shape[0]), BF16)

    @pl.when(j < ctx_blocks)
    def _():
        emit(kc_ref, vc_ref, False)

    @pl.when(j >= ctx_blocks)
    def _():
        emit(kl_ref, vl_ref, True)


def _kvprep(proj_c, proj, gain, cos, sin, batch, n_ctx, seq):
    tm = KV_PREP_ROWS
    assert n_ctx % tm == 0 and seq % tm == 0
    cb, lb = n_ctx // tm, seq // tm
    lk = n_ctx + seq
    ctx_row = lambda b, j: b * cb + jnp.minimum(j, cb - 1)
    lat_blk = lambda j: jnp.maximum(j - cb, 0)
    lat_row = lambda b, j: b * lb + lat_blk(j)
    vrows = ATTN_KV_HEADS * (ATTN_HEAD_DIM + ONES_ROWS)
    return pl.pallas_call(
        functools.partial(_kvprep_kernel, ctx_blocks=cb),
        grid=(batch, cb + lb),
        in_specs=[pl.BlockSpec((tm, ATTN_KV_W), lambda b, j: (ctx_row(b, j), COL_KA // ATTN_KV_W)),
                  pl.BlockSpec((tm, ATTN_KV_W), lambda b, j: (ctx_row(b, j), COL_VA // ATTN_KV_W)),
                  pl.BlockSpec((tm, ATTN_KV_W), lambda b, j: (lat_row(b, j), COL_KA // ATTN_KV_W)),
                  pl.BlockSpec((tm, ATTN_KV_W), lambda b, j: (lat_row(b, j), COL_VA // ATTN_KV_W)),
                  pl.BlockSpec((1, ATTN_HEAD_DIM), lambda b, j: (0, 0)),
                  pl.BlockSpec((tm, ATTN_HEAD_DIM), lambda b, j: (lat_blk(j), 0)),
                  pl.BlockSpec((tm, ATTN_HEAD_DIM), lambda b, j: (lat_blk(j), 0))],
        out_specs=[pl.BlockSpec((1, tm, ATTN_KV_W), lambda b, j: (b, j, 0)),
                   pl.BlockSpec((1, vrows, tm), lambda b, j: (b, 0, j))],
        out_shape=[jax.ShapeDtypeStruct((batch, lk, ATTN_KV_W), BF16),
                   jax.ShapeDtypeStruct((batch, vrows, lk), BF16)],
        compiler_params=_params(2),
        name="kvprep",
    )(proj_c, proj_c, proj, proj, gain, cos, sin)


def _ret_kernel(dch_ref, qf_ref, kf_ref, vf_ref, cosf_ref, sinf_ref, qb_ref, kb_ref, vb_ref, cosb_ref, sinb_ref,
                dmat_ref, din_ref, dout_ref, s0_ref, of_ref, ob_ref, sfin_ref, s_sc, *, use_rope, n_chunks, heads):
    hb = pl.program_id(1)
    c = pl.program_id(2)

    @pl.when(c == 0)
    def _():
        s_sc[...] = s0_ref[:, 0]

    def rope(x, cos, sin):
        return jnp.concatenate([_rope_block(x[:, i * LANES:(i + 1) * LANES], cos[:, i * LANES:(i + 1) * LANES],
                                            sin[:, i * LANES:(i + 1) * LANES], RET_QK_DIM // 4)
                                for i in range(RET_QK_DIM // LANES)], axis=1)

    sides = ((qf_ref, kf_ref, vf_ref, cosf_ref, sinf_ref), (qb_ref, kb_ref, vb_ref, cosb_ref, sinb_ref))
    chains = [(d, j) for d in range(2) for j in range(heads)]
    qs, ks, vs, ss = {}, {}, {}, {}
    for d, j in chains:
        q_ref, k_ref, v_ref, cos_ref, sin_ref = sides[d]
        q = q_ref[:, j * RET_QK_DIM:(j + 1) * RET_QK_DIM].astype(F32)
        k = k_ref[:, j * RET_QK_DIM:(j + 1) * RET_QK_DIM].astype(F32)
        if use_rope:
            q = rope(q, cos_ref[...], sin_ref[...])
            k = rope(k, cos_ref[...], sin_ref[...])
        qs[d, j], ks[d, j] = q, k
        vs[d, j] = v_ref[:, j * RET_V_DIM:(j + 1) * RET_V_DIM]
        ss[d, j] = s_sc[d, j]
    scores = {ch: lax.dot_general(qs[ch].astype(BF16), ks[ch].astype(BF16), (((1,), (1,)), ((), ())),
                                  preferred_element_type=F32) * dmat_ref[ch[0], ch[1]] for ch in chains}
    inter = {ch: jnp.dot((qs[ch] * din_ref[ch[0], ch[1]]).astype(BF16), ss[ch].astype(BF16),
                         preferred_element_type=F32) for ch in chains}
    intra = {ch: jnp.dot(scores[ch].astype(BF16), vs[ch], preferred_element_type=F32) for ch in chains}
    o_refs = (of_ref, ob_ref)
    for d, j in chains:
        o_refs[d][:, j * RET_V_DIM:(j + 1) * RET_V_DIM] = (intra[d, j] + inter[d, j]).astype(of_ref.dtype)
    s_new = {ch: ss[ch] * dch_ref[ch[0] * RET_HEADS + hb * heads + ch[1]]
             + jnp.dot((ks[ch] * dout_ref[ch[0], ch[1]]).T.astype(BF16), vs[ch], preferred_element_type=F32)
             for ch in chains}
    for d, j in chains:
        s_sc[d, j] = s_new[d, j]

    @pl.when(c == n_chunks - 1)
    def _():
        for d, j in chains:
            sfin_ref[d, 0, j] = s_new[d, j]


def _ret_tables(log_gamma, chunk, reverse, k_scale):
    pos = jnp.arange(chunk, dtype=F32)
    diff = pos[:, None] - pos[None, :]
    if reverse:
        diff = -diff
        mask = diff > 0
        p_in = chunk - pos
        p_out = pos
    else:
        mask = diff >= 0
        p_in = pos + 1.0
        p_out = chunk - 1.0 - pos
    lg = log_gamma.astype(F32)
    dmat = jnp.where(mask[None], jnp.exp(lg[:, None, None] * jnp.maximum(diff, 0.0)[None]), 0.0) * k_scale
    d_in = jnp.exp(lg[:, None] * p_in)
    d_out = jnp.exp(lg[:, None] * p_out) * k_scale
    d_in = jnp.broadcast_to(d_in[:, :, None], (RET_HEADS, chunk, RET_QK_DIM))
    d_out = jnp.broadcast_to(d_out[:, :, None], (RET_HEADS, chunk, RET_QK_DIM))
    d_chunk = jnp.exp(lg * chunk)
    return d_chunk, dmat, d_in, d_out


def _retention(proj, cos, sin, lg_f, lg_b, s0, batch, seq, use_rope):
    chunk = min(RET_CHUNK, seq)
    nc = seq // chunk
    hs = RET_HEADS_PER_STEP
    qw, vw = hs * RET_QK_DIM, hs * RET_V_DIM
    k_scale = RET_QK_DIM ** -0.5
    tabs = [_ret_tables(lg_f, chunk, False, k_scale), _ret_tables(lg_b, chunk, True, k_scale)]
    d_chunk, dmat, d_in, d_out = [jnp.stack([tabs[0][i], tabs[1][i]]) for i in range(4)]
    fwd = lambda c: c
    bwd = lambda c: nc - 1 - c

    def side(cidx):
        row = lambda b, c: b * nc + cidx(c)
        return [pl.BlockSpec((chunk, qw), lambda b, h, c: (row(b, c), COL_QR // qw + h)),
                pl.BlockSpec((chunk, qw), lambda b, h, c: (row(b, c), COL_KR // qw + h)),
                pl.BlockSpec((chunk, vw), lambda b, h, c: (row(b, c), COL_VR // vw + h)),
                pl.BlockSpec((chunk, RET_QK_DIM), lambda b, h, c: (cidx(c), 0)),
                pl.BlockSpec((chunk, RET_QK_DIM), lambda b, h, c: (cidx(c), 0))]

    state_spec = pl.BlockSpec((2, 1, hs, RET_QK_DIM, RET_V_DIM), lambda b, h, c: (0, b, h, 0, 0))
    return pl.pallas_call(
        functools.partial(_ret_kernel, use_rope=use_rope, n_chunks=nc, heads=hs),
        grid=(batch, RET_HEADS // hs, nc),
        in_specs=[pl.BlockSpec(memory_space=pltpu.SMEM)] + side(fwd) + side(bwd) + [
            pl.BlockSpec((2, hs, chunk, chunk), lambda b, h, c: (0, h, 0, 0)),
            pl.BlockSpec((2, hs, chunk, RET_QK_DIM), lambda b, h, c: (0, h, 0, 0)),
            pl.BlockSpec((2, hs, chunk, RET_QK_DIM), lambda b, h, c: (0, h, 0, 0)),
            state_spec],
        out_specs=[pl.BlockSpec((chunk, vw), lambda b, h, c: (b * nc + c, h)),
                   pl.BlockSpec((chunk, vw), lambda b, h, c: (b * nc + nc - 1 - c, h)),
                   state_spec],
        out_shape=[jax.ShapeDtypeStruct((batch * seq, RET_V_W), BF16),
                   jax.ShapeDtypeStruct((batch * seq, RET_V_W), BF16),
                   jax.ShapeDtypeStruct((2, batch, RET_HEADS, RET_QK_DIM, RET_V_DIM), F32)],
        scratch_shapes=[pltpu.VMEM((2, hs, RET_QK_DIM, RET_V_DIM), F32)],
        compiler_params=_params(3),
        name="ret",
    )(d_chunk.reshape(-1), proj, proj, proj, cos, sin, proj, proj, proj, cos, sin, dmat, d_in, d_out, s0)


def _attn_kernel(q_ref, g_ref, cos_ref, sin_ref, k_ref, vt_ref, o_ref, qt_sc, m_sc, acc_sc,
                 s0_sc, s1_sc, p0_sc, p1_sc, a0_sc, a1_sc, *, tq, kc, n_chunks):
    s_bufs = (s0_sc, s1_sc)
    p_bufs = (p0_sc, p1_sc)
    a_bufs = (a0_sc, a1_sc)

    q = q_ref[...].astype(F32)
    scale = ATTN_HEAD_DIM ** -0.5 * np.log2(np.e)
    for g in range(ATTN_GROUP):
        qh = _rms(q[:, g * ATTN_HEAD_DIM:(g + 1) * ATTN_HEAD_DIM]) * g_ref[...]
        qh = _rope_block(qh, cos_ref[...], sin_ref[...], ATTN_HEAD_DIM // 4) * scale
        qt_sc[:, g * tq:(g + 1) * tq] = qh.T.astype(BF16)
    m_sc[...] = jnp.full(m_sc.shape, -jnp.inf, F32)
    acc_sc[...] = jnp.zeros(acc_sc.shape, F32)

    def stage_s(c, slot):
        off = pl.multiple_of(c * kc, kc)
        s_bufs[slot][...] = jnp.dot(k_ref[0, pl.ds(off, kc), :], qt_sc[...], preferred_element_type=F32)

    def stage_f(slot):
        m_prev = m_sc[...]
        m_new = jnp.maximum(m_prev, jnp.max(s_bufs[slot][...], axis=0, keepdims=True))
        m_sc[...] = m_new
        a_bufs[slot][...] = jnp.exp2(m_prev - m_new)
        for r in range(0, kc, ATTN_EXP_SLAB):
            p_bufs[slot][r:r + ATTN_EXP_SLAB, :] = jnp.exp2(
                s_bufs[slot][r:r + ATTN_EXP_SLAB, :] - m_new).astype(BF16)

    def stage_a(c, slot):
        off = pl.multiple_of(c * kc, kc)
        acc_sc[...] = a_bufs[slot][...] * acc_sc[...] + jnp.dot(
            vt_ref[0, :, pl.ds(off, kc)], p_bufs[slot][...], preferred_element_type=F32)

    def tick(t, parity):
        stage_s(t, parity)
        stage_f(1 - parity)
        stage_a(t - 2, parity)

    n = n_chunks
    stage_s(0, 0)
    if n > 1:
        stage_s(1, 1)
    stage_f(0)
    first = 2
    if n > 2 and (n - 2) % 2 == 1:
        tick(2, 0)
        first = 3
    n_pairs = (n - first) // 2 if n > first else 0
    if n_pairs > 0:
        def pair(u, carry):
            t = first + 2 * u
            tick(t, first % 2)
            tick(t + 1, 1 - first % 2)
            return carry
        lax.fori_loop(0, n_pairs, pair, 0)
    if n > 1:
        stage_f((n - 1) % 2)
        stage_a(n - 2, (n - 2) % 2)
    stage_a(n - 1, (n - 1) % 2)

    acc = acc_sc[...]
    o = acc[:ATTN_HEAD_DIM] / acc[ATTN_HEAD_DIM:ATTN_HEAD_DIM + 1]
    for g in range(ATTN_GROUP):
        o_ref[:, g * ATTN_HEAD_DIM:(g + 1) * ATTN_HEAD_DIM] = o[:, g * tq:(g + 1) * tq].T.astype(o_ref.dtype)


def _attention(proj, gain, cos, sin, k_all, vt_all, batch, seq):
    lk = k_all.shape[1]
    kc = _largest_tile(lk, ATTN_KEY_CHUNK, MXU_DEPTH)
    tq = _largest_tile(seq, ATTN_Q_TILE, 8)
    nq = seq // tq
    qw = ATTN_GROUP * ATTN_HEAD_DIM
    cols = ATTN_GROUP * tq
    vrows = ATTN_HEAD_DIM + ONES_ROWS
    return pl.pallas_call(
        functools.partial(_attn_kernel, tq=tq, kc=kc, n_chunks=lk // kc),
        grid=(batch, ATTN_KV_HEADS, nq),
        in_specs=[pl.BlockSpec((tq, qw), lambda b, g, i: (b * nq + i, COL_QA // qw + g)),
                  pl.BlockSpec((1, ATTN_HEAD_DIM), lambda b, g, i: (0, 0)),
                  pl.BlockSpec((tq, ATTN_HEAD_DIM), lambda b, g, i: (i, 0)),
                  pl.BlockSpec((tq, ATTN_HEAD_DIM), lambda b, g, i: (i, 0)),
                  pl.BlockSpec((1, lk, ATTN_HEAD_DIM), lambda b, g, i: (b, 0, g)),
                  pl.BlockSpec((1, vrows, lk), lambda b, g, i: (b, g, 0))],
        out_specs=pl.BlockSpec((tq, qw), lambda b, g, i: (b * nq + i, g)),
        out_shape=jax.ShapeDtypeStruct((batch * seq, ATTN_Q_W), BF16),
        scratch_shapes=[pltpu.VMEM((ATTN_HEAD_DIM, cols), BF16),
                        pltpu.VMEM((1, cols), F32),
                        pltpu.VMEM((vrows, cols), F32),
                        pltpu.VMEM((kc, cols), F32),
                        pltpu.VMEM((kc, cols), F32),
                        pltpu.VMEM((kc, cols), BF16),
                        pltpu.VMEM((kc, cols), BF16),
                        pltpu.VMEM((1, cols), F32),
                        pltpu.VMEM((1, cols), F32)],
        compiler_params=_params(3),
        name="attn",
    )(proj, gain, cos, sin, k_all, vt_all)


def _merge_kernel(x_ref, g1_ref, of_ref, ob_ref, gr_ref, ao_ref, gtr_ref, gta_ref,
                  wro_ref, wao_ref, wo_ref, o_ref):
    ro = of_ref[...].astype(F32) + ob_ref[...].astype(F32)
    gr = gr_ref[...].astype(F32)
    parts = []
    for h in range(RET_HEADS):
        sl = slice(h * RET_V_DIM, (h + 1) * RET_V_DIM)
        g = gr[:, sl]
        parts.append((g * jax.nn.sigmoid(g) * _rms(ro[:, sl])).astype(BF16))
    ret_in = jnp.concatenate(parts, axis=1)
    ret_branch = jnp.dot(ret_in, wro_ref[...], preferred_element_type=F32)
    attn_branch = jnp.dot(ao_ref[...], wao_ref[...], preferred_element_type=F32)
    y = (jax.nn.sigmoid(gtr_ref[...].astype(F32)) * ret_branch
         + jax.nn.sigmoid(gta_ref[...].astype(F32)) * attn_branch)
    y = jnp.dot(y.astype(BF16), wo_ref[...], preferred_element_type=F32)
    o_ref[...] = x_ref[...] + g1_ref[0] * y


def _merge(x2, g1, o_f, o_b, proj, attn_o, w_ret_o, w_attn_o, w_out, seq):
    t = x2.shape[0]
    tm = _largest_tile(seq, 256, 8)
    per_b = seq // tm
    full = lambda i: (0, 0)
    return pl.pallas_call(
        _merge_kernel,
        grid=(t // tm,),
        in_specs=[pl.BlockSpec((tm, D_MODEL), lambda i: (i, 0)),
                  pl.BlockSpec((1, 1, D_MODEL), lambda i: (i // per_b, 0, 0)),
                  pl.BlockSpec((tm, RET_V_W), lambda i: (i, 0)),
                  pl.BlockSpec((tm, RET_V_W), lambda i: (i, 0)),
                  pl.BlockSpec((tm, RET_V_W), lambda i: (i, COL_GR // RET_V_W)),
                  pl.BlockSpec((tm, ATTN_Q_W), lambda i: (i, 0)),
                  pl.BlockSpec((tm, D_MODEL), lambda i: (i, COL_GATE_R // D_MODEL)),
                  pl.BlockSpec((tm, D_MODEL), lambda i: (i, COL_GATE_A // D_MODEL)),
                  pl.BlockSpec((RET_V_W, D_MODEL), full),
                  pl.BlockSpec((ATTN_Q_W, D_MODEL), full),
                  pl.BlockSpec((D_MODEL, D_MODEL), full)],
        out_specs=pl.BlockSpec((tm, D_MODEL), lambda i: (i, 0)),
        out_shape=jax.ShapeDtypeStruct((t, D_MODEL), F32),
        compiler_params=_params(1),
        name="merge",
    )(x2, g1, o_f, o_b, proj, attn_o, proj, proj, w_ret_o, w_attn_o, w_out)


def _router_kernel(x_ref, n_ref, sc_ref, sh_ref, wr_ref, br_ref, h_ref, ti_ref, tw_ref):
    h = _rms(x_ref[...]) * n_ref[...] * (1.0 + sc_ref[0]) + sh_ref[0]
    h_ref[...] = _pack_pairs(h)
    nt = (((1,), (1,)), ((), ()))
    w = wr_ref[...]
    h_hi = h.astype(BF16)
    h_lo = (h - h_hi.astype(F32)).astype(BF16)
    w_hi = w.astype(BF16)
    w_lo = (w - w_hi.astype(F32)).astype(BF16)
    logits = (lax.dot_general(w_hi, h_hi, nt, preferred_element_type=F32)
              + lax.dot_general(w_lo, h_hi, nt, preferred_element_type=F32)
              + lax.dot_general(w_hi, h_lo, nt, preferred_element_type=F32)) + br_ref[...]
    eid = lax.broadcasted_iota(jnp.int32, logits.shape, 0)
    vals = logits
    top_v = []
    top_i = []
    for _ in range(TOP_K):
        m = jnp.max(vals, axis=0, keepdims=True)
        idx = jnp.min(jnp.where(vals == m, eid, N_EXPERTS), axis=0, keepdims=True)
        top_v.append(m)
        top_i.append(idx)
        vals = jnp.where(eid == idx, -jnp.inf, vals)
    ex = [jnp.exp(v - top_v[0]) for v in top_v]
    denom = ex[0] + ex[1] + ex[2] + ex[3]
    ti_ref[...] = jnp.concatenate(top_i, axis=0)
    tw_ref[...] = jnp.concatenate([e / denom for e in ex], axis=0)


def _router(x1, norm, sc, sh, w_router_t, b_router, seq):
    t = x1.shape[0]
    tm = _largest_tile(seq, 1024, LANES)
    per_b = seq // tm
    mod_map = lambda i: (i // per_b, 0, 0)
    return pl.pallas_call(
        _router_kernel,
        grid=(t // tm,),
        in_specs=[pl.BlockSpec((tm, D_MODEL), lambda i: (i, 0)),
                  pl.BlockSpec((1, D_MODEL), lambda i: (0, 0)),
                  pl.BlockSpec((1, 1, D_MODEL), mod_map),
                  pl.BlockSpec((1, 1, D_MODEL), mod_map),
                  pl.BlockSpec((N_EXPERTS, D_MODEL), lambda i: (0, 0)),
                  pl.BlockSpec((N_EXPERTS, 1), lambda i: (0, 0))],
        out_specs=[pl.BlockSpec((tm, PACK_W), lambda i: (i, 0)),
                   pl.BlockSpec((TOP_K, tm), lambda i: (0, i)),
                   pl.BlockSpec((TOP_K, tm), lambda i: (0, i))],
        out_shape=[jax.ShapeDtypeStruct((t, PACK_W), PACK_DTYPE),
                   jax.ShapeDtypeStruct((TOP_K, t), jnp.int32),
                   jax.ShapeDtypeStruct((TOP_K, t), F32)],
        compiler_params=_params(1),
        name="router",
    )(x1, norm, sc, sh, w_router_t, b_router)


def _gather_rows(src, idx):
    n, w = src.shape
    r = idx.shape[0]
    workers = SC_CORES * SC_SUBCORES
    per_w = r // workers
    n_win = per_w // SC_GATHER_ROWS
    assert per_w * workers == r and n_win * SC_GATHER_ROWS == per_w, (r, workers, SC_GATHER_ROWS)
    mesh = plsc.VectorSubcoreMesh(core_axis_name="c", subcore_axis_name="s")

    @functools.partial(
        pl.kernel, mesh=mesh, out_type=jax.ShapeDtypeStruct((r, w), src.dtype),
        scratch_types=[pltpu.VMEM((SC_GATHER_ROWS,), jnp.int32),
                       pltpu.VMEM((SC_GATHER_ROWS, w), src.dtype),
                       pltpu.SemaphoreType.DMA])
    def gather(src_hbm, idx_hbm, out_hbm, idx_v, rows_v, sem):
        wid = lax.axis_index("s") * SC_CORES + lax.axis_index("c")
        base = wid * per_w

        @pl.loop(0, n_win)
        def _(win):
            off = base + win * SC_GATHER_ROWS
            pltpu.sync_copy(idx_hbm.at[pl.ds(off, SC_GATHER_ROWS)], idx_v)
            pltpu.async_copy(src_hbm.at[idx_v], rows_v, sem).wait()
            pltpu.sync_copy(rows_v, out_hbm.at[pl.ds(off, SC_GATHER_ROWS)])

    return gather(src, idx)


def _scatter_rows(src, dest, n_out, first, group_rows):
    n, w = src.shape
    n_assign = dest.shape[0]
    workers = SC_CORES * SC_SUBCORES
    per_w = n_assign // workers
    n_win = per_w // SC_GATHER_ROWS
    assert per_w * workers == n_assign and n_win * SC_GATHER_ROWS == per_w and group_rows % per_w == 0
    mesh = plsc.VectorSubcoreMesh(core_axis_name="c", subcore_axis_name="s")

    @functools.partial(
        pl.kernel, mesh=mesh, out_type=jax.ShapeDtypeStruct((n_out, w), src.dtype),
        scratch_types=[pltpu.VMEM((SC_GATHER_ROWS,), jnp.int32),
                       pltpu.VMEM((SC_GATHER_ROWS, w), src.dtype),
                       pltpu.SemaphoreType.DMA])
    def scatter(src_hbm, dest_hbm, out_hbm, idx_v, rows_v, sem):
        wid = lax.axis_index("s") * SC_CORES + lax.axis_index("c")
        base = wid * per_w
        row_base = first + lax.rem(base, group_rows)

        @pl.loop(0, n_win)
        def _(win):
            pltpu.sync_copy(dest_hbm.at[pl.ds(base + win * SC_GATHER_ROWS, SC_GATHER_ROWS)], idx_v)
            pltpu.sync_copy(src_hbm.at[pl.ds(row_base + win * SC_GATHER_ROWS, SC_GATHER_ROWS)], rows_v)
            pltpu.async_copy(rows_v, out_hbm.at[idx_v], sem).wait()

    return scatter(src, dest)


def _ffn_kernel(te_ref, tf_ref, tr_ref, x_ref, w1_ref, b1_ref, w2_ref, b2_ref, o_ref, w1_sc, w2_sc):
    flag = tf_ref[pl.program_id(0)]
    n_rows = tr_ref[pl.program_id(0)]

    @pl.when(flag == TILE_FIRST)
    def _():
        w1_sc[...] = w1_ref[0].astype(BF16)
        w2_sc[...] = w2_ref[0].astype(BF16)

    @pl.when(flag == TILE_PAD)
    def _():
        o_ref[...] = jnp.zeros(o_ref.shape, o_ref.dtype)

    @pl.when(flag != TILE_PAD)
    def _():
        xw = x_ref[...]
        rid = lax.broadcasted_iota(jnp.int32, xw.shape, 0)
        lo, hi = _unpack_pairs(jnp.where(rid < n_rows, xw, jnp.zeros_like(xw)))
        x = jnp.concatenate([lo, hi], axis=1).astype(BF16)
        a = jnp.dot(x, w1_sc[...], preferred_element_type=F32) + b1_ref[0]
        gate = jnp.minimum(a[:, :EXPERT_FF], SWIGLU_LIMIT)
        up = jnp.clip(a[:, EXPERT_FF:], -SWIGLU_LIMIT, SWIGLU_LIMIT)
        act = gate * jax.nn.sigmoid(SWIGLU_ALPHA * gate) * (up + 1.0)
        y = jnp.dot(act.astype(BF16), w2_sc[...], preferred_element_type=F32) + b2_ref[0]
        o_ref[...] = _pack_pairs(y)


def _ffn(xs, tile_expert, tile_flag, tile_rows, w1, b1, w2, b2, tg):
    p = xs.shape[0]
    grid_spec = pltpu.PrefetchScalarGridSpec(
        num_scalar_prefetch=3,
        grid=(p // tg,),
        in_specs=[pl.BlockSpec((tg, PACK_W), lambda j, te, tf, tr: (j, 0)),
                  pl.BlockSpec((1, D_MODEL, 2 * EXPERT_FF), lambda j, te, tf, tr: (te[j], 0, 0)),
                  pl.BlockSpec((1, 1, 2 * EXPERT_FF), lambda j, te, tf, tr: (te[j], 0, 0)),
                  pl.BlockSpec((1, EXPERT_FF, D_MODEL), lambda j, te, tf, tr: (te[j], 0, 0)),
                  pl.BlockSpec((1, 1, D_MODEL), lambda j, te, tf, tr: (te[j], 0, 0))],
        out_specs=pl.BlockSpec((tg, PACK_W), lambda j, te, tf, tr: (j, 0)),
        scratch_shapes=[pltpu.VMEM((D_MODEL, 2 * EXPERT_FF), BF16),
                        pltpu.VMEM((EXPERT_FF, D_MODEL), BF16)],
    )
    return pl.pallas_call(
        _ffn_kernel,
        grid_spec=grid_spec,
        out_shape=jax.ShapeDtypeStruct((p, PACK_W), PACK_DTYPE),
        compiler_params=_params(1),
        name="ffn",
    )(tile_expert, tile_flag, tile_rows, xs, w1, b1, w2, b2)


def _combine_kernel(x_ref, g2_ref, w_ref, y_ref, o_ref):
    w = w_ref[...]
    acc_lo = None
    for k in range(TOP_K):
        lo, hi = _unpack_pairs(y_ref[k])
        wk = w[:, k:k + 1]
        acc_lo = wk * lo if acc_lo is None else acc_lo + wk * lo
        acc_hi = wk * hi if k == 0 else acc_hi + wk * hi
    acc = jnp.concatenate([acc_lo, acc_hi], axis=1)
    o_ref[...] = x_ref[...] + g2_ref[0] * acc


def _combine_into_kernel(x_ref, g2_ref, w_ref, y_ref, prev_ref, o_ref):
    del prev_ref
    _combine_kernel(x_ref, g2_ref, w_ref, y_ref, o_ref)


def _combine(x1, g2, w_tok, yk, seq, first_row, prev):
    t = x1.shape[0]
    rows = yk.shape[1]
    tm = _largest_tile(seq, 512, 8)
    per_b = seq // tm
    i0 = first_row // tm
    in_specs = [pl.BlockSpec((tm, D_MODEL), lambda i: (i0 + i, 0)),
                pl.BlockSpec((1, 1, D_MODEL), lambda i: ((i0 + i) // per_b, 0, 0)),
                pl.BlockSpec((tm, TOP_K), lambda i: (i0 + i, 0)),
                pl.BlockSpec((TOP_K, tm, PACK_W), lambda i: (0, i, 0))]
    args = [x1, g2, w_tok, yk]
    if prev is not None:
        in_specs.append(pl.BlockSpec(memory_space=pl.ANY))
        args.append(prev)
    return pl.pallas_call(
        _combine_kernel if prev is None else _combine_into_kernel,
        grid=(rows // tm,),
        in_specs=in_specs,
        out_specs=pl.BlockSpec((tm, D_MODEL), lambda i: (i0 + i, 0)),
        out_shape=jax.ShapeDtypeStruct((t, D_MODEL), F32),
        input_output_aliases={} if prev is None else {len(args) - 1: 0},
        compiler_params=_params(1),
        name="combine",
    )(*args)


def _count_kernel(ti_ref, cnt_ref):
    @pl.when(pl.program_id(0) == 0)
    def _():
        cnt_ref[...] = jnp.zeros(cnt_ref.shape, F32)

    bt = ti_ref.shape[1]
    eid = lax.broadcasted_iota(jnp.int32, (N_EXPERTS, bt), 0)
    acc = jnp.zeros(cnt_ref.shape, F32)
    for k in range(TOP_K):
        m = (eid == ti_ref[k:k + 1, :]).astype(F32)
        for c in range(bt // LANES):
            acc = acc + m[:, c * LANES:(c + 1) * LANES]
    cnt_ref[...] += acc


def _rank_kernel(ti_ref, off_ref, tri_ref, dest_ref, run_sc):
    @pl.when(pl.program_id(0) == 0)
    def _():
        run_sc[...] = off_ref[...] - 1.0

    bt = ti_ref.shape[1]
    eid = lax.broadcasted_iota(jnp.int32, (N_EXPERTS, bt), 0)
    run = run_sc[...]
    for k in range(TOP_K):
        m = eid == ti_ref[k:k + 1, :]
        pre = jnp.dot(jnp.where(m, 1.0, 0.0).astype(BF16), tri_ref[...], preferred_element_type=F32)
        slot = jnp.sum(jnp.where(m, pre + run, 0.0), axis=0, keepdims=True)
        dest_ref[k:k + 1, :] = slot.astype(jnp.int32)
        run = run + pre[:, bt - 1:bt]
    run_sc[...] = run


def _plan(top_i, tg):
    t = top_i.shape[1]
    n_assign = TOP_K * t
    p = n_assign + N_EXPERTS * tg
    n_tiles = p // tg
    bt = _largest_tile(t, PLAN_BLOCK, LANES)
    cnt = pl.pallas_call(
        _count_kernel,
        grid=(t // bt,),
        in_specs=[pl.BlockSpec((TOP_K, bt), lambda i: (0, i))],
        out_specs=pl.BlockSpec((N_EXPERTS, LANES), lambda i: (0, 0)),
        out_shape=jax.ShapeDtypeStruct((N_EXPERTS, LANES), F32),
        compiler_params=_params(1),
        name="count",
    )(top_i)
    counts = jnp.sum(cnt, axis=1).astype(jnp.int32)
    padded = ((counts + tg - 1) // tg) * tg
    off_end = jnp.cumsum(padded)
    off = off_end - padded
    tri = (jnp.arange(bt)[:, None] <= jnp.arange(bt)[None, :]).astype(BF16)
    dest = pl.pallas_call(
        _rank_kernel,
        grid=(t // bt,),
        in_specs=[pl.BlockSpec((TOP_K, bt), lambda i: (0, i)),
                  pl.BlockSpec((N_EXPERTS, 1), lambda i: (0, 0)),
                  pl.BlockSpec((bt, bt), lambda i: (0, 0))],
        out_specs=pl.BlockSpec((TOP_K, bt), lambda i: (0, i)),
        out_shape=jax.ShapeDtypeStruct((TOP_K, t), jnp.int32),
        scratch_shapes=[pltpu.VMEM((N_EXPERTS, 1), F32)],
        compiler_params=_params(1),
        name="rank",
    )(top_i, off.astype(F32).reshape(N_EXPERTS, 1), tri)
    tile_start = jnp.arange(n_tiles, dtype=jnp.int32) * tg
    tile_valid = tile_start < off_end[-1]
    te = jnp.sum((tile_start[:, None] >= off_end[None, :]).astype(jnp.int32), axis=1)
    last_e = jnp.sum(((off_end[-1] - 1) >= off_end).astype(jnp.int32))
    tile_expert = jnp.where(tile_valid, te, last_e)
    tile_rows = jnp.where(tile_valid, jnp.clip(counts[tile_expert] - (tile_start - off[tile_expert]), 0, tg), 0)
    changed = jnp.concatenate([jnp.ones((1,), jnp.bool_), tile_expert[1:] != tile_expert[:-1]])
    tile_flag = jnp.where(tile_valid, jnp.where(changed, TILE_FIRST, TILE_BODY), TILE_PAD).astype(jnp.int32)
    return dest.reshape(-1), tile_expert, tile_flag, tile_rows.astype(jnp.int32)


def kernel(x, c, ctx, c_ctx, norm1, norm2, w_ada, b_ada, w_in, ret_decay_f, ret_decay_b, attn_q_norm, attn_k_norm,
           w_ret_o, w_attn_o, w_out, w_router, b_router, w_exp_in, b_exp_in, w_exp_out, b_exp_out):
    assert w_in.shape[0] == 1, "single-layer block"
    b, seq, d = x.shape
    n_ctx = ctx.shape[1]
    t = b * seq
    rows = seq // GRID_W

    idx = np.cumsum(IN_SIZES)[:-1].tolist()
    wq_r, wk_r, wv_r, wg_r, wq_a, wk_a, wv_a, wgt_r, wgt_a = jnp.split(w_in[0], idx, axis=-1)
    w_in_p = jnp.concatenate([wq_r, wk_r, wv_r, wg_r, wq_a, wgt_r, wgt_a, wk_a, wv_a], axis=-1)
    w1 = w_exp_in[0]
    w2 = w_exp_out[0]
    b1 = b_exp_in[0].reshape(N_EXPERTS, 1, 2 * EXPERT_FF)
    b2 = b_exp_out[0].reshape(N_EXPERTS, 1, D_MODEL)

    pad = (-(b + 1)) % 8
    c_all = jnp.concatenate([c, c_ctx[None, :], jnp.zeros((pad, d), F32)], axis=0)
    mod = _ada(c_all, w_ada[0], b_ada[0])
    sh1, sc1, g1, sh2, sc2, g2 = [m.reshape(-1, 1, d) for m in jnp.split(mod, 6, axis=-1)]
    lat = lambda m: m[:b]
    cx = lambda m: m[b:b + 1]

    x2 = x.reshape(t, d)
    proj = _inproj(x2, norm1, lat(sc1), lat(sh1), w_in_p, seq)
    proj_c = _inproj(ctx.reshape(b * n_ctx, d), norm1, cx(sc1), cx(sh1), w_in_p, n_ctx)

    cos_r, sin_r = _rope_tables(rows, RET_QK_DIM)
    cos_a, sin_a = _rope_tables(rows, ATTN_HEAD_DIM)
    lg_f = -jax.nn.softplus(ret_decay_f[0].astype(F32))
    lg_b = -jax.nn.softplus(ret_decay_b[0].astype(F32))
    zero_state = jnp.zeros((2, b, RET_HEADS, RET_QK_DIM, RET_V_DIM), F32)
    cos_c = jnp.ones((n_ctx, RET_QK_DIM), F32)
    sin_c = jnp.zeros((n_ctx, RET_QK_DIM), F32)
    _, _, s_ctx = _retention(proj_c, cos_c, sin_c, lg_f, lg_b, zero_state, b, n_ctx, False)
    o_f, o_b, _ = _retention(proj, cos_r, sin_r, lg_f, lg_b, s_ctx, b, seq, True)

    k_all, vt_all = _kvprep(proj_c, proj, attn_k_norm, cos_a, sin_a, b, n_ctx, seq)
    attn_o = _attention(proj, attn_q_norm, cos_a, sin_a, k_all, vt_all, b, seq)

    x1 = _merge(x2, lat(g1), o_f, o_b, proj, attn_o, w_ret_o[0].astype(BF16), w_attn_o[0].astype(BF16),
                w_out[0].astype(BF16), seq)

    h2, top_i, top_w = _router(x1, norm2, lat(sc2), lat(sh2), w_router[0].T, b_router[0].reshape(N_EXPERTS, 1), seq)
    n_groups = MOE_GROUPS if b % MOE_GROUPS == 0 else 1
    tgrp = t // n_groups
    tg = _largest_tile(TOP_K * tgrp, 512, 8)
    w_tok = top_w.T
    out = None
    for grp in range(n_groups):
        first = grp * tgrp
        dest, tile_expert, tile_flag, tile_rows = _plan(top_i[:, first:first + tgrp], tg)
        xs = _scatter_rows(h2, dest, TOP_K * tgrp + N_EXPERTS * tg, first, tgrp)
        ys = _ffn(xs, tile_expert, tile_flag, tile_rows, w1, b1, w2, b2, tg)
        yk = _gather_rows(ys, dest).reshape(TOP_K, tgrp, PACK_W)
        out = _combine(x1, lat(g2), w_tok, yk, seq, first, out)
    return out.reshape(b, seq, d)
```

```python
import functools

import jax
import jax.numpy as jnp
import numpy as np
from jax import lax
from jax.experimental import pallas as pl
from jax.experimental.pallas import tpu as pltpu
from jax.experimental.pallas import tpu_sc as plsc

F32 = jnp.float32
BF16 = jnp.bfloat16

D_MODEL = 1024
GRID_W = 64
EPS = 1e-6
RET_HEADS = 4
RET_QK_DIM = 256
RET_V_DIM = 512
ATTN_HEADS = 8
ATTN_KV_HEADS = 2
ATTN_GROUP = ATTN_HEADS // ATTN_KV_HEADS
ATTN_HEAD_DIM = 128
ROPE_THETA = 10000.0
N_EXPERTS = 32
TOP_K = 4
EXPERT_FF = 1024
SWIGLU_LIMIT = 7.0
SWIGLU_ALPHA = 1.702

RET_QK_W = RET_HEADS * RET_QK_DIM
RET_V_W = RET_HEADS * RET_V_DIM
ATTN_Q_W = ATTN_HEADS * ATTN_HEAD_DIM
ATTN_KV_W = ATTN_KV_HEADS * ATTN_HEAD_DIM
IN_SIZES = (RET_QK_W, RET_QK_W, RET_V_W, RET_V_W, ATTN_Q_W, ATTN_KV_W, ATTN_KV_W, D_MODEL, D_MODEL)
IN_WIDTH = sum(IN_SIZES)
COL_QR = 0
COL_KR = COL_QR + RET_QK_W
COL_VR = COL_KR + RET_QK_W
COL_GR = COL_VR + RET_V_W
COL_QA = COL_GR + RET_V_W
COL_GATE_R = COL_QA + ATTN_Q_W
COL_GATE_A = COL_GATE_R + D_MODEL
COL_KA = COL_GATE_A + D_MODEL
COL_VA = COL_KA + ATTN_KV_W

RET_CHUNK = 256
RET_HEADS_PER_STEP = 4
LANES = 128
SC_CORES = 2
SC_SUBCORES = 16
SC_GATHER_ROWS = 128
MXU_DEPTH = 256
ATTN_KEY_CHUNK = 768
ATTN_Q_TILE = 512
ATTN_EXP_SLAB = 32
ONES_ROWS = 16
KV_PREP_ROWS = 256
PACK_DTYPE = jnp.int32
PACK_W = D_MODEL // 2
MOE_GROUPS = 2
PLAN_BLOCK = 512
TILE_PAD, TILE_BODY, TILE_FIRST = 0, 1, 2
VMEM_LIMIT = 56 * 1024 * 1024

ARB = pltpu.ARBITRARY


def _params(n_axes, **kw):
    return pltpu.CompilerParams(dimension_semantics=(ARB,) * n_axes, vmem_limit_bytes=VMEM_LIMIT, **kw)


def _largest_tile(n, cap, mult):
    best = None
    for t in range(mult, min(n, cap) + 1, mult):
        if n % t == 0:
            best = t
    assert best is not None, (n, cap, mult)
    return best


def _rms(x):
    return x * lax.rsqrt(jnp.mean(x * x, axis=-1, keepdims=True) + EPS)


def _pack_pairs(x):
    half = x.shape[1] // 2
    lo = lax.bitcast_convert_type(x[:, :half].astype(BF16).astype(F32), jnp.int32)
    hi = lax.bitcast_convert_type(x[:, half:].astype(BF16).astype(F32), jnp.int32)
    return lax.bitwise_or(lax.bitwise_and(hi, jnp.int32(-65536)), lax.shift_right_logical(lo, jnp.int32(16)))


def _unpack_pairs(w):
    lo = lax.bitcast_convert_type(lax.shift_left(w, jnp.int32(16)), F32)
    hi = lax.bitcast_convert_type(lax.bitwise_and(w, jnp.int32(-65536)), F32)
    return lo, hi


def _rope_block(x, cos, sin, half):
    if 2 * half == LANES:
        swapped = pltpu.roll(x, half, 1)
    else:
        lane = lax.broadcasted_iota(jnp.int32, x.shape, 1)
        first = (lane % (2 * half)) < half
        swapped = jnp.where(first, pltpu.roll(x, LANES - half, 1), pltpu.roll(x, half, 1))
    return x * cos + swapped * sin


def _rope_tables(rows, head_dim):
    n_freq = head_dim // 4
    inv_freq = ROPE_THETA ** (-jnp.arange(n_freq, dtype=F32) / n_freq)
    ang_r = jnp.arange(rows, dtype=F32)[:, None] * inv_freq
    ang_c = jnp.arange(GRID_W, dtype=F32)[:, None] * inv_freq
    per_row = lambda a: jnp.broadcast_to(a[:, None, :], (rows, GRID_W, n_freq)).reshape(rows * GRID_W, n_freq)
    per_col = lambda a: jnp.broadcast_to(a[None, :, :], (rows, GRID_W, n_freq)).reshape(rows * GRID_W, n_freq)
    cos_r, sin_r = per_row(jnp.cos(ang_r)), per_row(jnp.sin(ang_r))
    cos_c, sin_c = per_col(jnp.cos(ang_c)), per_col(jnp.sin(ang_c))
    cos = jnp.concatenate([cos_r, cos_r, cos_c, cos_c], axis=1)
    sin = jnp.concatenate([-sin_r, sin_r, -sin_c, sin_c], axis=1)
    return cos, sin


def _ada_kernel(c_ref, w_ref, b_ref, o_ref):
    c = c_ref[...]
    s = c * jax.nn.sigmoid(c)
    o_ref[...] = jnp.dot(s, w_ref[...], preferred_element_type=F32,
                         precision=lax.Precision.HIGHEST) + b_ref[...]


def _ada(c_pad, w_ada, b_ada):
    rows = c_pad.shape[0]
    n = w_ada.shape[1]
    tn = _largest_tile(n, 1536, LANES)
    return pl.pallas_call(
        _ada_kernel,
        grid=(n // tn,),
        in_specs=[pl.BlockSpec((rows, D_MODEL), lambda j: (0, 0)),
                  pl.BlockSpec((D_MODEL, tn), lambda j: (0, j)),
                  pl.BlockSpec((1, tn), lambda j: (0, j))],
        out_specs=pl.BlockSpec((rows, tn), lambda j: (0, j)),
        out_shape=jax.ShapeDtypeStruct((rows, n), F32),
        compiler_params=_params(1),
        name="ada",
    )(c_pad, w_ada, b_ada.reshape(1, n))


def _inproj_kernel(x_ref, n_ref, sc_ref, sh_ref, w_ref, o_ref, w_sc):
    @pl.when(pl.program_id(1) == 0)
    def _():
        w_sc[...] = w_ref[...].astype(BF16)

    y = _rms(x_ref[...]) * n_ref[...]
    h = (y * (1.0 + sc_ref[0]) + sh_ref[0]).astype(BF16)
    o_ref[...] = jnp.dot(h, w_sc[...], preferred_element_type=F32).astype(o_ref.dtype)


def _inproj(x2, norm, sc, sh, w, rows_per_batch):
    t = x2.shape[0]
    n = w.shape[1]
    tm = _largest_tile(rows_per_batch, 512, 8)
    tn = _largest_tile(n, 2560, LANES)
    per_b = rows_per_batch // tm
    if sc.shape[0] == 1:
        mod_map = lambda j, i: (0, 0, 0)
    else:
        mod_map = lambda j, i: (i // per_b, 0, 0)
    return pl.pallas_call(
        _inproj_kernel,
        grid=(n // tn, t // tm),
        in_specs=[pl.BlockSpec((tm, D_MODEL), lambda j, i: (i, 0)),
                  pl.BlockSpec((1, D_MODEL), lambda j, i: (0, 0)),
                  pl.BlockSpec((1, 1, D_MODEL), mod_map),
                  pl.BlockSpec((1, 1, D_MODEL), mod_map),
                  pl.BlockSpec((D_MODEL, tn), lambda j, i: (0, j))],
        out_specs=pl.BlockSpec((tm, tn), lambda j, i: (i, j)),
        out_shape=jax.ShapeDtypeStruct((t, n), BF16),
        scratch_shapes=[pltpu.VMEM((D_MODEL, tn), BF16)],
        compiler_params=_params(2),
        name="inproj",
    )(x2, norm, sc, sh, w)


def _kvprep_kernel(kc_ref, vc_ref, kl_ref, vl_ref, g_ref, cos_ref, sin_ref, k_ref, vt_ref, *, ctx_blocks):
    j = pl.program_id(1)
    vrows = ATTN_HEAD_DIM + ONES_ROWS

    def emit(kin_ref, vin_ref, use_rope):
        k = kin_ref[...].astype(F32)
        v = vin_ref[...].astype(F32)
        for g in range(ATTN_KV_HEADS):
            sl = slice(g * ATTN_HEAD_DIM, (g + 1) * ATTN_HEAD_DIM)
            kh = _rms(k[:, sl]) * g_ref[...]
            if use_rope:
                kh = _rope_block(kh, cos_ref[...], sin_ref[...], ATTN_HEAD_DIM // 4)
            k_ref[0, :, sl] = kh.astype(BF16)
            vt_ref[0, g * vrows:g * vrows + ATTN_HEAD_DIM, :] = v[:, sl].T.astype(BF16)
            vt_ref[0, g * vrows + ATTN_HEAD_DIM:(g + 1) * vrows, :] = jnp.ones((ONES_ROWS, v.shape[0]), BF16)

    @pl.when(j < ctx_blocks)
    def _():
        emit(kc_ref, vc_ref, False)

    @pl.when(j >= ctx_blocks)
    def _():
        emit(kl_ref, vl_ref, True)


def _kvprep(proj_c, proj, gain, cos, sin, batch, n_ctx, seq):
    tm = KV_PREP_ROWS
    assert n_ctx % tm == 0 and seq % tm == 0
    cb, lb = n_ctx // tm, seq // tm
    lk = n_ctx + seq
    ctx_row = lambda b, j: b * cb + jnp.minimum(j, cb - 1)
    lat_blk = lambda j: jnp.maximum(j - cb, 0)
    lat_row = lambda b, j: b * lb + lat_blk(j)
    vrows = ATTN_KV_HEADS * (ATTN_HEAD_DIM + ONES_ROWS)
    return pl.pallas_call(
        functools.partial(_kvprep_kernel, ctx_blocks=cb),
        grid=(batch, cb + lb),
        in_specs=[pl.BlockSpec((tm, ATTN_KV_W), lambda b, j: (ctx_row(b, j), COL_KA // ATTN_KV_W)),
                  pl.BlockSpec((tm, ATTN_KV_W), lambda b, j: (ctx_row(b, j), COL_VA // ATTN_KV_W)),
                  pl.BlockSpec((tm, ATTN_KV_W), lambda b, j: (lat_row(b, j), COL_KA // ATTN_KV_W)),
                  pl.BlockSpec((tm, ATTN_KV_W), lambda b, j: (lat_row(b, j), COL_VA // ATTN_KV_W)),
                  pl.BlockSpec((1, ATTN_HEAD_DIM), lambda b, j: (0, 0)),
                  pl.BlockSpec((tm, ATTN_HEAD_DIM), lambda b, j: (lat_blk(j), 0)),
                  pl.BlockSpec((tm, ATTN_HEAD_DIM), lambda b, j: (lat_blk(j), 0))],
        out_specs=[pl.BlockSpec((1, tm, ATTN_KV_W), lambda b, j: (b, j, 0)),
                   pl.BlockSpec((1, vrows, tm), lambda b, j: (b, 0, j))],
        out_shape=[jax.ShapeDtypeStruct((batch, lk, ATTN_KV_W), BF16),
                   jax.ShapeDtypeStruct((batch, vrows, lk), BF16)],
        compiler_params=_params(2),
        name="kvprep",
    )(proj_c, proj_c, proj, proj, gain, cos, sin)


def _ret_kernel(dch_ref, qf_ref, kf_ref, vf_ref, cosf_ref, sinf_ref, qb_ref, kb_ref, vb_ref, cosb_ref, sinb_ref,
                dmat_ref, din_ref, dout_ref, s0_ref, of_ref, ob_ref, sfin_ref, s_sc, *, use_rope, n_chunks, heads):
    hb = pl.program_id(1)
    c = pl.program_id(2)

    @pl.when(c == 0)
    def _():
        s_sc[...] = s0_ref[:, 0]

    def rope(x, cos, sin):
        return jnp.concatenate([_rope_block(x[:, i * LANES:(i + 1) * LANES], cos[:, i * LANES:(i + 1) * LANES],
                                            sin[:, i * LANES:(i + 1) * LANES], RET_QK_DIM // 4)
                                for i in range(RET_QK_DIM // LANES)], axis=1)

    sides = ((qf_ref, kf_ref, vf_ref, cosf_ref, sinf_ref), (qb_ref, kb_ref, vb_ref, cosb_ref, sinb_ref))
    chains = [(d, j) for d in range(2) for j in range(heads)]
    qs, ks, vs, ss = {}, {}, {}, {}
    for d, j in chains:
        q_ref, k_ref, v_ref, cos_ref, sin_ref = sides[d]
        q = q_ref[:, j * RET_QK_DIM:(j + 1) * RET_QK_DIM].astype(F32)
        k = k_ref[:, j * RET_QK_DIM:(j + 1) * RET_QK_DIM].astype(F32)
        if use_rope:
            q = rope(q, cos_ref[...], sin_ref[...])
            k = rope(k, cos_ref[...], sin_ref[...])
        qs[d, j], ks[d, j] = q, k
        vs[d, j] = v_ref[:, j * RET_V_DIM:(j + 1) * RET_V_DIM]
        ss[d, j] = s_sc[d, j]
    scores = {ch: lax.dot_general(qs[ch].astype(BF16), ks[ch].astype(BF16), (((1,), (1,)), ((), ())),
                                  preferred_element_type=F32) * dmat_ref[ch[0], ch[1]] for ch in chains}
    inter = {ch: jnp.dot((qs[ch] * din_ref[ch[0], ch[1]]).astype(BF16), ss[ch].astype(BF16),
                         preferred_element_type=F32) for ch in chains}
    intra = {ch: jnp.dot(scores[ch].astype(BF16), vs[ch], preferred_element_type=F32) for ch in chains}
    o_refs = (of_ref, ob_ref)
    for d, j in chains:
        o_refs[d][:, j * RET_V_DIM:(j + 1) * RET_V_DIM] = (intra[d, j] + inter[d, j]).astype(of_ref.dtype)
    s_new = {ch: ss[ch] * dch_ref[ch[0] * RET_HEADS + hb * heads + ch[1]]
             + jnp.dot((ks[ch] * dout_ref[ch[0], ch[1]]).T.astype(BF16), vs[ch], preferred_element_type=F32)
             for ch in chains}
    for d, j in chains:
        s_sc[d, j] = s_new[d, j]

    @pl.when(c == n_chunks - 1)
    def _():
        for d, j in chains:
            sfin_ref[d, 0, j] = s_new[d, j]


def _ret_tables(log_gamma, chunk, reverse, k_scale):
    pos = jnp.arange(chunk, dtype=F32)
    diff = pos[:, None] - pos[None, :]
    if reverse:
        diff = -diff
        mask = diff > 0
        p_in = chunk - pos
        p_out = pos
    else:
        mask = diff >= 0
        p_in = pos + 1.0
        p_out = chunk - 1.0 - pos
    lg = log_gamma.astype(F32)
    dmat = jnp.where(mask[None], jnp.exp(lg[:, None, None] * jnp.maximum(diff, 0.0)[None]), 0.0) * k_scale
    d_in = jnp.exp(lg[:, None] * p_in)
    d_out = jnp.exp(lg[:, None] * p_out) * k_scale
    d_in = jnp.broadcast_to(d_in[:, :, None], (RET_HEADS, chunk, RET_QK_DIM))
    d_out = jnp.broadcast_to(d_out[:, :, None], (RET_HEADS, chunk, RET_QK_DIM))
    d_chunk = jnp.exp(lg * chunk)
    return d_chunk, dmat, d_in, d_out


def _retention(proj, cos, sin, lg_f, lg_b, s0, batch, seq, use_rope):
    chunk = min(RET_CHUNK, seq)
    nc = seq // chunk
    hs = RET_HEADS_PER_STEP
    qw, vw = hs * RET_QK_DIM, hs * RET_V_DIM
    k_scale = RET_QK_DIM ** -0.5
    tabs = [_ret_tables(lg_f, chunk, False, k_scale), _ret_tables(lg_b, chunk, True, k_scale)]
    d_chunk, dmat, d_in, d_out = [jnp.stack([tabs[0][i], tabs[1][i]]) for i in range(4)]
    fwd = lambda c: c
    bwd = lambda c: nc - 1 - c

    def side(cidx):
        row = lambda b, c: b * nc + cidx(c)
        return [pl.BlockSpec((chunk, qw), lambda b, h, c: (row(b, c), COL_QR // qw + h)),
                pl.BlockSpec((chunk, qw), lambda b, h, c: (row(b, c), COL_KR // qw + h)),
                pl.BlockSpec((chunk, vw), lambda b, h, c: (row(b, c), COL_VR // vw + h)),
                pl.BlockSpec((chunk, RET_QK_DIM), lambda b, h, c: (cidx(c), 0)),
                pl.BlockSpec((chunk, RET_QK_DIM), lambda b, h, c: (cidx(c), 0))]

    state_spec = pl.BlockSpec((2, 1, hs, RET_QK_DIM, RET_V_DIM), lambda b, h, c: (0, b, h, 0, 0))
    return pl.pallas_call(
        functools.partial(_ret_kernel, use_rope=use_rope, n_chunks=nc, heads=hs),
        grid=(batch, RET_HEADS // hs, nc),
        in_specs=[pl.BlockSpec(memory_space=pltpu.SMEM)] + side(fwd) + side(bwd) + [
            pl.BlockSpec((2, hs, chunk, chunk), lambda b, h, c: (0, h, 0, 0)),
            pl.BlockSpec((2, hs, chunk, RET_QK_DIM), lambda b, h, c: (0, h, 0, 0)),
            pl.BlockSpec((2, hs, chunk, RET_QK_DIM), lambda b, h, c: (0, h, 0, 0)),
            state_spec],
        out_specs=[pl.BlockSpec((chunk, vw), lambda b, h, c: (b * nc + c, h)),
                   pl.BlockSpec((chunk, vw), lambda b, h, c: (b * nc + nc - 1 - c, h)),
                   state_spec],
        out_shape=[jax.ShapeDtypeStruct((batch * seq, RET_V_W), BF16),
                   jax.ShapeDtypeStruct((batch * seq, RET_V_W), BF16),
                   jax.ShapeDtypeStruct((2, batch, RET_HEADS, RET_QK_DIM, RET_V_DIM), F32)],
        scratch_shapes=[pltpu.VMEM((2, hs, RET_QK_DIM, RET_V_DIM), F32)],
        compiler_params=_params(3),
        name="ret",
    )(d_chunk.reshape(-1), proj, proj, proj, cos, sin, proj, proj, proj, cos, sin, dmat, d_in, d_out, s0)


def _attn_kernel(q_ref, g_ref, cos_ref, sin_ref, k_ref, vt_ref, o_ref, qt_sc, m_sc, acc_sc,
                 s0_sc, s1_sc, p0_sc, p1_sc, a0_sc, a1_sc, *, tq, kc, n_chunks):
    s_bufs = (s0_sc, s1_sc)
    p_bufs = (p0_sc, p1_sc)
    a_bufs = (a0_sc, a1_sc)

    q = q_ref[...].astype(F32)
    scale = ATTN_HEAD_DIM ** -0.5 * np.log2(np.e)
    for g in range(ATTN_GROUP):
        qh = _rms(q[:, g * ATTN_HEAD_DIM:(g + 1) * ATTN_HEAD_DIM]) * g_ref[...]
        qh = _rope_block(qh, cos_ref[...], sin_ref[...], ATTN_HEAD_DIM // 4) * scale
        qt_sc[:, g * tq:(g + 1) * tq] = qh.T.astype(BF16)
    m_sc[...] = jnp.full(m_sc.shape, -jnp.inf, F32)
    acc_sc[...] = jnp.zeros(acc_sc.shape, F32)

    def stage_s(c, slot):
        off = pl.multiple_of(c * kc, kc)
        s_bufs[slot][...] = jnp.dot(k_ref[0, pl.ds(off, kc), :], qt_sc[...], preferred_element_type=F32)

    def stage_f(slot):
        m_prev = m_sc[...]
        m_new = jnp.maximum(m_prev, jnp.max(s_bufs[slot][...], axis=0, keepdims=True))
        m_sc[...] = m_new
        a_bufs[slot][...] = jnp.exp2(m_prev - m_new)
        for r in range(0, kc, ATTN_EXP_SLAB):
            p_bufs[slot][r:r + ATTN_EXP_SLAB, :] = jnp.exp2(
                s_bufs[slot][r:r + ATTN_EXP_SLAB, :] - m_new).astype(BF16)

    def stage_a(c, slot):
        off = pl.multiple_of(c * kc, kc)
        acc_sc[...] = a_bufs[slot][...] * acc_sc[...] + jnp.dot(
            vt_ref[0, :, pl.ds(off, kc)], p_bufs[slot][...], preferred_element_type=F32)

    def tick(t, parity):
        stage_s(t, parity)
        stage_f(1 - parity)
        stage_a(t - 2, parity)

    n = n_chunks
    stage_s(0, 0)
    if n > 1:
        stage_s(1, 1)
    stage_f(0)
    first = 2
    if n > 2 and (n - 2) % 2 == 1:
        tick(2, 0)
        first = 3
    n_pairs = (n - first) // 2 if n > first else 0
    if n_pairs > 0:
        def pair(u, carry):
            t = first + 2 * u
            tick(t, first % 2)
            tick(t + 1, 1 - first % 2)
            return carry
        lax.fori_loop(0, n_pairs, pair, 0)
    if n > 1:
        stage_f((n - 1) % 2)
        stage_a(n - 2, (n - 2) % 2)
    stage_a(n - 1, (n - 1) % 2)

    acc = acc_sc[...]
    o = acc[:ATTN_HEAD_DIM] / acc[ATTN_HEAD_DIM:ATTN_HEAD_DIM + 1]
    for g in range(ATTN_GROUP):
        o_ref[:, g * ATTN_HEAD_DIM:(g + 1) * ATTN_HEAD_DIM] = o[:, g * tq:(g + 1) * tq].T.astype(o_ref.dtype)


def _attention(proj, gain, cos, sin, k_all, vt_all, batch, seq):
    lk = k_all.shape[1]
    kc = _largest_tile(lk, ATTN_KEY_CHUNK, MXU_DEPTH)
    tq = _largest_tile(seq, ATTN_Q_TILE, 8)
    nq = seq // tq
    qw = ATTN_GROUP * ATTN_HEAD_DIM
    cols = ATTN_GROUP * tq
    vrows = ATTN_HEAD_DIM + ONES_ROWS
    return pl.pallas_call(
        functools.partial(_attn_kernel, tq=tq, kc=kc, n_chunks=lk // kc),
        grid=(batch, ATTN_KV_HEADS, nq),
        in_specs=[pl.BlockSpec((tq, qw), lambda b, g, i: (b * nq + i, COL_QA // qw + g)),
                  pl.BlockSpec((1, ATTN_HEAD_DIM), lambda b, g, i: (0, 0)),
                  pl.BlockSpec((tq, ATTN_HEAD_DIM), lambda b, g, i: (i, 0)),
                  pl.BlockSpec((tq, ATTN_HEAD_DIM), lambda b, g, i: (i, 0)),
                  pl.BlockSpec((1, lk, ATTN_HEAD_DIM), lambda b, g, i: (b, 0, g)),
                  pl.BlockSpec((1, vrows, lk), lambda b, g, i: (b, g, 0))],
        out_specs=pl.BlockSpec((tq, qw), lambda b, g, i: (b * nq + i, g)),
        out_shape=jax.ShapeDtypeStruct((batch * seq, ATTN_Q_W), BF16),
        scratch_shapes=[pltpu.VMEM((ATTN_HEAD_DIM, cols), BF16),
                        pltpu.VMEM((1, cols), F32),
                        pltpu.VMEM((vrows, cols), F32),
                        pltpu.VMEM((kc, cols), F32),
                        pltpu.VMEM((kc, cols), F32),
                        pltpu.VMEM((kc, cols), BF16),
                        pltpu.VMEM((kc, cols), BF16),
                        pltpu.VMEM((1, cols), F32),
                        pltpu.VMEM((1, cols), F32)],
        compiler_params=_params(3),
        name="attn",
    )(proj, gain, cos, sin, k_all, vt_all)


def _merge_kernel(x_ref, g1_ref, of_ref, ob_ref, gr_ref, ao_ref, gtr_ref, gta_ref,
                  wro_ref, wao_ref, wo_ref, o_ref):
    ro = of_ref[...].astype(F32) + ob_ref[...].astype(F32)
    gr = gr_ref[...].astype(F32)
    parts = []
    for h in range(RET_HEADS):
        sl = slice(h * RET_V_DIM, (h + 1) * RET_V_DIM)
        g = gr[:, sl]
        parts.append((g * jax.nn.sigmoid(g) * _rms(ro[:, sl])).astype(BF16))
    ret_in = jnp.concatenate(parts, axis=1)
    ret_branch = jnp.dot(ret_in, wro_ref[...], preferred_element_type=F32)
    attn_branch = jnp.dot(ao_ref[...], wao_ref[...], preferred_element_type=F32)
    y = (jax.nn.sigmoid(gtr_ref[...].astype(F32)) * ret_branch
         + jax.nn.sigmoid(gta_ref[...].astype(F32)) * attn_branch)
    y = jnp.dot(y.astype(BF16), wo_ref[...], preferred_element_type=F32)
    o_ref[...] = x_ref[...] + g1_ref[0] * y


def _merge(x2, g1, o_f, o_b, proj, attn_o, w_ret_o, w_attn_o, w_out, seq):
    t = x2.shape[0]
    tm = _largest_tile(seq, 256, 8)
    per_b = seq // tm
    full = lambda i: (0, 0)
    return pl.pallas_call(
        _merge_kernel,
        grid=(t // tm,),
        in_specs=[pl.BlockSpec((tm, D_MODEL), lambda i: (i, 0)),
                  pl.BlockSpec((1, 1, D_MODEL), lambda i: (i // per_b, 0, 0)),
                  pl.BlockSpec((tm, RET_V_W), lambda i: (i, 0)),
                  pl.BlockSpec((tm, RET_V_W), lambda i: (i, 0)),
                  pl.BlockSpec((tm, RET_V_W), lambda i: (i, COL_GR // RET_V_W)),
                  pl.BlockSpec((tm, ATTN_Q_W), lambda i: (i, 0)),
                  pl.BlockSpec((tm, D_MODEL), lambda i: (i, COL_GATE_R // D_MODEL)),
                  pl.BlockSpec((tm, D_MODEL), lambda i: (i, COL_GATE_A // D_MODEL)),
                  pl.BlockSpec((RET_V_W, D_MODEL), full),
                  pl.BlockSpec((ATTN_Q_W, D_MODEL), full),
                  pl.BlockSpec((D_MODEL, D_MODEL), full)],
        out_specs=pl.BlockSpec((tm, D_MODEL), lambda i: (i, 0)),
        out_shape=jax.ShapeDtypeStruct((t, D_MODEL), F32),
        compiler_params=_params(1),
        name="merge",
    )(x2, g1, o_f, o_b, proj, attn_o, proj, proj, w_ret_o, w_attn_o, w_out)


def _router_kernel(x_ref, n_ref, sc_ref, sh_ref, wr_ref, br_ref, h_ref, ti_ref, tw_ref):
    h = _rms(x_ref[...]) * n_ref[...] * (1.0 + sc_ref[0]) + sh_ref[0]
    h_ref[...] = _pack_pairs(h)
    nt = (((1,), (1,)), ((), ()))
    w = wr_ref[...]
    h_hi = h.astype(BF16)
    h_lo = (h - h_hi.astype(F32)).astype(BF16)
    w_hi = w.astype(BF16)
    w_lo = (w - w_hi.astype(F32)).astype(BF16)
    logits = (lax.dot_general(w_hi, h_hi, nt, preferred_element_type=F32)
              + lax.dot_general(w_lo, h_hi, nt, preferred_element_type=F32)
              + lax.dot_general(w_hi, h_lo, nt, preferred_element_type=F32)) + br_ref[...]
    eid = lax.broadcasted_iota(jnp.int32, logits.shape, 0)
    vals = logits
    top_v = []
    top_i = []
    for _ in range(TOP_K):
        m = jnp.max(vals, axis=0, keepdims=True)
        idx = jnp.min(jnp.where(vals == m, eid, N_EXPERTS), axis=0, keepdims=True)
        top_v.append(m)
        top_i.append(idx)
        vals = jnp.where(eid == idx, -jnp.inf, vals)
    ex = [jnp.exp(v - top_v[0]) for v in top_v]
    denom = ex[0] + ex[1] + ex[2] + ex[3]
    ti_ref[...] = jnp.concatenate(top_i, axis=0)
    tw_ref[...] = jnp.concatenate([e / denom for e in ex], axis=0)


def _router(x1, norm, sc, sh, w_router_t, b_router, seq):
    t = x1.shape[0]
    tm = _largest_tile(seq, 1024, LANES)
    per_b = seq // tm
    mod_map = lambda i: (i // per_b, 0, 0)
    return pl.pallas_call(
        _router_kernel,
        grid=(t // tm,),
        in_specs=[pl.BlockSpec((tm, D_MODEL), lambda i: (i, 0)),
                  pl.BlockSpec((1, D_MODEL), lambda i: (0, 0)),
                  pl.BlockSpec((1, 1, D_MODEL), mod_map),
                  pl.BlockSpec((1, 1, D_MODEL), mod_map),
                  pl.BlockSpec((N_EXPERTS, D_MODEL), lambda i: (0, 0)),
                  pl.BlockSpec((N_EXPERTS, 1), lambda i: (0, 0))],
        out_specs=[pl.BlockSpec((tm, PACK_W), lambda i: (i, 0)),
                   pl.BlockSpec((TOP_K, tm), lambda i: (0, i)),
                   pl.BlockSpec((TOP_K, tm), lambda i: (0, i))],
        out_shape=[jax.ShapeDtypeStruct((t, PACK_W), PACK_DTYPE),
                   jax.ShapeDtypeStruct((TOP_K, t), jnp.int32),
                   jax.ShapeDtypeStruct((TOP_K, t), F32)],
        compiler_params=_params(1),
        name="router",
    )(x1, norm, sc, sh, w_router_t, b_router)


def _gather_rows(src, idx):
    n, w = src.shape
    r = idx.shape[0]
    workers = SC_CORES * SC_SUBCORES
    per_w = r // workers
    n_win = per_w // SC_GATHER_ROWS
    assert per_w * workers == r and n_win * SC_GATHER_ROWS == per_w, (r, workers, SC_GATHER_ROWS)
    mesh = plsc.VectorSubcoreMesh(core_axis_name="c", subcore_axis_name="s")

    @functools.partial(
        pl.kernel, mesh=mesh, out_type=jax.ShapeDtypeStruct((r, w), src.dtype),
        scratch_types=[pltpu.VMEM((SC_GATHER_ROWS,), jnp.int32),
                       pltpu.VMEM((SC_GATHER_ROWS, w), src.dtype),
                       pltpu.SemaphoreType.DMA])
    def gather(src_hbm, idx_hbm, out_hbm, idx_v, rows_v, sem):
        wid = lax.axis_index("s") * SC_CORES + lax.axis_index("c")
        base = wid * per_w

        @pl.loop(0, n_win)
        def _(win):
            off = base + win * SC_GATHER_ROWS
            pltpu.sync_copy(idx_hbm.at[pl.ds(off, SC_GATHER_ROWS)], idx_v)
            pltpu.async_copy(src_hbm.at[idx_v], rows_v, sem).wait()
            pltpu.sync_copy(rows_v, out_hbm.at[pl.ds(off, SC_GATHER_ROWS)])

    return gather(src, idx)


def _scatter_rows(src, dest, n_out, first, group_rows):
    n, w = src.shape
    n_assign = dest.shape[0]
    workers = SC_CORES * SC_SUBCORES
    per_w = n_assign // workers
    n_win = per_w // SC_GATHER_ROWS
    assert per_w * workers == n_assign and n_win * SC_GATHER_ROWS == per_w and group_rows % per_w == 0
    mesh = plsc.VectorSubcoreMesh(core_axis_name="c", subcore_axis_name="s")

    @functools.partial(
        pl.kernel, mesh=mesh, out_type=jax.ShapeDtypeStruct((n_out, w), src.dtype),
        scratch_types=[pltpu.VMEM((SC_GATHER_ROWS,), jnp.int32),
                       pltpu.VMEM((SC_GATHER_ROWS, w), src.dtype),
                       pltpu.SemaphoreType.DMA])
    def scatter(src_hbm, dest_hbm, out_hbm, idx_v, rows_v, sem):
        wid = lax.axis_index("s") * SC_CORES + lax.axis_index("c")
        base = wid * per_w
        row_base = first + lax.rem(base, group_rows)

        @pl.loop(0, n_win)
        def _(win):
            pltpu.sync_copy(dest_hbm.at[pl.ds(base + win * SC_GATHER_ROWS, SC_GATHER_ROWS)], idx_v)
            pltpu.sync_copy(src_hbm.at[pl.ds(row_base + win * SC_GATHER_ROWS, SC_GATHER_ROWS)], rows_v)
            pltpu.async_copy(rows_v, out_hbm.at[idx_v], sem).wait()

    return scatter(src, dest)


def _ffn_kernel(te_ref, tf_ref, tr_ref, x_ref, w1_ref, b1_ref, w2_ref, b2_ref, o_ref, w1_sc, w2_sc):
    flag = tf_ref[pl.program_id(0)]
    n_rows = tr_ref[pl.program_id(0)]

    @pl.when(flag == TILE_FIRST)
    def _():
        w1_sc[...] = w1_ref[0].astype(BF16)
        w2_sc[...] = w2_ref[0].astype(BF16)

    @pl.when(flag == TILE_PAD)
    def _():
        o_ref[...] = jnp.zeros(o_ref.shape, o_ref.dtype)

    @pl.when(flag != TILE_PAD)
    def _():
        xw = x_ref[...]
        rid = lax.broadcasted_iota(jnp.int32, xw.shape, 0)
        lo, hi = _unpack_pairs(jnp.where(rid < n_rows, xw, jnp.zeros_like(xw)))
        x = jnp.concatenate([lo, hi], axis=1).astype(BF16)
        a = jnp.dot(x, w1_sc[...], preferred_element_type=F32) + b1_ref[0]
        gate = jnp.minimum(a[:, :EXPERT_FF], SWIGLU_LIMIT)
        up = jnp.clip(a[:, EXPERT_FF:], -SWIGLU_LIMIT, SWIGLU_LIMIT)
        act = gate * jax.nn.sigmoid(SWIGLU_ALPHA * gate) * (up + 1.0)
        y = jnp.dot(act.astype(BF16), w2_sc[...], preferred_element_type=F32) + b2_ref[0]
        o_ref[...] = _pack_pairs(y)


def _ffn(xs, tile_expert, tile_flag, tile_rows, w1, b1, w2, b2, tg):
    p = xs.shape[0]
    grid_spec = pltpu.PrefetchScalarGridSpec(
        num_scalar_prefetch=3,
        grid=(p // tg,),
        in_specs=[pl.BlockSpec((tg, PACK_W), lambda j, te, tf, tr: (j, 0)),
                  pl.BlockSpec((1, D_MODEL, 2 * EXPERT_FF), lambda j, te, tf, tr: (te[j], 0, 0)),
                  pl.BlockSpec((1, 1, 2 * EXPERT_FF), lambda j, te, tf, tr: (te[j], 0, 0)),
                  pl.BlockSpec((1, EXPERT_FF, D_MODEL), lambda j, te, tf, tr: (te[j], 0, 0)),
                  pl.BlockSpec((1, 1, D_MODEL), lambda j, te, tf, tr: (te[j], 0, 0))],
        out_specs=pl.BlockSpec((tg, PACK_W), lambda j, te, tf, tr: (j, 0)),
        scratch_shapes=[pltpu.VMEM((D_MODEL, 2 * EXPERT_FF), BF16),
                        pltpu.VMEM((EXPERT_FF, D_MODEL), BF16)],
    )
    return pl.pallas_call(
        _ffn_kernel,
        grid_spec=grid_spec,
        out_shape=jax.ShapeDtypeStruct((p, PACK_W), PACK_DTYPE),
        compiler_params=_params(1),
        name="ffn",
    )(tile_expert, tile_flag, tile_rows, xs, w1, b1, w2, b2)


def _combine_kernel(x_ref, g2_ref, w_ref, y_ref, o_ref):
    w = w_ref[...]
    acc_lo = None
    for k in range(TOP_K):
        lo, hi = _unpack_pairs(y_ref[k])
        wk = w[:, k:k + 1]
        acc_lo = wk * lo if acc_lo is None else acc_lo + wk * lo
        acc_hi = wk * hi if k == 0 else acc_hi + wk * hi
    acc = jnp.concatenate([acc_lo, acc_hi], axis=1)
    o_ref[...] = x_ref[...] + g2_ref[0] * acc


def _combine_into_kernel(x_ref, g2_ref, w_ref, y_ref, prev_ref, o_ref):
    del prev_ref
    _combine_kernel(x_ref, g2_ref, w_ref, y_ref, o_ref)


def _combine(x1, g2, w_tok, yk, seq, first_row, prev):
    t = x1.shape[0]
    rows = yk.shape[1]
    tm = _largest_tile(seq, 512, 8)
    per_b = seq // tm
    i0 = first_row // tm
    in_specs = [pl.BlockSpec((tm, D_MODEL), lambda i: (i0 + i, 0)),
                pl.BlockSpec((1, 1, D_MODEL), lambda i: ((i0 + i) // per_b, 0, 0)),
                pl.BlockSpec((tm, TOP_K), lambda i: (i0 + i, 0)),
                pl.BlockSpec((TOP_K, tm, PACK_W), lambda i: (0, i, 0))]
    args = [x1, g2, w_tok, yk]
    if prev is not None:
        in_specs.append(pl.BlockSpec(memory_space=pl.ANY))
        args.append(prev)
    return pl.pallas_call(
        _combine_kernel if prev is None else _combine_into_kernel,
        grid=(rows // tm,),
        in_specs=in_specs,
        out_specs=pl.BlockSpec((tm, D_MODEL), lambda i: (i0 + i, 0)),
        out_shape=jax.ShapeDtypeStruct((t, D_MODEL), F32),
        input_output_aliases={} if prev is None else {len(args) - 1: 0},
        compiler_params=_params(1),
        name="combine",
    )(*args)


def _count_kernel(ti_ref, cnt_ref):
    @pl.when(pl.program_id(0) == 0)
    def _():
        cnt_ref[...] = jnp.zeros(cnt_ref.shape, F32)

    bt = ti_ref.shape[1]
    eid = lax.broadcasted_iota(jnp.int32, (N_EXPERTS, bt), 0)
    acc = jnp.zeros(cnt_ref.shape, F32)
    for k in range(TOP_K):
        m = (eid == ti_ref[k:k + 1, :]).astype(F32)
        for c in range(bt // LANES):
            acc = acc + m[:, c * LANES:(c + 1) * LANES]
    cnt_ref[...] += acc


def _rank_kernel(ti_ref, off_ref, tri_ref, dest_ref, run_sc):
    @pl.when(pl.program_id(0) == 0)
    def _():
        run_sc[...] = off_ref[...] - 1.0

    bt = ti_ref.shape[1]
    eid = lax.broadcasted_iota(jnp.int32, (N_EXPERTS, bt), 0)
    run = run_sc[...]
    for k in range(TOP_K):
        m = eid == ti_ref[k:k + 1, :]
        pre = jnp.dot(jnp.where(m, 1.0, 0.0).astype(BF16), tri_ref[...], preferred_element_type=F32)
        slot = jnp.sum(jnp.where(m, pre + run, 0.0), axis=0, keepdims=True)
        dest_ref[k:k + 1, :] = slot.astype(jnp.int32)
        run = run + pre[:, bt - 1:bt]
    run_sc[...] = run


def _plan(top_i, tg):
    t = top_i.shape[1]
    n_assign = TOP_K * t
    p = n_assign + N_EXPERTS * tg
    n_tiles = p // tg
    bt = _largest_tile(t, PLAN_BLOCK, LANES)
    cnt = pl.pallas_call(
        _count_kernel,
        grid=(t // bt,),
        in_specs=[pl.BlockSpec((TOP_K, bt), lambda i: (0, i))],
        out_specs=pl.BlockSpec((N_EXPERTS, LANES), lambda i: (0, 0)),
        out_shape=jax.ShapeDtypeStruct((N_EXPERTS, LANES), F32),
        compiler_params=_params(1),
        name="count",
    )(top_i)
    counts = jnp.sum(cnt, axis=1).astype(jnp.int32)
    padded = ((counts + tg - 1) // tg) * tg
    off_end = jnp.cumsum(padded)
    off = off_end - padded
    tri = (jnp.arange(bt)[:, None] <= jnp.arange(bt)[None, :]).astype(BF16)
    dest = pl.pallas_call(
        _rank_kernel,
        grid=(t // bt,),
        in_specs=[pl.BlockSpec((TOP_K, bt), lambda i: (0, i)),
                  pl.BlockSpec((N_EXPERTS, 1), lambda i: (0, 0)),
                  pl.BlockSpec((bt, bt), lambda i: (0, 0))],
        out_specs=pl.BlockSpec((TOP_K, bt), lambda i: (0, i)),
        out_shape=jax.ShapeDtypeStruct((TOP_K, t), jnp.int32),
        scratch_shapes=[pltpu.VMEM((N_EXPERTS, 1), F32)],
        compiler_params=_params(1),
        name="rank",
    )(top_i, off.astype(F32).reshape(N_EXPERTS, 1), tri)
    tile_start = jnp.arange(n_tiles, dtype=jnp.int32) * tg
    tile_valid = tile_start < off_end[-1]
    te = jnp.sum((tile_start[:, None] >= off_end[None, :]).astype(jnp.int32), axis=1)
    last_e = jnp.sum(((off_end[-1] - 1) >= off_end).astype(jnp.int32))
    tile_expert = jnp.where(tile_valid, te, last_e)
    tile_rows = jnp.where(tile_valid, jnp.clip(counts[tile_expert] - (tile_start - off[tile_expert]), 0, tg), 0)
    changed = jnp.concatenate([jnp.ones((1,), jnp.bool_), tile_expert[1:] != tile_expert[:-1]])
    tile_flag = jnp.where(tile_valid, jnp.where(changed, TILE_FIRST, TILE_BODY), TILE_PAD).astype(jnp.int32)
    return dest.reshape(-1), tile_expert, tile_flag, tile_rows.astype(jnp.int32)


def kernel(x, c, ctx, c_ctx, norm1, norm2, w_ada, b_ada, w_in, ret_decay_f, ret_decay_b, attn_q_norm, attn_k_norm,
           w_ret_o, w_attn_o, w_out, w_router, b_router, w_exp_in, b_exp_in, w_exp_out, b_exp_out):
    assert w_in.shape[0] == 1, "single-layer block"
    b, seq, d = x.shape
    n_ctx = ctx.shape[1]
    t = b * seq
    rows = seq // GRID_W

    idx = np.cumsum(IN_SIZES)[:-1].tolist()
    wq_r, wk_r, wv_r, wg_r, wq_a, wk_a, wv_a, wgt_r, wgt_a = jnp.split(w_in[0], idx, axis=-1)
    w_in_p = jnp.concatenate([wq_r, wk_r, wv_r, wg_r, wq_a, wgt_r, wgt_a, wk_a, wv_a], axis=-1)
    w1 = w_exp_in[0]
    w2 = w_exp_out[0]
    b1 = b_exp_in[0].reshape(N_EXPERTS, 1, 2 * EXPERT_FF)
    b2 = b_exp_out[0].reshape(N_EXPERTS, 1, D_MODEL)

    pad = (-(b + 1)) % 8
    c_all = jnp.concatenate([c, c_ctx[None, :], jnp.zeros((pad, d), F32)], axis=0)
    mod = _ada(c_all, w_ada[0], b_ada[0])
    sh1, sc1, g1, sh2, sc2, g2 = [m.reshape(-1, 1, d) for m in jnp.split(mod, 6, axis=-1)]
    lat = lambda m: m[:b]
    cx = lambda m: m[b:b + 1]

    x2 = x.reshape(t, d)
    proj = _inproj(x2, norm1, lat(sc1), lat(sh1), w_in_p, seq)
    proj_c = _inproj(ctx.reshape(b * n_ctx, d), norm1, cx(sc1), cx(sh1), w_in_p, n_ctx)

    cos_r, sin_r = _rope_tables(rows, RET_QK_DIM)
    cos_a, sin_a = _rope_tables(rows, ATTN_HEAD_DIM)
    lg_f = -jax.nn.softplus(ret_decay_f[0].astype(F32))
    lg_b = -jax.nn.softplus(ret_decay_b[0].astype(F32))
    zero_state = jnp.zeros((2, b, RET_HEADS, RET_QK_DIM, RET_V_DIM), F32)
    cos_c = jnp.ones((n_ctx, RET_QK_DIM), F32)
    sin_c = jnp.zeros((n_ctx, RET_QK_DIM), F32)
    _, _, s_ctx = _retention(proj_c, cos_c, sin_c, lg_f, lg_b, zero_state, b, n_ctx, False)
    o_f, o_b, _ = _retention(proj, cos_r, sin_r, lg_f, lg_b, s_ctx, b, seq, True)

    k_all, vt_all = _kvprep(proj_c, proj, attn_k_norm, cos_a, sin_a, b, n_ctx, seq)
    attn_o = _attention(proj, attn_q_norm, cos_a, sin_a, k_all, vt_all, b, seq)

    x1 = _merge(x2, lat(g1), o_f, o_b, proj, attn_o, w_ret_o[0].astype(BF16), w_attn_o[0].astype(BF16),
                w_out[0].astype(BF16), seq)

    h2, top_i, top_w = _router(x1, norm2, lat(sc2), lat(sh2), w_router[0].T, b_router[0].reshape(N_EXPERTS, 1), seq)
    n_groups = MOE_GROUPS if b % MOE_GROUPS == 0 else 1
    tgrp = t // n_groups
    tg = _largest_tile(TOP_K * tgrp, 512, 8)
    w_tok = top_w.T
    out = None
    for grp in range(n_groups):
        first = grp * tgrp
        dest, tile_expert, tile_flag, tile_rows = _plan(top_i[:, first:first + tgrp], tg)
        xs = _scatter_rows(h2, dest, TOP_K * tgrp + N_EXPERTS * tg, first, tgrp)
        ys = _ffn(xs, tile_expert, tile_flag, tile_rows, w1, b1, w2, b2, tg)
        yk = _gather_rows(ys, dest).reshape(TOP_K, tgrp, PACK_W)
        out = _combine(x1, lat(g2), w_tok, yk, seq, first, out)
    return out.reshape(b, seq, d)
```

```python
import functools

import jax
import jax.numpy as jnp
import numpy as np
from jax import lax
from jax.experimental import pallas as pl
from jax.experimental.pallas import tpu as pltpu
from jax.experimental.pallas import tpu_sc as plsc

F32 = jnp.float32
BF16 = jnp.bfloat16

D_MODEL = 1024
GRID_W = 64
EPS = 1e-6
RET_HEADS = 4
RET_QK_DIM = 256
RET_V_DIM = 512
ATTN_HEADS = 8
ATTN_KV_HEADS = 2
ATTN_GROUP = ATTN_HEADS // ATTN_KV_HEADS
ATTN_HEAD_DIM = 128
ROPE_THETA = 10000.0
N_EXPERTS = 32
TOP_K = 4
EXPERT_FF = 1024
SWIGLU_LIMIT = 7.0
SWIGLU_ALPHA = 1.702

RET_QK_W = RET_HEADS * RET_QK_DIM
RET_V_W = RET_HEADS * RET_V_DIM
ATTN_Q_W = ATTN_HEADS * ATTN_HEAD_DIM
ATTN_KV_W = ATTN_KV_HEADS * ATTN_HEAD_DIM
IN_SIZES = (RET_QK_W, RET_QK_W, RET_V_W, RET_V_W, ATTN_Q_W, ATTN_KV_W, ATTN_KV_W, D_MODEL, D_MODEL)
IN_WIDTH = sum(IN_SIZES)
COL_QR = 0
COL_KR = COL_QR + RET_QK_W
COL_VR = COL_KR + RET_QK_W
COL_GR = COL_VR + RET_V_W
COL_QA = COL_GR + RET_V_W
COL_GATE_R = COL_QA + ATTN_Q_W
COL_GATE_A = COL_GATE_R + D_MODEL
COL_KA = COL_GATE_A + D_MODEL
COL_VA = COL_KA + ATTN_KV_W

RET_CHUNK = 256
RET_HEADS_PER_STEP = 4
LANES = 128
SUBLANES = 8
SC_CORES = 2
SC_SUBCORES = 16
SC_GATHER_ROWS = 128
MXU_DEPTH = 256
ADA_COLS = 1536
INPROJ_ROWS = 1024
INPROJ_COLS = 2560
MERGE_ROWS = 256
ROUTER_ROWS = 1024
COMBINE_ROWS = 512
FFN_ROWS = 512
ATTN_KEY_CHUNK = 768
ATTN_Q_TILE = 512
ATTN_EXP_SLAB = 32
ONES_ROWS = 16
KV_PREP_ROWS = 256
PACK_DTYPE = jnp.int32
PACK_W = D_MODEL // 2
MOE_GROUPS = 2
PLAN_BLOCK = 512
TILE_PAD, TILE_BODY, TILE_FIRST = 0, 1, 2
VMEM_LIMIT = 56 * 1024 * 1024

ARB = pltpu.ARBITRARY


def _params(n_axes, **kw):
    return pltpu.CompilerParams(dimension_semantics=(ARB,) * n_axes, vmem_limit_bytes=VMEM_LIMIT, **kw)


def _largest_tile(n, cap, mult):
    best = None
    for t in range(mult, min(n, cap) + 1, mult):
        if n % t == 0:
            best = t
    assert best is not None, (n, cap, mult)
    return best


def _rms(x):
    return x * lax.rsqrt(jnp.mean(x * x, axis=-1, keepdims=True) + EPS)


def _pack_pairs(x):
    half = x.shape[1] // 2
    lo = lax.bitcast_convert_type(x[:, :half].astype(BF16).astype(F32), jnp.int32)
    hi = lax.bitcast_convert_type(x[:, half:].astype(BF16).astype(F32), jnp.int32)
    return lax.bitwise_or(lax.bitwise_and(hi, jnp.int32(-65536)), lax.shift_right_logical(lo, jnp.int32(16)))


def _unpack_pairs(w):
    lo = lax.bitcast_convert_type(lax.shift_left(w, jnp.int32(16)), F32)
    hi = lax.bitcast_convert_type(lax.bitwise_and(w, jnp.int32(-65536)), F32)
    return lo, hi


def _rope_block(x, cos, sin, half):
    if 2 * half == LANES:
        swapped = pltpu.roll(x, half, 1)
    else:
        lane = lax.broadcasted_iota(jnp.int32, x.shape, 1)
        first = (lane % (2 * half)) < half
        swapped = jnp.where(first, pltpu.roll(x, LANES - half, 1), pltpu.roll(x, half, 1))
    return x * cos + swapped * sin


def _rope_tables(rows, head_dim):
    n_freq = head_dim // 4
    inv_freq = ROPE_THETA ** (-jnp.arange(n_freq, dtype=F32) / n_freq)
    ang_r = jnp.arange(rows, dtype=F32)[:, None] * inv_freq
    ang_c = jnp.arange(GRID_W, dtype=F32)[:, None] * inv_freq
    per_row = lambda a: jnp.broadcast_to(a[:, None, :], (rows, GRID_W, n_freq)).reshape(rows * GRID_W, n_freq)
    per_col = lambda a: jnp.broadcast_to(a[None, :, :], (rows, GRID_W, n_freq)).reshape(rows * GRID_W, n_freq)
    cos_r, sin_r = per_row(jnp.cos(ang_r)), per_row(jnp.sin(ang_r))
    cos_c, sin_c = per_col(jnp.cos(ang_c)), per_col(jnp.sin(ang_c))
    cos = jnp.concatenate([cos_r, cos_r, cos_c, cos_c], axis=1)
    sin = jnp.concatenate([-sin_r, sin_r, -sin_c, sin_c], axis=1)
    return cos, sin


def _ada_kernel(c_ref, w_ref, b_ref, o_ref):
    c = c_ref[...]
    s = c * jax.nn.sigmoid(c)
    o_ref[...] = jnp.dot(s, w_ref[...], preferred_element_type=F32,
                         precision=lax.Precision.HIGHEST) + b_ref[...]


def _ada(c_pad, w_ada, b_ada):
    rows = c_pad.shape[0]
    n = w_ada.shape[1]
    tn = _largest_tile(n, ADA_COLS, LANES)
    return pl.pallas_call(
        _ada_kernel,
        grid=(n // tn,),
        in_specs=[pl.BlockSpec((rows, D_MODEL), lambda j: (0, 0)),
                  pl.BlockSpec((D_MODEL, tn), lambda j: (0, j)),
                  pl.BlockSpec((1, tn), lambda j: (0, j))],
        out_specs=pl.BlockSpec((rows, tn), lambda j: (0, j)),
        out_shape=jax.ShapeDtypeStruct((rows, n), F32),
        compiler_params=_params(1),
        name="ada",
    )(c_pad, w_ada, b_ada.reshape(1, n))


def _inproj_kernel(x_ref, n_ref, sc_ref, sh_ref, w_ref, o_ref, w_sc):
    @pl.when(pl.program_id(1) == 0)
    def _():
        w_sc[...] = w_ref[...].astype(BF16)

    y = _rms(x_ref[...]) * n_ref[...]
    h = (y * (1.0 + sc_ref[0]) + sh_ref[0]).astype(BF16)
    o_ref[...] = jnp.dot(h, w_sc[...], preferred_element_type=F32).astype(o_ref.dtype)


def _inproj(x2, norm, sc, sh, w, rows_per_batch):
    t = x2.shape[0]
    n = w.shape[1]
    tm = _largest_tile(rows_per_batch, INPROJ_ROWS, SUBLANES)
    tn = _largest_tile(n, INPROJ_COLS, LANES)
    per_b = rows_per_batch // tm
    if sc.shape[0] == 1:
        mod_map = lambda j, i: (0, 0, 0)
    else:
        mod_map = lambda j, i: (i // per_b, 0, 0)
    return pl.pallas_call(
        _inproj_kernel,
        grid=(n // tn, t // tm),
        in_specs=[pl.BlockSpec((tm, D_MODEL), lambda j, i: (i, 0)),
                  pl.BlockSpec((1, D_MODEL), lambda j, i: (0, 0)),
                  pl.BlockSpec((1, 1, D_MODEL), mod_map),
                  pl.BlockSpec((1, 1, D_MODEL), mod_map),
                  pl.BlockSpec((D_MODEL, tn), lambda j, i: (0, j))],
        out_specs=pl.BlockSpec((tm, tn), lambda j, i: (i, j)),
        out_shape=jax.ShapeDtypeStruct((t, n), BF16),
        scratch_shapes=[pltpu.VMEM((D_MODEL, tn), BF16)],
        compiler_params=_params(2),
        name="inproj",
    )(x2, norm, sc, sh, w)


def _kvprep_kernel(kc_ref, vc_ref, kl_ref, vl_ref, g_ref, cos_ref, sin_ref, k_ref, vt_ref, *, ctx_blocks):
    j = pl.program_id(1)
    vrows = ATTN_HEAD_DIM + ONES_ROWS

    def emit(kin_ref, vin_ref, use_rope):
        k = kin_ref[...].astype(F32)
        v = vin_ref[...].astype(F32)
        for g in range(ATTN_KV_HEADS):
            sl = slice(g * ATTN_HEAD_DIM, (g + 1) * ATTN_HEAD_DIM)
            kh = _rms(k[:, sl]) * g_ref[...]
            if use_rope:
                kh = _rope_block(kh, cos_ref[...], sin_ref[...], ATTN_HEAD_DIM // 4)
            k_ref[0, :, sl] = kh.astype(BF16)
            vt_ref[0, g * vrows:g * vrows + ATTN_HEAD_DIM, :] = v[:, sl].T.astype(BF16)
            vt_ref[0, g * vrows + ATTN_HEAD_DIM:(g + 1) * vrows, :] = jnp.ones((ONES_ROWS, v.shape[0]), BF16)

    @pl.when(j < ctx_blocks)
    def _():
        emit(kc_ref, vc_ref, False)

    @pl.when(j >= ctx_blocks)
    def _():
        emit(kl_ref, vl_ref, True)


def _kvprep(proj_c, proj, gain, cos, sin, batch, n_ctx, seq):
    tm = KV_PREP_ROWS
    assert n_ctx % tm == 0 and seq % tm == 0
    cb, lb = n_ctx // tm, seq // tm
    lk = n_ctx + seq
    ctx_row = lambda b, j: b * cb + jnp.minimum(j, cb - 1)
    lat_blk = lambda j: jnp.maximum(j - cb, 0)
    lat_row = lambda b, j: b * lb + lat_blk(j)
    vrows = ATTN_KV_HEADS * (ATTN_HEAD_DIM + ONES_ROWS)
    return pl.pallas_call(
        functools.partial(_kvprep_kernel, ctx_blocks=cb),
        grid=(batch, cb + lb),
        in_specs=[pl.BlockSpec((tm, ATTN_KV_W), lambda b, j: (ctx_row(b, j), COL_KA // ATTN_KV_W)),
                  pl.BlockSpec((tm, ATTN_KV_W), lambda b, j: (ctx_row(b, j), COL_VA // ATTN_KV_W)),
                  pl.BlockSpec((tm, ATTN_KV_W), lambda b, j: (lat_row(b, j), COL_KA // ATTN_KV_W)),
                  pl.BlockSpec((tm, ATTN_KV_W), lambda b, j: (lat_row(b, j), COL_VA // ATTN_KV_W)),
                  pl.BlockSpec((1, ATTN_HEAD_DIM), lambda b, j: (0, 0)),
                  pl.BlockSpec((tm, ATTN_HEAD_DIM), lambda b, j: (lat_blk(j), 0)),
                  pl.BlockSpec((tm, ATTN_HEAD_DIM), lambda b, j: (lat_blk(j), 0))],
        out_specs=[pl.BlockSpec((1, tm, ATTN_KV_W), lambda b, j: (b, j, 0)),
                   pl.BlockSpec((1, vrows, tm), lambda b, j: (b, 0, j))],
        out_shape=[jax.ShapeDtypeStruct((batch, lk, ATTN_KV_W), BF16),
                   jax.ShapeDtypeStruct((batch, vrows, lk), BF16)],
        compiler_params=_params(2),
        name="kvprep",
    )(proj_c, proj_c, proj, proj, gain, cos, sin)


def _ret_kernel(dch_ref, qf_ref, kf_ref, vf_ref, cosf_ref, sinf_ref, qb_ref, kb_ref, vb_ref, cosb_ref, sinb_ref,
                dmat_ref, din_ref, dout_ref, s0_ref, of_ref, ob_ref, sfin_ref, s_sc, *, use_rope, n_chunks, heads):
    hb = pl.program_id(1)
    c = pl.program_id(2)

    @pl.when(c == 0)
    def _():
        s_sc[...] = s0_ref[:, 0]

    def rope(x, cos, sin):
        return jnp.concatenate([_rope_block(x[:, i * LANES:(i + 1) * LANES], cos[:, i * LANES:(i + 1) * LANES],
                                            sin[:, i * LANES:(i + 1) * LANES], RET_QK_DIM // 4)
                                for i in range(RET_QK_DIM // LANES)], axis=1)

    sides = ((qf_ref, kf_ref, vf_ref, cosf_ref, sinf_ref), (qb_ref, kb_ref, vb_ref, cosb_ref, sinb_ref))
    chains = [(d, j) for d in range(2) for j in range(heads)]
    qs, ks, vs, ss = {}, {}, {}, {}
    for d, j in chains:
        q_ref, k_ref, v_ref, cos_ref, sin_ref = sides[d]
        q = q_ref[:, j * RET_QK_DIM:(j + 1) * RET_QK_DIM].astype(F32)
        k = k_ref[:, j * RET_QK_DIM:(j + 1) * RET_QK_DIM].astype(F32)
        if use_rope:
            q = rope(q, cos_ref[...], sin_ref[...])
            k = rope(k, cos_ref[...], sin_ref[...])
        qs[d, j], ks[d, j] = q, k
        vs[d, j] = v_ref[:, j * RET_V_DIM:(j + 1) * RET_V_DIM]
        ss[d, j] = s_sc[d, j]
    scores = {ch: lax.dot_general(qs[ch].astype(BF16), ks[ch].astype(BF16), (((1,), (1,)), ((), ())),
                                  preferred_element_type=F32) * dmat_ref[ch[0], ch[1]] for ch in chains}
    inter = {ch: jnp.dot((qs[ch] * din_ref[ch[0], ch[1]]).astype(BF16), ss[ch].astype(BF16),
                         preferred_element_type=F32) for ch in chains}
    intra = {ch: jnp.dot(scores[ch].astype(BF16), vs[ch], preferred_element_type=F32) for ch in chains}
    o_refs = (of_ref, ob_ref)
    for d, j in chains:
        o_refs[d][:, j * RET_V_DIM:(j + 1) * RET_V_DIM] = (intra[d, j] + inter[d, j]).astype(of_ref.dtype)
    s_new = {ch: ss[ch] * dch_ref[ch[0] * RET_HEADS + hb * heads + ch[1]]
             + jnp.dot((ks[ch] * dout_ref[ch[0], ch[1]]).T.astype(BF16), vs[ch], preferred_element_type=F32)
             for ch in chains}
    for d, j in chains:
        s_sc[d, j] = s_new[d, j]

    @pl.when(c == n_chunks - 1)
    def _():
        for d, j in chains:
            sfin_ref[d, 0, j] = s_new[d, j]


def _ret_tables(log_gamma, chunk, reverse, k_scale):
    pos = jnp.arange(chunk, dtype=F32)
    diff = pos[:, None] - pos[None, :]
    if reverse:
        diff = -diff
        mask = diff > 0
        p_in = chunk - pos
        p_out = pos
    else:
        mask = diff >= 0
        p_in = pos + 1.0
        p_out = chunk - 1.0 - pos
    lg = log_gamma.astype(F32)
    dmat = jnp.where(mask[None], jnp.exp(lg[:, None, None] * jnp.maximum(diff, 0.0)[None]), 0.0) * k_scale
    d_in = jnp.exp(lg[:, None] * p_in)
    d_out = jnp.exp(lg[:, None] * p_out) * k_scale
    d_in = jnp.broadcast_to(d_in[:, :, None], (RET_HEADS, chunk, RET_QK_DIM))
    d_out = jnp.broadcast_to(d_out[:, :, None], (RET_HEADS, chunk, RET_QK_DIM))
    d_chunk = jnp.exp(lg * chunk)
    return d_chunk, dmat, d_in, d_out


def _retention(proj, cos, sin, lg_f, lg_b, s0, batch, seq, use_rope):
    chunk = min(RET_CHUNK, seq)
    nc = seq // chunk
    hs = RET_HEADS_PER_STEP
    qw, vw = hs * RET_QK_DIM, hs * RET_V_DIM
    k_scale = RET_QK_DIM ** -0.5
    tabs = [_ret_tables(lg_f, chunk, False, k_scale), _ret_tables(lg_b, chunk, True, k_scale)]
    d_chunk, dmat, d_in, d_out = [jnp.stack([tabs[0][i], tabs[1][i]]) for i in range(4)]
    fwd = lambda c: c
    bwd = lambda c: nc - 1 - c

    def side(cidx):
        row = lambda b, c: b * nc + cidx(c)
        return [pl.BlockSpec((chunk, qw), lambda b, h, c: (row(b, c), COL_QR // qw + h)),
                pl.BlockSpec((chunk, qw), lambda b, h, c: (row(b, c), COL_KR // qw + h)),
                pl.BlockSpec((chunk, vw), lambda b, h, c: (row(b, c), COL_VR // vw + h)),
                pl.BlockSpec((chunk, RET_QK_DIM), lambda b, h, c: (cidx(c), 0)),
                pl.BlockSpec((chunk, RET_QK_DIM), lambda b, h, c: (cidx(c), 0))]

    state_spec = pl.BlockSpec((2, 1, hs, RET_QK_DIM, RET_V_DIM), lambda b, h, c: (0, b, h, 0, 0))
    return pl.pallas_call(
        functools.partial(_ret_kernel, use_rope=use_rope, n_chunks=nc, heads=hs),
        grid=(batch, RET_HEADS // hs, nc),
        in_specs=[pl.BlockSpec(memory_space=pltpu.SMEM)] + side(fwd) + side(bwd) + [
            pl.BlockSpec((2, hs, chunk, chunk), lambda b, h, c: (0, h, 0, 0)),
            pl.BlockSpec((2, hs, chunk, RET_QK_DIM), lambda b, h, c: (0, h, 0, 0)),
            pl.BlockSpec((2, hs, chunk, RET_QK_DIM), lambda b, h, c: (0, h, 0, 0)),
            state_spec],
        out_specs=[pl.BlockSpec((chunk, vw), lambda b, h, c: (b * nc + c, h)),
                   pl.BlockSpec((chunk, vw), lambda b, h, c: (b * nc + nc - 1 - c, h)),
                   state_spec],
        out_shape=[jax.ShapeDtypeStruct((batch * seq, RET_V_W), BF16),
                   jax.ShapeDtypeStruct((batch * seq, RET_V_W), BF16),
                   jax.ShapeDtypeStruct((2, batch, RET_HEADS, RET_QK_DIM, RET_V_DIM), F32)],
        scratch_shapes=[pltpu.VMEM((2, hs, RET_QK_DIM, RET_V_DIM), F32)],
        compiler_params=_params(3),
        name="ret",
    )(d_chunk.reshape(-1), proj, proj, proj, cos, sin, proj, proj, proj, cos, sin, dmat, d_in, d_out, s0)


def _attn_kernel(q_ref, g_ref, cos_ref, sin_ref, k_ref, vt_ref, o_ref, qt_sc, m_sc, acc_sc,
                 s0_sc, s1_sc, p0_sc, p1_sc, a0_sc, a1_sc, *, tq, kc, n_chunks):
    s_bufs = (s0_sc, s1_sc)
    p_bufs = (p0_sc, p1_sc)
    a_bufs = (a0_sc, a1_sc)

    q = q_ref[...].astype(F32)
    scale = ATTN_HEAD_DIM ** -0.5 * np.log2(np.e)
    for g in range(ATTN_GROUP):
        qh = _rms(q[:, g * ATTN_HEAD_DIM:(g + 1) * ATTN_HEAD_DIM]) * g_ref[...]
        qh = _rope_block(qh, cos_ref[...], sin_ref[...], ATTN_HEAD_DIM // 4) * scale
        qt_sc[:, g * tq:(g + 1) * tq] = qh.T.astype(BF16)
    m_sc[...] = jnp.full(m_sc.shape, -jnp.inf, F32)
    acc_sc[...] = jnp.zeros(acc_sc.shape, F32)

    def stage_s(c, slot):
        off = pl.multiple_of(c * kc, kc)
        s_bufs[slot][...] = jnp.dot(k_ref[0, pl.ds(off, kc), :], qt_sc[...], preferred_element_type=F32)

    def stage_f(slot):
        m_prev = m_sc[...]
        m_new = jnp.maximum(m_prev, jnp.max(s_bufs[slot][...], axis=0, keepdims=True))
        m_sc[...] = m_new
        a_bufs[slot][...] = jnp.exp2(m_prev - m_new)
        for r in range(0, kc, ATTN_EXP_SLAB):
            p_bufs[slot][r:r + ATTN_EXP_SLAB, :] = jnp.exp2(
                s_bufs[slot][r:r + ATTN_EXP_SLAB, :] - m_new).astype(BF16)

    def stage_a(c, slot):
        off = pl.multiple_of(c * kc, kc)
        acc_sc[...] = a_bufs[slot][...] * acc_sc[...] + jnp.dot(
            vt_ref[0, :, pl.ds(off, kc)], p_bufs[slot][...], preferred_element_type=F32)

    def tick(t, parity):
        stage_s(t, parity)
        stage_f(1 - parity)
        stage_a(t - 2, parity)

    n = n_chunks
    stage_s(0, 0)
    if n > 1:
        stage_s(1, 1)
    stage_f(0)
    first = 2
    if n > 2 and (n - 2) % 2 == 1:
        tick(2, 0)
        first = 3
    n_pairs = (n - first) // 2 if n > first else 0
    if n_pairs > 0:
        def pair(u, carry):
            t = first + 2 * u
            tick(t, first % 2)
            tick(t + 1, 1 - first % 2)
            return carry
        lax.fori_loop(0, n_pairs, pair, 0)
    if n > 1:
        stage_f((n - 1) % 2)
        stage_a(n - 2, (n - 2) % 2)
    stage_a(n - 1, (n - 1) % 2)

    acc = acc_sc[...]
    o = acc[:ATTN_HEAD_DIM] / acc[ATTN_HEAD_DIM:ATTN_HEAD_DIM + 1]
    for g in range(ATTN_GROUP):
        o_ref[:, g * ATTN_HEAD_DIM:(g + 1) * ATTN_HEAD_DIM] = o[:, g * tq:(g + 1) * tq].T.astype(o_ref.dtype)


def _attention(proj, gain, cos, sin, k_all, vt_all, batch, seq):
    lk = k_all.shape[1]
    kc = _largest_tile(lk, ATTN_KEY_CHUNK, MXU_DEPTH)
    tq = _largest_tile(seq, ATTN_Q_TILE, SUBLANES)
    nq = seq // tq
    qw = ATTN_GROUP * ATTN_HEAD_DIM
    cols = ATTN_GROUP * tq
    vrows = ATTN_HEAD_DIM + ONES_ROWS
    return pl.pallas_call(
        functools.partial(_attn_kernel, tq=tq, kc=kc, n_chunks=lk // kc),
        grid=(batch, ATTN_KV_HEADS, nq),
        in_specs=[pl.BlockSpec((tq, qw), lambda b, g, i: (b * nq + i, COL_QA // qw + g)),
                  pl.BlockSpec((1, ATTN_HEAD_DIM), lambda b, g, i: (0, 0)),
                  pl.BlockSpec((tq, ATTN_HEAD_DIM), lambda b, g, i: (i, 0)),
                  pl.BlockSpec((tq, ATTN_HEAD_DIM), lambda b, g, i: (i, 0)),
                  pl.BlockSpec((1, lk, ATTN_HEAD_DIM), lambda b, g, i: (b, 0, g)),
                  pl.BlockSpec((1, vrows, lk), lambda b, g, i: (b, g, 0))],
        out_specs=pl.BlockSpec((tq, qw), lambda b, g, i: (b * nq + i, g)),
        out_shape=jax.ShapeDtypeStruct((batch * seq, ATTN_Q_W), BF16),
        scratch_shapes=[pltpu.VMEM((ATTN_HEAD_DIM, cols), BF16),
                        pltpu.VMEM((1, cols), F32),
                        pltpu.VMEM((vrows, cols), F32),
                        pltpu.VMEM((kc, cols), F32),
                        pltpu.VMEM((kc, cols), F32),
                        pltpu.VMEM((kc, cols), BF16),
                        pltpu.VMEM((kc, cols), BF16),
                        pltpu.VMEM((1, cols), F32),
                        pltpu.VMEM((1, cols), F32)],
        compiler_params=_params(3),
        name="attn",
    )(proj, gain, cos, sin, k_all, vt_all)


def _merge_kernel(x_ref, g1_ref, of_ref, ob_ref, gr_ref, ao_ref, gtr_ref, gta_ref,
                  wro_ref, wao_ref, wo_ref, o_ref):
    ro = of_ref[...].astype(F32) + ob_ref[...].astype(F32)
    gr = gr_ref[...].astype(F32)
    parts = []
    for h in range(RET_HEADS):
        sl = slice(h * RET_V_DIM, (h + 1) * RET_V_DIM)
        g = gr[:, sl]
        parts.append((g * jax.nn.sigmoid(g) * _rms(ro[:, sl])).astype(BF16))
    ret_in = jnp.concatenate(parts, axis=1)
    ret_branch = jnp.dot(ret_in, wro_ref[...], preferred_element_type=F32)
    attn_branch = jnp.dot(ao_ref[...], wao_ref[...], preferred_element_type=F32)
    y = (jax.nn.sigmoid(gtr_ref[...].astype(F32)) * ret_branch
         + jax.nn.sigmoid(gta_ref[...].astype(F32)) * attn_branch)
    y = jnp.dot(y.astype(BF16), wo_ref[...], preferred_element_type=F32)
    o_ref[...] = x_ref[...] + g1_ref[0] * y


def _merge(x2, g1, o_f, o_b, proj, attn_o, w_ret_o, w_attn_o, w_out, seq):
    t = x2.shape[0]
    tm = _largest_tile(seq, MERGE_ROWS, SUBLANES)
    per_b = seq // tm
    full = lambda i: (0, 0)
    return pl.pallas_call(
        _merge_kernel,
        grid=(t // tm,),
        in_specs=[pl.BlockSpec((tm, D_MODEL), lambda i: (i, 0)),
                  pl.BlockSpec((1, 1, D_MODEL), lambda i: (i // per_b, 0, 0)),
                  pl.BlockSpec((tm, RET_V_W), lambda i: (i, 0)),
                  pl.BlockSpec((tm, RET_V_W), lambda i: (i, 0)),
                  pl.BlockSpec((tm, RET_V_W), lambda i: (i, COL_GR // RET_V_W)),
                  pl.BlockSpec((tm, ATTN_Q_W), lambda i: (i, 0)),
                  pl.BlockSpec((tm, D_MODEL), lambda i: (i, COL_GATE_R // D_MODEL)),
                  pl.BlockSpec((tm, D_MODEL), lambda i: (i, COL_GATE_A // D_MODEL)),
                  pl.BlockSpec((RET_V_W, D_MODEL), full),
                  pl.BlockSpec((ATTN_Q_W, D_MODEL), full),
                  pl.BlockSpec((D_MODEL, D_MODEL), full)],
        out_specs=pl.BlockSpec((tm, D_MODEL), lambda i: (i, 0)),
        out_shape=jax.ShapeDtypeStruct((t, D_MODEL), F32),
        compiler_params=_params(1),
        name="merge",
    )(x2, g1, o_f, o_b, proj, attn_o, proj, proj, w_ret_o, w_attn_o, w_out)


def _router_kernel(x_ref, n_ref, sc_ref, sh_ref, wr_ref, br_ref, h_ref, ti_ref, tw_ref):
    h = _rms(x_ref[...]) * n_ref[...] * (1.0 + sc_ref[0]) + sh_ref[0]
    h_ref[...] = _pack_pairs(h)
    nt = (((1,), (1,)), ((), ()))
    w = wr_ref[...]
    h_hi = h.astype(BF16)
    h_lo = (h - h_hi.astype(F32)).astype(BF16)
    w_hi = w.astype(BF16)
    w_lo = (w - w_hi.astype(F32)).astype(BF16)
    logits = (lax.dot_general(w_hi, h_hi, nt, preferred_element_type=F32)
              + lax.dot_general(w_lo, h_hi, nt, preferred_element_type=F32)
              + lax.dot_general(w_hi, h_lo, nt, preferred_element_type=F32)) + br_ref[...]
    eid = lax.broadcasted_iota(jnp.int32, logits.shape, 0)
    vals = logits
    top_v = []
    top_i = []
    for _ in range(TOP_K):
        m = jnp.max(vals, axis=0, keepdims=True)
        idx = jnp.min(jnp.where(vals == m, eid, N_EXPERTS), axis=0, keepdims=True)
        top_v.append(m)
        top_i.append(idx)
        vals = jnp.where(eid == idx, -jnp.inf, vals)
    ex = [jnp.exp(v - top_v[0]) for v in top_v]
    denom = ex[0] + ex[1] + ex[2] + ex[3]
    ti_ref[...] = jnp.concatenate(top_i, axis=0)
    tw_ref[...] = jnp.concatenate([e / denom for e in ex], axis=0)


def _router(x1, norm, sc, sh, w_router_t, b_router, seq):
    t = x1.shape[0]
    tm = _largest_tile(seq, ROUTER_ROWS, LANES)
    per_b = seq // tm
    mod_map = lambda i: (i // per_b, 0, 0)
    return pl.pallas_call(
        _router_kernel,
        grid=(t // tm,),
        in_specs=[pl.BlockSpec((tm, D_MODEL), lambda i: (i, 0)),
                  pl.BlockSpec((1, D_MODEL), lambda i: (0, 0)),
                  pl.BlockSpec((1, 1, D_MODEL), mod_map),
                  pl.BlockSpec((1, 1, D_MODEL), mod_map),
                  pl.BlockSpec((N_EXPERTS, D_MODEL), lambda i: (0, 0)),
                  pl.BlockSpec((N_EXPERTS, 1), lambda i: (0, 0))],
        out_specs=[pl.BlockSpec((tm, PACK_W), lambda i: (i, 0)),
                   pl.BlockSpec((TOP_K, tm), lambda i: (0, i)),
                   pl.BlockSpec((TOP_K, tm), lambda i: (0, i))],
        out_shape=[jax.ShapeDtypeStruct((t, PACK_W), PACK_DTYPE),
                   jax.ShapeDtypeStruct((TOP_K, t), jnp.int32),
                   jax.ShapeDtypeStruct((TOP_K, t), F32)],
        compiler_params=_params(1),
        name="router",
    )(x1, norm, sc, sh, w_router_t, b_router)


def _gather_rows(src, idx):
    n, w = src.shape
    r = idx.shape[0]
    workers = SC_CORES * SC_SUBCORES
    per_w = r // workers
    n_win = per_w // SC_GATHER_ROWS
    assert per_w * workers == r and n_win * SC_GATHER_ROWS == per_w, (r, workers, SC_GATHER_ROWS)
    mesh = plsc.VectorSubcoreMesh(core_axis_name="c", subcore_axis_name="s")

    @functools.partial(
        pl.kernel, mesh=mesh, out_type=jax.ShapeDtypeStruct((r, w), src.dtype),
        scratch_types=[pltpu.VMEM((SC_GATHER_ROWS,), jnp.int32),
                       pltpu.VMEM((SC_GATHER_ROWS, w), src.dtype),
                       pltpu.SemaphoreType.DMA])
    def gather(src_hbm, idx_hbm, out_hbm, idx_v, rows_v, sem):
        wid = lax.axis_index("s") * SC_CORES + lax.axis_index("c")
        base = wid * per_w

        @pl.loop(0, n_win)
        def _(win):
            off = base + win * SC_GATHER_ROWS
            pltpu.sync_copy(idx_hbm.at[pl.ds(off, SC_GATHER_ROWS)], idx_v)
            pltpu.async_copy(src_hbm.at[idx_v], rows_v, sem).wait()
            pltpu.sync_copy(rows_v, out_hbm.at[pl.ds(off, SC_GATHER_ROWS)])

    return gather(src, idx)


def _scatter_rows(src, dest, n_out, first, group_rows):
    n, w = src.shape
    n_assign = dest.shape[0]
    workers = SC_CORES * SC_SUBCORES
    per_w = n_assign // workers
    n_win = per_w // SC_GATHER_ROWS
    assert per_w * workers == n_assign and n_win * SC_GATHER_ROWS == per_w and group_rows % per_w == 0
    mesh = plsc.VectorSubcoreMesh(core_axis_name="c", subcore_axis_name="s")

    @functools.partial(
        pl.kernel, mesh=mesh, out_type=jax.ShapeDtypeStruct((n_out, w), src.dtype),
        scratch_types=[pltpu.VMEM((SC_GATHER_ROWS,), jnp.int32),
                       pltpu.VMEM((SC_GATHER_ROWS, w), src.dtype),
                       pltpu.SemaphoreType.DMA])
    def scatter(src_hbm, dest_hbm, out_hbm, idx_v, rows_v, sem):
        wid = lax.axis_index("s") * SC_CORES + lax.axis_index("c")
        base = wid * per_w
        row_base = first + lax.rem(base, group_rows)

        @pl.loop(0, n_win)
        def _(win):
            pltpu.sync_copy(dest_hbm.at[pl.ds(base + win * SC_GATHER_ROWS, SC_GATHER_ROWS)], idx_v)
            pltpu.sync_copy(src_hbm.at[pl.ds(row_base + win * SC_GATHER_ROWS, SC_GATHER_ROWS)], rows_v)
            pltpu.async_copy(rows_v, out_hbm.at[idx_v], sem).wait()

    return scatter(src, dest)


def _ffn_kernel(te_ref, tf_ref, tr_ref, x_ref, w1_ref, b1_ref, w2_ref, b2_ref, o_ref, w1_sc, w2_sc):
    flag = tf_ref[pl.program_id(0)]
    n_rows = tr_ref[pl.program_id(0)]

    @pl.when(flag == TILE_FIRST)
    def _():
        w1_sc[...] = w1_ref[0].astype(BF16)
        w2_sc[...] = w2_ref[0].astype(BF16)

    @pl.when(flag == TILE_PAD)
    def _():
        o_ref[...] = jnp.zeros(o_ref.shape, o_ref.dtype)

    @pl.when(flag != TILE_PAD)
    def _():
        xw = x_ref[...]
        rid = lax.broadcasted_iota(jnp.int32, xw.shape, 0)
        lo, hi = _unpack_pairs(jnp.where(rid < n_rows, xw, jnp.zeros_like(xw)))
        x = jnp.concatenate([lo, hi], axis=1).astype(BF16)
        a = jnp.dot(x, w1_sc[...], preferred_element_type=F32) + b1_ref[0]
        gate = jnp.minimum(a[:, :EXPERT_FF], SWIGLU_LIMIT)
        up = jnp.clip(a[:, EXPERT_FF:], -SWIGLU_LIMIT, SWIGLU_LIMIT)
        act = gate * jax.nn.sigmoid(SWIGLU_ALPHA * gate) * (up + 1.0)
        y = jnp.dot(act.astype(BF16), w2_sc[...], preferred_element_type=F32) + b2_ref[0]
        o_ref[...] = _pack_pairs(y)


def _ffn(xs, tile_expert, tile_flag, tile_rows, w1, b1, w2, b2, tg):
    p = xs.shape[0]
    grid_spec = pltpu.PrefetchScalarGridSpec(
        num_scalar_prefetch=3,
        grid=(p // tg,),
        in_specs=[pl.BlockSpec((tg, PACK_W), lambda j, te, tf, tr: (j, 0)),
                  pl.BlockSpec((1, D_MODEL, 2 * EXPERT_FF), lambda j, te, tf, tr: (te[j], 0, 0)),
                  pl.BlockSpec((1, 1, 2 * EXPERT_FF), lambda j, te, tf, tr: (te[j], 0, 0)),
                  pl.BlockSpec((1, EXPERT_FF, D_MODEL), lambda j, te, tf, tr: (te[j], 0, 0)),
                  pl.BlockSpec((1, 1, D_MODEL), lambda j, te, tf, tr: (te[j], 0, 0))],
        out_specs=pl.BlockSpec((tg, PACK_W), lambda j, te, tf, tr: (j, 0)),
        scratch_shapes=[pltpu.VMEM((D_MODEL, 2 * EXPERT_FF), BF16),
                        pltpu.VMEM((EXPERT_FF, D_MODEL), BF16)],
    )
    return pl.pallas_call(
        _ffn_kernel,
        grid_spec=grid_spec,
        out_shape=jax.ShapeDtypeStruct((p, PACK_W), PACK_DTYPE),
        compiler_params=_params(1),
        name="ffn",
    )(tile_expert, tile_flag, tile_rows, xs, w1, b1, w2, b2)


def _combine_kernel(x_ref, g2_ref, w_ref, y_ref, o_ref):
    w = w_ref[...]
    acc_lo = None
    for k in range(TOP_K):
        lo, hi = _unpack_pairs(y_ref[k])
        wk = w[:, k:k + 1]
        acc_lo = wk * lo if acc_lo is None else acc_lo + wk * lo
        acc_hi = wk * hi if k == 0 else acc_hi + wk * hi
    acc = jnp.concatenate([acc_lo, acc_hi], axis=1)
    o_ref[...] = x_ref[...] + g2_ref[0] * acc


def _combine_into_kernel(x_ref, g2_ref, w_ref, y_ref, prev_ref, o_ref):
    del prev_ref
    _combine_kernel(x_ref, g2_ref, w_ref, y_ref, o_ref)


def _combine(x1, g2, w_tok, yk, seq, first_row, prev):
    t = x1.shape[0]
    rows = yk.shape[1]
    tm = _largest_tile(seq, COMBINE_ROWS, SUBLANES)
    per_b = seq // tm
    i0 = first_row // tm
    in_specs = [pl.BlockSpec((tm, D_MODEL), lambda i: (i0 + i, 0)),
                pl.BlockSpec((1, 1, D_MODEL), lambda i: ((i0 + i) // per_b, 0, 0)),
                pl.BlockSpec((tm, TOP_K), lambda i: (i0 + i, 0)),
                pl.BlockSpec((TOP_K, tm, PACK_W), lambda i: (0, i, 0))]
    args = [x1, g2, w_tok, yk]
    if prev is not None:
        in_specs.append(pl.BlockSpec(memory_space=pl.ANY))
        args.append(prev)
    return pl.pallas_call(
        _combine_kernel if prev is None else _combine_into_kernel,
        grid=(rows // tm,),
        in_specs=in_specs,
        out_specs=pl.BlockSpec((tm, D_MODEL), lambda i: (i0 + i, 0)),
        out_shape=jax.ShapeDtypeStruct((t, D_MODEL), F32),
        input_output_aliases={} if prev is None else {len(args) - 1: 0},
        compiler_params=_params(1),
        name="combine",
    )(*args)


def _count_kernel(ti_ref, cnt_ref):
    @pl.when(pl.program_id(0) == 0)
    def _():
        cnt_ref[...] = jnp.zeros(cnt_ref.shape, F32)

    bt = ti_ref.shape[1]
    eid = lax.broadcasted_iota(jnp.int32, (N_EXPERTS, bt), 0)
    acc = jnp.zeros(cnt_ref.shape, F32)
    for k in range(TOP_K):
        m = (eid == ti_ref[k:k + 1, :]).astype(F32)
        for c in range(bt // LANES):
            acc = acc + m[:, c * LANES:(c + 1) * LANES]
    cnt_ref[...] += acc


def _rank_kernel(ti_ref, off_ref, tri_ref, dest_ref, run_sc):
    @pl.when(pl.program_id(0) == 0)
    def _():
        run_sc[...] = off_ref[...] - 1.0

    bt = ti_ref.shape[1]
    eid = lax.broadcasted_iota(jnp.int32, (N_EXPERTS, bt), 0)
    run = run_sc[...]
    for k in range(TOP_K):
        m = eid == ti_ref[k:k + 1, :]
        pre = jnp.dot(jnp.where(m, 1.0, 0.0).astype(BF16), tri_ref[...], preferred_element_type=F32)
        slot = jnp.sum(jnp.where(m, pre + run, 0.0), axis=0, keepdims=True)
        dest_ref[k:k + 1, :] = slot.astype(jnp.int32)
        run = run + pre[:, bt - 1:bt]
    run_sc[...] = run


def _plan(top_i, tg):
    t = top_i.shape[1]
    n_assign = TOP_K * t
    p = n_assign + N_EXPERTS * tg
    n_tiles = p // tg
    bt = _largest_tile(t, PLAN_BLOCK, LANES)
    cnt = pl.pallas_call(
        _count_kernel,
        grid=(t // bt,),
        in_specs=[pl.BlockSpec((TOP_K, bt), lambda i: (0, i))],
        out_specs=pl.BlockSpec((N_EXPERTS, LANES), lambda i: (0, 0)),
        out_shape=jax.ShapeDtypeStruct((N_EXPERTS, LANES), F32),
        compiler_params=_params(1),
        name="count",
    )(top_i)
    counts = jnp.sum(cnt, axis=1).astype(jnp.int32)
    padded = ((counts + tg - 1) // tg) * tg
    off_end = jnp.cumsum(padded)
    off = off_end - padded
    tri = (jnp.arange(bt)[:, None] <= jnp.arange(bt)[None, :]).astype(BF16)
    dest = pl.pallas_call(
        _rank_kernel,
        grid=(t // bt,),
        in_specs=[pl.BlockSpec((TOP_K, bt), lambda i: (0, i)),
                  pl.BlockSpec((N_EXPERTS, 1), lambda i: (0, 0)),
                  pl.BlockSpec((bt, bt), lambda i: (0, 0))],
        out_specs=pl.BlockSpec((TOP_K, bt), lambda i: (0, i)),
        out_shape=jax.ShapeDtypeStruct((TOP_K, t), jnp.int32),
        scratch_shapes=[pltpu.VMEM((N_EXPERTS, 1), F32)],
        compiler_params=_params(1),
        name="rank",
    )(top_i, off.astype(F32).reshape(N_EXPERTS, 1), tri)
    tile_start = jnp.arange(n_tiles, dtype=jnp.int32) * tg
    tile_valid = tile_start < off_end[-1]
    te = jnp.sum((tile_start[:, None] >= off_end[None, :]).astype(jnp.int32), axis=1)
    last_e = jnp.sum(((off_end[-1] - 1) >= off_end).astype(jnp.int32))
    tile_expert = jnp.where(tile_valid, te, last_e)
    tile_rows = jnp.where(tile_valid, jnp.clip(counts[tile_expert] - (tile_start - off[tile_expert]), 0, tg), 0)
    changed = jnp.concatenate([jnp.ones((1,), jnp.bool_), tile_expert[1:] != tile_expert[:-1]])
    tile_flag = jnp.where(tile_valid, jnp.where(changed, TILE_FIRST, TILE_BODY), TILE_PAD).astype(jnp.int32)
    return dest.reshape(-1), tile_expert, tile_flag, tile_rows.astype(jnp.int32)


def kernel(x, c, ctx, c_ctx, norm1, norm2, w_ada, b_ada, w_in, ret_decay_f, ret_decay_b, attn_q_norm, attn_k_norm,
           w_ret_o, w_attn_o, w_out, w_router, b_router, w_exp_in, b_exp_in, w_exp_out, b_exp_out):
    assert w_in.shape[0] == 1, "single-layer block"
    b, seq, d = x.shape
    n_ctx = ctx.shape[1]
    t = b * seq
    rows = seq // GRID_W

    idx = np.cumsum(IN_SIZES)[:-1].tolist()
    wq_r, wk_r, wv_r, wg_r, wq_a, wk_a, wv_a, wgt_r, wgt_a = jnp.split(w_in[0], idx, axis=-1)
    w_in_p = jnp.concatenate([wq_r, wk_r, wv_r, wg_r, wq_a, wgt_r, wgt_a, wk_a, wv_a], axis=-1)
    w1 = w_exp_in[0]
    w2 = w_exp_out[0]
    b1 = b_exp_in[0].reshape(N_EXPERTS, 1, 2 * EXPERT_FF)
    b2 = b_exp_out[0].reshape(N_EXPERTS, 1, D_MODEL)

    pad = (-(b + 1)) % SUBLANES
    c_all = jnp.concatenate([c, c_ctx[None, :], jnp.zeros((pad, d), F32)], axis=0)
    mod = _ada(c_all, w_ada[0], b_ada[0])
    sh1, sc1, g1, sh2, sc2, g2 = [m.reshape(-1, 1, d) for m in jnp.split(mod, 6, axis=-1)]
    lat = lambda m: m[:b]
    cx = lambda m: m[b:b + 1]

    x2 = x.reshape(t, d)
    proj = _inproj(x2, norm1, lat(sc1), lat(sh1), w_in_p, seq)
    proj_c = _inproj(ctx.reshape(b * n_ctx, d), norm1, cx(sc1), cx(sh1), w_in_p, n_ctx)

    cos_r, sin_r = _rope_tables(rows, RET_QK_DIM)
    cos_a, sin_a = _rope_tables(rows, ATTN_HEAD_DIM)
    lg_f = -jax.nn.softplus(ret_decay_f[0].astype(F32))
    lg_b = -jax.nn.softplus(ret_decay_b[0].astype(F32))
    zero_state = jnp.zeros((2, b, RET_HEADS, RET_QK_DIM, RET_V_DIM), F32)
    cos_c = jnp.ones((n_ctx, RET_QK_DIM), F32)
    sin_c = jnp.zeros((n_ctx, RET_QK_DIM), F32)
    _, _, s_ctx = _retention(proj_c, cos_c, sin_c, lg_f, lg_b, zero_state, b, n_ctx, False)
    o_f, o_b, _ = _retention(proj, cos_r, sin_r, lg_f, lg_b, s_ctx, b, seq, True)

    k_all, vt_all = _kvprep(proj_c, proj, attn_k_norm, cos_a, sin_a, b, n_ctx, seq)
    attn_o = _attention(proj, attn_q_norm, cos_a, sin_a, k_all, vt_all, b, seq)

    x1 = _merge(x2, lat(g1), o_f, o_b, proj, attn_o, w_ret_o[0].astype(BF16), w_attn_o[0].astype(BF16),
                w_out[0].astype(BF16), seq)

    h2, top_i, top_w = _router(x1, norm2, lat(sc2), lat(sh2), w_router[0].T, b_router[0].reshape(N_EXPERTS, 1), seq)
    n_groups = MOE_GROUPS if b % MOE_GROUPS == 0 else 1
    tgrp = t // n_groups
    tg = _largest_tile(TOP_K * tgrp, FFN_ROWS, SUBLANES)
    w_tok = top_w.T
    out = None
    for grp in range(n_groups):
        first = grp * tgrp
        dest, tile_expert, tile_flag, tile_rows = _plan(top_i[:, first:first + tgrp], tg)
        xs = _scatter_rows(h2, dest, TOP_K * tgrp + N_EXPERTS * tg, first, tgrp)
        ys = _ffn(xs, tile_expert, tile_flag, tile_rows, w1, b1, w2, b2, tg)
        yk = _gather_rows(ys, dest).reshape(TOP_K, tgrp, PACK_W)
        out = _combine(x1, lat(g2), w_tok, yk, seq, first, out)
    return out.reshape(b, seq, d)
```

```python
import functools

import jax
import jax.numpy as jnp
import numpy as np
from jax import lax
from jax.experimental import pallas as pl
from jax.experimental.pallas import tpu as pltpu
from jax.experimental.pallas import tpu_sc as plsc

F32 = jnp.float32
BF16 = jnp.bfloat16

D_MODEL = 1024
GRID_W = 64
EPS = 1e-6
RET_HEADS = 4
RET_QK_DIM = 256
RET_V_DIM = 512
ATTN_HEADS = 8
ATTN_KV_HEADS = 2
ATTN_GROUP = ATTN_HEADS // ATTN_KV_HEADS
ATTN_HEAD_DIM = 128
ROPE_THETA = 10000.0
N_EXPERTS = 32
TOP_K = 4
EXPERT_FF = 1024
SWIGLU_LIMIT = 7.0
SWIGLU_ALPHA = 1.702

RET_QK_W = RET_HEADS * RET_QK_DIM
RET_V_W = RET_HEADS * RET_V_DIM
ATTN_Q_W = ATTN_HEADS * ATTN_HEAD_DIM
ATTN_KV_W = ATTN_KV_HEADS * ATTN_HEAD_DIM
IN_SIZES = (RET_QK_W, RET_QK_W, RET_V_W, RET_V_W, ATTN_Q_W, ATTN_KV_W, ATTN_KV_W, D_MODEL, D_MODEL)
IN_WIDTH = sum(IN_SIZES)
COL_QR = 0
COL_KR = COL_QR + RET_QK_W
COL_VR = COL_KR + RET_QK_W
COL_GR = COL_VR + RET_V_W
COL_QA = COL_GR + RET_V_W
COL_GATE_R = COL_QA + ATTN_Q_W
COL_GATE_A = COL_GATE_R + D_MODEL
COL_KA = COL_GATE_A + D_MODEL
COL_VA = COL_KA + ATTN_KV_W

RET_CHUNK = 256
RET_HEADS_PER_STEP = 4
LANES = 128
SUBLANES = 8
SC_CORES = 2
SC_SUBCORES = 16
SC_GATHER_ROWS = 128
MXU_DEPTH = 256
ADA_COLS = 1536
INPROJ_ROWS = 1024
INPROJ_COLS = 2560
MERGE_ROWS = 512
ROUTER_ROWS = 1024
COMBINE_ROWS = 512
FFN_ROWS = 512
ATTN_KEY_CHUNK = 768
ATTN_Q_TILE = 1024
ATTN_EXP_SLAB = 32
ONES_ROWS = 16
KV_PREP_ROWS = 256
PACK_DTYPE = jnp.int32
PACK_W = D_MODEL // 2
MOE_GROUPS = 2
PLAN_BLOCK = 512
TILE_PAD, TILE_BODY, TILE_FIRST = 0, 1, 2
VMEM_LIMIT = 56 * 1024 * 1024

ARB = pltpu.ARBITRARY


def _params(n_axes, **kw):
    return pltpu.CompilerParams(dimension_semantics=(ARB,) * n_axes, vmem_limit_bytes=VMEM_LIMIT, **kw)


def _largest_tile(n, cap, mult):
    best = None
    for t in range(mult, min(n, cap) + 1, mult):
        if n % t == 0:
            best = t
    assert best is not None, (n, cap, mult)
    return best


def _rms(x):
    return x * lax.rsqrt(jnp.mean(x * x, axis=-1, keepdims=True) + EPS)


def _pack_pairs(x):
    half = x.shape[1] // 2
    lo = lax.bitcast_convert_type(x[:, :half].astype(BF16).astype(F32), jnp.int32)
    hi = lax.bitcast_convert_type(x[:, half:].astype(BF16).astype(F32), jnp.int32)
    return lax.bitwise_or(lax.bitwise_and(hi, jnp.int32(-65536)), lax.shift_right_logical(lo, jnp.int32(16)))


def _unpack_pairs(w):
    lo = lax.bitcast_convert_type(lax.shift_left(w, jnp.int32(16)), F32)
    hi = lax.bitcast_convert_type(lax.bitwise_and(w, jnp.int32(-65536)), F32)
    return lo, hi


def _rope_block(x, cos, sin, half):
    if 2 * half == LANES:
        swapped = pltpu.roll(x, half, 1)
    else:
        lane = lax.broadcasted_iota(jnp.int32, x.shape, 1)
        first = (lane % (2 * half)) < half
        swapped = jnp.where(first, pltpu.roll(x, LANES - half, 1), pltpu.roll(x, half, 1))
    return x * cos + swapped * sin


def _rope_tables(rows, head_dim):
    n_freq = head_dim // 4
    inv_freq = ROPE_THETA ** (-jnp.arange(n_freq, dtype=F32) / n_freq)
    ang_r = jnp.arange(rows, dtype=F32)[:, None] * inv_freq
    ang_c = jnp.arange(GRID_W, dtype=F32)[:, None] * inv_freq
    per_row = lambda a: jnp.broadcast_to(a[:, None, :], (rows, GRID_W, n_freq)).reshape(rows * GRID_W, n_freq)
    per_col = lambda a: jnp.broadcast_to(a[None, :, :], (rows, GRID_W, n_freq)).reshape(rows * GRID_W, n_freq)
    cos_r, sin_r = per_row(jnp.cos(ang_r)), per_row(jnp.sin(ang_r))
    cos_c, sin_c = per_col(jnp.cos(ang_c)), per_col(jnp.sin(ang_c))
    cos = jnp.concatenate([cos_r, cos_r, cos_c, cos_c], axis=1)
    sin = jnp.concatenate([-sin_r, sin_r, -sin_c, sin_c], axis=1)
    return cos, sin


def _ada_kernel(c_ref, w_ref, b_ref, o_ref):
    c = c_ref[...]
    s = c * jax.nn.sigmoid(c)
    o_ref[...] = jnp.dot(s, w_ref[...], preferred_element_type=F32,
                         precision=lax.Precision.HIGHEST) + b_ref[...]


def _ada(c_pad, w_ada, b_ada):
    rows = c_pad.shape[0]
    n = w_ada.shape[1]
    tn = _largest_tile(n, ADA_COLS, LANES)
    return pl.pallas_call(
        _ada_kernel,
        grid=(n // tn,),
        in_specs=[pl.BlockSpec((rows, D_MODEL), lambda j: (0, 0)),
                  pl.BlockSpec((D_MODEL, tn), lambda j: (0, j)),
                  pl.BlockSpec((1, tn), lambda j: (0, j))],
        out_specs=pl.BlockSpec((rows, tn), lambda j: (0, j)),
        out_shape=jax.ShapeDtypeStruct((rows, n), F32),
        compiler_params=_params(1),
        name="ada",
    )(c_pad, w_ada, b_ada.reshape(1, n))


def _inproj_kernel(x_ref, n_ref, sc_ref, sh_ref, w_ref, o_ref, w_sc):
    @pl.when(pl.program_id(1) == 0)
    def _():
        w_sc[...] = w_ref[...].astype(BF16)

    y = _rms(x_ref[...]) * n_ref[...]
    h = (y * (1.0 + sc_ref[0]) + sh_ref[0]).astype(BF16)
    o_ref[...] = jnp.dot(h, w_sc[...], preferred_element_type=F32).astype(o_ref.dtype)


def _inproj(x2, norm, sc, sh, w, rows_per_batch):
    t = x2.shape[0]
    n = w.shape[1]
    tm = _largest_tile(rows_per_batch, INPROJ_ROWS, SUBLANES)
    tn = _largest_tile(n, INPROJ_COLS, LANES)
    per_b = rows_per_batch // tm
    if sc.shape[0] == 1:
        mod_map = lambda j, i: (0, 0, 0)
    else:
        mod_map = lambda j, i: (i // per_b, 0, 0)
    return pl.pallas_call(
        _inproj_kernel,
        grid=(n // tn, t // tm),
        in_specs=[pl.BlockSpec((tm, D_MODEL), lambda j, i: (i, 0)),
                  pl.BlockSpec((1, D_MODEL), lambda j, i: (0, 0)),
                  pl.BlockSpec((1, 1, D_MODEL), mod_map),
                  pl.BlockSpec((1, 1, D_MODEL), mod_map),
                  pl.BlockSpec((D_MODEL, tn), lambda j, i: (0, j))],
        out_specs=pl.BlockSpec((tm, tn), lambda j, i: (i, j)),
        out_shape=jax.ShapeDtypeStruct((t, n), BF16),
        scratch_shapes=[pltpu.VMEM((D_MODEL, tn), BF16)],
        compiler_params=_params(2),
        name="inproj",
    )(x2, norm, sc, sh, w)


def _kvprep_kernel(kc_ref, vc_ref, kl_ref, vl_ref, g_ref, cos_ref, sin_ref, k_ref, vt_ref, *, ctx_blocks):
    j = pl.program_id(1)
    vrows = ATTN_HEAD_DIM + ONES_ROWS

    def emit(kin_ref, vin_ref, use_rope):
        k = kin_ref[...].astype(F32)
        v = vin_ref[...].astype(F32)
        for g in range(ATTN_KV_HEADS):
            sl = slice(g * ATTN_HEAD_DIM, (g + 1) * ATTN_HEAD_DIM)
            kh = _rms(k[:, sl]) * g_ref[...]
            if use_rope:
                kh = _rope_block(kh, cos_ref[...], sin_ref[...], ATTN_HEAD_DIM // 4)
            k_ref[0, :, sl] = kh.astype(BF16)
            vt_ref[0, g * vrows:g * vrows + ATTN_HEAD_DIM, :] = v[:, sl].T.astype(BF16)
            vt_ref[0, g * vrows + ATTN_HEAD_DIM:(g + 1) * vrows, :] = jnp.ones((ONES_ROWS, v.shape[0]), BF16)

    @pl.when(j < ctx_blocks)
    def _():
        emit(kc_ref, vc_ref, False)

    @pl.when(j >= ctx_blocks)
    def _():
        emit(kl_ref, vl_ref, True)


def _kvprep(proj_c, proj, gain, cos, sin, batch, n_ctx, seq):
    tm = KV_PREP_ROWS
    assert n_ctx % tm == 0 and seq % tm == 0
    cb, lb = n_ctx // tm, seq // tm
    lk = n_ctx + seq
    ctx_row = lambda b, j: b * cb + jnp.minimum(j, cb - 1)
    lat_blk = lambda j: jnp.maximum(j - cb, 0)
    lat_row = lambda b, j: b * lb + lat_blk(j)
    vrows = ATTN_KV_HEADS * (ATTN_HEAD_DIM + ONES_ROWS)
    return pl.pallas_call(
        functools.partial(_kvprep_kernel, ctx_blocks=cb),
        grid=(batch, cb + lb),
        in_specs=[pl.BlockSpec((tm, ATTN_KV_W), lambda b, j: (ctx_row(b, j), COL_KA // ATTN_KV_W)),
                  pl.BlockSpec((tm, ATTN_KV_W), lambda b, j: (ctx_row(b, j), COL_VA // ATTN_KV_W)),
                  pl.BlockSpec((tm, ATTN_KV_W), lambda b, j: (lat_row(b, j), COL_KA // ATTN_KV_W)),
                  pl.BlockSpec((tm, ATTN_KV_W), lambda b, j: (lat_row(b, j), COL_VA // ATTN_KV_W)),
                  pl.BlockSpec((1, ATTN_HEAD_DIM), lambda b, j: (0, 0)),
                  pl.BlockSpec((tm, ATTN_HEAD_DIM), lambda b, j: (lat_blk(j), 0)),
                  pl.BlockSpec((tm, ATTN_HEAD_DIM), lambda b, j: (lat_blk(j), 0))],
        out_specs=[pl.BlockSpec((1, tm, ATTN_KV_W), lambda b, j: (b, j, 0)),
                   pl.BlockSpec((1, vrows, tm), lambda b, j: (b, 0, j))],
        out_shape=[jax.ShapeDtypeStruct((batch, lk, ATTN_KV_W), BF16),
                   jax.ShapeDtypeStruct((batch, vrows, lk), BF16)],
        compiler_params=_params(2),
        name="kvprep",
    )(proj_c, proj_c, proj, proj, gain, cos, sin)


def _ret_kernel(dch_ref, qf_ref, kf_ref, vf_ref, cosf_ref, sinf_ref, qb_ref, kb_ref, vb_ref, cosb_ref, sinb_ref,
                dmat_ref, din_ref, dout_ref, s0_ref, of_ref, ob_ref, sfin_ref, s_sc, *, use_rope, n_chunks, heads):
    hb = pl.program_id(1)
    c = pl.program_id(2)

    @pl.when(c == 0)
    def _():
        s_sc[...] = s0_ref[:, 0]

    def rope(x, cos, sin):
        return jnp.concatenate([_rope_block(x[:, i * LANES:(i + 1) * LANES], cos[:, i * LANES:(i + 1) * LANES],
                                            sin[:, i * LANES:(i + 1) * LANES], RET_QK_DIM // 4)
                                for i in range(RET_QK_DIM // LANES)], axis=1)

    sides = ((qf_ref, kf_ref, vf_ref, cosf_ref, sinf_ref), (qb_ref, kb_ref, vb_ref, cosb_ref, sinb_ref))
    chains = [(d, j) for d in range(2) for j in range(heads)]
    qs, ks, vs, ss = {}, {}, {}, {}
    for d, j in chains:
        q_ref, k_ref, v_ref, cos_ref, sin_ref = sides[d]
        q = q_ref[:, j * RET_QK_DIM:(j + 1) * RET_QK_DIM].astype(F32)
        k = k_ref[:, j * RET_QK_DIM:(j + 1) * RET_QK_DIM].astype(F32)
        if use_rope:
            q = rope(q, cos_ref[...], sin_ref[...])
            k = rope(k, cos_ref[...], sin_ref[...])
        qs[d, j], ks[d, j] = q, k
        vs[d, j] = v_ref[:, j * RET_V_DIM:(j + 1) * RET_V_DIM]
        ss[d, j] = s_sc[d, j]
    scores = {ch: lax.dot_general(qs[ch].astype(BF16), ks[ch].astype(BF16), (((1,), (1,)), ((), ())),
                                  preferred_element_type=F32) * dmat_ref[ch[0], ch[1]] for ch in chains}
    inter = {ch: jnp.dot((qs[ch] * din_ref[ch[0], ch[1]]).astype(BF16), ss[ch].astype(BF16),
                         preferred_element_type=F32) for ch in chains}
    intra = {ch: jnp.dot(scores[ch].astype(BF16), vs[ch], preferred_element_type=F32) for ch in chains}
    o_refs = (of_ref, ob_ref)
    for d, j in chains:
        o_refs[d][:, j * RET_V_DIM:(j + 1) * RET_V_DIM] = (intra[d, j] + inter[d, j]).astype(of_ref.dtype)
    s_new = {ch: ss[ch] * dch_ref[ch[0] * RET_HEADS + hb * heads + ch[1]]
             + jnp.dot((ks[ch] * dout_ref[ch[0], ch[1]]).T.astype(BF16), vs[ch], preferred_element_type=F32)
             for ch in chains}
    for d, j in chains:
        s_sc[d, j] = s_new[d, j]

    @pl.when(c == n_chunks - 1)
    def _():
        for d, j in chains:
            sfin_ref[d, 0, j] = s_new[d, j]


def _ret_tables(log_gamma, chunk, reverse, k_scale):
    pos = jnp.arange(chunk, dtype=F32)
    diff = pos[:, None] - pos[None, :]
    if reverse:
        diff = -diff
        mask = diff > 0
        p_in = chunk - pos
        p_out = pos
    else:
        mask = diff >= 0
        p_in = pos + 1.0
        p_out = chunk - 1.0 - pos
    lg = log_gamma.astype(F32)
    dmat = jnp.where(mask[None], jnp.exp(lg[:, None, None] * jnp.maximum(diff, 0.0)[None]), 0.0) * k_scale
    d_in = jnp.exp(lg[:, None] * p_in)
    d_out = jnp.exp(lg[:, None] * p_out) * k_scale
    d_in = jnp.broadcast_to(d_in[:, :, None], (RET_HEADS, chunk, RET_QK_DIM))
    d_out = jnp.broadcast_to(d_out[:, :, None], (RET_HEADS, chunk, RET_QK_DIM))
    d_chunk = jnp.exp(lg * chunk)
    return d_chunk, dmat, d_in, d_out


def _retention(proj, cos, sin, lg_f, lg_b, s0, batch, seq, use_rope):
    chunk = min(RET_CHUNK, seq)
    nc = seq // chunk
    hs = RET_HEADS_PER_STEP
    qw, vw = hs * RET_QK_DIM, hs * RET_V_DIM
    k_scale = RET_QK_DIM ** -0.5
    tabs = [_ret_tables(lg_f, chunk, False, k_scale), _ret_tables(lg_b, chunk, True, k_scale)]
    d_chunk, dmat, d_in, d_out = [jnp.stack([tabs[0][i], tabs[1][i]]) for i in range(4)]
    fwd = lambda c: c
    bwd = lambda c: nc - 1 - c

    def side(cidx):
        row = lambda b, c: b * nc + cidx(c)
        return [pl.BlockSpec((chunk, qw), lambda b, h, c: (row(b, c), COL_QR // qw + h)),
                pl.BlockSpec((chunk, qw), lambda b, h, c: (row(b, c), COL_KR // qw + h)),
                pl.BlockSpec((chunk, vw), lambda b, h, c: (row(b, c), COL_VR // vw + h)),
                pl.BlockSpec((chunk, RET_QK_DIM), lambda b, h, c: (cidx(c), 0)),
                pl.BlockSpec((chunk, RET_QK_DIM), lambda b, h, c: (cidx(c), 0))]

    state_spec = pl.BlockSpec((2, 1, hs, RET_QK_DIM, RET_V_DIM), lambda b, h, c: (0, b, h, 0, 0))
    return pl.pallas_call(
        functools.partial(_ret_kernel, use_rope=use_rope, n_chunks=nc, heads=hs),
        grid=(batch, RET_HEADS // hs, nc),
        in_specs=[pl.BlockSpec(memory_space=pltpu.SMEM)] + side(fwd) + side(bwd) + [
            pl.BlockSpec((2, hs, chunk, chunk), lambda b, h, c: (0, h, 0, 0)),
            pl.BlockSpec((2, hs, chunk, RET_QK_DIM), lambda b, h, c: (0, h, 0, 0)),
            pl.BlockSpec((2, hs, chunk, RET_QK_DIM), lambda b, h, c: (0, h, 0, 0)),
            state_spec],
        out_specs=[pl.BlockSpec((chunk, vw), lambda b, h, c: (b * nc + c, h)),
                   pl.BlockSpec((chunk, vw), lambda b, h, c: (b * nc + nc - 1 - c, h)),
                   state_spec],
        out_shape=[jax.ShapeDtypeStruct((batch * seq, RET_V_W), BF16),
                   jax.ShapeDtypeStruct((batch * seq, RET_V_W), BF16),
                   jax.ShapeDtypeStruct((2, batch, RET_HEADS, RET_QK_DIM, RET_V_DIM), F32)],
        scratch_shapes=[pltpu.VMEM((2, hs, RET_QK_DIM, RET_V_DIM), F32)],
        compiler_params=_params(3),
        name="ret",
    )(d_chunk.reshape(-1), proj, proj, proj, cos, sin, proj, proj, proj, cos, sin, dmat, d_in, d_out, s0)


def _attn_kernel(q_ref, g_ref, cos_ref, sin_ref, k_ref, vt_ref, o_ref, qt_sc, m_sc, acc_sc,
                 s0_sc, s1_sc, p0_sc, p1_sc, a0_sc, a1_sc, *, tq, kc, n_chunks):
    s_bufs = (s0_sc, s1_sc)
    p_bufs = (p0_sc, p1_sc)
    a_bufs = (a0_sc, a1_sc)

    q = q_ref[...].astype(F32)
    scale = ATTN_HEAD_DIM ** -0.5 * np.log2(np.e)
    for g in range(ATTN_GROUP):
        qh = _rms(q[:, g * ATTN_HEAD_DIM:(g + 1) * ATTN_HEAD_DIM]) * g_ref[...]
        qh = _rope_block(qh, cos_ref[...], sin_ref[...], ATTN_HEAD_DIM // 4) * scale
        qt_sc[:, g * tq:(g + 1) * tq] = qh.T.astype(BF16)
    m_sc[...] = jnp.full(m_sc.shape, -jnp.inf, F32)
    acc_sc[...] = jnp.zeros(acc_sc.shape, F32)

    def stage_s(c, slot):
        off = pl.multiple_of(c * kc, kc)
        s_bufs[slot][...] = jnp.dot(k_ref[0, pl.ds(off, kc), :], qt_sc[...], preferred_element_type=F32)

    def stage_f(slot):
        m_prev = m_sc[...]
        m_new = jnp.maximum(m_prev, jnp.max(s_bufs[slot][...], axis=0, keepdims=True))
        m_sc[...] = m_new
        a_bufs[slot][...] = jnp.exp2(m_prev - m_new)
        for r in range(0, kc, ATTN_EXP_SLAB):
            p_bufs[slot][r:r + ATTN_EXP_SLAB, :] = jnp.exp2(
                s_bufs[slot][r:r + ATTN_EXP_SLAB, :] - m_new).astype(BF16)

    def stage_a(c, slot):
        off = pl.multiple_of(c * kc, kc)
        acc_sc[...] = a_bufs[slot][...] * acc_sc[...] + jnp.dot(
            vt_ref[0, :, pl.ds(off, kc)], p_bufs[slot][...], preferred_element_type=F32)

    def tick(t, parity):
        stage_s(t, parity)
        stage_f(1 - parity)
        stage_a(t - 2, parity)

    n = n_chunks
    stage_s(0, 0)
    if n > 1:
        stage_s(1, 1)
    stage_f(0)
    first = 2
    if n > 2 and (n - 2) % 2 == 1:
        tick(2, 0)
        first = 3
    n_pairs = (n - first) // 2 if n > first else 0
    if n_pairs > 0:
        def pair(u, carry):
            t = first + 2 * u
            tick(t, first % 2)
            tick(t + 1, 1 - first % 2)
            return carry
        lax.fori_loop(0, n_pairs, pair, 0)
    if n > 1:
        stage_f((n - 1) % 2)
        stage_a(n - 2, (n - 2) % 2)
    stage_a(n - 1, (n - 1) % 2)

    acc = acc_sc[...]
    o = acc[:ATTN_HEAD_DIM] / acc[ATTN_HEAD_DIM:ATTN_HEAD_DIM + 1]
    for g in range(ATTN_GROUP):
        o_ref[:, g * ATTN_HEAD_DIM:(g + 1) * ATTN_HEAD_DIM] = o[:, g * tq:(g + 1) * tq].T.astype(o_ref.dtype)


def _attention(proj, gain, cos, sin, k_all, vt_all, batch, seq):
    lk = k_all.shape[1]
    kc = _largest_tile(lk, ATTN_KEY_CHUNK, MXU_DEPTH)
    tq = _largest_tile(seq, ATTN_Q_TILE, SUBLANES)
    nq = seq // tq
    qw = ATTN_GROUP * ATTN_HEAD_DIM
    cols = ATTN_GROUP * tq
    vrows = ATTN_HEAD_DIM + ONES_ROWS
    return pl.pallas_call(
        functools.partial(_attn_kernel, tq=tq, kc=kc, n_chunks=lk // kc),
        grid=(batch, ATTN_KV_HEADS, nq),
        in_specs=[pl.BlockSpec((tq, qw), lambda b, g, i: (b * nq + i, COL_QA // qw + g)),
                  pl.BlockSpec((1, ATTN_HEAD_DIM), lambda b, g, i: (0, 0)),
                  pl.BlockSpec((tq, ATTN_HEAD_DIM), lambda b, g, i: (i, 0)),
                  pl.BlockSpec((tq, ATTN_HEAD_DIM), lambda b, g, i: (i, 0)),
                  pl.BlockSpec((1, lk, ATTN_HEAD_DIM), lambda b, g, i: (b, 0, g), pipeline_mode=pl.Buffered(1)),
                  pl.BlockSpec((1, vrows, lk), lambda b, g, i: (b, g, 0), pipeline_mode=pl.Buffered(1))],
        out_specs=pl.BlockSpec((tq, qw), lambda b, g, i: (b * nq + i, g)),
        out_shape=jax.ShapeDtypeStruct((batch * seq, ATTN_Q_W), BF16),
        scratch_shapes=[pltpu.VMEM((ATTN_HEAD_DIM, cols), BF16),
                        pltpu.VMEM((1, cols), F32),
                        pltpu.VMEM((vrows, cols), F32),
                        pltpu.VMEM((kc, cols), F32),
                        pltpu.VMEM((kc, cols), F32),
                        pltpu.VMEM((kc, cols), BF16),
                        pltpu.VMEM((kc, cols), BF16),
                        pltpu.VMEM((1, cols), F32),
                        pltpu.VMEM((1, cols), F32)],
        compiler_params=_params(3),
        name="attn",
    )(proj, gain, cos, sin, k_all, vt_all)


def _merge_kernel(x_ref, g1_ref, of_ref, ob_ref, gr_ref, ao_ref, gtr_ref, gta_ref,
                  wro_ref, wao_ref, wo_ref, o_ref):
    ro = of_ref[...].astype(F32) + ob_ref[...].astype(F32)
    gr = gr_ref[...].astype(F32)
    parts = []
    for h in range(RET_HEADS):
        sl = slice(h * RET_V_DIM, (h + 1) * RET_V_DIM)
        g = gr[:, sl]
        parts.append((g * jax.nn.sigmoid(g) * _rms(ro[:, sl])).astype(BF16))
    ret_in = jnp.concatenate(parts, axis=1)
    ret_branch = jnp.dot(ret_in, wro_ref[...], preferred_element_type=F32)
    attn_branch = jnp.dot(ao_ref[...], wao_ref[...], preferred_element_type=F32)
    y = (jax.nn.sigmoid(gtr_ref[...].astype(F32)) * ret_branch
         + jax.nn.sigmoid(gta_ref[...].astype(F32)) * attn_branch)
    y = jnp.dot(y.astype(BF16), wo_ref[...], preferred_element_type=F32)
    o_ref[...] = x_ref[...] + g1_ref[0] * y


def _merge(x2, g1, o_f, o_b, proj, attn_o, w_ret_o, w_attn_o, w_out, seq):
    t = x2.shape[0]
    tm = _largest_tile(seq, MERGE_ROWS, SUBLANES)
    per_b = seq // tm
    full = lambda i: (0, 0)
    return pl.pallas_call(
        _merge_kernel,
        grid=(t // tm,),
        in_specs=[pl.BlockSpec((tm, D_MODEL), lambda i: (i, 0)),
                  pl.BlockSpec((1, 1, D_MODEL), lambda i: (i // per_b, 0, 0)),
                  pl.BlockSpec((tm, RET_V_W), lambda i: (i, 0)),
                  pl.BlockSpec((tm, RET_V_W), lambda i: (i, 0)),
                  pl.BlockSpec((tm, RET_V_W), lambda i: (i, COL_GR // RET_V_W)),
                  pl.BlockSpec((tm, ATTN_Q_W), lambda i: (i, 0)),
                  pl.BlockSpec((tm, D_MODEL), lambda i: (i, COL_GATE_R // D_MODEL)),
                  pl.BlockSpec((tm, D_MODEL), lambda i: (i, COL_GATE_A // D_MODEL)),
                  pl.BlockSpec((RET_V_W, D_MODEL), full, pipeline_mode=pl.Buffered(1)),
                  pl.BlockSpec((ATTN_Q_W, D_MODEL), full, pipeline_mode=pl.Buffered(1)),
                  pl.BlockSpec((D_MODEL, D_MODEL), full, pipeline_mode=pl.Buffered(1))],
        out_specs=pl.BlockSpec((tm, D_MODEL), lambda i: (i, 0)),
        out_shape=jax.ShapeDtypeStruct((t, D_MODEL), F32),
        compiler_params=_params(1),
        name="merge",
    )(x2, g1, o_f, o_b, proj, attn_o, proj, proj, w_ret_o, w_attn_o, w_out)


def _router_kernel(x_ref, n_ref, sc_ref, sh_ref, wr_ref, br_ref, h_ref, ti_ref, tw_ref):
    h = _rms(x_ref[...]) * n_ref[...] * (1.0 + sc_ref[0]) + sh_ref[0]
    h_ref[...] = _pack_pairs(h)
    nt = (((1,), (1,)), ((), ()))
    w = wr_ref[...]
    h_hi = h.astype(BF16)
    h_lo = (h - h_hi.astype(F32)).astype(BF16)
    w_hi = w.astype(BF16)
    w_lo = (w - w_hi.astype(F32)).astype(BF16)
    logits = (lax.dot_general(w_hi, h_hi, nt, preferred_element_type=F32)
              + lax.dot_general(w_lo, h_hi, nt, preferred_element_type=F32)
              + lax.dot_general(w_hi, h_lo, nt, preferred_element_type=F32)) + br_ref[...]
    eid = lax.broadcasted_iota(jnp.int32, logits.shape, 0)
    vals = logits
    top_v = []
    top_i = []
    for _ in range(TOP_K):
        m = jnp.max(vals, axis=0, keepdims=True)
        idx = jnp.min(jnp.where(vals == m, eid, N_EXPERTS), axis=0, keepdims=True)
        top_v.append(m)
        top_i.append(idx)
        vals = jnp.where(eid == idx, -jnp.inf, vals)
    ex = [jnp.exp(v - top_v[0]) for v in top_v]
    denom = ex[0] + ex[1] + ex[2] + ex[3]
    ti_ref[...] = jnp.concatenate(top_i, axis=0)
    tw_ref[...] = jnp.concatenate([e / denom for e in ex], axis=0)


def _router(x1, norm, sc, sh, w_router_t, b_router, seq):
    t = x1.shape[0]
    tm = _largest_tile(seq, ROUTER_ROWS, LANES)
    per_b = seq // tm
    mod_map = lambda i: (i // per_b, 0, 0)
    return pl.pallas_call(
        _router_kernel,
        grid=(t // tm,),
        in_specs=[pl.BlockSpec((tm, D_MODEL), lambda i: (i, 0)),
                  pl.BlockSpec((1, D_MODEL), lambda i: (0, 0)),
                  pl.BlockSpec((1, 1, D_MODEL), mod_map),
                  pl.BlockSpec((1, 1, D_MODEL), mod_map),
                  pl.BlockSpec((N_EXPERTS, D_MODEL), lambda i: (0, 0)),
                  pl.BlockSpec((N_EXPERTS, 1), lambda i: (0, 0))],
        out_specs=[pl.BlockSpec((tm, PACK_W), lambda i: (i, 0)),
                   pl.BlockSpec((TOP_K, tm), lambda i: (0, i)),
                   pl.BlockSpec((TOP_K, tm), lambda i: (0, i))],
        out_shape=[jax.ShapeDtypeStruct((t, PACK_W), PACK_DTYPE),
                   jax.ShapeDtypeStruct((TOP_K, t), jnp.int32),
                   jax.ShapeDtypeStruct((TOP_K, t), F32)],
        compiler_params=_params(1),
        name="router",
    )(x1, norm, sc, sh, w_router_t, b_router)


def _gather_rows(src, idx):
    n, w = src.shape
    r = idx.shape[0]
    workers = SC_CORES * SC_SUBCORES
    per_w = r // workers
    n_win = per_w // SC_GATHER_ROWS
    assert per_w * workers == r and n_win * SC_GATHER_ROWS == per_w, (r, workers, SC_GATHER_ROWS)
    mesh = plsc.VectorSubcoreMesh(core_axis_name="c", subcore_axis_name="s")

    @functools.partial(
        pl.kernel, mesh=mesh, out_type=jax.ShapeDtypeStruct((r, w), src.dtype),
        scratch_types=[pltpu.VMEM((SC_GATHER_ROWS,), jnp.int32),
                       pltpu.VMEM((SC_GATHER_ROWS, w), src.dtype),
                       pltpu.SemaphoreType.DMA])
    def gather(src_hbm, idx_hbm, out_hbm, idx_v, rows_v, sem):
        wid = lax.axis_index("s") * SC_CORES + lax.axis_index("c")
        base = wid * per_w

        @pl.loop(0, n_win)
        def _(win):
            off = base + win * SC_GATHER_ROWS
            pltpu.sync_copy(idx_hbm.at[pl.ds(off, SC_GATHER_ROWS)], idx_v)
            pltpu.async_copy(src_hbm.at[idx_v], rows_v, sem).wait()
            pltpu.sync_copy(rows_v, out_hbm.at[pl.ds(off, SC_GATHER_ROWS)])

    return gather(src, idx)


def _scatter_rows(src, dest, n_out, first, group_rows):
    n, w = src.shape
    n_assign = dest.shape[0]
    workers = SC_CORES * SC_SUBCORES
    per_w = n_assign // workers
    n_win = per_w // SC_GATHER_ROWS
    assert per_w * workers == n_assign and n_win * SC_GATHER_ROWS == per_w and group_rows % per_w == 0
    mesh = plsc.VectorSubcoreMesh(core_axis_name="c", subcore_axis_name="s")

    @functools.partial(
        pl.kernel, mesh=mesh, out_type=jax.ShapeDtypeStruct((n_out, w), src.dtype),
        scratch_types=[pltpu.VMEM((SC_GATHER_ROWS,), jnp.int32),
                       pltpu.VMEM((SC_GATHER_ROWS, w), src.dtype),
                       pltpu.SemaphoreType.DMA])
    def scatter(src_hbm, dest_hbm, out_hbm, idx_v, rows_v, sem):
        wid = lax.axis_index("s") * SC_CORES + lax.axis_index("c")
        base = wid * per_w
        row_base = first + lax.rem(base, group_rows)

        @pl.loop(0, n_win)
        def _(win):
            pltpu.sync_copy(dest_hbm.at[pl.ds(base + win * SC_GATHER_ROWS, SC_GATHER_ROWS)], idx_v)
            pltpu.sync_copy(src_hbm.at[pl.ds(row_base + win * SC_GATHER_ROWS, SC_GATHER_ROWS)], rows_v)
            pltpu.async_copy(rows_v, out_hbm.at[idx_v], sem).wait()

    return scatter(src, dest)


def _ffn_kernel(te_ref, tf_ref, tr_ref, x_ref, w1_ref, b1_ref, w2_ref, b2_ref, o_ref, w1_sc, w2_sc):
    flag = tf_ref[pl.program_id(0)]
    n_rows = tr_ref[pl.program_id(0)]

    @pl.when(flag == TILE_FIRST)
    def _():
        w1_sc[...] = w1_ref[0].astype(BF16)
        w2_sc[...] = w2_ref[0].astype(BF16)

    @pl.when(flag == TILE_PAD)
    def _():
        o_ref[...] = jnp.zeros(o_ref.shape, o_ref.dtype)

    @pl.when(flag != TILE_PAD)
    def _():
        xw = x_ref[...]
        rid = lax.broadcasted_iota(jnp.int32, xw.shape, 0)
        lo, hi = _unpack_pairs(jnp.where(rid < n_rows, xw, jnp.zeros_like(xw)))
        x = jnp.concatenate([lo, hi], axis=1).astype(BF16)
        a = jnp.dot(x, w1_sc[...], preferred_element_type=F32) + b1_ref[0]
        gate = jnp.minimum(a[:, :EXPERT_FF], SWIGLU_LIMIT)
        up = jnp.clip(a[:, EXPERT_FF:], -SWIGLU_LIMIT, SWIGLU_LIMIT)
        act = gate * jax.nn.sigmoid(SWIGLU_ALPHA * gate) * (up + 1.0)
        y = jnp.dot(act.astype(BF16), w2_sc[...], preferred_element_type=F32) + b2_ref[0]
        o_ref[...] = _pack_pairs(y)


def _ffn(xs, tile_expert, tile_flag, tile_rows, w1, b1, w2, b2, tg):
    p = xs.shape[0]
    grid_spec = pltpu.PrefetchScalarGridSpec(
        num_scalar_prefetch=3,
        grid=(p // tg,),
        in_specs=[pl.BlockSpec((tg, PACK_W), lambda j, te, tf, tr: (j, 0)),
                  pl.BlockSpec((1, D_MODEL, 2 * EXPERT_FF), lambda j, te, tf, tr: (te[j], 0, 0)),
                  pl.BlockSpec((1, 1, 2 * EXPERT_FF), lambda j, te, tf, tr: (te[j], 0, 0)),
                  pl.BlockSpec((1, EXPERT_FF, D_MODEL), lambda j, te, tf, tr: (te[j], 0, 0)),
                  pl.BlockSpec((1, 1, D_MODEL), lambda j, te, tf, tr: (te[j], 0, 0))],
        out_specs=pl.BlockSpec((tg, PACK_W), lambda j, te, tf, tr: (j, 0)),
        scratch_shapes=[pltpu.VMEM((D_MODEL, 2 * EXPERT_FF), BF16),
                        pltpu.VMEM((EXPERT_FF, D_MODEL), BF16)],
    )
    return pl.pallas_call(
        _ffn_kernel,
        grid_spec=grid_spec,
        out_shape=jax.ShapeDtypeStruct((p, PACK_W), PACK_DTYPE),
        compiler_params=_params(1),
        name="ffn",
    )(tile_expert, tile_flag, tile_rows, xs, w1, b1, w2, b2)


def _combine_kernel(x_ref, g2_ref, w_ref, y_ref, o_ref):
    w = w_ref[...]
    acc_lo = None
    for k in range(TOP_K):
        lo, hi = _unpack_pairs(y_ref[k])
        wk = w[:, k:k + 1]
        acc_lo = wk * lo if acc_lo is None else acc_lo + wk * lo
        acc_hi = wk * hi if k == 0 else acc_hi + wk * hi
    acc = jnp.concatenate([acc_lo, acc_hi], axis=1)
    o_ref[...] = x_ref[...] + g2_ref[0] * acc


def _combine_into_kernel(x_ref, g2_ref, w_ref, y_ref, prev_ref, o_ref):
    del prev_ref
    _combine_kernel(x_ref, g2_ref, w_ref, y_ref, o_ref)


def _combine(x1, g2, w_tok, yk, seq, first_row, prev):
    t = x1.shape[0]
    rows = yk.shape[1]
    tm = _largest_tile(seq, COMBINE_ROWS, SUBLANES)
    per_b = seq // tm
    i0 = first_row // tm
    in_specs = [pl.BlockSpec((tm, D_MODEL), lambda i: (i0 + i, 0)),
                pl.BlockSpec((1, 1, D_MODEL), lambda i: ((i0 + i) // per_b, 0, 0)),
                pl.BlockSpec((tm, TOP_K), lambda i: (i0 + i, 0)),
                pl.BlockSpec((TOP_K, tm, PACK_W), lambda i: (0, i, 0))]
    args = [x1, g2, w_tok, yk]
    if prev is not None:
        in_specs.append(pl.BlockSpec(memory_space=pl.ANY))
        args.append(prev)
    return pl.pallas_call(
        _combine_kernel if prev is None else _combine_into_kernel,
        grid=(rows // tm,),
        in_specs=in_specs,
        out_specs=pl.BlockSpec((tm, D_MODEL), lambda i: (i0 + i, 0)),
        out_shape=jax.ShapeDtypeStruct((t, D_MODEL), F32),
        input_output_aliases={} if prev is None else {len(args) - 1: 0},
        compiler_params=_params(1),
        name="combine",
    )(*args)


def _count_kernel(ti_ref, cnt_ref):
    @pl.when(pl.program_id(0) == 0)
    def _():
        cnt_ref[...] = jnp.zeros(cnt_ref.shape, F32)

    bt = ti_ref.shape[1]
    eid = lax.broadcasted_iota(jnp.int32, (N_EXPERTS, bt), 0)
    acc = jnp.zeros(cnt_ref.shape, F32)
    for k in range(TOP_K):
        m = (eid == ti_ref[k:k + 1, :]).astype(F32)
        for c in range(bt // LANES):
            acc = acc + m[:, c * LANES:(c + 1) * LANES]
    cnt_ref[...] += acc


def _rank_kernel(ti_ref, off_ref, tri_ref, dest_ref, run_sc):
    @pl.when(pl.program_id(0) == 0)
    def _():
        run_sc[...] = off_ref[...] - 1.0

    bt = ti_ref.shape[1]
    eid = lax.broadcasted_iota(jnp.int32, (N_EXPERTS, bt), 0)
    run = run_sc[...]
    for k in range(TOP_K):
        m = eid == ti_ref[k:k + 1, :]
        pre = jnp.dot(jnp.where(m, 1.0, 0.0).astype(BF16), tri_ref[...], preferred_element_type=F32)
        slot = jnp.sum(jnp.where(m, pre + run, 0.0), axis=0, keepdims=True)
        dest_ref[k:k + 1, :] = slot.astype(jnp.int32)
        run = run + pre[:, bt - 1:bt]
    run_sc[...] = run


def _plan(top_i, tg):
    t = top_i.shape[1]
    n_assign = TOP_K * t
    p = n_assign + N_EXPERTS * tg
    n_tiles = p // tg
    bt = _largest_tile(t, PLAN_BLOCK, LANES)
    cnt = pl.pallas_call(
        _count_kernel,
        grid=(t // bt,),
        in_specs=[pl.BlockSpec((TOP_K, bt), lambda i: (0, i))],
        out_specs=pl.BlockSpec((N_EXPERTS, LANES), lambda i: (0, 0)),
        out_shape=jax.ShapeDtypeStruct((N_EXPERTS, LANES), F32),
        compiler_params=_params(1),
        name="count",
    )(top_i)
    counts = jnp.sum(cnt, axis=1).astype(jnp.int32)
    padded = ((counts + tg - 1) // tg) * tg
    off_end = jnp.cumsum(padded)
    off = off_end - padded
    tri = (jnp.arange(bt)[:, None] <= jnp.arange(bt)[None, :]).astype(BF16)
    dest = pl.pallas_call(
        _rank_kernel,
        grid=(t // bt,),
        in_specs=[pl.BlockSpec((TOP_K, bt), lambda i: (0, i)),
                  pl.BlockSpec((N_EXPERTS, 1), lambda i: (0, 0)),
                  pl.BlockSpec((bt, bt), lambda i: (0, 0))],
        out_specs=pl.BlockSpec((TOP_K, bt), lambda i: (0, i)),
        out_shape=jax.ShapeDtypeStruct((TOP_K, t), jnp.int32),
        scratch_shapes=[pltpu.VMEM((N_EXPERTS, 1), F32)],
        compiler_params=_params(1),
        name="rank",
    )(top_i, off.astype(F32).reshape(N_EXPERTS, 1), tri)
    tile_start = jnp.arange(n_tiles, dtype=jnp.int32) * tg
    tile_valid = tile_start < off_end[-1]
    te = jnp.sum((tile_start[:, None] >= off_end[None, :]).astype(jnp.int32), axis=1)
    last_e = jnp.sum(((off_end[-1] - 1) >= off_end).astype(jnp.int32))
    tile_expert = jnp.where(tile_valid, te, last_e)
    tile_rows = jnp.where(tile_valid, jnp.clip(counts[tile_expert] - (tile_start - off[tile_expert]), 0, tg), 0)
    changed = jnp.concatenate([jnp.ones((1,), jnp.bool_), tile_expert[1:] != tile_expert[:-1]])
    tile_flag = jnp.where(tile_valid, jnp.where(changed, TILE_FIRST, TILE_BODY), TILE_PAD).astype(jnp.int32)
    return dest.reshape(-1), tile_expert, tile_flag, tile_rows.astype(jnp.int32)


def kernel(x, c, ctx, c_ctx, norm1, norm2, w_ada, b_ada, w_in, ret_decay_f, ret_decay_b, attn_q_norm, attn_k_norm,
           w_ret_o, w_attn_o, w_out, w_router, b_router, w_exp_in, b_exp_in, w_exp_out, b_exp_out):
    assert w_in.shape[0] == 1, "single-layer block"
    b, seq, d = x.shape
    n_ctx = ctx.shape[1]
    t = b * seq
    rows = seq // GRID_W

    idx = np.cumsum(IN_SIZES)[:-1].tolist()
    wq_r, wk_r, wv_r, wg_r, wq_a, wk_a, wv_a, wgt_r, wgt_a = jnp.split(w_in[0], idx, axis=-1)
    w_in_p = jnp.concatenate([wq_r, wk_r, wv_r, wg_r, wq_a, wgt_r, wgt_a, wk_a, wv_a], axis=-1)
    w1 = w_exp_in[0]
    w2 = w_exp_out[0]
    b1 = b_exp_in[0].reshape(N_EXPERTS, 1, 2 * EXPERT_FF)
    b2 = b_exp_out[0].reshape(N_EXPERTS, 1, D_MODEL)

    pad = (-(b + 1)) % SUBLANES
    c_all = jnp.concatenate([c, c_ctx[None, :], jnp.zeros((pad, d), F32)], axis=0)
    mod = _ada(c_all, w_ada[0], b_ada[0])
    sh1, sc1, g1, sh2, sc2, g2 = [m.reshape(-1, 1, d) for m in jnp.split(mod, 6, axis=-1)]
    lat = lambda m: m[:b]
    cx = lambda m: m[b:b + 1]

    x2 = x.reshape(t, d)
    proj = _inproj(x2, norm1, lat(sc1), lat(sh1), w_in_p, seq)
    proj_c = _inproj(ctx.reshape(b * n_ctx, d), norm1, cx(sc1), cx(sh1), w_in_p, n_ctx)

    cos_r, sin_r = _rope_tables(rows, RET_QK_DIM)
    cos_a, sin_a = _rope_tables(rows, ATTN_HEAD_DIM)
    lg_f = -jax.nn.softplus(ret_decay_f[0].astype(F32))
    lg_b = -jax.nn.softplus(ret_decay_b[0].astype(F32))
    zero_state = jnp.zeros((2, b, RET_HEADS, RET_QK_DIM, RET_V_DIM), F32)
    cos_c = jnp.ones((n_ctx, RET_QK_DIM), F32)
    sin_c = jnp.zeros((n_ctx, RET_QK_DIM), F32)
    _, _, s_ctx = _retention(proj_c, cos_c, sin_c, lg_f, lg_b, zero_state, b, n_ctx, False)
    o_f, o_b, _ = _retention(proj, cos_r, sin_r, lg_f, lg_b, s_ctx, b, seq, True)

    k_all, vt_all = _kvprep(proj_c, proj, attn_k_norm, cos_a, sin_a, b, n_ctx, seq)
    attn_o = _attention(proj, attn_q_norm, cos_a, sin_a, k_all, vt_all, b, seq)

    x1 = _merge(x2, lat(g1), o_f, o_b, proj, attn_o, w_ret_o[0].astype(BF16), w_attn_o[0].astype(BF16),
                w_out[0].astype(BF16), seq)

    h2, top_i, top_w = _router(x1, norm2, lat(sc2), lat(sh2), w_router[0].T, b_router[0].reshape(N_EXPERTS, 1), seq)
    n_groups = MOE_GROUPS if b % MOE_GROUPS == 0 else 1
    tgrp = t // n_groups
    tg = _largest_tile(TOP_K * tgrp, FFN_ROWS, SUBLANES)
    w_tok = top_w.T
    out = None
    for grp in range(n_groups):
        first = grp * tgrp
        dest, tile_expert, tile_flag, tile_rows = _plan(top_i[:, first:first + tgrp], tg)
        xs = _scatter_rows(h2, dest, TOP_K * tgrp + N_EXPERTS * tg, first, tgrp)
        ys = _ffn(xs, tile_expert, tile_flag, tile_rows, w1, b1, w2, b2, tg)
        yk = _gather_rows(ys, dest).reshape(TOP_K, tgrp, PACK_W)
        out = _combine(x1, lat(g2), w_tok, yk, seq, first, out)
    return out.reshape(b, seq, d)
```

```python
import functools

import jax
import jax.numpy as jnp
import numpy as np
from jax import lax
from jax.experimental import pallas as pl
from jax.experimental.pallas import tpu as pltpu
from jax.experimental.pallas import tpu_sc as plsc

F32 = jnp.float32
BF16 = jnp.bfloat16

D_MODEL = 1024
GRID_W = 64
EPS = 1e-6
RET_HEADS = 4
RET_QK_DIM = 256
RET_V_DIM = 512
ATTN_HEADS = 8
ATTN_KV_HEADS = 2
ATTN_GROUP = ATTN_HEADS // ATTN_KV_HEADS
ATTN_HEAD_DIM = 128
ROPE_THETA = 10000.0
N_EXPERTS = 32
TOP_K = 4
EXPERT_FF = 1024
SWIGLU_LIMIT = 7.0
SWIGLU_ALPHA = 1.702

RET_QK_W = RET_HEADS * RET_QK_DIM
RET_V_W = RET_HEADS * RET_V_DIM
ATTN_Q_W = ATTN_HEADS * ATTN_HEAD_DIM
ATTN_KV_W = ATTN_KV_HEADS * ATTN_HEAD_DIM
IN_SIZES = (RET_QK_W, RET_QK_W, RET_V_W, RET_V_W, ATTN_Q_W, ATTN_KV_W, ATTN_KV_W, D_MODEL, D_MODEL)
IN_WIDTH = sum(IN_SIZES)
COL_QR = 0
COL_KR = COL_QR + RET_QK_W
COL_VR = COL_KR + RET_QK_W
COL_GR = COL_VR + RET_V_W
COL_QA = COL_GR + RET_V_W
COL_GATE_R = COL_QA + ATTN_Q_W
COL_GATE_A = COL_GATE_R + D_MODEL
COL_KA = COL_GATE_A + D_MODEL
COL_VA = COL_KA + ATTN_KV_W

RET_CHUNK = 256
RET_HEADS_PER_STEP = 4
LANES = 128
SUBLANES = 8
SC_CORES = 2
SC_SUBCORES = 16
SC_GATHER_ROWS = 128
MXU_DEPTH = 256
ADA_COLS = 1536
INPROJ_ROWS = 1024
INPROJ_COLS = 2560
MERGE_ROWS = 512
ROUTER_ROWS = 1024
COMBINE_ROWS = 512
FFN_ROWS = 512
ATTN_KEY_CHUNK = 768
ATTN_Q_TILE = 1024
ATTN_EXP_SLAB = 32
ONES_ROWS = 16
KV_PREP_ROWS = 256
PACK_DTYPE = jnp.int32
PACK_W = D_MODEL // 2
MOE_GROUPS = 2
PLAN_BLOCK = 512
TILE_PAD, TILE_BODY, TILE_FIRST = 0, 1, 2
VMEM_LIMIT = 56 * 1024 * 1024

ARB = pltpu.ARBITRARY


def _params(n_axes, **kw):
    return pltpu.CompilerParams(dimension_semantics=(ARB,) * n_axes, vmem_limit_bytes=VMEM_LIMIT, **kw)


def _largest_tile(n, cap, mult):
    best = None
    for t in range(mult, min(n, cap) + 1, mult):
        if n % t == 0:
            best = t
    assert best is not None, (n, cap, mult)
    return best


def _rms(x):
    return x * lax.rsqrt(jnp.mean(x * x, axis=-1, keepdims=True) + EPS)


def _pack_pairs(x):
    half = x.shape[1] // 2
    lo = lax.bitcast_convert_type(x[:, :half].astype(BF16).astype(F32), jnp.int32)
    hi = lax.bitcast_convert_type(x[:, half:].astype(BF16).astype(F32), jnp.int32)
    return lax.bitwise_or(lax.bitwise_and(hi, jnp.int32(-65536)), lax.shift_right_logical(lo, jnp.int32(16)))


def _unpack_pairs(w):
    lo = lax.bitcast_convert_type(lax.shift_left(w, jnp.int32(16)), F32)
    hi = lax.bitcast_convert_type(lax.bitwise_and(w, jnp.int32(-65536)), F32)
    return lo, hi


def _rope_block(x, cos, sin, half):
    if 2 * half == LANES:
        swapped = pltpu.roll(x, half, 1)
    else:
        lane = lax.broadcasted_iota(jnp.int32, x.shape, 1)
        first = (lane % (2 * half)) < half
        swapped = jnp.where(first, pltpu.roll(x, LANES - half, 1), pltpu.roll(x, half, 1))
    return x * cos + swapped * sin


def _rope_tables(rows, head_dim):
    n_freq = head_dim // 4
    inv_freq = ROPE_THETA ** (-jnp.arange(n_freq, dtype=F32) / n_freq)
    ang_r = jnp.arange(rows, dtype=F32)[:, None] * inv_freq
    ang_c = jnp.arange(GRID_W, dtype=F32)[:, None] * inv_freq
    per_row = lambda a: jnp.broadcast_to(a[:, None, :], (rows, GRID_W, n_freq)).reshape(rows * GRID_W, n_freq)
    per_col = lambda a: jnp.broadcast_to(a[None, :, :], (rows, GRID_W, n_freq)).reshape(rows * GRID_W, n_freq)
    cos_r, sin_r = per_row(jnp.cos(ang_r)), per_row(jnp.sin(ang_r))
    cos_c, sin_c = per_col(jnp.cos(ang_c)), per_col(jnp.sin(ang_c))
    cos = jnp.concatenate([cos_r, cos_r, cos_c, cos_c], axis=1)
    sin = jnp.concatenate([-sin_r, sin_r, -sin_c, sin_c], axis=1)
    return cos, sin


def _ada_kernel(c_ref, w_ref, b_ref, o_ref):
    c = c_ref[...]
    s = c * jax.nn.sigmoid(c)
    o_ref[...] = jnp.dot(s, w_ref[...], preferred_element_type=F32,
                         precision=lax.Precision.HIGHEST) + b_ref[...]


def _ada(c_pad, w_ada, b_ada):
    rows = c_pad.shape[0]
    n = w_ada.shape[1]
    tn = _largest_tile(n, ADA_COLS, LANES)
    return pl.pallas_call(
        _ada_kernel,
        grid=(n // tn,),
        in_specs=[pl.BlockSpec((rows, D_MODEL), lambda j: (0, 0)),
                  pl.BlockSpec((D_MODEL, tn), lambda j: (0, j)),
                  pl.BlockSpec((1, tn), lambda j: (0, j))],
        out_specs=pl.BlockSpec((rows, tn), lambda j: (0, j)),
        out_shape=jax.ShapeDtypeStruct((rows, n), F32),
        compiler_params=_params(1),
        name="ada",
    )(c_pad, w_ada, b_ada.reshape(1, n))


def _inproj_kernel(x_ref, n_ref, sc_ref, sh_ref, w_ref, o_ref, w_sc):
    @pl.when(pl.program_id(1) == 0)
    def _():
        w_sc[...] = w_ref[...].astype(BF16)

    y = _rms(x_ref[...]) * n_ref[...]
    h = (y * (1.0 + sc_ref[0]) + sh_ref[0]).astype(BF16)
    o_ref[...] = jnp.dot(h, w_sc[...], preferred_element_type=F32).astype(o_ref.dtype)


def _inproj(x2, norm, sc, sh, w, rows_per_batch):
    t = x2.shape[0]
    n = w.shape[1]
    tm = _largest_tile(rows_per_batch, INPROJ_ROWS, SUBLANES)
    tn = _largest_tile(n, INPROJ_COLS, LANES)
    per_b = rows_per_batch // tm
    if sc.shape[0] == 1:
        mod_map = lambda j, i: (0, 0, 0)
    else:
        mod_map = lambda j, i: (i // per_b, 0, 0)
    return pl.pallas_call(
        _inproj_kernel,
        grid=(n // tn, t // tm),
        in_specs=[pl.BlockSpec((tm, D_MODEL), lambda j, i: (i, 0)),
                  pl.BlockSpec((1, D_MODEL), lambda j, i: (0, 0)),
                  pl.BlockSpec((1, 1, D_MODEL), mod_map),
                  pl.BlockSpec((1, 1, D_MODEL), mod_map),
                  pl.BlockSpec((D_MODEL, tn), lambda j, i: (0, j))],
        out_specs=pl.BlockSpec((tm, tn), lambda j, i: (i, j)),
        out_shape=jax.ShapeDtypeStruct((t, n), BF16),
        scratch_shapes=[pltpu.VMEM((D_MODEL, tn), BF16)],
        compiler_params=_params(2),
        name="inproj",
    )(x2, norm, sc, sh, w)


def _kvprep_kernel(kc_ref, vc_ref, kl_ref, vl_ref, g_ref, cos_ref, sin_ref, k_ref, vt_ref, *, ctx_blocks):
    j = pl.program_id(1)
    vrows = ATTN_HEAD_DIM + ONES_ROWS

    def emit(kin_ref, vin_ref, use_rope):
        k = kin_ref[...].astype(F32)
        v = vin_ref[...].astype(F32)
        for g in range(ATTN_KV_HEADS):
            sl = slice(g * ATTN_HEAD_DIM, (g + 1) * ATTN_HEAD_DIM)
            kh = _rms(k[:, sl]) * g_ref[...]
            if use_rope:
                kh = _rope_block(kh, cos_ref[...], sin_ref[...], ATTN_HEAD_DIM // 4)
            k_ref[0, g] = kh.astype(BF16)
            vt_ref[0, g * vrows:g * vrows + ATTN_HEAD_DIM, :] = v[:, sl].T.astype(BF16)
            vt_ref[0, g * vrows + ATTN_HEAD_DIM:(g + 1) * vrows, :] = jnp.ones((ONES_ROWS, v.shape[0]), BF16)

    @pl.when(j < ctx_blocks)
    def _():
        emit(kc_ref, vc_ref, False)

    @pl.when(j >= ctx_blocks)
    def _():
        emit(kl_ref, vl_ref, True)


def _kvprep(proj_c, proj, gain, cos, sin, batch, n_ctx, seq):
    tm = KV_PREP_ROWS
    assert n_ctx % tm == 0 and seq % tm == 0
    cb, lb = n_ctx // tm, seq // tm
    lk = n_ctx + seq
    ctx_row = lambda b, j: b * cb + jnp.minimum(j, cb - 1)
    lat_blk = lambda j: jnp.maximum(j - cb, 0)
    lat_row = lambda b, j: b * lb + lat_blk(j)
    vrows = ATTN_KV_HEADS * (ATTN_HEAD_DIM + ONES_ROWS)
    return pl.pallas_call(
        functools.partial(_kvprep_kernel, ctx_blocks=cb),
        grid=(batch, cb + lb),
        in_specs=[pl.BlockSpec((tm, ATTN_KV_W), lambda b, j: (ctx_row(b, j), COL_KA // ATTN_KV_W)),
                  pl.BlockSpec((tm, ATTN_KV_W), lambda b, j: (ctx_row(b, j), COL_VA // ATTN_KV_W)),
                  pl.BlockSpec((tm, ATTN_KV_W), lambda b, j: (lat_row(b, j), COL_KA // ATTN_KV_W)),
                  pl.BlockSpec((tm, ATTN_KV_W), lambda b, j: (lat_row(b, j), COL_VA // ATTN_KV_W)),
                  pl.BlockSpec((1, ATTN_HEAD_DIM), lambda b, j: (0, 0)),
                  pl.BlockSpec((tm, ATTN_HEAD_DIM), lambda b, j: (lat_blk(j), 0)),
                  pl.BlockSpec((tm, ATTN_HEAD_DIM), lambda b, j: (lat_blk(j), 0))],
        out_specs=[pl.BlockSpec((1, ATTN_KV_HEADS, tm, ATTN_HEAD_DIM), lambda b, j: (b, 0, j, 0)),
                   pl.BlockSpec((1, vrows, tm), lambda b, j: (b, 0, j))],
        out_shape=[jax.ShapeDtypeStruct((batch, ATTN_KV_HEADS, lk, ATTN_HEAD_DIM), BF16),
                   jax.ShapeDtypeStruct((batch, vrows, lk), BF16)],
        compiler_params=_params(2),
        name="kvprep",
    )(proj_c, proj_c, proj, proj, gain, cos, sin)


def _ret_kernel(dch_ref, qf_ref, kf_ref, vf_ref, cosf_ref, sinf_ref, qb_ref, kb_ref, vb_ref, cosb_ref, sinb_ref,
                dmat_ref, din_ref, dout_ref, s0_ref, of_ref, ob_ref, sfin_ref, s_sc, *, use_rope, n_chunks, heads):
    hb = pl.program_id(1)
    c = pl.program_id(2)

    @pl.when(c == 0)
    def _():
        s_sc[...] = s0_ref[:, 0]

    def rope(x, cos, sin):
        return jnp.concatenate([_rope_block(x[:, i * LANES:(i + 1) * LANES], cos[:, i * LANES:(i + 1) * LANES],
                                            sin[:, i * LANES:(i + 1) * LANES], RET_QK_DIM // 4)
                                for i in range(RET_QK_DIM // LANES)], axis=1)

    sides = ((qf_ref, kf_ref, vf_ref, cosf_ref, sinf_ref), (qb_ref, kb_ref, vb_ref, cosb_ref, sinb_ref))
    chains = [(d, j) for d in range(2) for j in range(heads)]
    qs, ks, vs, ss = {}, {}, {}, {}
    for d, j in chains:
        q_ref, k_ref, v_ref, cos_ref, sin_ref = sides[d]
        q = q_ref[:, j * RET_QK_DIM:(j + 1) * RET_QK_DIM].astype(F32)
        k = k_ref[:, j * RET_QK_DIM:(j + 1) * RET_QK_DIM].astype(F32)
        if use_rope:
            q = rope(q, cos_ref[...], sin_ref[...])
            k = rope(k, cos_ref[...], sin_ref[...])
        qs[d, j], ks[d, j] = q, k
        vs[d, j] = v_ref[:, j * RET_V_DIM:(j + 1) * RET_V_DIM]
        ss[d, j] = s_sc[d, j]
    scores = {ch: lax.dot_general(qs[ch].astype(BF16), ks[ch].astype(BF16), (((1,), (1,)), ((), ())),
                                  preferred_element_type=F32) * dmat_ref[ch[0], ch[1]] for ch in chains}
    inter = {ch: jnp.dot((qs[ch] * din_ref[ch[0], ch[1]]).astype(BF16), ss[ch].astype(BF16),
                         preferred_element_type=F32) for ch in chains}
    intra = {ch: jnp.dot(scores[ch].astype(BF16), vs[ch], preferred_element_type=F32) for ch in chains}
    o_refs = (of_ref, ob_ref)
    for d, j in chains:
        o_refs[d][:, j * RET_V_DIM:(j + 1) * RET_V_DIM] = (intra[d, j] + inter[d, j]).astype(of_ref.dtype)
    s_new = {ch: ss[ch] * dch_ref[ch[0] * RET_HEADS + hb * heads + ch[1]]
             + jnp.dot((ks[ch] * dout_ref[ch[0], ch[1]]).T.astype(BF16), vs[ch], preferred_element_type=F32)
             for ch in chains}
    for d, j in chains:
        s_sc[d, j] = s_new[d, j]

    @pl.when(c == n_chunks - 1)
    def _():
        for d, j in chains:
            sfin_ref[d, 0, j] = s_new[d, j]


def _ret_tables(log_gamma, chunk, reverse, k_scale):
    pos = jnp.arange(chunk, dtype=F32)
    diff = pos[:, None] - pos[None, :]
    if reverse:
        diff = -diff
        mask = diff > 0
        p_in = chunk - pos
        p_out = pos
    else:
        mask = diff >= 0
        p_in = pos + 1.0
        p_out = chunk - 1.0 - pos
    lg = log_gamma.astype(F32)
    dmat = jnp.where(mask[None], jnp.exp(lg[:, None, None] * jnp.maximum(diff, 0.0)[None]), 0.0) * k_scale
    d_in = jnp.exp(lg[:, None] * p_in)
    d_out = jnp.exp(lg[:, None] * p_out) * k_scale
    d_in = jnp.broadcast_to(d_in[:, :, None], (RET_HEADS, chunk, RET_QK_DIM))
    d_out = jnp.broadcast_to(d_out[:, :, None], (RET_HEADS, chunk, RET_QK_DIM))
    d_chunk = jnp.exp(lg * chunk)
    return d_chunk, dmat, d_in, d_out


def _retention(proj, cos, sin, lg_f, lg_b, s0, batch, seq, use_rope):
    chunk = min(RET_CHUNK, seq)
    nc = seq // chunk
    hs = RET_HEADS_PER_STEP
    qw, vw = hs * RET_QK_DIM, hs * RET_V_DIM
    k_scale = RET_QK_DIM ** -0.5
    tabs = [_ret_tables(lg_f, chunk, False, k_scale), _ret_tables(lg_b, chunk, True, k_scale)]
    d_chunk, dmat, d_in, d_out = [jnp.stack([tabs[0][i], tabs[1][i]]) for i in range(4)]
    fwd = lambda c: c
    bwd = lambda c: nc - 1 - c

    def side(cidx):
        row = lambda b, c: b * nc + cidx(c)
        return [pl.BlockSpec((chunk, qw), lambda b, h, c: (row(b, c), COL_QR // qw + h)),
                pl.BlockSpec((chunk, qw), lambda b, h, c: (row(b, c), COL_KR // qw + h)),
                pl.BlockSpec((chunk, vw), lambda b, h, c: (row(b, c), COL_VR // vw + h)),
                pl.BlockSpec((chunk, RET_QK_DIM), lambda b, h, c: (cidx(c), 0)),
                pl.BlockSpec((chunk, RET_QK_DIM), lambda b, h, c: (cidx(c), 0))]

    state_spec = pl.BlockSpec((2, 1, hs, RET_QK_DIM, RET_V_DIM), lambda b, h, c: (0, b, h, 0, 0))
    return pl.pallas_call(
        functools.partial(_ret_kernel, use_rope=use_rope, n_chunks=nc, heads=hs),
        grid=(batch, RET_HEADS // hs, nc),
        in_specs=[pl.BlockSpec(memory_space=pltpu.SMEM)] + side(fwd) + side(bwd) + [
            pl.BlockSpec((2, hs, chunk, chunk), lambda b, h, c: (0, h, 0, 0)),
            pl.BlockSpec((2, hs, chunk, RET_QK_DIM), lambda b, h, c: (0, h, 0, 0)),
            pl.BlockSpec((2, hs, chunk, RET_QK_DIM), lambda b, h, c: (0, h, 0, 0)),
            state_spec],
        out_specs=[pl.BlockSpec((chunk, vw), lambda b, h, c: (b * nc + c, h)),
                   pl.BlockSpec((chunk, vw), lambda b, h, c: (b * nc + nc - 1 - c, h)),
                   state_spec],
        out_shape=[jax.ShapeDtypeStruct((batch * seq, RET_V_W), BF16),
                   jax.ShapeDtypeStruct((batch * seq, RET_V_W), BF16),
                   jax.ShapeDtypeStruct((2, batch, RET_HEADS, RET_QK_DIM, RET_V_DIM), F32)],
        scratch_shapes=[pltpu.VMEM((2, hs, RET_QK_DIM, RET_V_DIM), F32)],
        compiler_params=_params(3),
        name="ret",
    )(d_chunk.reshape(-1), proj, proj, proj, cos, sin, proj, proj, proj, cos, sin, dmat, d_in, d_out, s0)


def _attn_kernel(q_ref, g_ref, cos_ref, sin_ref, k_ref, vt_ref, o_ref, qt_sc, m_sc, acc_sc,
                 s0_sc, s1_sc, p0_sc, p1_sc, a0_sc, a1_sc, *, tq, kc, n_chunks):
    s_bufs = (s0_sc, s1_sc)
    p_bufs = (p0_sc, p1_sc)
    a_bufs = (a0_sc, a1_sc)

    q = q_ref[...].astype(F32)
    scale = ATTN_HEAD_DIM ** -0.5 * np.log2(np.e)
    for g in range(ATTN_GROUP):
        qh = _rms(q[:, g * ATTN_HEAD_DIM:(g + 1) * ATTN_HEAD_DIM]) * g_ref[...]
        qh = _rope_block(qh, cos_ref[...], sin_ref[...], ATTN_HEAD_DIM // 4) * scale
        qt_sc[:, g * tq:(g + 1) * tq] = qh.T.astype(BF16)
    m_sc[...] = jnp.full(m_sc.shape, -jnp.inf, F32)
    acc_sc[...] = jnp.zeros(acc_sc.shape, F32)

    def stage_s(c, slot):
        off = pl.multiple_of(c * kc, kc)
        s_bufs[slot][...] = jnp.dot(k_ref[0, 0, pl.ds(off, kc), :], qt_sc[...], preferred_element_type=F32)

    def stage_f(slot):
        m_prev = m_sc[...]
        m_new = jnp.maximum(m_prev, jnp.max(s_bufs[slot][...], axis=0, keepdims=True))
        m_sc[...] = m_new
        a_bufs[slot][...] = jnp.exp2(m_prev - m_new)
        for r in range(0, kc, ATTN_EXP_SLAB):
            p_bufs[slot][r:r + ATTN_EXP_SLAB, :] = jnp.exp2(
                s_bufs[slot][r:r + ATTN_EXP_SLAB, :] - m_new).astype(BF16)

    def stage_a(c, slot):
        off = pl.multiple_of(c * kc, kc)
        acc_sc[...] = a_bufs[slot][...] * acc_sc[...] + jnp.dot(
            vt_ref[0, :, pl.ds(off, kc)], p_bufs[slot][...], preferred_element_type=F32)

    def tick(t, parity):
        stage_s(t, parity)
        stage_f(1 - parity)
        stage_a(t - 2, parity)

    n = n_chunks
    stage_s(0, 0)
    if n > 1:
        stage_s(1, 1)
    stage_f(0)
    first = 2
    if n > 2 and (n - 2) % 2 == 1:
        tick(2, 0)
        first = 3
    n_pairs = (n - first) // 2 if n > first else 0
    if n_pairs > 0:
        def pair(u, carry):
            t = first + 2 * u
            tick(t, first % 2)
            tick(t + 1, 1 - first % 2)
            return carry
        lax.fori_loop(0, n_pairs, pair, 0)
    if n > 1:
        stage_f((n - 1) % 2)
        stage_a(n - 2, (n - 2) % 2)
    stage_a(n - 1, (n - 1) % 2)

    acc = acc_sc[...]
    o = acc[:ATTN_HEAD_DIM] / acc[ATTN_HEAD_DIM:ATTN_HEAD_DIM + 1]
    for g in range(ATTN_GROUP):
        o_ref[:, g * ATTN_HEAD_DIM:(g + 1) * ATTN_HEAD_DIM] = o[:, g * tq:(g + 1) * tq].T.astype(o_ref.dtype)


def _attention(proj, gain, cos, sin, k_all, vt_all, batch, seq):
    lk = k_all.shape[2]
    kc = _largest_tile(lk, ATTN_KEY_CHUNK, MXU_DEPTH)
    tq = _largest_tile(seq, ATTN_Q_TILE, SUBLANES)
    nq = seq // tq
    qw = ATTN_GROUP * ATTN_HEAD_DIM
    cols = ATTN_GROUP * tq
    vrows = ATTN_HEAD_DIM + ONES_ROWS
    return pl.pallas_call(
        functools.partial(_attn_kernel, tq=tq, kc=kc, n_chunks=lk // kc),
        grid=(batch, ATTN_KV_HEADS, nq),
        in_specs=[pl.BlockSpec((tq, qw), lambda b, g, i: (b * nq + i, COL_QA // qw + g)),
                  pl.BlockSpec((1, ATTN_HEAD_DIM), lambda b, g, i: (0, 0)),
                  pl.BlockSpec((tq, ATTN_HEAD_DIM), lambda b, g, i: (i, 0)),
                  pl.BlockSpec((tq, ATTN_HEAD_DIM), lambda b, g, i: (i, 0)),
                  pl.BlockSpec((1, 1, lk, ATTN_HEAD_DIM), lambda b, g, i: (b, g, 0, 0), pipeline_mode=pl.Buffered(1)),
                  pl.BlockSpec((1, vrows, lk), lambda b, g, i: (b, g, 0), pipeline_mode=pl.Buffered(1))],
        out_specs=pl.BlockSpec((tq, qw), lambda b, g, i: (b * nq + i, g)),
        out_shape=jax.ShapeDtypeStruct((batch * seq, ATTN_Q_W), BF16),
        scratch_shapes=[pltpu.VMEM((ATTN_HEAD_DIM, cols), BF16),
                        pltpu.VMEM((1, cols), F32),
                        pltpu.VMEM((vrows, cols), F32),
                        pltpu.VMEM((kc, cols), F32),
                        pltpu.VMEM((kc, cols), F32),
                        pltpu.VMEM((kc, cols), BF16),
                        pltpu.VMEM((kc, cols), BF16),
                        pltpu.VMEM((1, cols), F32),
                        pltpu.VMEM((1, cols), F32)],
        compiler_params=_params(3),
        name="attn",
    )(proj, gain, cos, sin, k_all, vt_all)


def _merge_kernel(x_ref, g1_ref, of_ref, ob_ref, gr_ref, ao_ref, gtr_ref, gta_ref,
                  wro_ref, wao_ref, wo_ref, o_ref):
    ro = of_ref[...].astype(F32) + ob_ref[...].astype(F32)
    gr = gr_ref[...].astype(F32)
    parts = []
    for h in range(RET_HEADS):
        sl = slice(h * RET_V_DIM, (h + 1) * RET_V_DIM)
        g = gr[:, sl]
        parts.append((g * jax.nn.sigmoid(g) * _rms(ro[:, sl])).astype(BF16))
    ret_in = jnp.concatenate(parts, axis=1)
    ret_branch = jnp.dot(ret_in, wro_ref[...], preferred_element_type=F32)
    attn_branch = jnp.dot(ao_ref[...], wao_ref[...], preferred_element_type=F32)
    y = (jax.nn.sigmoid(gtr_ref[...].astype(F32)) * ret_branch
         + jax.nn.sigmoid(gta_ref[...].astype(F32)) * attn_branch)
    y = jnp.dot(y.astype(BF16), wo_ref[...], preferred_element_type=F32)
    o_ref[...] = x_ref[...] + g1_ref[0] * y


def _merge(x2, g1, o_f, o_b, proj, attn_o, w_ret_o, w_attn_o, w_out, seq):
    t = x2.shape[0]
    tm = _largest_tile(seq, MERGE_ROWS, SUBLANES)
    per_b = seq // tm
    full = lambda i: (0, 0)
    return pl.pallas_call(
        _merge_kernel,
        grid=(t // tm,),
        in_specs=[pl.BlockSpec((tm, D_MODEL), lambda i: (i, 0)),
                  pl.BlockSpec((1, 1, D_MODEL), lambda i: (i // per_b, 0, 0)),
                  pl.BlockSpec((tm, RET_V_W), lambda i: (i, 0)),
                  pl.BlockSpec((tm, RET_V_W), lambda i: (i, 0)),
                  pl.BlockSpec((tm, RET_V_W), lambda i: (i, COL_GR // RET_V_W)),
                  pl.BlockSpec((tm, ATTN_Q_W), lambda i: (i, 0)),
                  pl.BlockSpec((tm, D_MODEL), lambda i: (i, COL_GATE_R // D_MODEL)),
                  pl.BlockSpec((tm, D_MODEL), lambda i: (i, COL_GATE_A // D_MODEL)),
                  pl.BlockSpec((RET_V_W, D_MODEL), full, pipeline_mode=pl.Buffered(1)),
                  pl.BlockSpec((ATTN_Q_W, D_MODEL), full, pipeline_mode=pl.Buffered(1)),
                  pl.BlockSpec((D_MODEL, D_MODEL), full, pipeline_mode=pl.Buffered(1))],
        out_specs=pl.BlockSpec((tm, D_MODEL), lambda i: (i, 0)),
        out_shape=jax.ShapeDtypeStruct((t, D_MODEL), F32),
        compiler_params=_params(1),
        name="merge",
    )(x2, g1, o_f, o_b, proj, attn_o, proj, proj, w_ret_o, w_attn_o, w_out)


def _router_kernel(x_ref, n_ref, sc_ref, sh_ref, wr_ref, br_ref, h_ref, ti_ref, tw_ref):
    h = _rms(x_ref[...]) * n_ref[...] * (1.0 + sc_ref[0]) + sh_ref[0]
    h_ref[...] = _pack_pairs(h)
    nt = (((1,), (1,)), ((), ()))
    w = wr_ref[...]
    h_hi = h.astype(BF16)
    h_lo = (h - h_hi.astype(F32)).astype(BF16)
    w_hi = w.astype(BF16)
    w_lo = (w - w_hi.astype(F32)).astype(BF16)
    logits = (lax.dot_general(w_hi, h_hi, nt, preferred_element_type=F32)
              + lax.dot_general(w_lo, h_hi, nt, preferred_element_type=F32)
              + lax.dot_general(w_hi, h_lo, nt, preferred_element_type=F32)) + br_ref[...]
    eid = lax.broadcasted_iota(jnp.int32, logits.shape, 0)
    vals = logits
    top_v = []
    top_i = []
    for _ in range(TOP_K):
        m = jnp.max(vals, axis=0, keepdims=True)
        idx = jnp.min(jnp.where(vals == m, eid, N_EXPERTS), axis=0, keepdims=True)
        top_v.append(m)
        top_i.append(idx)
        vals = jnp.where(eid == idx, -jnp.inf, vals)
    ex = [jnp.exp(v - top_v[0]) for v in top_v]
    denom = ex[0] + ex[1] + ex[2] + ex[3]
    ti_ref[...] = jnp.concatenate(top_i, axis=0)
    tw_ref[...] = jnp.concatenate([e / denom for e in ex], axis=0)


def _router(x1, norm, sc, sh, w_router_t, b_router, seq):
    t = x1.shape[0]
    tm = _largest_tile(seq, ROUTER_ROWS, LANES)
    per_b = seq // tm
    mod_map = lambda i: (i // per_b, 0, 0)
    return pl.pallas_call(
        _router_kernel,
        grid=(t // tm,),
        in_specs=[pl.BlockSpec((tm, D_MODEL), lambda i: (i, 0)),
                  pl.BlockSpec((1, D_MODEL), lambda i: (0, 0)),
                  pl.BlockSpec((1, 1, D_MODEL), mod_map),
                  pl.BlockSpec((1, 1, D_MODEL), mod_map),
                  pl.BlockSpec((N_EXPERTS, D_MODEL), lambda i: (0, 0)),
                  pl.BlockSpec((N_EXPERTS, 1), lambda i: (0, 0))],
        out_specs=[pl.BlockSpec((tm, PACK_W), lambda i: (i, 0)),
                   pl.BlockSpec((TOP_K, tm), lambda i: (0, i)),
                   pl.BlockSpec((TOP_K, tm), lambda i: (0, i))],
        out_shape=[jax.ShapeDtypeStruct((t, PACK_W), PACK_DTYPE),
                   jax.ShapeDtypeStruct((TOP_K, t), jnp.int32),
                   jax.ShapeDtypeStruct((TOP_K, t), F32)],
        compiler_params=_params(1),
        name="router",
    )(x1, norm, sc, sh, w_router_t, b_router)


def _gather_rows(src, idx):
    n, w = src.shape
    r = idx.shape[0]
    workers = SC_CORES * SC_SUBCORES
    per_w = r // workers
    n_win = per_w // SC_GATHER_ROWS
    assert per_w * workers == r and n_win * SC_GATHER_ROWS == per_w, (r, workers, SC_GATHER_ROWS)
    mesh = plsc.VectorSubcoreMesh(core_axis_name="c", subcore_axis_name="s")

    @functools.partial(
        pl.kernel, mesh=mesh, out_type=jax.ShapeDtypeStruct((r, w), src.dtype),
        scratch_types=[pltpu.VMEM((SC_GATHER_ROWS,), jnp.int32),
                       pltpu.VMEM((SC_GATHER_ROWS, w), src.dtype),
                       pltpu.SemaphoreType.DMA])
    def gather(src_hbm, idx_hbm, out_hbm, idx_v, rows_v, sem):
        wid = lax.axis_index("s") * SC_CORES + lax.axis_index("c")
        base = wid * per_w

        @pl.loop(0, n_win)
        def _(win):
            off = base + win * SC_GATHER_ROWS
            pltpu.sync_copy(idx_hbm.at[pl.ds(off, SC_GATHER_ROWS)], idx_v)
            pltpu.async_copy(src_hbm.at[idx_v], rows_v, sem).wait()
            pltpu.sync_copy(rows_v, out_hbm.at[pl.ds(off, SC_GATHER_ROWS)])

    return gather(src, idx)


def _scatter_rows(src, dest, n_out, first, group_rows):
    n, w = src.shape
    n_assign = dest.shape[0]
    workers = SC_CORES * SC_SUBCORES
    per_w = n_assign // workers
    n_win = per_w // SC_GATHER_ROWS
    assert per_w * workers == n_assign and n_win * SC_GATHER_ROWS == per_w and group_rows % per_w == 0
    mesh = plsc.VectorSubcoreMesh(core_axis_name="c", subcore_axis_name="s")

    @functools.partial(
        pl.kernel, mesh=mesh, out_type=jax.ShapeDtypeStruct((n_out, w), src.dtype),
        scratch_types=[pltpu.VMEM((SC_GATHER_ROWS,), jnp.int32),
                       pltpu.VMEM((SC_GATHER_ROWS, w), src.dtype),
                       pltpu.SemaphoreType.DMA])
    def scatter(src_hbm, dest_hbm, out_hbm, idx_v, rows_v, sem):
        wid = lax.axis_index("s") * SC_CORES + lax.axis_index("c")
        base = wid * per_w
        row_base = first + lax.rem(base, group_rows)

        @pl.loop(0, n_win)
        def _(win):
            pltpu.sync_copy(dest_hbm.at[pl.ds(base + win * SC_GATHER_ROWS, SC_GATHER_ROWS)], idx_v)
            pltpu.sync_copy(src_hbm.at[pl.ds(row_base + win * SC_GATHER_ROWS, SC_GATHER_ROWS)], rows_v)
            pltpu.async_copy(rows_v, out_hbm.at[idx_v], sem).wait()

    return scatter(src, dest)


def _ffn_kernel(te_ref, tf_ref, tr_ref, x_ref, w1_ref, b1_ref, w2_ref, b2_ref, o_ref, w1_sc, w2_sc):
    flag = tf_ref[pl.program_id(0)]
    n_rows = tr_ref[pl.program_id(0)]

    @pl.when(flag == TILE_FIRST)
    def _():
        w1_sc[...] = w1_ref[0].astype(BF16)
        w2_sc[...] = w2_ref[0].astype(BF16)

    @pl.when(flag == TILE_PAD)
    def _():
        o_ref[...] = jnp.zeros(o_ref.shape, o_ref.dtype)

    @pl.when(flag != TILE_PAD)
    def _():
        xw = x_ref[...]
        rid = lax.broadcasted_iota(jnp.int32, xw.shape, 0)
        lo, hi = _unpack_pairs(jnp.where(rid < n_rows, xw, jnp.zeros_like(xw)))
        x = jnp.concatenate([lo, hi], axis=1).astype(BF16)
        a = jnp.dot(x, w1_sc[...], preferred_element_type=F32) + b1_ref[0]
        gate = jnp.minimum(a[:, :EXPERT_FF], SWIGLU_LIMIT)
        up = jnp.clip(a[:, EXPERT_FF:], -SWIGLU_LIMIT, SWIGLU_LIMIT)
        act = gate * jax.nn.sigmoid(SWIGLU_ALPHA * gate) * (up + 1.0)
        y = jnp.dot(act.astype(BF16), w2_sc[...], preferred_element_type=F32) + b2_ref[0]
        o_ref[...] = _pack_pairs(y)


def _ffn(xs, tile_expert, tile_flag, tile_rows, w1, b1, w2, b2, tg):
    p = xs.shape[0]
    grid_spec = pltpu.PrefetchScalarGridSpec(
        num_scalar_prefetch=3,
        grid=(p // tg,),
        in_specs=[pl.BlockSpec((tg, PACK_W), lambda j, te, tf, tr: (j, 0)),
                  pl.BlockSpec((1, D_MODEL, 2 * EXPERT_FF), lambda j, te, tf, tr: (te[j], 0, 0)),
                  pl.BlockSpec((1, 1, 2 * EXPERT_FF), lambda j, te, tf, tr: (te[j], 0, 0)),
                  pl.BlockSpec((1, EXPERT_FF, D_MODEL), lambda j, te, tf, tr: (te[j], 0, 0)),
                  pl.BlockSpec((1, 1, D_MODEL), lambda j, te, tf, tr: (te[j], 0, 0))],
        out_specs=pl.BlockSpec((tg, PACK_W), lambda j, te, tf, tr: (j, 0)),
        scratch_shapes=[pltpu.VMEM((D_MODEL, 2 * EXPERT_FF), BF16),
                        pltpu.VMEM((EXPERT_FF, D_MODEL), BF16)],
    )
    return pl.pallas_call(
        _ffn_kernel,
        grid_spec=grid_spec,
        out_shape=jax.ShapeDtypeStruct((p, PACK_W), PACK_DTYPE),
        compiler_params=_params(1),
        name="ffn",
    )(tile_expert, tile_flag, tile_rows, xs, w1, b1, w2, b2)


def _combine_kernel(x_ref, g2_ref, w_ref, y_ref, o_ref):
    w = w_ref[...]
    acc_lo = None
    for k in range(TOP_K):
        lo, hi = _unpack_pairs(y_ref[k])
        wk = w[:, k:k + 1]
        acc_lo = wk * lo if acc_lo is None else acc_lo + wk * lo
        acc_hi = wk * hi if k == 0 else acc_hi + wk * hi
    acc = jnp.concatenate([acc_lo, acc_hi], axis=1)
    o_ref[...] = x_ref[...] + g2_ref[0] * acc


def _combine_into_kernel(x_ref, g2_ref, w_ref, y_ref, prev_ref, o_ref):
    del prev_ref
    _combine_kernel(x_ref, g2_ref, w_ref, y_ref, o_ref)


def _combine(x1, g2, w_tok, yk, seq, first_row, prev):
    t = x1.shape[0]
    rows = yk.shape[1]
    tm = _largest_tile(seq, COMBINE_ROWS, SUBLANES)
    per_b = seq // tm
    i0 = first_row // tm
    in_specs = [pl.BlockSpec((tm, D_MODEL), lambda i: (i0 + i, 0)),
                pl.BlockSpec((1, 1, D_MODEL), lambda i: ((i0 + i) // per_b, 0, 0)),
                pl.BlockSpec((tm, TOP_K), lambda i: (i0 + i, 0)),
                pl.BlockSpec((TOP_K, tm, PACK_W), lambda i: (0, i, 0))]
    args = [x1, g2, w_tok, yk]
    if prev is not None:
        in_specs.append(pl.BlockSpec(memory_space=pl.ANY))
        args.append(prev)
    return pl.pallas_call(
        _combine_kernel if prev is None else _combine_into_kernel,
        grid=(rows // tm,),
        in_specs=in_specs,
        out_specs=pl.BlockSpec((tm, D_MODEL), lambda i: (i0 + i, 0)),
        out_shape=jax.ShapeDtypeStruct((t, D_MODEL), F32),
        input_output_aliases={} if prev is None else {len(args) - 1: 0},
        compiler_params=_params(1),
        name="combine",
    )(*args)


def _count_kernel(ti_ref, cnt_ref):
    @pl.when(pl.program_id(0) == 0)
    def _():
        cnt_ref[...] = jnp.zeros(cnt_ref.shape, F32)

    bt = ti_ref.shape[1]
    eid = lax.broadcasted_iota(jnp.int32, (N_EXPERTS, bt), 0)
    acc = jnp.zeros(cnt_ref.shape, F32)
    for k in range(TOP_K):
        m = (eid == ti_ref[k:k + 1, :]).astype(F32)
        for c in range(bt // LANES):
            acc = acc + m[:, c * LANES:(c + 1) * LANES]
    cnt_ref[...] += acc


def _rank_kernel(ti_ref, off_ref, tri_ref, dest_ref, run_sc):
    @pl.when(pl.program_id(0) == 0)
    def _():
        run_sc[...] = off_ref[...] - 1.0

    bt = ti_ref.shape[1]
    eid = lax.broadcasted_iota(jnp.int32, (N_EXPERTS, bt), 0)
    run = run_sc[...]
    for k in range(TOP_K):
        m = eid == ti_ref[k:k + 1, :]
        pre = jnp.dot(jnp.where(m, 1.0, 0.0).astype(BF16), tri_ref[...], preferred_element_type=F32)
        slot = jnp.sum(jnp.where(m, pre + run, 0.0), axis=0, keepdims=True)
        dest_ref[k:k + 1, :] = slot.astype(jnp.int32)
        run = run + pre[:, bt - 1:bt]
    run_sc[...] = run


def _plan(top_i, tg):
    t = top_i.shape[1]
    n_assign = TOP_K * t
    p = n_assign + N_EXPERTS * tg
    n_tiles = p // tg
    bt = _largest_tile(t, PLAN_BLOCK, LANES)
    cnt = pl.pallas_call(
        _count_kernel,
        grid=(t // bt,),
        in_specs=[pl.BlockSpec((TOP_K, bt), lambda i: (0, i))],
        out_specs=pl.BlockSpec((N_EXPERTS, LANES), lambda i: (0, 0)),
        out_shape=jax.ShapeDtypeStruct((N_EXPERTS, LANES), F32),
        compiler_params=_params(1),
        name="count",
    )(top_i)
    counts = jnp.sum(cnt, axis=1).astype(jnp.int32)
    padded = ((counts + tg - 1) // tg) * tg
    off_end = jnp.cumsum(padded)
    off = off_end - padded
    tri = (jnp.arange(bt)[:, None] <= jnp.arange(bt)[None, :]).astype(BF16)
    dest = pl.pallas_call(
        _rank_kernel,
        grid=(t // bt,),
        in_specs=[pl.BlockSpec((TOP_K, bt), lambda i: (0, i)),
                  pl.BlockSpec((N_EXPERTS, 1), lambda i: (0, 0)),
                  pl.BlockSpec((bt, bt), lambda i: (0, 0))],
        out_specs=pl.BlockSpec((TOP_K, bt), lambda i: (0, i)),
        out_shape=jax.ShapeDtypeStruct((TOP_K, t), jnp.int32),
        scratch_shapes=[pltpu.VMEM((N_EXPERTS, 1), F32)],
        compiler_params=_params(1),
        name="rank",
    )(top_i, off.astype(F32).reshape(N_EXPERTS, 1), tri)
    tile_start = jnp.arange(n_tiles, dtype=jnp.int32) * tg
    tile_valid = tile_start < off_end[-1]
    te = jnp.sum((tile_start[:, None] >= off_end[None, :]).astype(jnp.int32), axis=1)
    last_e = jnp.sum(((off_end[-1] - 1) >= off_end).astype(jnp.int32))
    tile_expert = jnp.where(tile_valid, te, last_e)
    tile_rows = jnp.where(tile_valid, jnp.clip(counts[tile_expert] - (tile_start - off[tile_expert]), 0, tg), 0)
    changed = jnp.concatenate([jnp.ones((1,), jnp.bool_), tile_expert[1:] != tile_expert[:-1]])
    tile_flag = jnp.where(tile_valid, jnp.where(changed, TILE_FIRST, TILE_BODY), TILE_PAD).astype(jnp.int32)
    return dest.reshape(-1), tile_expert, tile_flag, tile_rows.astype(jnp.int32)


def kernel(x, c, ctx, c_ctx, norm1, norm2, w_ada, b_ada, w_in, ret_decay_f, ret_decay_b, attn_q_norm, attn_k_norm,
           w_ret_o, w_attn_o, w_out, w_router, b_router, w_exp_in, b_exp_in, w_exp_out, b_exp_out):
    assert w_in.shape[0] == 1, "single-layer block"
    b, seq, d = x.shape
    n_ctx = ctx.shape[1]
    t = b * seq
    rows = seq // GRID_W

    idx = np.cumsum(IN_SIZES)[:-1].tolist()
    wq_r, wk_r, wv_r, wg_r, wq_a, wk_a, wv_a, wgt_r, wgt_a = jnp.split(w_in[0], idx, axis=-1)
    w_in_p = jnp.concatenate([wq_r, wk_r, wv_r, wg_r, wq_a, wgt_r, wgt_a, wk_a, wv_a], axis=-1)
    w1 = w_exp_in[0]
    w2 = w_exp_out[0]
    b1 = b_exp_in[0].reshape(N_EXPERTS, 1, 2 * EXPERT_FF)
    b2 = b_exp_out[0].reshape(N_EXPERTS, 1, D_MODEL)

    pad = (-(b + 1)) % SUBLANES
    c_all = jnp.concatenate([c, c_ctx[None, :], jnp.zeros((pad, d), F32)], axis=0)
    mod = _ada(c_all, w_ada[0], b_ada[0])
    sh1, sc1, g1, sh2, sc2, g2 = [m.reshape(-1, 1, d) for m in jnp.split(mod, 6, axis=-1)]
    lat = lambda m: m[:b]
    cx = lambda m: m[b:b + 1]

    x2 = x.reshape(t, d)
    proj = _inproj(x2, norm1, lat(sc1), lat(sh1), w_in_p, seq)
    proj_c = _inproj(ctx.reshape(b * n_ctx, d), norm1, cx(sc1), cx(sh1), w_in_p, n_ctx)

    cos_r, sin_r = _rope_tables(rows, RET_QK_DIM)
    cos_a, sin_a = _rope_tables(rows, ATTN_HEAD_DIM)
    lg_f = -jax.nn.softplus(ret_decay_f[0].astype(F32))
    lg_b = -jax.nn.softplus(ret_decay_b[0].astype(F32))
    zero_state = jnp.zeros((2, b, RET_HEADS, RET_QK_DIM, RET_V_DIM), F32)
    cos_c = jnp.ones((n_ctx, RET_QK_DIM), F32)
    sin_c = jnp.zeros((n_ctx, RET_QK_DIM), F32)
    _, _, s_ctx = _retention(proj_c, cos_c, sin_c, lg_f, lg_b, zero_state, b, n_ctx, False)
    o_f, o_b, _ = _retention(proj, cos_r, sin_r, lg_f, lg_b, s_ctx, b, seq, True)

    k_all, vt_all = _kvprep(proj_c, proj, attn_k_norm, cos_a, sin_a, b, n_ctx, seq)
    attn_o = _attention(proj, attn_q_norm, cos_a, sin_a, k_all, vt_all, b, seq)

    x1 = _merge(x2, lat(g1), o_f, o_b, proj, attn_o, w_ret_o[0].astype(BF16), w_attn_o[0].astype(BF16),
                w_out[0].astype(BF16), seq)

    h2, top_i, top_w = _router(x1, norm2, lat(sc2), lat(sh2), w_router[0].T, b_router[0].reshape(N_EXPERTS, 1), seq)
    n_groups = MOE_GROUPS if b % MOE_GROUPS == 0 else 1
    tgrp = t // n_groups
    tg = _largest_tile(TOP_K * tgrp, FFN_ROWS, SUBLANES)
    w_tok = top_w.T
    out = None
    for grp in range(n_groups):
        first = grp * tgrp
        dest, tile_expert, tile_flag, tile_rows = _plan(top_i[:, first:first + tgrp], tg)
        xs = _scatter_rows(h2, dest, TOP_K * tgrp + N_EXPERTS * tg, first, tgrp)
        ys = _ffn(xs, tile_expert, tile_flag, tile_rows, w1, b1, w2, b2, tg)
        yk = _gather_rows(ys, dest).reshape(TOP_K, tgrp, PACK_W)
        out = _combine(x1, lat(g2), w_tok, yk, seq, first, out)
    return out.reshape(b, seq, d)
```

```python
import functools

import jax
import jax.numpy as jnp
import numpy as np
from jax import lax
from jax.experimental import pallas as pl
from jax.experimental.pallas import tpu as pltpu
from jax.experimental.pallas import tpu_sc as plsc

F32 = jnp.float32
BF16 = jnp.bfloat16

D_MODEL = 1024
GRID_W = 64
EPS = 1e-6
RET_HEADS = 4
RET_QK_DIM = 256
RET_V_DIM = 512
ATTN_HEADS = 8
ATTN_KV_HEADS = 2
ATTN_GROUP = ATTN_HEADS // ATTN_KV_HEADS
ATTN_HEAD_DIM = 128
ROPE_THETA = 10000.0
N_EXPERTS = 32
TOP_K = 4
EXPERT_FF = 1024
SWIGLU_LIMIT = 7.0
SWIGLU_ALPHA = 1.702

RET_QK_W = RET_HEADS * RET_QK_DIM
RET_V_W = RET_HEADS * RET_V_DIM
ATTN_Q_W = ATTN_HEADS * ATTN_HEAD_DIM
ATTN_KV_W = ATTN_KV_HEADS * ATTN_HEAD_DIM
IN_SIZES = (RET_QK_W, RET_QK_W, RET_V_W, RET_V_W, ATTN_Q_W, ATTN_KV_W, ATTN_KV_W, D_MODEL, D_MODEL)
IN_WIDTH = sum(IN_SIZES)
COL_QR = 0
COL_KR = COL_QR + RET_QK_W
COL_VR = COL_KR + RET_QK_W
COL_GR = COL_VR + RET_V_W
COL_QA = COL_GR + RET_V_W
COL_GATE_R = COL_QA + ATTN_Q_W
COL_GATE_A = COL_GATE_R + D_MODEL
COL_KA = COL_GATE_A + D_MODEL
COL_VA = COL_KA + ATTN_KV_W

RET_CHUNK = 256
RET_HEADS_PER_STEP = 4
LANES = 128
SUBLANES = 8
SC_CORES = 2
SC_SUBCORES = 16
SC_GATHER_ROWS = 128
MXU_DEPTH = 256
ADA_COLS = 1536
INPROJ_ROWS = 1024
INPROJ_COLS = 2560
MERGE_ROWS = 512
ROUTER_ROWS = 2048
COMBINE_ROWS = 1024
FFN_ROWS = 512
ATTN_KEY_CHUNK = 768
ATTN_Q_TILE = 1024
ATTN_EXP_SLAB = 32
ONES_ROWS = 16
KV_PREP_ROWS = 256
PACK_DTYPE = jnp.int32
PACK_W = D_MODEL // 2
MOE_GROUPS = 2
PLAN_BLOCK = 512
TILE_PAD, TILE_BODY, TILE_FIRST = 0, 1, 2
VMEM_LIMIT = 56 * 1024 * 1024

ARB = pltpu.ARBITRARY


def _params(n_axes, **kw):
    return pltpu.CompilerParams(dimension_semantics=(ARB,) * n_axes, vmem_limit_bytes=VMEM_LIMIT, **kw)


def _largest_tile(n, cap, mult):
    best = None
    for t in range(mult, min(n, cap) + 1, mult):
        if n % t == 0:
            best = t
    assert best is not None, (n, cap, mult)
    return best


def _rms(x):
    return x * lax.rsqrt(jnp.mean(x * x, axis=-1, keepdims=True) + EPS)


def _pack_pairs(x):
    half = x.shape[1] // 2
    lo = lax.bitcast_convert_type(x[:, :half].astype(BF16).astype(F32), jnp.int32)
    hi = lax.bitcast_convert_type(x[:, half:].astype(BF16).astype(F32), jnp.int32)
    return lax.bitwise_or(lax.bitwise_and(hi, jnp.int32(-65536)), lax.shift_right_logical(lo, jnp.int32(16)))


def _unpack_pairs(w):
    lo = lax.bitcast_convert_type(lax.shift_left(w, jnp.int32(16)), F32)
    hi = lax.bitcast_convert_type(lax.bitwise_and(w, jnp.int32(-65536)), F32)
    return lo, hi


def _rope_block(x, cos, sin, half):
    if 2 * half == LANES:
        swapped = pltpu.roll(x, half, 1)
    else:
        lane = lax.broadcasted_iota(jnp.int32, x.shape, 1)
        first = (lane % (2 * half)) < half
        swapped = jnp.where(first, pltpu.roll(x, LANES - half, 1), pltpu.roll(x, half, 1))
    return x * cos + swapped * sin


def _rope_tables(rows, head_dim):
    n_freq = head_dim // 4
    inv_freq = ROPE_THETA ** (-jnp.arange(n_freq, dtype=F32) / n_freq)
    ang_r = jnp.arange(rows, dtype=F32)[:, None] * inv_freq
    ang_c = jnp.arange(GRID_W, dtype=F32)[:, None] * inv_freq
    per_row = lambda a: jnp.broadcast_to(a[:, None, :], (rows, GRID_W, n_freq)).reshape(rows * GRID_W, n_freq)
    per_col = lambda a: jnp.broadcast_to(a[None, :, :], (rows, GRID_W, n_freq)).reshape(rows * GRID_W, n_freq)
    cos_r, sin_r = per_row(jnp.cos(ang_r)), per_row(jnp.sin(ang_r))
    cos_c, sin_c = per_col(jnp.cos(ang_c)), per_col(jnp.sin(ang_c))
    cos = jnp.concatenate([cos_r, cos_r, cos_c, cos_c], axis=1)
    sin = jnp.concatenate([-sin_r, sin_r, -sin_c, sin_c], axis=1)
    return cos, sin


def _ada_kernel(c_ref, w_ref, b_ref, o_ref):
    c = c_ref[...]
    s = c * jax.nn.sigmoid(c)
    o_ref[...] = jnp.dot(s, w_ref[...], preferred_element_type=F32,
                         precision=lax.Precision.HIGHEST) + b_ref[...]


def _ada(c_pad, w_ada, b_ada):
    rows = c_pad.shape[0]
    n = w_ada.shape[1]
    tn = _largest_tile(n, ADA_COLS, LANES)
    return pl.pallas_call(
        _ada_kernel,
        grid=(n // tn,),
        in_specs=[pl.BlockSpec((rows, D_MODEL), lambda j: (0, 0)),
                  pl.BlockSpec((D_MODEL, tn), lambda j: (0, j)),
                  pl.BlockSpec((1, tn), lambda j: (0, j))],
        out_specs=pl.BlockSpec((rows, tn), lambda j: (0, j)),
        out_shape=jax.ShapeDtypeStruct((rows, n), F32),
        compiler_params=_params(1),
        name="ada",
    )(c_pad, w_ada, b_ada.reshape(1, n))


def _inproj_kernel(x_ref, n_ref, sc_ref, sh_ref, w_ref, o_ref, w_sc):
    @pl.when(pl.program_id(1) == 0)
    def _():
        w_sc[...] = w_ref[...].astype(BF16)

    y = _rms(x_ref[...]) * n_ref[...]
    h = (y * (1.0 + sc_ref[0]) + sh_ref[0]).astype(BF16)
    o_ref[...] = jnp.dot(h, w_sc[...], preferred_element_type=F32).astype(o_ref.dtype)


def _inproj(x2, norm, sc, sh, w, rows_per_batch):
    t = x2.shape[0]
    n = w.shape[1]
    tm = _largest_tile(rows_per_batch, INPROJ_ROWS, SUBLANES)
    tn = _largest_tile(n, INPROJ_COLS, LANES)
    per_b = rows_per_batch // tm
    if sc.shape[0] == 1:
        mod_map = lambda j, i: (0, 0, 0)
    else:
        mod_map = lambda j, i: (i // per_b, 0, 0)
    return pl.pallas_call(
        _inproj_kernel,
        grid=(n // tn, t // tm),
        in_specs=[pl.BlockSpec((tm, D_MODEL), lambda j, i: (i, 0)),
                  pl.BlockSpec((1, D_MODEL), lambda j, i: (0, 0)),
                  pl.BlockSpec((1, 1, D_MODEL), mod_map),
                  pl.BlockSpec((1, 1, D_MODEL), mod_map),
                  pl.BlockSpec((D_MODEL, tn), lambda j, i: (0, j))],
        out_specs=pl.BlockSpec((tm, tn), lambda j, i: (i, j)),
        out_shape=jax.ShapeDtypeStruct((t, n), BF16),
        scratch_shapes=[pltpu.VMEM((D_MODEL, tn), BF16)],
        compiler_params=_params(2),
        name="inproj",
    )(x2, norm, sc, sh, w)


def _kvprep_kernel(kc_ref, vc_ref, kl_ref, vl_ref, g_ref, cos_ref, sin_ref, k_ref, vt_ref, *, ctx_blocks):
    j = pl.program_id(1)
    vrows = ATTN_HEAD_DIM + ONES_ROWS

    def emit(kin_ref, vin_ref, use_rope):
        k = kin_ref[...].astype(F32)
        v = vin_ref[...].astype(F32)
        for g in range(ATTN_KV_HEADS):
            sl = slice(g * ATTN_HEAD_DIM, (g + 1) * ATTN_HEAD_DIM)
            kh = _rms(k[:, sl]) * g_ref[...]
            if use_rope:
                kh = _rope_block(kh, cos_ref[...], sin_ref[...], ATTN_HEAD_DIM // 4)
            k_ref[0, g] = kh.astype(BF16)
            vt_ref[0, g * vrows:g * vrows + ATTN_HEAD_DIM, :] = v[:, sl].T.astype(BF16)
            vt_ref[0, g * vrows + ATTN_HEAD_DIM:(g + 1) * vrows, :] = jnp.ones((ONES_ROWS, v.shape[0]), BF16)

    @pl.when(j < ctx_blocks)
    def _():
        emit(kc_ref, vc_ref, False)

    @pl.when(j >= ctx_blocks)
    def _():
        emit(kl_ref, vl_ref, True)


def _kvprep(proj_c, proj, gain, cos, sin, batch, n_ctx, seq):
    tm = KV_PREP_ROWS
    assert n_ctx % tm == 0 and seq % tm == 0
    cb, lb = n_ctx // tm, seq // tm
    lk = n_ctx + seq
    ctx_row = lambda b, j: b * cb + jnp.minimum(j, cb - 1)
    lat_blk = lambda j: jnp.maximum(j - cb, 0)
    lat_row = lambda b, j: b * lb + lat_blk(j)
    vrows = ATTN_KV_HEADS * (ATTN_HEAD_DIM + ONES_ROWS)
    return pl.pallas_call(
        functools.partial(_kvprep_kernel, ctx_blocks=cb),
        grid=(batch, cb + lb),
        in_specs=[pl.BlockSpec((tm, ATTN_KV_W), lambda b, j: (ctx_row(b, j), COL_KA // ATTN_KV_W)),
                  pl.BlockSpec((tm, ATTN_KV_W), lambda b, j: (ctx_row(b, j), COL_VA // ATTN_KV_W)),
                  pl.BlockSpec((tm, ATTN_KV_W), lambda b, j: (lat_row(b, j), COL_KA // ATTN_KV_W)),
                  pl.BlockSpec((tm, ATTN_KV_W), lambda b, j: (lat_row(b, j), COL_VA // ATTN_KV_W)),
                  pl.BlockSpec((1, ATTN_HEAD_DIM), lambda b, j: (0, 0)),
                  pl.BlockSpec((tm, ATTN_HEAD_DIM), lambda b, j: (lat_blk(j), 0)),
                  pl.BlockSpec((tm, ATTN_HEAD_DIM), lambda b, j: (lat_blk(j), 0))],
        out_specs=[pl.BlockSpec((1, ATTN_KV_HEADS, tm, ATTN_HEAD_DIM), lambda b, j: (b, 0, j, 0)),
                   pl.BlockSpec((1, vrows, tm), lambda b, j: (b, 0, j))],
        out_shape=[jax.ShapeDtypeStruct((batch, ATTN_KV_HEADS, lk, ATTN_HEAD_DIM), BF16),
                   jax.ShapeDtypeStruct((batch, vrows, lk), BF16)],
        compiler_params=_params(2),
        name="kvprep",
    )(proj_c, proj_c, proj, proj, gain, cos, sin)


def _ret_kernel(dch_ref, qf_ref, kf_ref, vf_ref, cosf_ref, sinf_ref, qb_ref, kb_ref, vb_ref, cosb_ref, sinb_ref,
                dmat_ref, din_ref, dout_ref, s0_ref, of_ref, ob_ref, sfin_ref, s_sc, *, use_rope, n_chunks, heads):
    hb = pl.program_id(1)
    c = pl.program_id(2)

    @pl.when(c == 0)
    def _():
        s_sc[...] = s0_ref[:, 0]

    def rope(x, cos, sin):
        return jnp.concatenate([_rope_block(x[:, i * LANES:(i + 1) * LANES], cos[:, i * LANES:(i + 1) * LANES],
                                            sin[:, i * LANES:(i + 1) * LANES], RET_QK_DIM // 4)
                                for i in range(RET_QK_DIM // LANES)], axis=1)

    sides = ((qf_ref, kf_ref, vf_ref, cosf_ref, sinf_ref), (qb_ref, kb_ref, vb_ref, cosb_ref, sinb_ref))
    chains = [(d, j) for d in range(2) for j in range(heads)]
    qs, ks, vs, ss = {}, {}, {}, {}
    for d, j in chains:
        q_ref, k_ref, v_ref, cos_ref, sin_ref = sides[d]
        q = q_ref[:, j * RET_QK_DIM:(j + 1) * RET_QK_DIM].astype(F32)
        k = k_ref[:, j * RET_QK_DIM:(j + 1) * RET_QK_DIM].astype(F32)
        if use_rope:
            q = rope(q, cos_ref[...], sin_ref[...])
            k = rope(k, cos_ref[...], sin_ref[...])
        qs[d, j], ks[d, j] = q, k
        vs[d, j] = v_ref[:, j * RET_V_DIM:(j + 1) * RET_V_DIM]
        ss[d, j] = s_sc[d, j]
    scores = {ch: lax.dot_general(qs[ch].astype(BF16), ks[ch].astype(BF16), (((1,), (1,)), ((), ())),
                                  preferred_element_type=F32) * dmat_ref[ch[0], ch[1]] for ch in chains}
    inter = {ch: jnp.dot((qs[ch] * din_ref[ch[0], ch[1]]).astype(BF16), ss[ch].astype(BF16),
                         preferred_element_type=F32) for ch in chains}
    intra = {ch: jnp.dot(scores[ch].astype(BF16), vs[ch], preferred_element_type=F32) for ch in chains}
    o_refs = (of_ref, ob_ref)
    for d, j in chains:
        o_refs[d][:, j * RET_V_DIM:(j + 1) * RET_V_DIM] = (intra[d, j] + inter[d, j]).astype(of_ref.dtype)
    s_new = {ch: ss[ch] * dch_ref[ch[0] * RET_HEADS + hb * heads + ch[1]]
             + jnp.dot((ks[ch] * dout_ref[ch[0], ch[1]]).T.astype(BF16), vs[ch], preferred_element_type=F32)
             for ch in chains}
    for d, j in chains:
        s_sc[d, j] = s_new[d, j]

    @pl.when(c == n_chunks - 1)
    def _():
        for d, j in chains:
            sfin_ref[d, 0, j] = s_new[d, j]


def _ret_tables(log_gamma, chunk, reverse, k_scale):
    pos = jnp.arange(chunk, dtype=F32)
    diff = pos[:, None] - pos[None, :]
    if reverse:
        diff = -diff
        mask = diff > 0
        p_in = chunk - pos
        p_out = pos
    else:
        mask = diff >= 0
        p_in = pos + 1.0
        p_out = chunk - 1.0 - pos
    lg = log_gamma.astype(F32)
    dmat = jnp.where(mask[None], jnp.exp(lg[:, None, None] * jnp.maximum(diff, 0.0)[None]), 0.0) * k_scale
    d_in = jnp.exp(lg[:, None] * p_in)
    d_out = jnp.exp(lg[:, None] * p_out) * k_scale
    d_in = jnp.broadcast_to(d_in[:, :, None], (RET_HEADS, chunk, RET_QK_DIM))
    d_out = jnp.broadcast_to(d_out[:, :, None], (RET_HEADS, chunk, RET_QK_DIM))
    d_chunk = jnp.exp(lg * chunk)
    return d_chunk, dmat, d_in, d_out


def _retention(proj, cos, sin, lg_f, lg_b, s0, batch, seq, use_rope):
    chunk = min(RET_CHUNK, seq)
    nc = seq // chunk
    hs = RET_HEADS_PER_STEP
    qw, vw = hs * RET_QK_DIM, hs * RET_V_DIM
    k_scale = RET_QK_DIM ** -0.5
    tabs = [_ret_tables(lg_f, chunk, False, k_scale), _ret_tables(lg_b, chunk, True, k_scale)]
    d_chunk, dmat, d_in, d_out = [jnp.stack([tabs[0][i], tabs[1][i]]) for i in range(4)]
    fwd = lambda c: c
    bwd = lambda c: nc - 1 - c

    def side(cidx):
        row = lambda b, c: b * nc + cidx(c)
        return [pl.BlockSpec((chunk, qw), lambda b, h, c: (row(b, c), COL_QR // qw + h)),
                pl.BlockSpec((chunk, qw), lambda b, h, c: (row(b, c), COL_KR // qw + h)),
                pl.BlockSpec((chunk, vw), lambda b, h, c: (row(b, c), COL_VR // vw + h)),
                pl.BlockSpec((chunk, RET_QK_DIM), lambda b, h, c: (cidx(c), 0)),
                pl.BlockSpec((chunk, RET_QK_DIM), lambda b, h, c: (cidx(c), 0))]

    state_spec = pl.BlockSpec((2, 1, hs, RET_QK_DIM, RET_V_DIM), lambda b, h, c: (0, b, h, 0, 0))
    return pl.pallas_call(
        functools.partial(_ret_kernel, use_rope=use_rope, n_chunks=nc, heads=hs),
        grid=(batch, RET_HEADS // hs, nc),
        in_specs=[pl.BlockSpec(memory_space=pltpu.SMEM)] + side(fwd) + side(bwd) + [
            pl.BlockSpec((2, hs, chunk, chunk), lambda b, h, c: (0, h, 0, 0)),
            pl.BlockSpec((2, hs, chunk, RET_QK_DIM), lambda b, h, c: (0, h, 0, 0)),
            pl.BlockSpec((2, hs, chunk, RET_QK_DIM), lambda b, h, c: (0, h, 0, 0)),
            state_spec],
        out_specs=[pl.BlockSpec((chunk, vw), lambda b, h, c: (b * nc + c, h)),
                   pl.BlockSpec((chunk, vw), lambda b, h, c: (b * nc + nc - 1 - c, h)),
                   state_spec],
        out_shape=[jax.ShapeDtypeStruct((batch * seq, RET_V_W), BF16),
                   jax.ShapeDtypeStruct((batch * seq, RET_V_W), BF16),
                   jax.ShapeDtypeStruct((2, batch, RET_HEADS, RET_QK_DIM, RET_V_DIM), F32)],
        scratch_shapes=[pltpu.VMEM((2, hs, RET_QK_DIM, RET_V_DIM), F32)],
        compiler_params=_params(3),
        name="ret",
    )(d_chunk.reshape(-1), proj, proj, proj, cos, sin, proj, proj, proj, cos, sin, dmat, d_in, d_out, s0)


def _attn_kernel(q_ref, g_ref, cos_ref, sin_ref, k_ref, vt_ref, o_ref, qt_sc, m_sc, acc_sc,
                 s0_sc, s1_sc, p0_sc, p1_sc, a0_sc, a1_sc, *, tq, kc, n_chunks):
    s_bufs = (s0_sc, s1_sc)
    p_bufs = (p0_sc, p1_sc)
    a_bufs = (a0_sc, a1_sc)

    q = q_ref[...].astype(F32)
    scale = ATTN_HEAD_DIM ** -0.5 * np.log2(np.e)
    for g in range(ATTN_GROUP):
        qh = _rms(q[:, g * ATTN_HEAD_DIM:(g + 1) * ATTN_HEAD_DIM]) * g_ref[...]
        qh = _rope_block(qh, cos_ref[...], sin_ref[...], ATTN_HEAD_DIM // 4) * scale
        qt_sc[:, g * tq:(g + 1) * tq] = qh.T.astype(BF16)
    m_sc[...] = jnp.full(m_sc.shape, -jnp.inf, F32)
    acc_sc[...] = jnp.zeros(acc_sc.shape, F32)

    def stage_s(c, slot):
        off = pl.multiple_of(c * kc, kc)
        s_bufs[slot][...] = jnp.dot(k_ref[0, 0, pl.ds(off, kc), :], qt_sc[...], preferred_element_type=F32)

    def stage_f(slot):
        m_prev = m_sc[...]
        m_new = jnp.maximum(m_prev, jnp.max(s_bufs[slot][...], axis=0, keepdims=True))
        m_sc[...] = m_new
        a_bufs[slot][...] = jnp.exp2(m_prev - m_new)
        for r in range(0, kc, ATTN_EXP_SLAB):
            p_bufs[slot][r:r + ATTN_EXP_SLAB, :] = jnp.exp2(
                s_bufs[slot][r:r + ATTN_EXP_SLAB, :] - m_new).astype(BF16)

    def stage_a(c, slot):
        off = pl.multiple_of(c * kc, kc)
        acc_sc[...] = a_bufs[slot][...] * acc_sc[...] + jnp.dot(
            vt_ref[0, :, pl.ds(off, kc)], p_bufs[slot][...], preferred_element_type=F32)

    def tick(t, parity):
        stage_s(t, parity)
        stage_f(1 - parity)
        stage_a(t - 2, parity)

    n = n_chunks
    stage_s(0, 0)
    if n > 1:
        stage_s(1, 1)
    stage_f(0)
    first = 2
    if n > 2 and (n - 2) % 2 == 1:
        tick(2, 0)
        first = 3
    n_pairs = (n - first) // 2 if n > first else 0
    if n_pairs > 0:
        def pair(u, carry):
            t = first + 2 * u
            tick(t, first % 2)
            tick(t + 1, 1 - first % 2)
            return carry
        lax.fori_loop(0, n_pairs, pair, 0)
    if n > 1:
        stage_f((n - 1) % 2)
        stage_a(n - 2, (n - 2) % 2)
    stage_a(n - 1, (n - 1) % 2)

    acc = acc_sc[...]
    o = acc[:ATTN_HEAD_DIM] / acc[ATTN_HEAD_DIM:ATTN_HEAD_DIM + 1]
    for g in range(ATTN_GROUP):
        o_ref[:, g * ATTN_HEAD_DIM:(g + 1) * ATTN_HEAD_DIM] = o[:, g * tq:(g + 1) * tq].T.astype(o_ref.dtype)


def _attention(proj, gain, cos, sin, k_all, vt_all, batch, seq):
    lk = k_all.shape[2]
    kc = _largest_tile(lk, ATTN_KEY_CHUNK, MXU_DEPTH)
    tq = _largest_tile(seq, ATTN_Q_TILE, SUBLANES)
    nq = seq // tq
    qw = ATTN_GROUP * ATTN_HEAD_DIM
    cols = ATTN_GROUP * tq
    vrows = ATTN_HEAD_DIM + ONES_ROWS
    return pl.pallas_call(
        functools.partial(_attn_kernel, tq=tq, kc=kc, n_chunks=lk // kc),
        grid=(batch, ATTN_KV_HEADS, nq),
        in_specs=[pl.BlockSpec((tq, qw), lambda b, g, i: (b * nq + i, COL_QA // qw + g)),
                  pl.BlockSpec((1, ATTN_HEAD_DIM), lambda b, g, i: (0, 0)),
                  pl.BlockSpec((tq, ATTN_HEAD_DIM), lambda b, g, i: (i, 0)),
                  pl.BlockSpec((tq, ATTN_HEAD_DIM), lambda b, g, i: (i, 0)),
                  pl.BlockSpec((1, 1, lk, ATTN_HEAD_DIM), lambda b, g, i: (b, g, 0, 0), pipeline_mode=pl.Buffered(1)),
                  pl.BlockSpec((1, vrows, lk), lambda b, g, i: (b, g, 0), pipeline_mode=pl.Buffered(1))],
        out_specs=pl.BlockSpec((tq, qw), lambda b, g, i: (b * nq + i, g)),
        out_shape=jax.ShapeDtypeStruct((batch * seq, ATTN_Q_W), BF16),
        scratch_shapes=[pltpu.VMEM((ATTN_HEAD_DIM, cols), BF16),
                        pltpu.VMEM((1, cols), F32),
                        pltpu.VMEM((vrows, cols), F32),
                        pltpu.VMEM((kc, cols), F32),
                        pltpu.VMEM((kc, cols), F32),
                        pltpu.VMEM((kc, cols), BF16),
                        pltpu.VMEM((kc, cols), BF16),
                        pltpu.VMEM((1, cols), F32),
                        pltpu.VMEM((1, cols), F32)],
        compiler_params=_params(3),
        name="attn",
    )(proj, gain, cos, sin, k_all, vt_all)


def _merge_kernel(x_ref, g1_ref, of_ref, ob_ref, gr_ref, ao_ref, gtr_ref, gta_ref,
                  wro_ref, wao_ref, wo_ref, o_ref):
    ro = of_ref[...].astype(F32) + ob_ref[...].astype(F32)
    gr = gr_ref[...].astype(F32)
    parts = []
    for h in range(RET_HEADS):
        sl = slice(h * RET_V_DIM, (h + 1) * RET_V_DIM)
        g = gr[:, sl]
        parts.append((g * jax.nn.sigmoid(g) * _rms(ro[:, sl])).astype(BF16))
    ret_in = jnp.concatenate(parts, axis=1)
    ret_branch = jnp.dot(ret_in, wro_ref[...], preferred_element_type=F32)
    attn_branch = jnp.dot(ao_ref[...], wao_ref[...], preferred_element_type=F32)
    y = (jax.nn.sigmoid(gtr_ref[...].astype(F32)) * ret_branch
         + jax.nn.sigmoid(gta_ref[...].astype(F32)) * attn_branch)
    y = jnp.dot(y.astype(BF16), wo_ref[...], preferred_element_type=F32)
    o_ref[...] = x_ref[...] + g1_ref[0] * y


def _merge(x2, g1, o_f, o_b, proj, attn_o, w_ret_o, w_attn_o, w_out, seq):
    t = x2.shape[0]
    tm = _largest_tile(seq, MERGE_ROWS, SUBLANES)
    per_b = seq // tm
    full = lambda i: (0, 0)
    return pl.pallas_call(
        _merge_kernel,
        grid=(t // tm,),
        in_specs=[pl.BlockSpec((tm, D_MODEL), lambda i: (i, 0)),
                  pl.BlockSpec((1, 1, D_MODEL), lambda i: (i // per_b, 0, 0)),
                  pl.BlockSpec((tm, RET_V_W), lambda i: (i, 0)),
                  pl.BlockSpec((tm, RET_V_W), lambda i: (i, 0)),
                  pl.BlockSpec((tm, RET_V_W), lambda i: (i, COL_GR // RET_V_W)),
                  pl.BlockSpec((tm, ATTN_Q_W), lambda i: (i, 0)),
                  pl.BlockSpec((tm, D_MODEL), lambda i: (i, COL_GATE_R // D_MODEL)),
                  pl.BlockSpec((tm, D_MODEL), lambda i: (i, COL_GATE_A // D_MODEL)),
                  pl.BlockSpec((RET_V_W, D_MODEL), full, pipeline_mode=pl.Buffered(1)),
                  pl.BlockSpec((ATTN_Q_W, D_MODEL), full, pipeline_mode=pl.Buffered(1)),
                  pl.BlockSpec((D_MODEL, D_MODEL), full, pipeline_mode=pl.Buffered(1))],
        out_specs=pl.BlockSpec((tm, D_MODEL), lambda i: (i, 0)),
        out_shape=jax.ShapeDtypeStruct((t, D_MODEL), F32),
        compiler_params=_params(1),
        name="merge",
    )(x2, g1, o_f, o_b, proj, attn_o, proj, proj, w_ret_o, w_attn_o, w_out)


def _router_kernel(x_ref, n_ref, sc_ref, sh_ref, wr_ref, br_ref, h_ref, ti_ref, tw_ref):
    h = _rms(x_ref[...]) * n_ref[...] * (1.0 + sc_ref[0]) + sh_ref[0]
    h_ref[...] = _pack_pairs(h)
    nt = (((1,), (1,)), ((), ()))
    w = wr_ref[...]
    h_hi = h.astype(BF16)
    h_lo = (h - h_hi.astype(F32)).astype(BF16)
    w_hi = w.astype(BF16)
    w_lo = (w - w_hi.astype(F32)).astype(BF16)
    logits = (lax.dot_general(w_hi, h_hi, nt, preferred_element_type=F32)
              + lax.dot_general(w_lo, h_hi, nt, preferred_element_type=F32)
              + lax.dot_general(w_hi, h_lo, nt, preferred_element_type=F32)) + br_ref[...]
    eid = lax.broadcasted_iota(jnp.int32, logits.shape, 0)
    vals = logits
    top_v = []
    top_i = []
    for _ in range(TOP_K):
        m = jnp.max(vals, axis=0, keepdims=True)
        idx = jnp.min(jnp.where(vals == m, eid, N_EXPERTS), axis=0, keepdims=True)
        top_v.append(m)
        top_i.append(idx)
        vals = jnp.where(eid == idx, -jnp.inf, vals)
    ex = [jnp.exp(v - top_v[0]) for v in top_v]
    denom = ex[0] + ex[1] + ex[2] + ex[3]
    ti_ref[...] = jnp.concatenate(top_i, axis=0)
    tw_ref[...] = jnp.concatenate([e / denom for e in ex], axis=0)


def _router(x1, norm, sc, sh, w_router_t, b_router, seq):
    t = x1.shape[0]
    tm = _largest_tile(seq, ROUTER_ROWS, LANES)
    per_b = seq // tm
    mod_map = lambda i: (i // per_b, 0, 0)
    return pl.pallas_call(
        _router_kernel,
        grid=(t // tm,),
        in_specs=[pl.BlockSpec((tm, D_MODEL), lambda i: (i, 0)),
                  pl.BlockSpec((1, D_MODEL), lambda i: (0, 0)),
                  pl.BlockSpec((1, 1, D_MODEL), mod_map),
                  pl.BlockSpec((1, 1, D_MODEL), mod_map),
                  pl.BlockSpec((N_EXPERTS, D_MODEL), lambda i: (0, 0)),
                  pl.BlockSpec((N_EXPERTS, 1), lambda i: (0, 0))],
        out_specs=[pl.BlockSpec((tm, PACK_W), lambda i: (i, 0)),
                   pl.BlockSpec((TOP_K, tm), lambda i: (0, i)),
                   pl.BlockSpec((TOP_K, tm), lambda i: (0, i))],
        out_shape=[jax.ShapeDtypeStruct((t, PACK_W), PACK_DTYPE),
                   jax.ShapeDtypeStruct((TOP_K, t), jnp.int32),
                   jax.ShapeDtypeStruct((TOP_K, t), F32)],
        compiler_params=_params(1),
        name="router",
    )(x1, norm, sc, sh, w_router_t, b_router)


def _gather_rows(src, idx):
    n, w = src.shape
    r = idx.shape[0]
    workers = SC_CORES * SC_SUBCORES
    per_w = r // workers
    n_win = per_w // SC_GATHER_ROWS
    assert per_w * workers == r and n_win * SC_GATHER_ROWS == per_w, (r, workers, SC_GATHER_ROWS)
    mesh = plsc.VectorSubcoreMesh(core_axis_name="c", subcore_axis_name="s")

    @functools.partial(
        pl.kernel, mesh=mesh, out_type=jax.ShapeDtypeStruct((r, w), src.dtype),
        scratch_types=[pltpu.VMEM((SC_GATHER_ROWS,), jnp.int32),
                       pltpu.VMEM((SC_GATHER_ROWS, w), src.dtype),
                       pltpu.SemaphoreType.DMA])
    def gather(src_hbm, idx_hbm, out_hbm, idx_v, rows_v, sem):
        wid = lax.axis_index("s") * SC_CORES + lax.axis_index("c")
        base = wid * per_w

        @pl.loop(0, n_win)
        def _(win):
            off = base + win * SC_GATHER_ROWS
            pltpu.sync_copy(idx_hbm.at[pl.ds(off, SC_GATHER_ROWS)], idx_v)
            pltpu.async_copy(src_hbm.at[idx_v], rows_v, sem).wait()
            pltpu.sync_copy(rows_v, out_hbm.at[pl.ds(off, SC_GATHER_ROWS)])

    return gather(src, idx)


def _scatter_rows(src, dest, n_out, first, group_rows):
    n, w = src.shape
    n_assign = dest.shape[0]
    workers = SC_CORES * SC_SUBCORES
    per_w = n_assign // workers
    n_win = per_w // SC_GATHER_ROWS
    assert per_w * workers == n_assign and n_win * SC_GATHER_ROWS == per_w and group_rows % per_w == 0
    mesh = plsc.VectorSubcoreMesh(core_axis_name="c", subcore_axis_name="s")

    @functools.partial(
        pl.kernel, mesh=mesh, out_type=jax.ShapeDtypeStruct((n_out, w), src.dtype),
        scratch_types=[pltpu.VMEM((SC_GATHER_ROWS,), jnp.int32),
                       pltpu.VMEM((SC_GATHER_ROWS, w), src.dtype),
                       pltpu.SemaphoreType.DMA])
    def scatter(src_hbm, dest_hbm, out_hbm, idx_v, rows_v, sem):
        wid = lax.axis_index("s") * SC_CORES + lax.axis_index("c")
        base = wid * per_w
        row_base = first + lax.rem(base, group_rows)

        @pl.loop(0, n_win)
        def _(win):
            pltpu.sync_copy(dest_hbm.at[pl.ds(base + win * SC_GATHER_ROWS, SC_GATHER_ROWS)], idx_v)
            pltpu.sync_copy(src_hbm.at[pl.ds(row_base + win * SC_GATHER_ROWS, SC_GATHER_ROWS)], rows_v)
            pltpu.async_copy(rows_v, out_hbm.at[idx_v], sem).wait()

    return scatter(src, dest)


def _ffn_kernel(te_ref, tf_ref, tr_ref, x_ref, w1_ref, b1_ref, w2_ref, b2_ref, o_ref, w1_sc, w2_sc):
    flag = tf_ref[pl.program_id(0)]
    n_rows = tr_ref[pl.program_id(0)]

    @pl.when(flag == TILE_FIRST)
    def _():
        w1_sc[...] = w1_ref[0].astype(BF16)
        w2_sc[...] = w2_ref[0].astype(BF16)

    @pl.when(flag == TILE_PAD)
    def _():
        o_ref[...] = jnp.zeros(o_ref.shape, o_ref.dtype)

    @pl.when(flag != TILE_PAD)
    def _():
        xw = x_ref[...]
        rid = lax.broadcasted_iota(jnp.int32, xw.shape, 0)
        lo, hi = _unpack_pairs(jnp.where(rid < n_rows, xw, jnp.zeros_like(xw)))
        x = jnp.concatenate([lo, hi], axis=1).astype(BF16)
        a = jnp.dot(x, w1_sc[...], preferred_element_type=F32) + b1_ref[0]
        gate = jnp.minimum(a[:, :EXPERT_FF], SWIGLU_LIMIT)
        up = jnp.clip(a[:, EXPERT_FF:], -SWIGLU_LIMIT, SWIGLU_LIMIT)
        act = gate * jax.nn.sigmoid(SWIGLU_ALPHA * gate) * (up + 1.0)
        y = jnp.dot(act.astype(BF16), w2_sc[...], preferred_element_type=F32) + b2_ref[0]
        o_ref[...] = _pack_pairs(y)


def _ffn(xs, tile_expert, tile_flag, tile_rows, w1, b1, w2, b2, tg):
    p = xs.shape[0]
    grid_spec = pltpu.PrefetchScalarGridSpec(
        num_scalar_prefetch=3,
        grid=(p // tg,),
        in_specs=[pl.BlockSpec((tg, PACK_W), lambda j, te, tf, tr: (j, 0)),
                  pl.BlockSpec((1, D_MODEL, 2 * EXPERT_FF), lambda j, te, tf, tr: (te[j], 0, 0)),
                  pl.BlockSpec((1, 1, 2 * EXPERT_FF), lambda j, te, tf, tr: (te[j], 0, 0)),
                  pl.BlockSpec((1, EXPERT_FF, D_MODEL), lambda j, te, tf, tr: (te[j], 0, 0)),
                  pl.BlockSpec((1, 1, D_MODEL), lambda j, te, tf, tr: (te[j], 0, 0))],
        out_specs=pl.BlockSpec((tg, PACK_W), lambda j, te, tf, tr: (j, 0)),
        scratch_shapes=[pltpu.VMEM((D_MODEL, 2 * EXPERT_FF), BF16),
                        pltpu.VMEM((EXPERT_FF, D_MODEL), BF16)],
    )
    return pl.pallas_call(
        _ffn_kernel,
        grid_spec=grid_spec,
        out_shape=jax.ShapeDtypeStruct((p, PACK_W), PACK_DTYPE),
        compiler_params=_params(1),
        name="ffn",
    )(tile_expert, tile_flag, tile_rows, xs, w1, b1, w2, b2)


def _combine_kernel(x_ref, g2_ref, w_ref, y_ref, o_ref):
    w = w_ref[...]
    acc_lo = None
    for k in range(TOP_K):
        lo, hi = _unpack_pairs(y_ref[k])
        wk = w[:, k:k + 1]
        acc_lo = wk * lo if acc_lo is None else acc_lo + wk * lo
        acc_hi = wk * hi if k == 0 else acc_hi + wk * hi
    acc = jnp.concatenate([acc_lo, acc_hi], axis=1)
    o_ref[...] = x_ref[...] + g2_ref[0] * acc


def _combine_into_kernel(x_ref, g2_ref, w_ref, y_ref, prev_ref, o_ref):
    del prev_ref
    _combine_kernel(x_ref, g2_ref, w_ref, y_ref, o_ref)


def _combine(x1, g2, w_tok, yk, seq, first_row, prev):
    t = x1.shape[0]
    rows = yk.shape[1]
    tm = _largest_tile(seq, COMBINE_ROWS, SUBLANES)
    per_b = seq // tm
    i0 = first_row // tm
    in_specs = [pl.BlockSpec((tm, D_MODEL), lambda i: (i0 + i, 0)),
                pl.BlockSpec((1, 1, D_MODEL), lambda i: ((i0 + i) // per_b, 0, 0)),
                pl.BlockSpec((tm, TOP_K), lambda i: (i0 + i, 0)),
                pl.BlockSpec((TOP_K, tm, PACK_W), lambda i: (0, i, 0))]
    args = [x1, g2, w_tok, yk]
    if prev is not None:
        in_specs.append(pl.BlockSpec(memory_space=pl.ANY))
        args.append(prev)
    return pl.pallas_call(
        _combine_kernel if prev is None else _combine_into_kernel,
        grid=(rows // tm,),
        in_specs=in_specs,
        out_specs=pl.BlockSpec((tm, D_MODEL), lambda i: (i0 + i, 0)),
        out_shape=jax.ShapeDtypeStruct((t, D_MODEL), F32),
        input_output_aliases={} if prev is None else {len(args) - 1: 0},
        compiler_params=_params(1),
        name="combine",
    )(*args)


def _count_kernel(ti_ref, cnt_ref):
    @pl.when(pl.program_id(0) == 0)
    def _():
        cnt_ref[...] = jnp.zeros(cnt_ref.shape, F32)

    bt = ti_ref.shape[1]
    eid = lax.broadcasted_iota(jnp.int32, (N_EXPERTS, bt), 0)
    acc = jnp.zeros(cnt_ref.shape, F32)
    for k in range(TOP_K):
        m = (eid == ti_ref[k:k + 1, :]).astype(F32)
        for c in range(bt // LANES):
            acc = acc + m[:, c * LANES:(c + 1) * LANES]
    cnt_ref[...] += acc


def _rank_kernel(ti_ref, off_ref, tri_ref, dest_ref, run_sc):
    @pl.when(pl.program_id(0) == 0)
    def _():
        run_sc[...] = off_ref[...] - 1.0

    bt = ti_ref.shape[1]
    eid = lax.broadcasted_iota(jnp.int32, (N_EXPERTS, bt), 0)
    run = run_sc[...]
    for k in range(TOP_K):
        m = eid == ti_ref[k:k + 1, :]
        pre = jnp.dot(jnp.where(m, 1.0, 0.0).astype(BF16), tri_ref[...], preferred_element_type=F32)
        slot = jnp.sum(jnp.where(m, pre + run, 0.0), axis=0, keepdims=True)
        dest_ref[k:k + 1, :] = slot.astype(jnp.int32)
        run = run + pre[:, bt - 1:bt]
    run_sc[...] = run


def _plan(top_i, tg):
    t = top_i.shape[1]
    n_assign = TOP_K * t
    p = n_assign + N_EXPERTS * tg
    n_tiles = p // tg
    bt = _largest_tile(t, PLAN_BLOCK, LANES)
    cnt = pl.pallas_call(
        _count_kernel,
        grid=(t // bt,),
        in_specs=[pl.BlockSpec((TOP_K, bt), lambda i: (0, i))],
        out_specs=pl.BlockSpec((N_EXPERTS, LANES), lambda i: (0, 0)),
        out_shape=jax.ShapeDtypeStruct((N_EXPERTS, LANES), F32),
        compiler_params=_params(1),
        name="count",
    )(top_i)
    counts = jnp.sum(cnt, axis=1).astype(jnp.int32)
    padded = ((counts + tg - 1) // tg) * tg
    off_end = jnp.cumsum(padded)
    off = off_end - padded
    tri = (jnp.arange(bt)[:, None] <= jnp.arange(bt)[None, :]).astype(BF16)
    dest = pl.pallas_call(
        _rank_kernel,
        grid=(t // bt,),
        in_specs=[pl.BlockSpec((TOP_K, bt), lambda i: (0, i)),
                  pl.BlockSpec((N_EXPERTS, 1), lambda i: (0, 0)),
                  pl.BlockSpec((bt, bt), lambda i: (0, 0))],
        out_specs=pl.BlockSpec((TOP_K, bt), lambda i: (0, i)),
        out_shape=jax.ShapeDtypeStruct((TOP_K, t), jnp.int32),
        scratch_shapes=[pltpu.VMEM((N_EXPERTS, 1), F32)],
        compiler_params=_params(1),
        name="rank",
    )(top_i, off.astype(F32).reshape(N_EXPERTS, 1), tri)
    tile_start = jnp.arange(n_tiles, dtype=jnp.int32) * tg
    tile_valid = tile_start < off_end[-1]
    te = jnp.sum((tile_start[:, None] >= off_end[None, :]).astype(jnp.int32), axis=1)
    last_e = jnp.sum(((off_end[-1] - 1) >= off_end).astype(jnp.int32))
    tile_expert = jnp.where(tile_valid, te, last_e)
    tile_rows = jnp.where(tile_valid, jnp.clip(counts[tile_expert] - (tile_start - off[tile_expert]), 0, tg), 0)
    changed = jnp.concatenate([jnp.ones((1,), jnp.bool_), tile_expert[1:] != tile_expert[:-1]])
    tile_flag = jnp.where(tile_valid, jnp.where(changed, TILE_FIRST, TILE_BODY), TILE_PAD).astype(jnp.int32)
    return dest.reshape(-1), tile_expert, tile_flag, tile_rows.astype(jnp.int32)


def kernel(x, c, ctx, c_ctx, norm1, norm2, w_ada, b_ada, w_in, ret_decay_f, ret_decay_b, attn_q_norm, attn_k_norm,
           w_ret_o, w_attn_o, w_out, w_router, b_router, w_exp_in, b_exp_in, w_exp_out, b_exp_out):
    assert w_in.shape[0] == 1, "single-layer block"
    b, seq, d = x.shape
    n_ctx = ctx.shape[1]
    t = b * seq
    rows = seq // GRID_W

    idx = np.cumsum(IN_SIZES)[:-1].tolist()
    wq_r, wk_r, wv_r, wg_r, wq_a, wk_a, wv_a, wgt_r, wgt_a = jnp.split(w_in[0], idx, axis=-1)
    w_in_p = jnp.concatenate([wq_r, wk_r, wv_r, wg_r, wq_a, wgt_r, wgt_a, wk_a, wv_a], axis=-1)
    w1 = w_exp_in[0]
    w2 = w_exp_out[0]
    b1 = b_exp_in[0].reshape(N_EXPERTS, 1, 2 * EXPERT_FF)
    b2 = b_exp_out[0].reshape(N_EXPERTS, 1, D_MODEL)

    pad = (-(b + 1)) % SUBLANES
    c_all = jnp.concatenate([c, c_ctx[None, :], jnp.zeros((pad, d), F32)], axis=0)
    mod = _ada(c_all, w_ada[0], b_ada[0])
    sh1, sc1, g1, sh2, sc2, g2 = [m.reshape(-1, 1, d) for m in jnp.split(mod, 6, axis=-1)]
    lat = lambda m: m[:b]
    cx = lambda m: m[b:b + 1]

    x2 = x.reshape(t, d)
    proj = _inproj(x2, norm1, lat(sc1), lat(sh1), w_in_p, seq)
    proj_c = _inproj(ctx.reshape(b * n_ctx, d), norm1, cx(sc1), cx(sh1), w_in_p, n_ctx)

    cos_r, sin_r = _rope_tables(rows, RET_QK_DIM)
    cos_a, sin_a = _rope_tables(rows, ATTN_HEAD_DIM)
    lg_f = -jax.nn.softplus(ret_decay_f[0].astype(F32))
    lg_b = -jax.nn.softplus(ret_decay_b[0].astype(F32))
    zero_state = jnp.zeros((2, b, RET_HEADS, RET_QK_DIM, RET_V_DIM), F32)
    cos_c = jnp.ones((n_ctx, RET_QK_DIM), F32)
    sin_c = jnp.zeros((n_ctx, RET_QK_DIM), F32)
    _, _, s_ctx = _retention(proj_c, cos_c, sin_c, lg_f, lg_b, zero_state, b, n_ctx, False)
    o_f, o_b, _ = _retention(proj, cos_r, sin_r, lg_f, lg_b, s_ctx, b, seq, True)

    k_all, vt_all = _kvprep(proj_c, proj, attn_k_norm, cos_a, sin_a, b, n_ctx, seq)
    attn_o = _attention(proj, attn_q_norm, cos_a, sin_a, k_all, vt_all, b, seq)

    x1 = _merge(x2, lat(g1), o_f, o_b, proj, attn_o, w_ret_o[0].astype(BF16), w_attn_o[0].astype(BF16),
                w_out[0].astype(BF16), seq)

    h2, top_i, top_w = _router(x1, norm2, lat(sc2), lat(sh2), w_router[0].T, b_router[0].reshape(N_EXPERTS, 1), seq)
    n_groups = MOE_GROUPS if b % MOE_GROUPS == 0 else 1
    tgrp = t // n_groups
    tg = _largest_tile(TOP_K * tgrp, FFN_ROWS, SUBLANES)
    w_tok = top_w.T
    out = None
    for grp in range(n_groups):
        first = grp * tgrp
        dest, tile_expert, tile_flag, tile_rows = _plan(top_i[:, first:first + tgrp], tg)
        xs = _scatter_rows(h2, dest, TOP_K * tgrp + N_EXPERTS * tg, first, tgrp)
        ys = _ffn(xs, tile_expert, tile_flag, tile_rows, w1, b1, w2, b2, tg)
        yk = _gather_rows(ys, dest).reshape(TOP_K, tgrp, PACK_W)
        out = _combine(x1, lat(g2), w_tok, yk, seq, first, out)
    return out.reshape(b, seq, d)
```

```python
import functools

import jax
import jax.numpy as jnp
import numpy as np
from jax import lax
from jax.experimental import pallas as pl
from jax.experimental.pallas import tpu as pltpu
from jax.experimental.pallas import tpu_sc as plsc

F32 = jnp.float32
BF16 = jnp.bfloat16

D_MODEL = 1024
GRID_W = 64
EPS = 1e-6
RET_HEADS = 4
RET_QK_DIM = 256
RET_V_DIM = 512
ATTN_HEADS = 8
ATTN_KV_HEADS = 2
ATTN_GROUP = ATTN_HEADS // ATTN_KV_HEADS
ATTN_HEAD_DIM = 128
ROPE_THETA = 10000.0
N_EXPERTS = 32
TOP_K = 4
EXPERT_FF = 1024
SWIGLU_LIMIT = 7.0
SWIGLU_ALPHA = 1.702

RET_QK_W = RET_HEADS * RET_QK_DIM
RET_V_W = RET_HEADS * RET_V_DIM
ATTN_Q_W = ATTN_HEADS * ATTN_HEAD_DIM
ATTN_KV_W = ATTN_KV_HEADS * ATTN_HEAD_DIM
IN_SIZES = (RET_QK_W, RET_QK_W, RET_V_W, RET_V_W, ATTN_Q_W, ATTN_KV_W, ATTN_KV_W, D_MODEL, D_MODEL)
IN_WIDTH = sum(IN_SIZES)
COL_QR = 0
COL_KR = COL_QR + RET_QK_W
COL_VR = COL_KR + RET_QK_W
COL_GR = COL_VR + RET_V_W
COL_QA = COL_GR + RET_V_W
COL_GATE_R = COL_QA + ATTN_Q_W
COL_GATE_A = COL_GATE_R + D_MODEL
COL_KA = COL_GATE_A + D_MODEL
COL_VA = COL_KA + ATTN_KV_W

RET_CHUNK = 256
RET_HEADS_PER_STEP = 4
LANES = 128
SUBLANES = 8
SC_CORES = 2
SC_SUBCORES = 16
SC_GATHER_ROWS = 128
MXU_DEPTH = 256
ADA_COLS = 1536
INPROJ_ROWS = 1024
INPROJ_COLS = 2560
MERGE_ROWS = 512
ROUTER_ROWS = 2048
COMBINE_ROWS = 1024
FFN_ROWS = 512
ATTN_KEY_CHUNK = 768
ATTN_Q_TILE = 1024
ATTN_EXP_SLAB = 32
ONES_ROWS = 16
KV_PREP_ROWS = 256
PACK_DTYPE = jnp.int32
PACK_W = D_MODEL // 2
MOE_GROUPS = 2
PLAN_BLOCK = 512
TILE_PAD, TILE_BODY, TILE_FIRST = 0, 1, 2
VMEM_LIMIT = 56 * 1024 * 1024

ARB = pltpu.ARBITRARY


def _params(n_axes, **kw):
    return pltpu.CompilerParams(dimension_semantics=(ARB,) * n_axes, vmem_limit_bytes=VMEM_LIMIT, **kw)


def _largest_tile(n, cap, mult):
    best = None
    for t in range(mult, min(n, cap) + 1, mult):
        if n % t == 0:
            best = t
    assert best is not None, (n, cap, mult)
    return best


def _rms(x):
    return x * lax.rsqrt(jnp.mean(x * x, axis=-1, keepdims=True) + EPS)


def _pack_pairs(x):
    half = x.shape[1] // 2
    lo = lax.bitcast_convert_type(x[:, :half].astype(BF16).astype(F32), jnp.int32)
    hi = lax.bitcast_convert_type(x[:, half:].astype(BF16).astype(F32), jnp.int32)
    return lax.bitwise_or(lax.bitwise_and(hi, jnp.int32(-65536)), lax.shift_right_logical(lo, jnp.int32(16)))


def _unpack_pairs(w):
    lo = lax.bitcast_convert_type(lax.shift_left(w, jnp.int32(16)), F32)
    hi = lax.bitcast_convert_type(lax.bitwise_and(w, jnp.int32(-65536)), F32)
    return lo, hi


def _rope_block(x, cos, sin, half):
    if 2 * half == LANES:
        swapped = pltpu.roll(x, half, 1)
    else:
        lane = lax.broadcasted_iota(jnp.int32, x.shape, 1)
        first = (lane % (2 * half)) < half
        swapped = jnp.where(first, pltpu.roll(x, LANES - half, 1), pltpu.roll(x, half, 1))
    return x * cos + swapped * sin


def _rope_tables(rows, head_dim):
    n_freq = head_dim // 4
    inv_freq = ROPE_THETA ** (-jnp.arange(n_freq, dtype=F32) / n_freq)
    ang_r = jnp.arange(rows, dtype=F32)[:, None] * inv_freq
    ang_c = jnp.arange(GRID_W, dtype=F32)[:, None] * inv_freq
    per_row = lambda a: jnp.broadcast_to(a[:, None, :], (rows, GRID_W, n_freq)).reshape(rows * GRID_W, n_freq)
    per_col = lambda a: jnp.broadcast_to(a[None, :, :], (rows, GRID_W, n_freq)).reshape(rows * GRID_W, n_freq)
    cos_r, sin_r = per_row(jnp.cos(ang_r)), per_row(jnp.sin(ang_r))
    cos_c, sin_c = per_col(jnp.cos(ang_c)), per_col(jnp.sin(ang_c))
    cos = jnp.concatenate([cos_r, cos_r, cos_c, cos_c], axis=1)
    sin = jnp.concatenate([-sin_r, sin_r, -sin_c, sin_c], axis=1)
    return cos, sin


def _ada_kernel(c_ref, w_ref, b_ref, o_ref):
    c = c_ref[...]
    s = c * jax.nn.sigmoid(c)
    o_ref[...] = jnp.dot(s, w_ref[...], preferred_element_type=F32,
                         precision=lax.Precision.HIGHEST) + b_ref[...]


def _ada(c_pad, w_ada, b_ada):
    rows = c_pad.shape[0]
    n = w_ada.shape[1]
    tn = _largest_tile(n, ADA_COLS, LANES)
    return pl.pallas_call(
        _ada_kernel,
        grid=(n // tn,),
        in_specs=[pl.BlockSpec((rows, D_MODEL), lambda j: (0, 0)),
                  pl.BlockSpec((D_MODEL, tn), lambda j: (0, j)),
                  pl.BlockSpec((1, tn), lambda j: (0, j))],
        out_specs=pl.BlockSpec((rows, tn), lambda j: (0, j)),
        out_shape=jax.ShapeDtypeStruct((rows, n), F32),
        compiler_params=_params(1),
        name="ada",
    )(c_pad, w_ada, b_ada.reshape(1, n))


def _inproj_kernel(x_ref, n_ref, sc_ref, sh_ref, w_ref, o_ref, w_sc):
    @pl.when(pl.program_id(1) == 0)
    def _():
        w_sc[...] = w_ref[...].astype(BF16)

    y = _rms(x_ref[...]) * n_ref[...]
    h = (y * (1.0 + sc_ref[0]) + sh_ref[0]).astype(BF16)
    o_ref[...] = jnp.dot(h, w_sc[...], preferred_element_type=F32).astype(o_ref.dtype)


def _inproj(x2, norm, sc, sh, w, rows_per_batch):
    t = x2.shape[0]
    n = w.shape[1]
    tm = _largest_tile(rows_per_batch, INPROJ_ROWS, SUBLANES)
    tn = _largest_tile(n, INPROJ_COLS, LANES)
    per_b = rows_per_batch // tm
    if sc.shape[0] == 1:
        mod_map = lambda j, i: (0, 0, 0)
    else:
        mod_map = lambda j, i: (i // per_b, 0, 0)
    return pl.pallas_call(
        _inproj_kernel,
        grid=(n // tn, t // tm),
        in_specs=[pl.BlockSpec((tm, D_MODEL), lambda j, i: (i, 0)),
                  pl.BlockSpec((1, D_MODEL), lambda j, i: (0, 0)),
                  pl.BlockSpec((1, 1, D_MODEL), mod_map),
                  pl.BlockSpec((1, 1, D_MODEL), mod_map),
                  pl.BlockSpec((D_MODEL, tn), lambda j, i: (0, j))],
        out_specs=pl.BlockSpec((tm, tn), lambda j, i: (i, j)),
        out_shape=jax.ShapeDtypeStruct((t, n), BF16),
        scratch_shapes=[pltpu.VMEM((D_MODEL, tn), BF16)],
        compiler_params=_params(2),
        name="inproj",
    )(x2, norm, sc, sh, w)


def _kvprep_kernel(kc_ref, vc_ref, kl_ref, vl_ref, g_ref, cos_ref, sin_ref, k_ref, vt_ref, *, ctx_blocks):
    j = pl.program_id(1)
    vrows = ATTN_HEAD_DIM + ONES_ROWS

    def emit(kin_ref, vin_ref, use_rope):
        k = kin_ref[...].astype(F32)
        for g in range(ATTN_KV_HEADS):
            sl = slice(g * ATTN_HEAD_DIM, (g + 1) * ATTN_HEAD_DIM)
            kh = _rms(k[:, sl]) * g_ref[...]
            if use_rope:
                kh = _rope_block(kh, cos_ref[...], sin_ref[...], ATTN_HEAD_DIM // 4)
            k_ref[0, g] = kh.astype(BF16)
            vt_ref[0, g * vrows:g * vrows + ATTN_HEAD_DIM, :] = vin_ref[:, sl].T
            vt_ref[0, g * vrows + ATTN_HEAD_DIM:(g + 1) * vrows, :] = jnp.ones((ONES_ROWS, k.shape[0]), BF16)

    @pl.when(j < ctx_blocks)
    def _():
        emit(kc_ref, vc_ref, False)

    @pl.when(j >= ctx_blocks)
    def _():
        emit(kl_ref, vl_ref, True)


def _kvprep(proj_c, proj, gain, cos, sin, batch, n_ctx, seq):
    tm = KV_PREP_ROWS
    assert n_ctx % tm == 0 and seq % tm == 0
    cb, lb = n_ctx // tm, seq // tm
    lk = n_ctx + seq
    ctx_row = lambda b, j: b * cb + jnp.minimum(j, cb - 1)
    lat_blk = lambda j: jnp.maximum(j - cb, 0)
    lat_row = lambda b, j: b * lb + lat_blk(j)
    vrows = ATTN_KV_HEADS * (ATTN_HEAD_DIM + ONES_ROWS)
    return pl.pallas_call(
        functools.partial(_kvprep_kernel, ctx_blocks=cb),
        grid=(batch, cb + lb),
        in_specs=[pl.BlockSpec((tm, ATTN_KV_W), lambda b, j: (ctx_row(b, j), COL_KA // ATTN_KV_W)),
                  pl.BlockSpec((tm, ATTN_KV_W), lambda b, j: (ctx_row(b, j), COL_VA // ATTN_KV_W)),
                  pl.BlockSpec((tm, ATTN_KV_W), lambda b, j: (lat_row(b, j), COL_KA // ATTN_KV_W)),
                  pl.BlockSpec((tm, ATTN_KV_W), lambda b, j: (lat_row(b, j), COL_VA // ATTN_KV_W)),
                  pl.BlockSpec((1, ATTN_HEAD_DIM), lambda b, j: (0, 0)),
                  pl.BlockSpec((tm, ATTN_HEAD_DIM), lambda b, j: (lat_blk(j), 0)),
                  pl.BlockSpec((tm, ATTN_HEAD_DIM), lambda b, j: (lat_blk(j), 0))],
        out_specs=[pl.BlockSpec((1, ATTN_KV_HEADS, tm, ATTN_HEAD_DIM), lambda b, j: (b, 0, j, 0)),
                   pl.BlockSpec((1, vrows, tm), lambda b, j: (b, 0, j))],
        out_shape=[jax.ShapeDtypeStruct((batch, ATTN_KV_HEADS, lk, ATTN_HEAD_DIM), BF16),
                   jax.ShapeDtypeStruct((batch, vrows, lk), BF16)],
        compiler_params=_params(2),
        name="kvprep",
    )(proj_c, proj_c, proj, proj, gain, cos, sin)


def _ret_kernel(dch_ref, qf_ref, kf_ref, vf_ref, cosf_ref, sinf_ref, qb_ref, kb_ref, vb_ref, cosb_ref, sinb_ref,
                dmat_ref, din_ref, dout_ref, s0_ref, of_ref, ob_ref, sfin_ref, s_sc, *, use_rope, n_chunks, heads):
    hb = pl.program_id(1)
    c = pl.program_id(2)

    @pl.when(c == 0)
    def _():
        s_sc[...] = s0_ref[:, 0]

    def rope(x, cos, sin):
        return jnp.concatenate([_rope_block(x[:, i * LANES:(i + 1) * LANES], cos[:, i * LANES:(i + 1) * LANES],
                                            sin[:, i * LANES:(i + 1) * LANES], RET_QK_DIM // 4)
                                for i in range(RET_QK_DIM // LANES)], axis=1)

    sides = ((qf_ref, kf_ref, vf_ref, cosf_ref, sinf_ref), (qb_ref, kb_ref, vb_ref, cosb_ref, sinb_ref))
    chains = [(d, j) for d in range(2) for j in range(heads)]
    qs, ks, vs, ss = {}, {}, {}, {}
    for d, j in chains:
        q_ref, k_ref, v_ref, cos_ref, sin_ref = sides[d]
        q = q_ref[:, j * RET_QK_DIM:(j + 1) * RET_QK_DIM].astype(F32)
        k = k_ref[:, j * RET_QK_DIM:(j + 1) * RET_QK_DIM].astype(F32)
        if use_rope:
            q = rope(q, cos_ref[...], sin_ref[...])
            k = rope(k, cos_ref[...], sin_ref[...])
        qs[d, j], ks[d, j] = q, k
        vs[d, j] = v_ref[:, j * RET_V_DIM:(j + 1) * RET_V_DIM]
        ss[d, j] = s_sc[d, j]
    o_refs = (of_ref, ob_ref)
    scores, inter, s_new = {}, {}, {}

    def first_half(ch):
        scores[ch] = lax.dot_general(qs[ch].astype(BF16), ks[ch].astype(BF16), (((1,), (1,)), ((), ())),
                                     preferred_element_type=F32) * dmat_ref[ch[0], ch[1]]
        inter[ch] = jnp.dot((qs[ch] * din_ref[ch[0], ch[1]]).astype(BF16), ss[ch].astype(BF16),
                            preferred_element_type=F32)

    def second_half(ch):
        d, j = ch
        intra = jnp.dot(scores[ch].astype(BF16), vs[ch], preferred_element_type=F32)
        o_refs[d][:, j * RET_V_DIM:(j + 1) * RET_V_DIM] = (intra + inter[ch]).astype(of_ref.dtype)
        s_new[ch] = ss[ch] * dch_ref[d * RET_HEADS + hb * heads + j] + jnp.dot(
            (ks[ch] * dout_ref[d, j]).T.astype(BF16), vs[ch], preferred_element_type=F32)
        s_sc[d, j] = s_new[ch]

    for i in range(len(chains) + 1):
        if i < len(chains):
            first_half(chains[i])
        if i >= 1:
            second_half(chains[i - 1])

    @pl.when(c == n_chunks - 1)
    def _():
        for d, j in chains:
            sfin_ref[d, 0, j] = s_new[d, j]


def _ret_tables(log_gamma, chunk, reverse, k_scale):
    pos = jnp.arange(chunk, dtype=F32)
    diff = pos[:, None] - pos[None, :]
    if reverse:
        diff = -diff
        mask = diff > 0
        p_in = chunk - pos
        p_out = pos
    else:
        mask = diff >= 0
        p_in = pos + 1.0
        p_out = chunk - 1.0 - pos
    lg = log_gamma.astype(F32)
    dmat = jnp.where(mask[None], jnp.exp(lg[:, None, None] * jnp.maximum(diff, 0.0)[None]), 0.0) * k_scale
    d_in = jnp.exp(lg[:, None] * p_in)
    d_out = jnp.exp(lg[:, None] * p_out) * k_scale
    d_in = jnp.broadcast_to(d_in[:, :, None], (RET_HEADS, chunk, RET_QK_DIM))
    d_out = jnp.broadcast_to(d_out[:, :, None], (RET_HEADS, chunk, RET_QK_DIM))
    d_chunk = jnp.exp(lg * chunk)
    return d_chunk, dmat, d_in, d_out


def _retention(proj, cos, sin, lg_f, lg_b, s0, batch, seq, use_rope):
    chunk = min(RET_CHUNK, seq)
    nc = seq // chunk
    hs = RET_HEADS_PER_STEP
    qw, vw = hs * RET_QK_DIM, hs * RET_V_DIM
    k_scale = RET_QK_DIM ** -0.5
    tabs = [_ret_tables(lg_f, chunk, False, k_scale), _ret_tables(lg_b, chunk, True, k_scale)]
    d_chunk, dmat, d_in, d_out = [jnp.stack([tabs[0][i], tabs[1][i]]) for i in range(4)]
    fwd = lambda c: c
    bwd = lambda c: nc - 1 - c

    def side(cidx):
        row = lambda b, c: b * nc + cidx(c)
        return [pl.BlockSpec((chunk, qw), lambda b, h, c: (row(b, c), COL_QR // qw + h)),
                pl.BlockSpec((chunk, qw), lambda b, h, c: (row(b, c), COL_KR // qw + h)),
                pl.BlockSpec((chunk, vw), lambda b, h, c: (row(b, c), COL_VR // vw + h)),
                pl.BlockSpec((chunk, RET_QK_DIM), lambda b, h, c: (cidx(c), 0)),
                pl.BlockSpec((chunk, RET_QK_DIM), lambda b, h, c: (cidx(c), 0))]

    state_spec = pl.BlockSpec((2, 1, hs, RET_QK_DIM, RET_V_DIM), lambda b, h, c: (0, b, h, 0, 0))
    return pl.pallas_call(
        functools.partial(_ret_kernel, use_rope=use_rope, n_chunks=nc, heads=hs),
        grid=(batch, RET_HEADS // hs, nc),
        in_specs=[pl.BlockSpec(memory_space=pltpu.SMEM)] + side(fwd) + side(bwd) + [
            pl.BlockSpec((2, hs, chunk, chunk), lambda b, h, c: (0, h, 0, 0)),
            pl.BlockSpec((2, hs, chunk, RET_QK_DIM), lambda b, h, c: (0, h, 0, 0)),
            pl.BlockSpec((2, hs, chunk, RET_QK_DIM), lambda b, h, c: (0, h, 0, 0)),
            state_spec],
        out_specs=[pl.BlockSpec((chunk, vw), lambda b, h, c: (b * nc + c, h)),
                   pl.BlockSpec((chunk, vw), lambda b, h, c: (b * nc + nc - 1 - c, h)),
                   state_spec],
        out_shape=[jax.ShapeDtypeStruct((batch * seq, RET_V_W), BF16),
                   jax.ShapeDtypeStruct((batch * seq, RET_V_W), BF16),
                   jax.ShapeDtypeStruct((2, batch, RET_HEADS, RET_QK_DIM, RET_V_DIM), F32)],
        scratch_shapes=[pltpu.VMEM((2, hs, RET_QK_DIM, RET_V_DIM), F32)],
        compiler_params=_params(3),
        name="ret",
    )(d_chunk.reshape(-1), proj, proj, proj, cos, sin, proj, proj, proj, cos, sin, dmat, d_in, d_out, s0)


def _attn_kernel(q_ref, g_ref, cos_ref, sin_ref, k_ref, vt_ref, o_ref, qt_sc, m_sc, acc_sc,
                 s0_sc, s1_sc, p0_sc, p1_sc, a0_sc, a1_sc, *, tq, kc, n_chunks):
    s_bufs = (s0_sc, s1_sc)
    p_bufs = (p0_sc, p1_sc)
    a_bufs = (a0_sc, a1_sc)

    q = q_ref[...].astype(F32)
    scale = ATTN_HEAD_DIM ** -0.5 * np.log2(np.e)
    for g in range(ATTN_GROUP):
        qh = _rms(q[:, g * ATTN_HEAD_DIM:(g + 1) * ATTN_HEAD_DIM]) * g_ref[...]
        qh = _rope_block(qh, cos_ref[...], sin_ref[...], ATTN_HEAD_DIM // 4) * scale
        qt_sc[:, g * tq:(g + 1) * tq] = qh.T.astype(BF16)
    m_sc[...] = jnp.full(m_sc.shape, -jnp.inf, F32)
    acc_sc[...] = jnp.zeros(acc_sc.shape, F32)

    def stage_s(c, slot):
        off = pl.multiple_of(c * kc, kc)
        s_bufs[slot][...] = jnp.dot(k_ref[0, 0, pl.ds(off, kc), :], qt_sc[...], preferred_element_type=F32)

    def stage_f(slot):
        m_prev = m_sc[...]
        m_new = jnp.maximum(m_prev, jnp.max(s_bufs[slot][...], axis=0, keepdims=True))
        m_sc[...] = m_new
        a_bufs[slot][...] = jnp.exp2(m_prev - m_new)
        for r in range(0, kc, ATTN_EXP_SLAB):
            p_bufs[slot][r:r + ATTN_EXP_SLAB, :] = jnp.exp2(
                s_bufs[slot][r:r + ATTN_EXP_SLAB, :] - m_new).astype(BF16)

    def stage_a(c, slot):
        off = pl.multiple_of(c * kc, kc)
        acc_sc[...] = a_bufs[slot][...] * acc_sc[...] + jnp.dot(
            vt_ref[0, :, pl.ds(off, kc)], p_bufs[slot][...], preferred_element_type=F32)

    def tick(t, parity):
        stage_s(t, parity)
        stage_f(1 - parity)
        stage_a(t - 2, parity)

    n = n_chunks
    stage_s(0, 0)
    if n > 1:
        stage_s(1, 1)
    stage_f(0)
    first = 2
    if n > 2 and (n - 2) % 2 == 1:
        tick(2, 0)
        first = 3
    n_pairs = (n - first) // 2 if n > first else 0
    if n_pairs > 0:
        def pair(u, carry):
            t = first + 2 * u
            tick(t, first % 2)
            tick(t + 1, 1 - first % 2)
            return carry
        lax.fori_loop(0, n_pairs, pair, 0)
    if n > 1:
        stage_f((n - 1) % 2)
        stage_a(n - 2, (n - 2) % 2)
    stage_a(n - 1, (n - 1) % 2)

    acc = acc_sc[...]
    o = acc[:ATTN_HEAD_DIM] / acc[ATTN_HEAD_DIM:ATTN_HEAD_DIM + 1]
    for g in range(ATTN_GROUP):
        o_ref[:, g * ATTN_HEAD_DIM:(g + 1) * ATTN_HEAD_DIM] = o[:, g * tq:(g + 1) * tq].T.astype(o_ref.dtype)


def _attention(proj, gain, cos, sin, k_all, vt_all, batch, seq):
    lk = k_all.shape[2]
    kc = _largest_tile(lk, ATTN_KEY_CHUNK, MXU_DEPTH)
    tq = _largest_tile(seq, ATTN_Q_TILE, SUBLANES)
    nq = seq // tq
    qw = ATTN_GROUP * ATTN_HEAD_DIM
    cols = ATTN_GROUP * tq
    vrows = ATTN_HEAD_DIM + ONES_ROWS
    return pl.pallas_call(
        functools.partial(_attn_kernel, tq=tq, kc=kc, n_chunks=lk // kc),
        grid=(batch, ATTN_KV_HEADS, nq),
        in_specs=[pl.BlockSpec((tq, qw), lambda b, g, i: (b * nq + i, COL_QA // qw + g)),
                  pl.BlockSpec((1, ATTN_HEAD_DIM), lambda b, g, i: (0, 0)),
                  pl.BlockSpec((tq, ATTN_HEAD_DIM), lambda b, g, i: (i, 0)),
                  pl.BlockSpec((tq, ATTN_HEAD_DIM), lambda b, g, i: (i, 0)),
                  pl.BlockSpec((1, 1, lk, ATTN_HEAD_DIM), lambda b, g, i: (b, g, 0, 0), pipeline_mode=pl.Buffered(1)),
                  pl.BlockSpec((1, vrows, lk), lambda b, g, i: (b, g, 0), pipeline_mode=pl.Buffered(1))],
        out_specs=pl.BlockSpec((tq, qw), lambda b, g, i: (b * nq + i, g)),
        out_shape=jax.ShapeDtypeStruct((batch * seq, ATTN_Q_W), BF16),
        scratch_shapes=[pltpu.VMEM((ATTN_HEAD_DIM, cols), BF16),
                        pltpu.VMEM((1, cols), F32),
                        pltpu.VMEM((vrows, cols), F32),
                        pltpu.VMEM((kc, cols), F32),
                        pltpu.VMEM((kc, cols), F32),
                        pltpu.VMEM((kc, cols), BF16),
                        pltpu.VMEM((kc, cols), BF16),
                        pltpu.VMEM((1, cols), F32),
                        pltpu.VMEM((1, cols), F32)],
        compiler_params=_params(3),
        name="attn",
    )(proj, gain, cos, sin, k_all, vt_all)


def _merge_kernel(x_ref, g1_ref, of_ref, ob_ref, gr_ref, ao_ref, gtr_ref, gta_ref,
                  wro_ref, wao_ref, wo_ref, o_ref):
    ro = of_ref[...].astype(F32) + ob_ref[...].astype(F32)
    gr = gr_ref[...].astype(F32)
    parts = []
    for h in range(RET_HEADS):
        sl = slice(h * RET_V_DIM, (h + 1) * RET_V_DIM)
        g = gr[:, sl]
        parts.append((g * jax.nn.sigmoid(g) * _rms(ro[:, sl])).astype(BF16))
    ret_in = jnp.concatenate(parts, axis=1)
    ret_branch = jnp.dot(ret_in, wro_ref[...], preferred_element_type=F32)
    attn_branch = jnp.dot(ao_ref[...], wao_ref[...], preferred_element_type=F32)
    y = (jax.nn.sigmoid(gtr_ref[...].astype(F32)) * ret_branch
         + jax.nn.sigmoid(gta_ref[...].astype(F32)) * attn_branch)
    y = jnp.dot(y.astype(BF16), wo_ref[...], preferred_element_type=F32)
    o_ref[...] = x_ref[...] + g1_ref[0] * y


def _merge(x2, g1, o_f, o_b, proj, attn_o, w_ret_o, w_attn_o, w_out, seq):
    t = x2.shape[0]
    tm = _largest_tile(seq, MERGE_ROWS, SUBLANES)
    per_b = seq // tm
    full = lambda i: (0, 0)
    return pl.pallas_call(
        _merge_kernel,
        grid=(t // tm,),
        in_specs=[pl.BlockSpec((tm, D_MODEL), lambda i: (i, 0)),
                  pl.BlockSpec((1, 1, D_MODEL), lambda i: (i // per_b, 0, 0)),
                  pl.BlockSpec((tm, RET_V_W), lambda i: (i, 0)),
                  pl.BlockSpec((tm, RET_V_W), lambda i: (i, 0)),
                  pl.BlockSpec((tm, RET_V_W), lambda i: (i, COL_GR // RET_V_W)),
                  pl.BlockSpec((tm, ATTN_Q_W), lambda i: (i, 0)),
                  pl.BlockSpec((tm, D_MODEL), lambda i: (i, COL_GATE_R // D_MODEL)),
                  pl.BlockSpec((tm, D_MODEL), lambda i: (i, COL_GATE_A // D_MODEL)),
                  pl.BlockSpec((RET_V_W, D_MODEL), full, pipeline_mode=pl.Buffered(1)),
                  pl.BlockSpec((ATTN_Q_W, D_MODEL), full, pipeline_mode=pl.Buffered(1)),
                  pl.BlockSpec((D_MODEL, D_MODEL), full, pipeline_mode=pl.Buffered(1))],
        out_specs=pl.BlockSpec((tm, D_MODEL), lambda i: (i, 0)),
        out_shape=jax.ShapeDtypeStruct((t, D_MODEL), F32),
        compiler_params=_params(1),
        name="merge",
    )(x2, g1, o_f, o_b, proj, attn_o, proj, proj, w_ret_o, w_attn_o, w_out)


def _router_kernel(x_ref, n_ref, sc_ref, sh_ref, wr_ref, br_ref, h_ref, ti_ref, tw_ref):
    h = _rms(x_ref[...]) * n_ref[...] * (1.0 + sc_ref[0]) + sh_ref[0]
    h_ref[...] = _pack_pairs(h)
    nt = (((1,), (1,)), ((), ()))
    w = wr_ref[...]
    h_hi = h.astype(BF16)
    h_lo = (h - h_hi.astype(F32)).astype(BF16)
    w_hi = w.astype(BF16)
    w_lo = (w - w_hi.astype(F32)).astype(BF16)
    logits = (lax.dot_general(w_hi, h_hi, nt, preferred_element_type=F32)
              + lax.dot_general(w_lo, h_hi, nt, preferred_element_type=F32)
              + lax.dot_general(w_hi, h_lo, nt, preferred_element_type=F32)) + br_ref[...]
    eid = lax.broadcasted_iota(jnp.int32, logits.shape, 0)
    vals = logits
    top_v = []
    top_i = []
    for _ in range(TOP_K):
        m = jnp.max(vals, axis=0, keepdims=True)
        idx = jnp.min(jnp.where(vals == m, eid, N_EXPERTS), axis=0, keepdims=True)
        top_v.append(m)
        top_i.append(idx)
        vals = jnp.where(eid == idx, -jnp.inf, vals)
    ex = [jnp.exp(v - top_v[0]) for v in top_v]
    denom = ex[0] + ex[1] + ex[2] + ex[3]
    ti_ref[...] = jnp.concatenate(top_i, axis=0)
    tw_ref[...] = jnp.concatenate([e / denom for e in ex], axis=0)


def _router(x1, norm, sc, sh, w_router_t, b_router, seq):
    t = x1.shape[0]
    tm = _largest_tile(seq, ROUTER_ROWS, LANES)
    per_b = seq // tm
    mod_map = lambda i: (i // per_b, 0, 0)
    return pl.pallas_call(
        _router_kernel,
        grid=(t // tm,),
        in_specs=[pl.BlockSpec((tm, D_MODEL), lambda i: (i, 0)),
                  pl.BlockSpec((1, D_MODEL), lambda i: (0, 0)),
                  pl.BlockSpec((1, 1, D_MODEL), mod_map),
                  pl.BlockSpec((1, 1, D_MODEL), mod_map),
                  pl.BlockSpec((N_EXPERTS, D_MODEL), lambda i: (0, 0)),
                  pl.BlockSpec((N_EXPERTS, 1), lambda i: (0, 0))],
        out_specs=[pl.BlockSpec((tm, PACK_W), lambda i: (i, 0)),
                   pl.BlockSpec((TOP_K, tm), lambda i: (0, i)),
                   pl.BlockSpec((TOP_K, tm), lambda i: (0, i))],
        out_shape=[jax.ShapeDtypeStruct((t, PACK_W), PACK_DTYPE),
                   jax.ShapeDtypeStruct((TOP_K, t), jnp.int32),
                   jax.ShapeDtypeStruct((TOP_K, t), F32)],
        compiler_params=_params(1),
        name="router",
    )(x1, norm, sc, sh, w_router_t, b_router)


def _gather_rows(src, idx):
    n, w = src.shape
    r = idx.shape[0]
    workers = SC_CORES * SC_SUBCORES
    per_w = r // workers
    n_win = per_w // SC_GATHER_ROWS
    assert per_w * workers == r and n_win * SC_GATHER_ROWS == per_w, (r, workers, SC_GATHER_ROWS)
    mesh = plsc.VectorSubcoreMesh(core_axis_name="c", subcore_axis_name="s")

    @functools.partial(
        pl.kernel, mesh=mesh, out_type=jax.ShapeDtypeStruct((r, w), src.dtype),
        scratch_types=[pltpu.VMEM((SC_GATHER_ROWS,), jnp.int32),
                       pltpu.VMEM((SC_GATHER_ROWS, w), src.dtype),
                       pltpu.SemaphoreType.DMA])
    def gather(src_hbm, idx_hbm, out_hbm, idx_v, rows_v, sem):
        wid = lax.axis_index("s") * SC_CORES + lax.axis_index("c")
        base = wid * per_w

        @pl.loop(0, n_win)
        def _(win):
            off = base + win * SC_GATHER_ROWS
            pltpu.sync_copy(idx_hbm.at[pl.ds(off, SC_GATHER_ROWS)], idx_v)
            pltpu.async_copy(src_hbm.at[idx_v], rows_v, sem).wait()
            pltpu.sync_copy(rows_v, out_hbm.at[pl.ds(off, SC_GATHER_ROWS)])

    return gather(src, idx)


def _scatter_rows(src, dest, n_out, first, group_rows):
    n, w = src.shape
    n_assign = dest.shape[0]
    workers = SC_CORES * SC_SUBCORES
    per_w = n_assign // workers
    n_win = per_w // SC_GATHER_ROWS
    assert per_w * workers == n_assign and n_win * SC_GATHER_ROWS == per_w and group_rows % per_w == 0
    mesh = plsc.VectorSubcoreMesh(core_axis_name="c", subcore_axis_name="s")

    @functools.partial(
        pl.kernel, mesh=mesh, out_type=jax.ShapeDtypeStruct((n_out, w), src.dtype),
        scratch_types=[pltpu.VMEM((SC_GATHER_ROWS,), jnp.int32),
                       pltpu.VMEM((SC_GATHER_ROWS, w), src.dtype),
                       pltpu.SemaphoreType.DMA])
    def scatter(src_hbm, dest_hbm, out_hbm, idx_v, rows_v, sem):
        wid = lax.axis_index("s") * SC_CORES + lax.axis_index("c")
        base = wid * per_w
        row_base = first + lax.rem(base, group_rows)

        @pl.loop(0, n_win)
        def _(win):
            pltpu.sync_copy(dest_hbm.at[pl.ds(base + win * SC_GATHER_ROWS, SC_GATHER_ROWS)], idx_v)
            pltpu.sync_copy(src_hbm.at[pl.ds(row_base + win * SC_GATHER_ROWS, SC_GATHER_ROWS)], rows_v)
            pltpu.async_copy(rows_v, out_hbm.at[idx_v], sem).wait()

    return scatter(src, dest)


def _ffn_kernel(te_ref, tf_ref, tr_ref, x_ref, w1_ref, b1_ref, w2_ref, b2_ref, o_ref, w1_sc, w2_sc):
    flag = tf_ref[pl.program_id(0)]
    n_rows = tr_ref[pl.program_id(0)]

    @pl.when(flag == TILE_FIRST)
    def _():
        w1_sc[...] = w1_ref[0].astype(BF16)
        w2_sc[...] = w2_ref[0].astype(BF16)

    @pl.when(flag == TILE_PAD)
    def _():
        o_ref[...] = jnp.zeros(o_ref.shape, o_ref.dtype)

    @pl.when(flag != TILE_PAD)
    def _():
        xw = x_ref[...]
        rid = lax.broadcasted_iota(jnp.int32, xw.shape, 0)
        lo, hi = _unpack_pairs(jnp.where(rid < n_rows, xw, jnp.zeros_like(xw)))
        x = jnp.concatenate([lo, hi], axis=1).astype(BF16)
        a = jnp.dot(x, w1_sc[...], preferred_element_type=F32) + b1_ref[0]
        gate = jnp.minimum(a[:, :EXPERT_FF], SWIGLU_LIMIT)
        up = jnp.clip(a[:, EXPERT_FF:], -SWIGLU_LIMIT, SWIGLU_LIMIT)
        act = gate * jax.nn.sigmoid(SWIGLU_ALPHA * gate) * (up + 1.0)
        y = jnp.dot(act.astype(BF16), w2_sc[...], preferred_element_type=F32) + b2_ref[0]
        o_ref[...] = _pack_pairs(y)


def _ffn(xs, tile_expert, tile_flag, tile_rows, w1, b1, w2, b2, tg):
    p = xs.shape[0]
    grid_spec = pltpu.PrefetchScalarGridSpec(
        num_scalar_prefetch=3,
        grid=(p // tg,),
        in_specs=[pl.BlockSpec((tg, PACK_W), lambda j, te, tf, tr: (j, 0)),
                  pl.BlockSpec((1, D_MODEL, 2 * EXPERT_FF), lambda j, te, tf, tr: (te[j], 0, 0)),
                  pl.BlockSpec((1, 1, 2 * EXPERT_FF), lambda j, te, tf, tr: (te[j], 0, 0)),
                  pl.BlockSpec((1, EXPERT_FF, D_MODEL), lambda j, te, tf, tr: (te[j], 0, 0)),
                  pl.BlockSpec((1, 1, D_MODEL), lambda j, te, tf, tr: (te[j], 0, 0))],
        out_specs=pl.BlockSpec((tg, PACK_W), lambda j, te, tf, tr: (j, 0)),
        scratch_shapes=[pltpu.VMEM((D_MODEL, 2 * EXPERT_FF), BF16),
                        pltpu.VMEM((EXPERT_FF, D_MODEL), BF16)],
    )
    return pl.pallas_call(
        _ffn_kernel,
        grid_spec=grid_spec,
        out_shape=jax.ShapeDtypeStruct((p, PACK_W), PACK_DTYPE),
        compiler_params=_params(1),
        name="ffn",
    )(tile_expert, tile_flag, tile_rows, xs, w1, b1, w2, b2)


def _combine_kernel(x_ref, g2_ref, w_ref, y_ref, o_ref):
    w = w_ref[...]
    acc_lo = None
    for k in range(TOP_K):
        lo, hi = _unpack_pairs(y_ref[k])
        wk = w[:, k:k + 1]
        acc_lo = wk * lo if acc_lo is None else acc_lo + wk * lo
        acc_hi = wk * hi if k == 0 else acc_hi + wk * hi
    acc = jnp.concatenate([acc_lo, acc_hi], axis=1)
    o_ref[...] = x_ref[...] + g2_ref[0] * acc


def _combine_into_kernel(x_ref, g2_ref, w_ref, y_ref, prev_ref, o_ref):
    del prev_ref
    _combine_kernel(x_ref, g2_ref, w_ref, y_ref, o_ref)


def _combine(x1, g2, w_tok, yk, seq, first_row, prev):
    t = x1.shape[0]
    rows = yk.shape[1]
    tm = _largest_tile(seq, COMBINE_ROWS, SUBLANES)
    per_b = seq // tm
    i0 = first_row // tm
    in_specs = [pl.BlockSpec((tm, D_MODEL), lambda i: (i0 + i, 0)),
                pl.BlockSpec((1, 1, D_MODEL), lambda i: ((i0 + i) // per_b, 0, 0)),
                pl.BlockSpec((tm, TOP_K), lambda i: (i0 + i, 0)),
                pl.BlockSpec((TOP_K, tm, PACK_W), lambda i: (0, i, 0))]
    args = [x1, g2, w_tok, yk]
    if prev is not None:
        in_specs.append(pl.BlockSpec(memory_space=pl.ANY))
        args.append(prev)
    return pl.pallas_call(
        _combine_kernel if prev is None else _combine_into_kernel,
        grid=(rows // tm,),
        in_specs=in_specs,
        out_specs=pl.BlockSpec((tm, D_MODEL), lambda i: (i0 + i, 0)),
        out_shape=jax.ShapeDtypeStruct((t, D_MODEL), F32),
        input_output_aliases={} if prev is None else {len(args) - 1: 0},
        compiler_params=_params(1),
        name="combine",
    )(*args)


def _count_kernel(ti_ref, cnt_ref):
    @pl.when(pl.program_id(0) == 0)
    def _():
        cnt_ref[...] = jnp.zeros(cnt_ref.shape, F32)

    bt = ti_ref.shape[1]
    eid = lax.broadcasted_iota(jnp.int32, (N_EXPERTS, bt), 0)
    acc = jnp.zeros(cnt_ref.shape, F32)
    for k in range(TOP_K):
        m = (eid == ti_ref[k:k + 1, :]).astype(F32)
        for c in range(bt // LANES):
            acc = acc + m[:, c * LANES:(c + 1) * LANES]
    cnt_ref[...] += acc


def _rank_kernel(ti_ref, off_ref, tri_ref, dest_ref, run_sc):
    @pl.when(pl.program_id(0) == 0)
    def _():
        run_sc[...] = off_ref[...] - 1.0

    bt = ti_ref.shape[1]
    eid = lax.broadcasted_iota(jnp.int32, (N_EXPERTS, bt), 0)
    run = run_sc[...]
    for k in range(TOP_K):
        m = eid == ti_ref[k:k + 1, :]
        pre = jnp.dot(jnp.where(m, 1.0, 0.0).astype(BF16), tri_ref[...], preferred_element_type=F32)
        slot = jnp.sum(jnp.where(m, pre + run, 0.0), axis=0, keepdims=True)
        dest_ref[k:k + 1, :] = slot.astype(jnp.int32)
        run = run + pre[:, bt - 1:bt]
    run_sc[...] = run


def _plan(top_i, tg):
    t = top_i.shape[1]
    n_assign = TOP_K * t
    p = n_assign + N_EXPERTS * tg
    n_tiles = p // tg
    bt = _largest_tile(t, PLAN_BLOCK, LANES)
    cnt = pl.pallas_call(
        _count_kernel,
        grid=(t // bt,),
        in_specs=[pl.BlockSpec((TOP_K, bt), lambda i: (0, i))],
        out_specs=pl.BlockSpec((N_EXPERTS, LANES), lambda i: (0, 0)),
        out_shape=jax.ShapeDtypeStruct((N_EXPERTS, LANES), F32),
        compiler_params=_params(1),
        name="count",
    )(top_i)
    counts = jnp.sum(cnt, axis=1).astype(jnp.int32)
    padded = ((counts + tg - 1) // tg) * tg
    off_end = jnp.cumsum(padded)
    off = off_end - padded
    tri = (jnp.arange(bt)[:, None] <= jnp.arange(bt)[None, :]).astype(BF16)
    dest = pl.pallas_call(
        _rank_kernel,
        grid=(t // bt,),
        in_specs=[pl.BlockSpec((TOP_K, bt), lambda i: (0, i)),
                  pl.BlockSpec((N_EXPERTS, 1), lambda i: (0, 0)),
                  pl.BlockSpec((bt, bt), lambda i: (0, 0))],
        out_specs=pl.BlockSpec((TOP_K, bt), lambda i: (0, i)),
        out_shape=jax.ShapeDtypeStruct((TOP_K, t), jnp.int32),
        scratch_shapes=[pltpu.VMEM((N_EXPERTS, 1), F32)],
        compiler_params=_params(1),
        name="rank",
    )(top_i, off.astype(F32).reshape(N_EXPERTS, 1), tri)
    tile_start = jnp.arange(n_tiles, dtype=jnp.int32) * tg
    tile_valid = tile_start < off_end[-1]
    te = jnp.sum((tile_start[:, None] >= off_end[None, :]).astype(jnp.int32), axis=1)
    last_e = jnp.sum(((off_end[-1] - 1) >= off_end).astype(jnp.int32))
    tile_expert = jnp.where(tile_valid, te, last_e)
    tile_rows = jnp.where(tile_valid, jnp.clip(counts[tile_expert] - (tile_start - off[tile_expert]), 0, tg), 0)
    changed = jnp.concatenate([jnp.ones((1,), jnp.bool_), tile_expert[1:] != tile_expert[:-1]])
    tile_flag = jnp.where(tile_valid, jnp.where(changed, TILE_FIRST, TILE_BODY), TILE_PAD).astype(jnp.int32)
    return dest.reshape(-1), tile_expert, tile_flag, tile_rows.astype(jnp.int32)


def kernel(x, c, ctx, c_ctx, norm1, norm2, w_ada, b_ada, w_in, ret_decay_f, ret_decay_b, attn_q_norm, attn_k_norm,
           w_ret_o, w_attn_o, w_out, w_router, b_router, w_exp_in, b_exp_in, w_exp_out, b_exp_out):
    assert w_in.shape[0] == 1, "single-layer block"
    b, seq, d = x.shape
    n_ctx = ctx.shape[1]
    t = b * seq
    rows = seq // GRID_W

    idx = np.cumsum(IN_SIZES)[:-1].tolist()
    wq_r, wk_r, wv_r, wg_r, wq_a, wk_a, wv_a, wgt_r, wgt_a = jnp.split(w_in[0], idx, axis=-1)
    w_in_p = jnp.concatenate([wq_r, wk_r, wv_r, wg_r, wq_a, wgt_r, wgt_a, wk_a, wv_a], axis=-1)
    w1 = w_exp_in[0]
    w2 = w_exp_out[0]
    b1 = b_exp_in[0].reshape(N_EXPERTS, 1, 2 * EXPERT_FF)
    b2 = b_exp_out[0].reshape(N_EXPERTS, 1, D_MODEL)

    pad = (-(b + 1)) % SUBLANES
    c_all = jnp.concatenate([c, c_ctx[None, :], jnp.zeros((pad, d), F32)], axis=0)
    mod = _ada(c_all, w_ada[0], b_ada[0])
    sh1, sc1, g1, sh2, sc2, g2 = [m.reshape(-1, 1, d) for m in jnp.split(mod, 6, axis=-1)]
    lat = lambda m: m[:b]
    cx = lambda m: m[b:b + 1]

    x2 = x.reshape(t, d)
    proj = _inproj(x2, norm1, lat(sc1), lat(sh1), w_in_p, seq)
    proj_c = _inproj(ctx.reshape(b * n_ctx, d), norm1, cx(sc1), cx(sh1), w_in_p, n_ctx)

    cos_r, sin_r = _rope_tables(rows, RET_QK_DIM)
    cos_a, sin_a = _rope_tables(rows, ATTN_HEAD_DIM)
    lg_f = -jax.nn.softplus(ret_decay_f[0].astype(F32))
    lg_b = -jax.nn.softplus(ret_decay_b[0].astype(F32))
    zero_state = jnp.zeros((2, b, RET_HEADS, RET_QK_DIM, RET_V_DIM), F32)
    cos_c = jnp.ones((n_ctx, RET_QK_DIM), F32)
    sin_c = jnp.zeros((n_ctx, RET_QK_DIM), F32)
    _, _, s_ctx = _retention(proj_c, cos_c, sin_c, lg_f, lg_b, zero_state, b, n_ctx, False)
    o_f, o_b, _ = _retention(proj, cos_r, sin_r, lg_f, lg_b, s_ctx, b, seq, True)

    k_all, vt_all = _kvprep(proj_c, proj, attn_k_norm, cos_a, sin_a, b, n_ctx, seq)
    attn_o = _attention(proj, attn_q_norm, cos_a, sin_a, k_all, vt_all, b, seq)

    x1 = _merge(x2, lat(g1), o_f, o_b, proj, attn_o, w_ret_o[0].astype(BF16), w_attn_o[0].astype(BF16),
                w_out[0].astype(BF16), seq)

    h2, top_i, top_w = _router(x1, norm2, lat(sc2), lat(sh2), w_router[0].T, b_router[0].reshape(N_EXPERTS, 1), seq)
    n_groups = MOE_GROUPS if b % MOE_GROUPS == 0 else 1
    tgrp = t // n_groups
    tg = _largest_tile(TOP_K * tgrp, FFN_ROWS, SUBLANES)
    w_tok = top_w.T
    out = None
    for grp in range(n_groups):
        first = grp * tgrp
        dest, tile_expert, tile_flag, tile_rows = _plan(top_i[:, first:first + tgrp], tg)
        xs = _scatter_rows(h2, dest, TOP_K * tgrp + N_EXPERTS * tg, first, tgrp)
        ys = _ffn(xs, tile_expert, tile_flag, tile_rows, w1, b1, w2, b2, tg)
        yk = _gather_rows(ys, dest).reshape(TOP_K, tgrp, PACK_W)
        out = _combine(x1, lat(g2), w_tok, yk, seq, first, out)
    return out.reshape(b, seq, d)
```

```python
import functools

import jax
import jax.numpy as jnp
import numpy as np
from jax import lax
from jax.experimental import pallas as pl
from jax.experimental.pallas import tpu as pltpu
from jax.experimental.pallas import tpu_sc as plsc

F32 = jnp.float32
BF16 = jnp.bfloat16

D_MODEL = 1024
GRID_W = 64
EPS = 1e-6
RET_HEADS = 4
RET_QK_DIM = 256
RET_V_DIM = 512
ATTN_HEADS = 8
ATTN_KV_HEADS = 2
ATTN_GROUP = ATTN_HEADS // ATTN_KV_HEADS
ATTN_HEAD_DIM = 128
ROPE_THETA = 10000.0
N_EXPERTS = 32
TOP_K = 4
EXPERT_FF = 1024
SWIGLU_LIMIT = 7.0
SWIGLU_ALPHA = 1.702

RET_QK_W = RET_HEADS * RET_QK_DIM
RET_V_W = RET_HEADS * RET_V_DIM
ATTN_Q_W = ATTN_HEADS * ATTN_HEAD_DIM
ATTN_KV_W = ATTN_KV_HEADS * ATTN_HEAD_DIM
IN_SIZES = (RET_QK_W, RET_QK_W, RET_V_W, RET_V_W, ATTN_Q_W, ATTN_KV_W, ATTN_KV_W, D_MODEL, D_MODEL)
IN_WIDTH = sum(IN_SIZES)
COL_QR = 0
COL_KR = COL_QR + RET_QK_W
COL_VR = COL_KR + RET_QK_W
COL_GR = COL_VR + RET_V_W
COL_QA = COL_GR + RET_V_W
COL_GATE_R = COL_QA + ATTN_Q_W
COL_GATE_A = COL_GATE_R + D_MODEL
COL_KA = COL_GATE_A + D_MODEL
COL_VA = COL_KA + ATTN_KV_W

RET_CHUNK = 256
RET_HEADS_PER_STEP = 4
LANES = 128
SUBLANES = 8
SC_CORES = 2
SC_SUBCORES = 16
SC_GATHER_ROWS = 128
MXU_DEPTH = 256
ADA_COLS = 1536
INPROJ_ROWS = 1024
INPROJ_COLS = 2560
MERGE_ROWS = 512
ROUTER_ROWS = 2048
COMBINE_ROWS = 1024
FFN_ROWS = 512
ATTN_KEY_CHUNK = 768
ATTN_Q_TILE = 1024
ATTN_EXP_SLAB = 32
ONES_ROWS = 16
KV_PREP_ROWS = 256
PACK_DTYPE = jnp.int32
PACK_W = D_MODEL // 2
MOE_GROUPS = 2
PLAN_BLOCK = 512
TILE_PAD, TILE_BODY, TILE_FIRST = 0, 1, 2
VMEM_LIMIT = 56 * 1024 * 1024

ARB = pltpu.ARBITRARY


def _params(n_axes, **kw):
    return pltpu.CompilerParams(dimension_semantics=(ARB,) * n_axes, vmem_limit_bytes=VMEM_LIMIT, **kw)


def _largest_tile(n, cap, mult):
    best = None
    for t in range(mult, min(n, cap) + 1, mult):
        if n % t == 0:
            best = t
    assert best is not None, (n, cap, mult)
    return best


def _rms(x):
    return x * lax.rsqrt(jnp.mean(x * x, axis=-1, keepdims=True) + EPS)


def _pack_pairs(x):
    half = x.shape[1] // 2
    lo = lax.bitcast_convert_type(x[:, :half].astype(BF16).astype(F32), jnp.int32)
    hi = lax.bitcast_convert_type(x[:, half:].astype(BF16).astype(F32), jnp.int32)
    return lax.bitwise_or(lax.bitwise_and(hi, jnp.int32(-65536)), lax.shift_right_logical(lo, jnp.int32(16)))


def _unpack_pairs(w):
    lo = lax.bitcast_convert_type(lax.shift_left(w, jnp.int32(16)), F32)
    hi = lax.bitcast_convert_type(lax.bitwise_and(w, jnp.int32(-65536)), F32)
    return lo, hi


def _rope_block(x, cos, sin, half):
    if 2 * half == LANES:
        swapped = pltpu.roll(x, half, 1)
    else:
        lane = lax.broadcasted_iota(jnp.int32, x.shape, 1)
        first = (lane % (2 * half)) < half
        swapped = jnp.where(first, pltpu.roll(x, LANES - half, 1), pltpu.roll(x, half, 1))
    return x * cos + swapped * sin


def _rope_tables(rows, head_dim):
    n_freq = head_dim // 4
    inv_freq = ROPE_THETA ** (-jnp.arange(n_freq, dtype=F32) / n_freq)
    ang_r = jnp.arange(rows, dtype=F32)[:, None] * inv_freq
    ang_c = jnp.arange(GRID_W, dtype=F32)[:, None] * inv_freq
    per_row = lambda a: jnp.broadcast_to(a[:, None, :], (rows, GRID_W, n_freq)).reshape(rows * GRID_W, n_freq)
    per_col = lambda a: jnp.broadcast_to(a[None, :, :], (rows, GRID_W, n_freq)).reshape(rows * GRID_W, n_freq)
    cos_r, sin_r = per_row(jnp.cos(ang_r)), per_row(jnp.sin(ang_r))
    cos_c, sin_c = per_col(jnp.cos(ang_c)), per_col(jnp.sin(ang_c))
    cos = jnp.concatenate([cos_r, cos_r, cos_c, cos_c], axis=1)
    sin = jnp.concatenate([-sin_r, sin_r, -sin_c, sin_c], axis=1)
    return cos, sin


def _ada_kernel(c_ref, w_ref, b_ref, o_ref):
    c = c_ref[...]
    s = c * jax.nn.sigmoid(c)
    o_ref[...] = jnp.dot(s, w_ref[...], preferred_element_type=F32,
                         precision=lax.Precision.HIGHEST) + b_ref[...]


def _ada(c_pad, w_ada, b_ada):
    rows = c_pad.shape[0]
    n = w_ada.shape[1]
    tn = _largest_tile(n, ADA_COLS, LANES)
    return pl.pallas_call(
        _ada_kernel,
        grid=(n // tn,),
        in_specs=[pl.BlockSpec((rows, D_MODEL), lambda j: (0, 0)),
                  pl.BlockSpec((D_MODEL, tn), lambda j: (0, j)),
                  pl.BlockSpec((1, tn), lambda j: (0, j))],
        out_specs=pl.BlockSpec((rows, tn), lambda j: (0, j)),
        out_shape=jax.ShapeDtypeStruct((rows, n), F32),
        compiler_params=_params(1),
        name="ada",
    )(c_pad, w_ada, b_ada.reshape(1, n))


def _inproj_kernel(x_ref, n_ref, sc_ref, sh_ref, w_ref, o_ref, w_sc):
    @pl.when(pl.program_id(1) == 0)
    def _():
        w_sc[...] = w_ref[...].astype(BF16)

    y = _rms(x_ref[...]) * n_ref[...]
    h = (y * (1.0 + sc_ref[0]) + sh_ref[0]).astype(BF16)
    o_ref[...] = jnp.dot(h, w_sc[...], preferred_element_type=F32).astype(o_ref.dtype)


def _inproj(x2, norm, sc, sh, w, rows_per_batch):
    t = x2.shape[0]
    n = w.shape[1]
    tm = _largest_tile(rows_per_batch, INPROJ_ROWS, SUBLANES)
    tn = _largest_tile(n, INPROJ_COLS, LANES)
    per_b = rows_per_batch // tm
    if sc.shape[0] == 1:
        mod_map = lambda j, i: (0, 0, 0)
    else:
        mod_map = lambda j, i: (i // per_b, 0, 0)
    return pl.pallas_call(
        _inproj_kernel,
        grid=(n // tn, t // tm),
        in_specs=[pl.BlockSpec((tm, D_MODEL), lambda j, i: (i, 0)),
                  pl.BlockSpec((1, D_MODEL), lambda j, i: (0, 0)),
                  pl.BlockSpec((1, 1, D_MODEL), mod_map),
                  pl.BlockSpec((1, 1, D_MODEL), mod_map),
                  pl.BlockSpec((D_MODEL, tn), lambda j, i: (0, j))],
        out_specs=pl.BlockSpec((tm, tn), lambda j, i: (i, j)),
        out_shape=jax.ShapeDtypeStruct((t, n), BF16),
        scratch_shapes=[pltpu.VMEM((D_MODEL, tn), BF16)],
        compiler_params=_params(2),
        name="inproj",
    )(x2, norm, sc, sh, w)


def _kvprep_kernel(kc_ref, vc_ref, kl_ref, vl_ref, g_ref, cos_ref, sin_ref, k_ref, vt_ref, *, ctx_blocks):
    j = pl.program_id(1)
    vrows = ATTN_HEAD_DIM + ONES_ROWS

    def emit(kin_ref, vin_ref, use_rope):
        k = kin_ref[...].astype(F32)
        v = vin_ref[...].astype(F32)
        for g in range(ATTN_KV_HEADS):
            sl = slice(g * ATTN_HEAD_DIM, (g + 1) * ATTN_HEAD_DIM)
            kh = _rms(k[:, sl]) * g_ref[...]
            if use_rope:
                kh = _rope_block(kh, cos_ref[...], sin_ref[...], ATTN_HEAD_DIM // 4)
            k_ref[0, g] = kh.astype(BF16)
            vt_ref[0, g * vrows:g * vrows + ATTN_HEAD_DIM, :] = v[:, sl].T.astype(BF16)
            vt_ref[0, g * vrows + ATTN_HEAD_DIM:(g + 1) * vrows, :] = jnp.ones((ONES_ROWS, v.shape[0]), BF16)

    @pl.when(j < ctx_blocks)
    def _():
        emit(kc_ref, vc_ref, False)

    @pl.when(j >= ctx_blocks)
    def _():
        emit(kl_ref, vl_ref, True)


def _kvprep(proj_c, proj, gain, cos, sin, batch, n_ctx, seq):
    tm = KV_PREP_ROWS
    assert n_ctx % tm == 0 and seq % tm == 0
    cb, lb = n_ctx // tm, seq // tm
    lk = n_ctx + seq
    ctx_row = lambda b, j: b * cb + jnp.minimum(j, cb - 1)
    lat_blk = lambda j: jnp.maximum(j - cb, 0)
    lat_row = lambda b, j: b * lb + lat_blk(j)
    vrows = ATTN_KV_HEADS * (ATTN_HEAD_DIM + ONES_ROWS)
    return pl.pallas_call(
        functools.partial(_kvprep_kernel, ctx_blocks=cb),
        grid=(batch, cb + lb),
        in_specs=[pl.BlockSpec((tm, ATTN_KV_W), lambda b, j: (ctx_row(b, j), COL_KA // ATTN_KV_W)),
                  pl.BlockSpec((tm, ATTN_KV_W), lambda b, j: (ctx_row(b, j), COL_VA // ATTN_KV_W)),
                  pl.BlockSpec((tm, ATTN_KV_W), lambda b, j: (lat_row(b, j), COL_KA // ATTN_KV_W)),
                  pl.BlockSpec((tm, ATTN_KV_W), lambda b, j: (lat_row(b, j), COL_VA // ATTN_KV_W)),
                  pl.BlockSpec((1, ATTN_HEAD_DIM), lambda b, j: (0, 0)),
                  pl.BlockSpec((tm, ATTN_HEAD_DIM), lambda b, j: (lat_blk(j), 0)),
                  pl.BlockSpec((tm, ATTN_HEAD_DIM), lambda b, j: (lat_blk(j), 0))],
        out_specs=[pl.BlockSpec((1, ATTN_KV_HEADS, tm, ATTN_HEAD_DIM), lambda b, j: (b, 0, j, 0)),
                   pl.BlockSpec((1, vrows, tm), lambda b, j: (b, 0, j))],
        out_shape=[jax.ShapeDtypeStruct((batch, ATTN_KV_HEADS, lk, ATTN_HEAD_DIM), BF16),
                   jax.ShapeDtypeStruct((batch, vrows, lk), BF16)],
        compiler_params=_params(2),
        name="kvprep",
    )(proj_c, proj_c, proj, proj, gain, cos, sin)


def _ret_kernel(dch_ref, qf_ref, kf_ref, vf_ref, cosf_ref, sinf_ref, qb_ref, kb_ref, vb_ref, cosb_ref, sinb_ref,
                dmat_ref, din_ref, dout_ref, s0_ref, of_ref, ob_ref, sfin_ref, s_sc, *, use_rope, n_chunks, heads):
    hb = pl.program_id(1)
    c = pl.program_id(2)

    @pl.when(c == 0)
    def _():
        s_sc[...] = s0_ref[:, 0]

    def rope(x, cos, sin):
        return jnp.concatenate([_rope_block(x[:, i * LANES:(i + 1) * LANES], cos[:, i * LANES:(i + 1) * LANES],
                                            sin[:, i * LANES:(i + 1) * LANES], RET_QK_DIM // 4)
                                for i in range(RET_QK_DIM // LANES)], axis=1)

    sides = ((qf_ref, kf_ref, vf_ref, cosf_ref, sinf_ref), (qb_ref, kb_ref, vb_ref, cosb_ref, sinb_ref))
    chains = [(d, j) for d in range(2) for j in range(heads)]
    qs, ks, vs, ss = {}, {}, {}, {}
    for d, j in chains:
        q_ref, k_ref, v_ref, cos_ref, sin_ref = sides[d]
        q = q_ref[:, j * RET_QK_DIM:(j + 1) * RET_QK_DIM].astype(F32)
        k = k_ref[:, j * RET_QK_DIM:(j + 1) * RET_QK_DIM].astype(F32)
        if use_rope:
            q = rope(q, cos_ref[...], sin_ref[...])
            k = rope(k, cos_ref[...], sin_ref[...])
        qs[d, j], ks[d, j] = q, k
        vs[d, j] = v_ref[:, j * RET_V_DIM:(j + 1) * RET_V_DIM]
        ss[d, j] = s_sc[d, j]
    scores = {ch: lax.dot_general(qs[ch].astype(BF16), ks[ch].astype(BF16), (((1,), (1,)), ((), ())),
                                  preferred_element_type=F32) * dmat_ref[ch[0], ch[1]] for ch in chains}
    inter = {ch: jnp.dot((qs[ch] * din_ref[ch[0], ch[1]]).astype(BF16), ss[ch].astype(BF16),
                         preferred_element_type=F32) for ch in chains}
    intra = {ch: jnp.dot(scores[ch].astype(BF16), vs[ch], preferred_element_type=F32) for ch in chains}
    o_refs = (of_ref, ob_ref)
    for d, j in chains:
        o_refs[d][:, j * RET_V_DIM:(j + 1) * RET_V_DIM] = (intra[d, j] + inter[d, j]).astype(of_ref.dtype)
    s_new = {ch: ss[ch] * dch_ref[ch[0] * RET_HEADS + hb * heads + ch[1]]
             + jnp.dot((ks[ch] * dout_ref[ch[0], ch[1]]).T.astype(BF16), vs[ch], preferred_element_type=F32)
             for ch in chains}
    for d, j in chains:
        s_sc[d, j] = s_new[d, j]

    @pl.when(c == n_chunks - 1)
    def _():
        for d, j in chains:
            sfin_ref[d, 0, j] = s_new[d, j]


def _ret_tables(log_gamma, chunk, reverse, k_scale):
    pos = jnp.arange(chunk, dtype=F32)
    diff = pos[:, None] - pos[None, :]
    if reverse:
        diff = -diff
        mask = diff > 0
        p_in = chunk - pos
        p_out = pos
    else:
        mask = diff >= 0
        p_in = pos + 1.0
        p_out = chunk - 1.0 - pos
    lg = log_gamma.astype(F32)
    dmat = jnp.where(mask[None], jnp.exp(lg[:, None, None] * jnp.maximum(diff, 0.0)[None]), 0.0) * k_scale
    d_in = jnp.exp(lg[:, None] * p_in)
    d_out = jnp.exp(lg[:, None] * p_out) * k_scale
    d_in = jnp.broadcast_to(d_in[:, :, None], (RET_HEADS, chunk, RET_QK_DIM))
    d_out = jnp.broadcast_to(d_out[:, :, None], (RET_HEADS, chunk, RET_QK_DIM))
    d_chunk = jnp.exp(lg * chunk)
    return d_chunk, dmat, d_in, d_out


def _retention(proj, cos, sin, lg_f, lg_b, s0, batch, seq, use_rope):
    chunk = min(RET_CHUNK, seq)
    nc = seq // chunk
    hs = RET_HEADS_PER_STEP
    qw, vw = hs * RET_QK_DIM, hs * RET_V_DIM
    k_scale = RET_QK_DIM ** -0.5
    tabs = [_ret_tables(lg_f, chunk, False, k_scale), _ret_tables(lg_b, chunk, True, k_scale)]
    d_chunk, dmat, d_in, d_out = [jnp.stack([tabs[0][i], tabs[1][i]]) for i in range(4)]
    fwd = lambda c: c
    bwd = lambda c: nc - 1 - c

    def side(cidx):
        row = lambda b, c: b * nc + cidx(c)
        return [pl.BlockSpec((chunk, qw), lambda b, h, c: (row(b, c), COL_QR // qw + h)),
                pl.BlockSpec((chunk, qw), lambda b, h, c: (row(b, c), COL_KR // qw + h)),
                pl.BlockSpec((chunk, vw), lambda b, h, c: (row(b, c), COL_VR // vw + h)),
                pl.BlockSpec((chunk, RET_QK_DIM), lambda b, h, c: (cidx(c), 0)),
                pl.BlockSpec((chunk, RET_QK_DIM), lambda b, h, c: (cidx(c), 0))]

    state_spec = pl.BlockSpec((2, 1, hs, RET_QK_DIM, RET_V_DIM), lambda b, h, c: (0, b, h, 0, 0))
    return pl.pallas_call(
        functools.partial(_ret_kernel, use_rope=use_rope, n_chunks=nc, heads=hs),
        grid=(batch, RET_HEADS // hs, nc),
        in_specs=[pl.BlockSpec(memory_space=pltpu.SMEM)] + side(fwd) + side(bwd) + [
            pl.BlockSpec((2, hs, chunk, chunk), lambda b, h, c: (0, h, 0, 0)),
            pl.BlockSpec((2, hs, chunk, RET_QK_DIM), lambda b, h, c: (0, h, 0, 0)),
            pl.BlockSpec((2, hs, chunk, RET_QK_DIM), lambda b, h, c: (0, h, 0, 0)),
            state_spec],
        out_specs=[pl.BlockSpec((chunk, vw), lambda b, h, c: (b * nc + c, h)),
                   pl.BlockSpec((chunk, vw), lambda b, h, c: (b * nc + nc - 1 - c, h)),
                   state_spec],
        out_shape=[jax.ShapeDtypeStruct((batch * seq, RET_V_W), BF16),
                   jax.ShapeDtypeStruct((batch * seq, RET_V_W), BF16),
                   jax.ShapeDtypeStruct((2, batch, RET_HEADS, RET_QK_DIM, RET_V_DIM), F32)],
        scratch_shapes=[pltpu.VMEM((2, hs, RET_QK_DIM, RET_V_DIM), F32)],
        compiler_params=_params(3),
        name="ret",
    )(d_chunk.reshape(-1), proj, proj, proj, cos, sin, proj, proj, proj, cos, sin, dmat, d_in, d_out, s0)


def _attn_kernel(q_ref, g_ref, cos_ref, sin_ref, k_ref, vt_ref, o_ref, qt_sc, m_sc, acc_sc,
                 s0_sc, s1_sc, p0_sc, p1_sc, a0_sc, a1_sc, *, tq, kc, n_chunks):
    s_bufs = (s0_sc, s1_sc)
    p_bufs = (p0_sc, p1_sc)
    a_bufs = (a0_sc, a1_sc)

    q = q_ref[...].astype(F32)
    scale = ATTN_HEAD_DIM ** -0.5 * np.log2(np.e)
    for g in range(ATTN_GROUP):
        qh = _rms(q[:, g * ATTN_HEAD_DIM:(g + 1) * ATTN_HEAD_DIM]) * g_ref[...]
        qh = _rope_block(qh, cos_ref[...], sin_ref[...], ATTN_HEAD_DIM // 4) * scale
        qt_sc[:, g * tq:(g + 1) * tq] = qh.T.astype(BF16)
    m_sc[...] = jnp.full(m_sc.shape, -jnp.inf, F32)
    acc_sc[...] = jnp.zeros(acc_sc.shape, F32)

    def stage_s(c, slot):
        off = pl.multiple_of(c * kc, kc)
        s_bufs[slot][...] = jnp.dot(k_ref[0, 0, pl.ds(off, kc), :], qt_sc[...], preferred_element_type=F32)

    def stage_f(slot):
        m_prev = m_sc[...]
        m_new = jnp.maximum(m_prev, jnp.max(s_bufs[slot][...], axis=0, keepdims=True))
        m_sc[...] = m_new
        a_bufs[slot][...] = jnp.exp2(m_prev - m_new)
        for r in range(0, kc, ATTN_EXP_SLAB):
            p_bufs[slot][r:r + ATTN_EXP_SLAB, :] = jnp.exp2(
                s_bufs[slot][r:r + ATTN_EXP_SLAB, :] - m_new).astype(BF16)

    def stage_a(c, slot):
        off = pl.multiple_of(c * kc, kc)
        acc_sc[...] = a_bufs[slot][...] * acc_sc[...] + jnp.dot(
            vt_ref[0, :, pl.ds(off, kc)], p_bufs[slot][...], preferred_element_type=F32)

    def tick(t, parity):
        stage_s(t, parity)
        stage_f(1 - parity)
        stage_a(t - 2, parity)

    n = n_chunks
    stage_s(0, 0)
    if n > 1:
        stage_s(1, 1)
    stage_f(0)
    first = 2
    if n > 2 and (n - 2) % 2 == 1:
        tick(2, 0)
        first = 3
    n_pairs = (n - first) // 2 if n > first else 0
    if n_pairs > 0:
        def pair(u, carry):
            t = first + 2 * u
            tick(t, first % 2)
            tick(t + 1, 1 - first % 2)
            return carry
        lax.fori_loop(0, n_pairs, pair, 0)
    if n > 1:
        stage_f((n - 1) % 2)
        stage_a(n - 2, (n - 2) % 2)
    stage_a(n - 1, (n - 1) % 2)

    acc = acc_sc[...]
    o = acc[:ATTN_HEAD_DIM] / acc[ATTN_HEAD_DIM:ATTN_HEAD_DIM + 1]
    for g in range(ATTN_GROUP):
        o_ref[:, g * ATTN_HEAD_DIM:(g + 1) * ATTN_HEAD_DIM] = o[:, g * tq:(g + 1) * tq].T.astype(o_ref.dtype)


def _attention(proj, gain, cos, sin, k_all, vt_all, batch, seq):
    lk = k_all.shape[2]
    kc = _largest_tile(lk, ATTN_KEY_CHUNK, MXU_DEPTH)
    tq = _largest_tile(seq, ATTN_Q_TILE, SUBLANES)
    nq = seq // tq
    qw = ATTN_GROUP * ATTN_HEAD_DIM
    cols = ATTN_GROUP * tq
    vrows = ATTN_HEAD_DIM + ONES_ROWS
    return pl.pallas_call(
        functools.partial(_attn_kernel, tq=tq, kc=kc, n_chunks=lk // kc),
        grid=(batch, ATTN_KV_HEADS, nq),
        in_specs=[pl.BlockSpec((tq, qw), lambda b, g, i: (b * nq + i, COL_QA // qw + g)),
                  pl.BlockSpec((1, ATTN_HEAD_DIM), lambda b, g, i: (0, 0)),
                  pl.BlockSpec((tq, ATTN_HEAD_DIM), lambda b, g, i: (i, 0)),
                  pl.BlockSpec((tq, ATTN_HEAD_DIM), lambda b, g, i: (i, 0)),
                  pl.BlockSpec((1, 1, lk, ATTN_HEAD_DIM), lambda b, g, i: (b, g, 0, 0), pipeline_mode=pl.Buffered(1)),
                  pl.BlockSpec((1, vrows, lk), lambda b, g, i: (b, g, 0), pipeline_mode=pl.Buffered(1))],
        out_specs=pl.BlockSpec((tq, qw), lambda b, g, i: (b * nq + i, g)),
        out_shape=jax.ShapeDtypeStruct((batch * seq, ATTN_Q_W), BF16),
        scratch_shapes=[pltpu.VMEM((ATTN_HEAD_DIM, cols), BF16),
                        pltpu.VMEM((1, cols), F32),
                        pltpu.VMEM((vrows, cols), F32),
                        pltpu.VMEM((kc, cols), F32),
                        pltpu.VMEM((kc, cols), F32),
                        pltpu.VMEM((kc, cols), BF16),
                        pltpu.VMEM((kc, cols), BF16),
                        pltpu.VMEM((1, cols), F32),
                        pltpu.VMEM((1, cols), F32)],
        compiler_params=_params(3),
        name="attn",
    )(proj, gain, cos, sin, k_all, vt_all)


def _merge_kernel(x_ref, g1_ref, of_ref, ob_ref, gr_ref, ao_ref, gtr_ref, gta_ref,
                  wro_ref, wao_ref, wo_ref, o_ref):
    ro = of_ref[...].astype(F32) + ob_ref[...].astype(F32)
    gr = gr_ref[...].astype(F32)
    parts = []
    for h in range(RET_HEADS):
        sl = slice(h * RET_V_DIM, (h + 1) * RET_V_DIM)
        g = gr[:, sl]
        parts.append((g * jax.nn.sigmoid(g) * _rms(ro[:, sl])).astype(BF16))
    ret_in = jnp.concatenate(parts, axis=1)
    ret_branch = jnp.dot(ret_in, wro_ref[...], preferred_element_type=F32)
    attn_branch = jnp.dot(ao_ref[...], wao_ref[...], preferred_element_type=F32)
    y = (jax.nn.sigmoid(gtr_ref[...].astype(F32)) * ret_branch
         + jax.nn.sigmoid(gta_ref[...].astype(F32)) * attn_branch)
    y = jnp.dot(y.astype(BF16), wo_ref[...], preferred_element_type=F32)
    o_ref[...] = x_ref[...] + g1_ref[0] * y


def _merge(x2, g1, o_f, o_b, proj, attn_o, w_ret_o, w_attn_o, w_out, seq):
    t = x2.shape[0]
    tm = _largest_tile(seq, MERGE_ROWS, SUBLANES)
    per_b = seq // tm
    full = lambda i: (0, 0)
    return pl.pallas_call(
        _merge_kernel,
        grid=(t // tm,),
        in_specs=[pl.BlockSpec((tm, D_MODEL), lambda i: (i, 0)),
                  pl.BlockSpec((1, 1, D_MODEL), lambda i: (i // per_b, 0, 0)),
                  pl.BlockSpec((tm, RET_V_W), lambda i: (i, 0)),
                  pl.BlockSpec((tm, RET_V_W), lambda i: (i, 0)),
                  pl.BlockSpec((tm, RET_V_W), lambda i: (i, COL_GR // RET_V_W)),
                  pl.BlockSpec((tm, ATTN_Q_W), lambda i: (i, 0)),
                  pl.BlockSpec((tm, D_MODEL), lambda i: (i, COL_GATE_R // D_MODEL)),
                  pl.BlockSpec((tm, D_MODEL), lambda i: (i, COL_GATE_A // D_MODEL)),
                  pl.BlockSpec((RET_V_W, D_MODEL), full, pipeline_mode=pl.Buffered(1)),
                  pl.BlockSpec((ATTN_Q_W, D_MODEL), full, pipeline_mode=pl.Buffered(1)),
                  pl.BlockSpec((D_MODEL, D_MODEL), full, pipeline_mode=pl.Buffered(1))],
        out_specs=pl.BlockSpec((tm, D_MODEL), lambda i: (i, 0)),
        out_shape=jax.ShapeDtypeStruct((t, D_MODEL), F32),
        compiler_params=_params(1),
        name="merge",
    )(x2, g1, o_f, o_b, proj, attn_o, proj, proj, w_ret_o, w_attn_o, w_out)


def _router_kernel(x_ref, n_ref, sc_ref, sh_ref, wr_ref, br_ref, h_ref, ti_ref, tw_ref):
    h = _rms(x_ref[...]) * n_ref[...] * (1.0 + sc_ref[0]) + sh_ref[0]
    h_ref[...] = _pack_pairs(h)
    nt = (((1,), (1,)), ((), ()))
    w = wr_ref[...]
    h_hi = h.astype(BF16)
    h_lo = (h - h_hi.astype(F32)).astype(BF16)
    w_hi = w.astype(BF16)
    w_lo = (w - w_hi.astype(F32)).astype(BF16)
    logits = (lax.dot_general(w_hi, h_hi, nt, preferred_element_type=F32)
              + lax.dot_general(w_lo, h_hi, nt, preferred_element_type=F32)
              + lax.dot_general(w_hi, h_lo, nt, preferred_element_type=F32)) + br_ref[...]
    eid = lax.broadcasted_iota(jnp.int32, logits.shape, 0)
    vals = logits
    top_v = []
    top_i = []
    for _ in range(TOP_K):
        m = jnp.max(vals, axis=0, keepdims=True)
        idx = jnp.min(jnp.where(vals == m, eid, N_EXPERTS), axis=0, keepdims=True)
        top_v.append(m)
        top_i.append(idx)
        vals = jnp.where(eid == idx, -jnp.inf, vals)
    ex = [jnp.exp(v - top_v[0]) for v in top_v]
    denom = ex[0] + ex[1] + ex[2] + ex[3]
    ti_ref[...] = jnp.concatenate(top_i, axis=0)
    tw_ref[...] = jnp.concatenate([e / denom for e in ex], axis=0)


def _router(x1, norm, sc, sh, w_router_t, b_router, seq):
    t = x1.shape[0]
    tm = _largest_tile(seq, ROUTER_ROWS, LANES)
    per_b = seq // tm
    mod_map = lambda i: (i // per_b, 0, 0)
    return pl.pallas_call(
        _router_kernel,
        grid=(t // tm,),
        in_specs=[pl.BlockSpec((tm, D_MODEL), lambda i: (i, 0)),
                  pl.BlockSpec((1, D_MODEL), lambda i: (0, 0)),
                  pl.BlockSpec((1, 1, D_MODEL), mod_map),
                  pl.BlockSpec((1, 1, D_MODEL), mod_map),
                  pl.BlockSpec((N_EXPERTS, D_MODEL), lambda i: (0, 0)),
                  pl.BlockSpec((N_EXPERTS, 1), lambda i: (0, 0))],
        out_specs=[pl.BlockSpec((tm, PACK_W), lambda i: (i, 0)),
                   pl.BlockSpec((TOP_K, tm), lambda i: (0, i)),
                   pl.BlockSpec((TOP_K, tm), lambda i: (0, i))],
        out_shape=[jax.ShapeDtypeStruct((t, PACK_W), PACK_DTYPE),
                   jax.ShapeDtypeStruct((TOP_K, t), jnp.int32),
                   jax.ShapeDtypeStruct((TOP_K, t), F32)],
        compiler_params=_params(1),
        name="router",
    )(x1, norm, sc, sh, w_router_t, b_router)


def _gather_rows(src, idx):
    n, w = src.shape
    r = idx.shape[0]
    workers = SC_CORES * SC_SUBCORES
    per_w = r // workers
    n_win = per_w // SC_GATHER_ROWS
    assert per_w * workers == r and n_win * SC_GATHER_ROWS == per_w, (r, workers, SC_GATHER_ROWS)
    mesh = plsc.VectorSubcoreMesh(core_axis_name="c", subcore_axis_name="s")

    @functools.partial(
        pl.kernel, mesh=mesh, out_type=jax.ShapeDtypeStruct((r, w), src.dtype),
        scratch_types=[pltpu.VMEM((SC_GATHER_ROWS,), jnp.int32),
                       pltpu.VMEM((SC_GATHER_ROWS, w), src.dtype),
                       pltpu.SemaphoreType.DMA])
    def gather(src_hbm, idx_hbm, out_hbm, idx_v, rows_v, sem):
        wid = lax.axis_index("s") * SC_CORES + lax.axis_index("c")
        base = wid * per_w

        @pl.loop(0, n_win)
        def _(win):
            off = base + win * SC_GATHER_ROWS
            pltpu.sync_copy(idx_hbm.at[pl.ds(off, SC_GATHER_ROWS)], idx_v)
            pltpu.async_copy(src_hbm.at[idx_v], rows_v, sem).wait()
            pltpu.sync_copy(rows_v, out_hbm.at[pl.ds(off, SC_GATHER_ROWS)])

    return gather(src, idx)


def _scatter_rows(src, dest, n_out, first, group_rows):
    n, w = src.shape
    n_assign = dest.shape[0]
    workers = SC_CORES * SC_SUBCORES
    per_w = n_assign // workers
    n_win = per_w // SC_GATHER_ROWS
    assert per_w * workers == n_assign and n_win * SC_GATHER_ROWS == per_w and group_rows % per_w == 0
    mesh = plsc.VectorSubcoreMesh(core_axis_name="c", subcore_axis_name="s")

    @functools.partial(
        pl.kernel, mesh=mesh, out_type=jax.ShapeDtypeStruct((n_out, w), src.dtype),
        scratch_types=[pltpu.VMEM((SC_GATHER_ROWS,), jnp.int32),
                       pltpu.VMEM((SC_GATHER_ROWS, w), src.dtype),
                       pltpu.SemaphoreType.DMA])
    def scatter(src_hbm, dest_hbm, out_hbm, idx_v, rows_v, sem):
        wid = lax.axis_index("s") * SC_CORES + lax.axis_index("c")
        base = wid * per_w
        row_base = first + lax.rem(base, group_rows)

        @pl.loop(0, n_win)
        def _(win):
            pltpu.sync_copy(dest_hbm.at[pl.ds(base + win * SC_GATHER_ROWS, SC_GATHER_ROWS)], idx_v)
            pltpu.sync_copy(src_hbm.at[pl.ds(row_base + win * SC_GATHER_ROWS, SC_GATHER_ROWS)], rows_v)
            pltpu.async_copy(rows_v, out_hbm.at[idx_v], sem).wait()

    return scatter(src, dest)


def _ffn_kernel(te_ref, tf_ref, tr_ref, ts_ref, tn_ref, tp_ref, x_ref, w1_hbm, b1_ref, w2_hbm, b2_ref, o_ref,
                w1_sc, w2_sc, w1_st, w2_st, sem):
    j = pl.program_id(0)
    flag = tf_ref[j]
    n_rows = tr_ref[j]

    def weight_copies(expert, slot):
        return (pltpu.make_async_copy(w1_hbm.at[expert], w1_st.at[slot], sem.at[0, slot]),
                pltpu.make_async_copy(w2_hbm.at[expert], w2_st.at[slot], sem.at[1, slot]))

    @pl.when(flag == TILE_FIRST)
    def _():
        slot = ts_ref[j]

        @pl.when(tp_ref[j] != 0)
        def _():
            for cp in weight_copies(te_ref[j], slot):
                cp.start()

        for cp in weight_copies(te_ref[j], slot):
            cp.wait()
        w1_sc[...] = w1_st[slot].astype(BF16)
        w2_sc[...] = w2_st[slot].astype(BF16)

        @pl.when(tn_ref[j] >= 0)
        def _():
            for cp in weight_copies(tn_ref[j], 1 - slot):
                cp.start()

    @pl.when(flag == TILE_PAD)
    def _():
        o_ref[...] = jnp.zeros(o_ref.shape, o_ref.dtype)

    @pl.when(flag != TILE_PAD)
    def _():
        xw = x_ref[...]
        rid = lax.broadcasted_iota(jnp.int32, xw.shape, 0)
        lo, hi = _unpack_pairs(jnp.where(rid < n_rows, xw, jnp.zeros_like(xw)))
        x = jnp.concatenate([lo, hi], axis=1).astype(BF16)
        a = jnp.dot(x, w1_sc[...], preferred_element_type=F32) + b1_ref[0]
        gate = jnp.minimum(a[:, :EXPERT_FF], SWIGLU_LIMIT)
        up = jnp.clip(a[:, EXPERT_FF:], -SWIGLU_LIMIT, SWIGLU_LIMIT)
        act = gate * jax.nn.sigmoid(SWIGLU_ALPHA * gate) * (up + 1.0)
        y = jnp.dot(act.astype(BF16), w2_sc[...], preferred_element_type=F32) + b2_ref[0]
        o_ref[...] = _pack_pairs(y)


def _ffn(xs, tile_expert, tile_flag, tile_rows, w1, b1, w2, b2, tg):
    p = xs.shape[0]
    n_tiles = p // tg
    tile_slot, tile_next, tile_prime = _ffn_ring(tile_expert, tile_flag)
    row_map = lambda j, te, tf, tr, ts, tn, tp: (j, 0)
    bias_map = lambda j, te, tf, tr, ts, tn, tp: (te[j], 0, 0)
    grid_spec = pltpu.PrefetchScalarGridSpec(
        num_scalar_prefetch=6,
        grid=(n_tiles,),
        in_specs=[pl.BlockSpec((tg, PACK_W), row_map),
                  pl.BlockSpec(memory_space=pl.ANY),
                  pl.BlockSpec((1, 1, 2 * EXPERT_FF), bias_map),
                  pl.BlockSpec(memory_space=pl.ANY),
                  pl.BlockSpec((1, 1, D_MODEL), bias_map)],
        out_specs=pl.BlockSpec((tg, PACK_W), row_map),
        scratch_shapes=[pltpu.VMEM((D_MODEL, 2 * EXPERT_FF), BF16),
                        pltpu.VMEM((EXPERT_FF, D_MODEL), BF16),
                        pltpu.VMEM((2, D_MODEL, 2 * EXPERT_FF), F32),
                        pltpu.VMEM((2, EXPERT_FF, D_MODEL), F32),
                        pltpu.SemaphoreType.DMA((2, 2))],
    )
    return pl.pallas_call(
        _ffn_kernel,
        grid_spec=grid_spec,
        out_shape=jax.ShapeDtypeStruct((p, PACK_W), PACK_DTYPE),
        compiler_params=_params(1),
        name="ffn",
    )(tile_expert, tile_flag, tile_rows, tile_slot, tile_next, tile_prime, xs, w1, b1, w2, b2)


def _ffn_ring(tile_expert, tile_flag):
    n_tiles = tile_flag.shape[0]
    first = tile_flag == TILE_FIRST
    ordinal = jnp.cumsum(first.astype(jnp.int32)) - 1
    tile_slot = jnp.where(first, ordinal % 2, 0).astype(jnp.int32)
    tile_prime = (first & (ordinal == 0)).astype(jnp.int32)
    tiles = jnp.arange(n_tiles, dtype=jnp.int32)
    first_at_or_after = lax.cummin(jnp.where(first, tiles, n_tiles)[::-1], axis=0)[::-1]
    next_first = jnp.concatenate([first_at_or_after[1:], jnp.full((1,), n_tiles, jnp.int32)])
    tile_next = jnp.where(first & (next_first < n_tiles),
                          tile_expert[jnp.minimum(next_first, n_tiles - 1)], -1).astype(jnp.int32)
    return tile_slot, tile_next, tile_prime


def _combine_kernel(x_ref, g2_ref, w_ref, y_ref, o_ref):
    w = w_ref[...]
    acc_lo = None
    for k in range(TOP_K):
        lo, hi = _unpack_pairs(y_ref[k])
        wk = w[:, k:k + 1]
        acc_lo = wk * lo if acc_lo is None else acc_lo + wk * lo
        acc_hi = wk * hi if k == 0 else acc_hi + wk * hi
    acc = jnp.concatenate([acc_lo, acc_hi], axis=1)
    o_ref[...] = x_ref[...] + g2_ref[0] * acc


def _combine_into_kernel(x_ref, g2_ref, w_ref, y_ref, prev_ref, o_ref):
    del prev_ref
    _combine_kernel(x_ref, g2_ref, w_ref, y_ref, o_ref)


def _combine(x1, g2, w_tok, yk, seq, first_row, prev):
    t = x1.shape[0]
    rows = yk.shape[1]
    tm = _largest_tile(seq, COMBINE_ROWS, SUBLANES)
    per_b = seq // tm
    i0 = first_row // tm
    in_specs = [pl.BlockSpec((tm, D_MODEL), lambda i: (i0 + i, 0)),
                pl.BlockSpec((1, 1, D_MODEL), lambda i: ((i0 + i) // per_b, 0, 0)),
                pl.BlockSpec((tm, TOP_K), lambda i: (i0 + i, 0)),
                pl.BlockSpec((TOP_K, tm, PACK_W), lambda i: (0, i, 0))]
    args = [x1, g2, w_tok, yk]
    if prev is not None:
        in_specs.append(pl.BlockSpec(memory_space=pl.ANY))
        args.append(prev)
    return pl.pallas_call(
        _combine_kernel if prev is None else _combine_into_kernel,
        grid=(rows // tm,),
        in_specs=in_specs,
        out_specs=pl.BlockSpec((tm, D_MODEL), lambda i: (i0 + i, 0)),
        out_shape=jax.ShapeDtypeStruct((t, D_MODEL), F32),
        input_output_aliases={} if prev is None else {len(args) - 1: 0},
        compiler_params=_params(1),
        name="combine",
    )(*args)


def _count_kernel(ti_ref, cnt_ref):
    @pl.when(pl.program_id(0) == 0)
    def _():
        cnt_ref[...] = jnp.zeros(cnt_ref.shape, F32)

    bt = ti_ref.shape[1]
    eid = lax.broadcasted_iota(jnp.int32, (N_EXPERTS, bt), 0)
    acc = jnp.zeros(cnt_ref.shape, F32)
    for k in range(TOP_K):
        m = (eid == ti_ref[k:k + 1, :]).astype(F32)
        for c in range(bt // LANES):
            acc = acc + m[:, c * LANES:(c + 1) * LANES]
    cnt_ref[...] += acc


def _rank_kernel(ti_ref, off_ref, tri_ref, dest_ref, run_sc):
    @pl.when(pl.program_id(0) == 0)
    def _():
        run_sc[...] = off_ref[...] - 1.0

    bt = ti_ref.shape[1]
    eid = lax.broadcasted_iota(jnp.int32, (N_EXPERTS, bt), 0)
    run = run_sc[...]
    for k in range(TOP_K):
        m = eid == ti_ref[k:k + 1, :]
        pre = jnp.dot(jnp.where(m, 1.0, 0.0).astype(BF16), tri_ref[...], preferred_element_type=F32)
        slot = jnp.sum(jnp.where(m, pre + run, 0.0), axis=0, keepdims=True)
        dest_ref[k:k + 1, :] = slot.astype(jnp.int32)
        run = run + pre[:, bt - 1:bt]
    run_sc[...] = run


def _plan(top_i, tg):
    t = top_i.shape[1]
    n_assign = TOP_K * t
    p = n_assign + N_EXPERTS * tg
    n_tiles = p // tg
    bt = _largest_tile(t, PLAN_BLOCK, LANES)
    cnt = pl.pallas_call(
        _count_kernel,
        grid=(t // bt,),
        in_specs=[pl.BlockSpec((TOP_K, bt), lambda i: (0, i))],
        out_specs=pl.BlockSpec((N_EXPERTS, LANES), lambda i: (0, 0)),
        out_shape=jax.ShapeDtypeStruct((N_EXPERTS, LANES), F32),
        compiler_params=_params(1),
        name="count",
    )(top_i)
    counts = jnp.sum(cnt, axis=1).astype(jnp.int32)
    padded = ((counts + tg - 1) // tg) * tg
    off_end = jnp.cumsum(padded)
    off = off_end - padded
    tri = (jnp.arange(bt)[:, None] <= jnp.arange(bt)[None, :]).astype(BF16)
    dest = pl.pallas_call(
        _rank_kernel,
        grid=(t // bt,),
        in_specs=[pl.BlockSpec((TOP_K, bt), lambda i: (0, i)),
                  pl.BlockSpec((N_EXPERTS, 1), lambda i: (0, 0)),
                  pl.BlockSpec((bt, bt), lambda i: (0, 0))],
        out_specs=pl.BlockSpec((TOP_K, bt), lambda i: (0, i)),
        out_shape=jax.ShapeDtypeStruct((TOP_K, t), jnp.int32),
        scratch_shapes=[pltpu.VMEM((N_EXPERTS, 1), F32)],
        compiler_params=_params(1),
        name="rank",
    )(top_i, off.astype(F32).reshape(N_EXPERTS, 1), tri)
    tile_start = jnp.arange(n_tiles, dtype=jnp.int32) * tg
    tile_valid = tile_start < off_end[-1]
    te = jnp.sum((tile_start[:, None] >= off_end[None, :]).astype(jnp.int32), axis=1)
    last_e = jnp.sum(((off_end[-1] - 1) >= off_end).astype(jnp.int32))
    tile_expert = jnp.where(tile_valid, te, last_e)
    tile_rows = jnp.where(tile_valid, jnp.clip(counts[tile_expert] - (tile_start - off[tile_expert]), 0, tg), 0)
    changed = jnp.concatenate([jnp.ones((1,), jnp.bool_), tile_expert[1:] != tile_expert[:-1]])
    tile_flag = jnp.where(tile_valid, jnp.where(changed, TILE_FIRST, TILE_BODY), TILE_PAD).astype(jnp.int32)
    return dest.reshape(-1), tile_expert, tile_flag, tile_rows.astype(jnp.int32)


def kernel(x, c, ctx, c_ctx, norm1, norm2, w_ada, b_ada, w_in, ret_decay_f, ret_decay_b, attn_q_norm, attn_k_norm,
           w_ret_o, w_attn_o, w_out, w_router, b_router, w_exp_in, b_exp_in, w_exp_out, b_exp_out):
    assert w_in.shape[0] == 1, "single-layer block"
    b, seq, d = x.shape
    n_ctx = ctx.shape[1]
    t = b * seq
    rows = seq // GRID_W

    idx = np.cumsum(IN_SIZES)[:-1].tolist()
    wq_r, wk_r, wv_r, wg_r, wq_a, wk_a, wv_a, wgt_r, wgt_a = jnp.split(w_in[0], idx, axis=-1)
    w_in_p = jnp.concatenate([wq_r, wk_r, wv_r, wg_r, wq_a, wgt_r, wgt_a, wk_a, wv_a], axis=-1)
    w1 = w_exp_in[0]
    w2 = w_exp_out[0]
    b1 = b_exp_in[0].reshape(N_EXPERTS, 1, 2 * EXPERT_FF)
    b2 = b_exp_out[0].reshape(N_EXPERTS, 1, D_MODEL)

    pad = (-(b + 1)) % SUBLANES
    c_all = jnp.concatenate([c, c_ctx[None, :], jnp.zeros((pad, d), F32)], axis=0)
    mod = _ada(c_all, w_ada[0], b_ada[0])
    sh1, sc1, g1, sh2, sc2, g2 = [m.reshape(-1, 1, d) for m in jnp.split(mod, 6, axis=-1)]
    lat = lambda m: m[:b]
    cx = lambda m: m[b:b + 1]

    x2 = x.reshape(t, d)
    proj = _inproj(x2, norm1, lat(sc1), lat(sh1), w_in_p, seq)
    proj_c = _inproj(ctx.reshape(b * n_ctx, d), norm1, cx(sc1), cx(sh1), w_in_p, n_ctx)

    cos_r, sin_r = _rope_tables(rows, RET_QK_DIM)
    cos_a, sin_a = _rope_tables(rows, ATTN_HEAD_DIM)
    lg_f = -jax.nn.softplus(ret_decay_f[0].astype(F32))
    lg_b = -jax.nn.softplus(ret_decay_b[0].astype(F32))
    zero_state = jnp.zeros((2, b, RET_HEADS, RET_QK_DIM, RET_V_DIM), F32)
    cos_c = jnp.ones((n_ctx, RET_QK_DIM), F32)
    sin_c = jnp.zeros((n_ctx, RET_QK_DIM), F32)
    _, _, s_ctx = _retention(proj_c, cos_c, sin_c, lg_f, lg_b, zero_state, b, n_ctx, False)
    o_f, o_b, _ = _retention(proj, cos_r, sin_r, lg_f, lg_b, s_ctx, b, seq, True)

    k_all, vt_all = _kvprep(proj_c, proj, attn_k_norm, cos_a, sin_a, b, n_ctx, seq)
    attn_o = _attention(proj, attn_q_norm, cos_a, sin_a, k_all, vt_all, b, seq)

    x1 = _merge(x2, lat(g1), o_f, o_b, proj, attn_o, w_ret_o[0].astype(BF16), w_attn_o[0].astype(BF16),
                w_out[0].astype(BF16), seq)

    h2, top_i, top_w = _router(x1, norm2, lat(sc2), lat(sh2), w_router[0].T, b_router[0].reshape(N_EXPERTS, 1), seq)
    n_groups = MOE_GROUPS if b % MOE_GROUPS == 0 else 1
    tgrp = t // n_groups
    tg = _largest_tile(TOP_K * tgrp, FFN_ROWS, SUBLANES)
    w_tok = top_w.T
    out = None
    for grp in range(n_groups):
        first = grp * tgrp
        dest, tile_expert, tile_flag, tile_rows = _plan(top_i[:, first:first + tgrp], tg)
        xs = _scatter_rows(h2, dest, TOP_K * tgrp + N_EXPERTS * tg, first, tgrp)
        ys = _ffn(xs, tile_expert, tile_flag, tile_rows, w1, b1, w2, b2, tg)
        yk = _gather_rows(ys, dest).reshape(TOP_K, tgrp, PACK_W)
        out = _combine(x1, lat(g2), w_tok, yk, seq, first, out)
    return out.reshape(b, seq, d)
```

```python
import functools

import jax
import jax.numpy as jnp
import numpy as np
from jax import lax
from jax.experimental import pallas as pl
from jax.experimental.pallas import tpu as pltpu
from jax.experimental.pallas import tpu_sc as plsc

F32 = jnp.float32
BF16 = jnp.bfloat16

D_MODEL = 1024
GRID_W = 64
EPS = 1e-6
RET_HEADS = 4
RET_QK_DIM = 256
RET_V_DIM = 512
ATTN_HEADS = 8
ATTN_KV_HEADS = 2
ATTN_GROUP = ATTN_HEADS // ATTN_KV_HEADS
ATTN_HEAD_DIM = 128
ROPE_THETA = 10000.0
N_EXPERTS = 32
TOP_K = 4
EXPERT_FF = 1024
SWIGLU_LIMIT = 7.0
SWIGLU_ALPHA = 1.702

RET_QK_W = RET_HEADS * RET_QK_DIM
RET_V_W = RET_HEADS * RET_V_DIM
ATTN_Q_W = ATTN_HEADS * ATTN_HEAD_DIM
ATTN_KV_W = ATTN_KV_HEADS * ATTN_HEAD_DIM
IN_SIZES = (RET_QK_W, RET_QK_W, RET_V_W, RET_V_W, ATTN_Q_W, ATTN_KV_W, ATTN_KV_W, D_MODEL, D_MODEL)
IN_WIDTH = sum(IN_SIZES)
COL_QR = 0
COL_KR = COL_QR + RET_QK_W
COL_VR = COL_KR + RET_QK_W
COL_GR = COL_VR + RET_V_W
COL_QA = COL_GR + RET_V_W
COL_GATE_R = COL_QA + ATTN_Q_W
COL_GATE_A = COL_GATE_R + D_MODEL
COL_KA = COL_GATE_A + D_MODEL
COL_VA = COL_KA + ATTN_KV_W

RET_CHUNK = 256
RET_HEADS_PER_STEP = 4
LANES = 128
SUBLANES = 8
SC_CORES = 2
SC_SUBCORES = 16
SC_GATHER_ROWS = 128
MXU_DEPTH = 256
ADA_COLS = 1536
INPROJ_ROWS = 1024
INPROJ_COLS = 2560
MERGE_ROWS = 512
ROUTER_ROWS = 2048
COMBINE_ROWS = 1024
FFN_ROWS = 512
ATTN_KEY_CHUNK = 768
ATTN_Q_TILE = 1024
ATTN_EXP_SLAB = 32
ONES_ROWS = 16
KV_PREP_ROWS = 256
PACK_DTYPE = jnp.int32
PACK_W = D_MODEL // 2
MOE_GROUPS = 2
PLAN_BLOCK = 512
TILE_PAD, TILE_BODY, TILE_FIRST = 0, 1, 2
VMEM_LIMIT = 56 * 1024 * 1024

ARB = pltpu.ARBITRARY


def _params(n_axes, **kw):
    return pltpu.CompilerParams(dimension_semantics=(ARB,) * n_axes, vmem_limit_bytes=VMEM_LIMIT, **kw)


def _largest_tile(n, cap, mult):
    best = None
    for t in range(mult, min(n, cap) + 1, mult):
        if n % t == 0:
            best = t
    assert best is not None, (n, cap, mult)
    return best


def _rms(x):
    return x * lax.rsqrt(jnp.mean(x * x, axis=-1, keepdims=True) + EPS)


def _pack_pairs(x):
    half = x.shape[1] // 2
    lo = lax.bitcast_convert_type(x[:, :half].astype(BF16).astype(F32), jnp.int32)
    hi = lax.bitcast_convert_type(x[:, half:].astype(BF16).astype(F32), jnp.int32)
    return lax.bitwise_or(lax.bitwise_and(hi, jnp.int32(-65536)), lax.shift_right_logical(lo, jnp.int32(16)))


def _unpack_pairs(w):
    lo = lax.bitcast_convert_type(lax.shift_left(w, jnp.int32(16)), F32)
    hi = lax.bitcast_convert_type(lax.bitwise_and(w, jnp.int32(-65536)), F32)
    return lo, hi


def _rope_block(x, cos, sin, half):
    if 2 * half == LANES:
        swapped = pltpu.roll(x, half, 1)
    else:
        lane = lax.broadcasted_iota(jnp.int32, x.shape, 1)
        first = (lane % (2 * half)) < half
        swapped = jnp.where(first, pltpu.roll(x, LANES - half, 1), pltpu.roll(x, half, 1))
    return x * cos + swapped * sin


def _rope_tables(rows, head_dim):
    n_freq = head_dim // 4
    inv_freq = ROPE_THETA ** (-jnp.arange(n_freq, dtype=F32) / n_freq)
    ang_r = jnp.arange(rows, dtype=F32)[:, None] * inv_freq
    ang_c = jnp.arange(GRID_W, dtype=F32)[:, None] * inv_freq
    per_row = lambda a: jnp.broadcast_to(a[:, None, :], (rows, GRID_W, n_freq)).reshape(rows * GRID_W, n_freq)
    per_col = lambda a: jnp.broadcast_to(a[None, :, :], (rows, GRID_W, n_freq)).reshape(rows * GRID_W, n_freq)
    cos_r, sin_r = per_row(jnp.cos(ang_r)), per_row(jnp.sin(ang_r))
    cos_c, sin_c = per_col(jnp.cos(ang_c)), per_col(jnp.sin(ang_c))
    cos = jnp.concatenate([cos_r, cos_r, cos_c, cos_c], axis=1)
    sin = jnp.concatenate([-sin_r, sin_r, -sin_c, sin_c], axis=1)
    return cos, sin


def _ada_kernel(c_ref, w_ref, b_ref, o_ref):
    c = c_ref[...]
    s = c * jax.nn.sigmoid(c)
    o_ref[...] = jnp.dot(s, w_ref[...], preferred_element_type=F32,
                         precision=lax.Precision.HIGHEST) + b_ref[...]


def _ada(c_pad, w_ada, b_ada):
    rows = c_pad.shape[0]
    n = w_ada.shape[1]
    tn = _largest_tile(n, ADA_COLS, LANES)
    return pl.pallas_call(
        _ada_kernel,
        grid=(n // tn,),
        in_specs=[pl.BlockSpec((rows, D_MODEL), lambda j: (0, 0)),
                  pl.BlockSpec((D_MODEL, tn), lambda j: (0, j)),
                  pl.BlockSpec((1, tn), lambda j: (0, j))],
        out_specs=pl.BlockSpec((rows, tn), lambda j: (0, j)),
        out_shape=jax.ShapeDtypeStruct((rows, n), F32),
        compiler_params=_params(1),
        name="ada",
    )(c_pad, w_ada, b_ada.reshape(1, n))


def _inproj_kernel(x_ref, n_ref, sc_ref, sh_ref, w_ref, o_ref, w_sc):
    @pl.when(pl.program_id(1) == 0)
    def _():
        w_sc[...] = w_ref[...].astype(BF16)

    y = _rms(x_ref[...]) * n_ref[...]
    h = (y * (1.0 + sc_ref[0]) + sh_ref[0]).astype(BF16)
    o_ref[...] = jnp.dot(h, w_sc[...], preferred_element_type=F32).astype(o_ref.dtype)


def _inproj(x2, norm, sc, sh, w, rows_per_batch):
    t = x2.shape[0]
    n = w.shape[1]
    tm = _largest_tile(rows_per_batch, INPROJ_ROWS, SUBLANES)
    tn = _largest_tile(n, INPROJ_COLS, LANES)
    per_b = rows_per_batch // tm
    if sc.shape[0] == 1:
        mod_map = lambda j, i: (0, 0, 0)
    else:
        mod_map = lambda j, i: (i // per_b, 0, 0)
    return pl.pallas_call(
        _inproj_kernel,
        grid=(n // tn, t // tm),
        in_specs=[pl.BlockSpec((tm, D_MODEL), lambda j, i: (i, 0)),
                  pl.BlockSpec((1, D_MODEL), lambda j, i: (0, 0)),
                  pl.BlockSpec((1, 1, D_MODEL), mod_map),
                  pl.BlockSpec((1, 1, D_MODEL), mod_map),
                  pl.BlockSpec((D_MODEL, tn), lambda j, i: (0, j))],
        out_specs=pl.BlockSpec((tm, tn), lambda j, i: (i, j)),
        out_shape=jax.ShapeDtypeStruct((t, n), BF16),
        scratch_shapes=[pltpu.VMEM((D_MODEL, tn), BF16)],
        compiler_params=_params(2),
        name="inproj",
    )(x2, norm, sc, sh, w)


def _kvprep_kernel(kc_ref, vc_ref, kl_ref, vl_ref, g_ref, cos_ref, sin_ref, k_ref, vt_ref, *, ctx_blocks):
    j = pl.program_id(1)
    vrows = ATTN_HEAD_DIM + ONES_ROWS

    def emit(kin_ref, vin_ref, use_rope):
        k = kin_ref[...].astype(F32)
        v = vin_ref[...].astype(F32)
        for g in range(ATTN_KV_HEADS):
            sl = slice(g * ATTN_HEAD_DIM, (g + 1) * ATTN_HEAD_DIM)
            kh = _rms(k[:, sl]) * g_ref[...]
            if use_rope:
                kh = _rope_block(kh, cos_ref[...], sin_ref[...], ATTN_HEAD_DIM // 4)
            k_ref[0, g] = kh.astype(BF16)
            vt_ref[0, g * vrows:g * vrows + ATTN_HEAD_DIM, :] = v[:, sl].T.astype(BF16)
            vt_ref[0, g * vrows + ATTN_HEAD_DIM:(g + 1) * vrows, :] = jnp.ones((ONES_ROWS, v.shape[0]), BF16)

    @pl.when(j < ctx_blocks)
    def _():
        emit(kc_ref, vc_ref, False)

    @pl.when(j >= ctx_blocks)
    def _():
        emit(kl_ref, vl_ref, True)


def _kvprep(proj_c, proj, gain, cos, sin, batch, n_ctx, seq):
    tm = KV_PREP_ROWS
    assert n_ctx % tm == 0 and seq % tm == 0
    cb, lb = n_ctx // tm, seq // tm
    lk = n_ctx + seq
    ctx_row = lambda b, j: b * cb + jnp.minimum(j, cb - 1)
    lat_blk = lambda j: jnp.maximum(j - cb, 0)
    lat_row = lambda b, j: b * lb + lat_blk(j)
    vrows = ATTN_KV_HEADS * (ATTN_HEAD_DIM + ONES_ROWS)
    return pl.pallas_call(
        functools.partial(_kvprep_kernel, ctx_blocks=cb),
        grid=(batch, cb + lb),
        in_specs=[pl.BlockSpec((tm, ATTN_KV_W), lambda b, j: (ctx_row(b, j), COL_KA // ATTN_KV_W)),
                  pl.BlockSpec((tm, ATTN_KV_W), lambda b, j: (ctx_row(b, j), COL_VA // ATTN_KV_W)),
                  pl.BlockSpec((tm, ATTN_KV_W), lambda b, j: (lat_row(b, j), COL_KA // ATTN_KV_W)),
                  pl.BlockSpec((tm, ATTN_KV_W), lambda b, j: (lat_row(b, j), COL_VA // ATTN_KV_W)),
                  pl.BlockSpec((1, ATTN_HEAD_DIM), lambda b, j: (0, 0)),
                  pl.BlockSpec((tm, ATTN_HEAD_DIM), lambda b, j: (lat_blk(j), 0)),
                  pl.BlockSpec((tm, ATTN_HEAD_DIM), lambda b, j: (lat_blk(j), 0))],
        out_specs=[pl.BlockSpec((1, ATTN_KV_HEADS, tm, ATTN_HEAD_DIM), lambda b, j: (b, 0, j, 0)),
                   pl.BlockSpec((1, vrows, tm), lambda b, j: (b, 0, j))],
        out_shape=[jax.ShapeDtypeStruct((batch, ATTN_KV_HEADS, lk, ATTN_HEAD_DIM), BF16),
                   jax.ShapeDtypeStruct((batch, vrows, lk), BF16)],
        compiler_params=_params(2),
        name="kvprep",
    )(proj_c, proj_c, proj, proj, gain, cos, sin)


def _ret_kernel(dch_ref, qf_ref, kf_ref, vf_ref, cosf_ref, sinf_ref, qb_ref, kb_ref, vb_ref, cosb_ref, sinb_ref,
                dmat_ref, din_ref, dout_ref, s0_ref, of_ref, ob_ref, sfin_ref, s_sc, *, use_rope, n_chunks, heads):
    hb = pl.program_id(1)
    c = pl.program_id(2)

    @pl.when(c == 0)
    def _():
        s_sc[...] = s0_ref[:, 0]

    def rope(x, cos, sin):
        return jnp.concatenate([_rope_block(x[:, i * LANES:(i + 1) * LANES], cos[:, i * LANES:(i + 1) * LANES],
                                            sin[:, i * LANES:(i + 1) * LANES], RET_QK_DIM // 4)
                                for i in range(RET_QK_DIM // LANES)], axis=1)

    sides = ((qf_ref, kf_ref, vf_ref, cosf_ref, sinf_ref), (qb_ref, kb_ref, vb_ref, cosb_ref, sinb_ref))
    chains = [(d, j) for d in range(2) for j in range(heads)]
    qs, ks, vs, ss = {}, {}, {}, {}
    for d, j in chains:
        q_ref, k_ref, v_ref, cos_ref, sin_ref = sides[d]
        q = q_ref[:, j * RET_QK_DIM:(j + 1) * RET_QK_DIM].astype(F32)
        k = k_ref[:, j * RET_QK_DIM:(j + 1) * RET_QK_DIM].astype(F32)
        if use_rope:
            q = rope(q, cos_ref[...], sin_ref[...])
            k = rope(k, cos_ref[...], sin_ref[...])
        qs[d, j], ks[d, j] = q, k
        vs[d, j] = v_ref[:, j * RET_V_DIM:(j + 1) * RET_V_DIM]
        ss[d, j] = s_sc[d, j]
    scores = {ch: lax.dot_general(qs[ch].astype(BF16), ks[ch].astype(BF16), (((1,), (1,)), ((), ())),
                                  preferred_element_type=F32) * dmat_ref[ch[0], ch[1]] for ch in chains}
    inter = {ch: jnp.dot((qs[ch] * din_ref[ch[0], ch[1]]).astype(BF16), ss[ch].astype(BF16),
                         preferred_element_type=F32) for ch in chains}
    intra = {ch: jnp.dot(scores[ch].astype(BF16), vs[ch], preferred_element_type=F32) for ch in chains}
    o_refs = (of_ref, ob_ref)
    for d, j in chains:
        o_refs[d][:, j * RET_V_DIM:(j + 1) * RET_V_DIM] = (intra[d, j] + inter[d, j]).astype(of_ref.dtype)
    s_new = {ch: ss[ch] * dch_ref[ch[0] * RET_HEADS + hb * heads + ch[1]]
             + jnp.dot((ks[ch] * dout_ref[ch[0], ch[1]]).T.astype(BF16), vs[ch], preferred_element_type=F32)
             for ch in chains}
    for d, j in chains:
        s_sc[d, j] = s_new[d, j]

    @pl.when(c == n_chunks - 1)
    def _():
        for d, j in chains:
            sfin_ref[d, 0, j] = s_new[d, j]


def _ret_tables(log_gamma, chunk, reverse, k_scale):
    pos = jnp.arange(chunk, dtype=F32)
    diff = pos[:, None] - pos[None, :]
    if reverse:
        diff = -diff
        mask = diff > 0
        p_in = chunk - pos
        p_out = pos
    else:
        mask = diff >= 0
        p_in = pos + 1.0
        p_out = chunk - 1.0 - pos
    lg = log_gamma.astype(F32)
    dmat = jnp.where(mask[None], jnp.exp(lg[:, None, None] * jnp.maximum(diff, 0.0)[None]), 0.0) * k_scale
    d_in = jnp.exp(lg[:, None] * p_in)
    d_out = jnp.exp(lg[:, None] * p_out) * k_scale
    d_in = jnp.broadcast_to(d_in[:, :, None], (RET_HEADS, chunk, RET_QK_DIM))
    d_out = jnp.broadcast_to(d_out[:, :, None], (RET_HEADS, chunk, RET_QK_DIM))
    d_chunk = jnp.exp(lg * chunk)
    return d_chunk, dmat, d_in, d_out


def _retention(proj, cos, sin, lg_f, lg_b, s0, batch, seq, use_rope):
    chunk = min(RET_CHUNK, seq)
    nc = seq // chunk
    hs = RET_HEADS_PER_STEP
    qw, vw = hs * RET_QK_DIM, hs * RET_V_DIM
    k_scale = RET_QK_DIM ** -0.5
    tabs = [_ret_tables(lg_f, chunk, False, k_scale), _ret_tables(lg_b, chunk, True, k_scale)]
    d_chunk, dmat, d_in, d_out = [jnp.stack([tabs[0][i], tabs[1][i]]) for i in range(4)]
    fwd = lambda c: c
    bwd = lambda c: nc - 1 - c

    def side(cidx):
        row = lambda b, c: b * nc + cidx(c)
        return [pl.BlockSpec((chunk, qw), lambda b, h, c: (row(b, c), COL_QR // qw + h)),
                pl.BlockSpec((chunk, qw), lambda b, h, c: (row(b, c), COL_KR // qw + h)),
                pl.BlockSpec((chunk, vw), lambda b, h, c: (row(b, c), COL_VR // vw + h)),
                pl.BlockSpec((chunk, RET_QK_DIM), lambda b, h, c: (cidx(c), 0)),
                pl.BlockSpec((chunk, RET_QK_DIM), lambda b, h, c: (cidx(c), 0))]

    state_spec = pl.BlockSpec((2, 1, hs, RET_QK_DIM, RET_V_DIM), lambda b, h, c: (0, b, h, 0, 0))
    return pl.pallas_call(
        functools.partial(_ret_kernel, use_rope=use_rope, n_chunks=nc, heads=hs),
        grid=(batch, RET_HEADS // hs, nc),
        in_specs=[pl.BlockSpec(memory_space=pltpu.SMEM)] + side(fwd) + side(bwd) + [
            pl.BlockSpec((2, hs, chunk, chunk), lambda b, h, c: (0, h, 0, 0)),
            pl.BlockSpec((2, hs, chunk, RET_QK_DIM), lambda b, h, c: (0, h, 0, 0)),
            pl.BlockSpec((2, hs, chunk, RET_QK_DIM), lambda b, h, c: (0, h, 0, 0)),
            state_spec],
        out_specs=[pl.BlockSpec((chunk, vw), lambda b, h, c: (b * nc + c, h)),
                   pl.BlockSpec((chunk, vw), lambda b, h, c: (b * nc + nc - 1 - c, h)),
                   state_spec],
        out_shape=[jax.ShapeDtypeStruct((batch * seq, RET_V_W), BF16),
                   jax.ShapeDtypeStruct((batch * seq, RET_V_W), BF16),
                   jax.ShapeDtypeStruct((2, batch, RET_HEADS, RET_QK_DIM, RET_V_DIM), F32)],
        scratch_shapes=[pltpu.VMEM((2, hs, RET_QK_DIM, RET_V_DIM), F32)],
        compiler_params=_params(3),
        name="ret",
    )(d_chunk.reshape(-1), proj, proj, proj, cos, sin, proj, proj, proj, cos, sin, dmat, d_in, d_out, s0)


def _attn_kernel(q_ref, g_ref, cos_ref, sin_ref, k_ref, vt_ref, o_ref, qt_sc, m_sc, acc_sc,
                 s0_sc, s1_sc, p0_sc, p1_sc, a0_sc, a1_sc, x0_sc, x1_sc, *, tq, kc, n_chunks):
    s_bufs = (s0_sc, s1_sc)
    p_bufs = (p0_sc, p1_sc)
    a_bufs = (a0_sc, a1_sc)
    x_bufs = (x0_sc, x1_sc)

    q = q_ref[...].astype(F32)
    scale = ATTN_HEAD_DIM ** -0.5 * np.log2(np.e)
    for g in range(ATTN_GROUP):
        qh = _rms(q[:, g * ATTN_HEAD_DIM:(g + 1) * ATTN_HEAD_DIM]) * g_ref[...]
        qh = _rope_block(qh, cos_ref[...], sin_ref[...], ATTN_HEAD_DIM // 4) * scale
        qt_sc[:, g * tq:(g + 1) * tq] = qh.T.astype(BF16)
    m_sc[...] = jnp.full(m_sc.shape, -jnp.inf, F32)
    acc_sc[...] = jnp.zeros(acc_sc.shape, F32)

    def stage_s(c, slot):
        off = pl.multiple_of(c * kc, kc)
        s_t = jnp.dot(k_ref[0, 0, pl.ds(off, kc), :], qt_sc[...], preferred_element_type=F32)
        s_bufs[slot][...] = s_t
        x_bufs[slot][...] = jnp.max(s_t, axis=0, keepdims=True)

    def stage_f(slot):
        m_prev = m_sc[...]
        m_new = jnp.maximum(m_prev, x_bufs[slot][...])
        m_sc[...] = m_new
        a_bufs[slot][...] = jnp.exp2(m_prev - m_new)
        for r in range(0, kc, ATTN_EXP_SLAB):
            p_bufs[slot][r:r + ATTN_EXP_SLAB, :] = jnp.exp2(
                s_bufs[slot][r:r + ATTN_EXP_SLAB, :] - m_new).astype(BF16)

    def stage_a(c, slot):
        off = pl.multiple_of(c * kc, kc)
        acc_sc[...] = a_bufs[slot][...] * acc_sc[...] + jnp.dot(
            vt_ref[0, :, pl.ds(off, kc)], p_bufs[slot][...], preferred_element_type=F32)

    def tick(t, parity):
        stage_s(t, parity)
        stage_f(1 - parity)
        stage_a(t - 2, parity)

    n = n_chunks
    stage_s(0, 0)
    if n > 1:
        stage_s(1, 1)
    stage_f(0)
    first = 2
    if n > 2 and (n - 2) % 2 == 1:
        tick(2, 0)
        first = 3
    n_pairs = (n - first) // 2 if n > first else 0
    if n_pairs > 0:
        def pair(u, carry):
            t = first + 2 * u
            tick(t, first % 2)
            tick(t + 1, 1 - first % 2)
            return carry
        lax.fori_loop(0, n_pairs, pair, 0)
    if n > 1:
        stage_f((n - 1) % 2)
        stage_a(n - 2, (n - 2) % 2)
    stage_a(n - 1, (n - 1) % 2)

    acc = acc_sc[...]
    o = acc[:ATTN_HEAD_DIM] / acc[ATTN_HEAD_DIM:ATTN_HEAD_DIM + 1]
    for g in range(ATTN_GROUP):
        o_ref[:, g * ATTN_HEAD_DIM:(g + 1) * ATTN_HEAD_DIM] = o[:, g * tq:(g + 1) * tq].T.astype(o_ref.dtype)


def _attention(proj, gain, cos, sin, k_all, vt_all, batch, seq):
    lk = k_all.shape[2]
    kc = _largest_tile(lk, ATTN_KEY_CHUNK, MXU_DEPTH)
    tq = _largest_tile(seq, ATTN_Q_TILE, SUBLANES)
    nq = seq // tq
    qw = ATTN_GROUP * ATTN_HEAD_DIM
    cols = ATTN_GROUP * tq
    vrows = ATTN_HEAD_DIM + ONES_ROWS
    return pl.pallas_call(
        functools.partial(_attn_kernel, tq=tq, kc=kc, n_chunks=lk // kc),
        grid=(batch, ATTN_KV_HEADS, nq),
        in_specs=[pl.BlockSpec((tq, qw), lambda b, g, i: (b * nq + i, COL_QA // qw + g)),
                  pl.BlockSpec((1, ATTN_HEAD_DIM), lambda b, g, i: (0, 0)),
                  pl.BlockSpec((tq, ATTN_HEAD_DIM), lambda b, g, i: (i, 0)),
                  pl.BlockSpec((tq, ATTN_HEAD_DIM), lambda b, g, i: (i, 0)),
                  pl.BlockSpec((1, 1, lk, ATTN_HEAD_DIM), lambda b, g, i: (b, g, 0, 0), pipeline_mode=pl.Buffered(1)),
                  pl.BlockSpec((1, vrows, lk), lambda b, g, i: (b, g, 0), pipeline_mode=pl.Buffered(1))],
        out_specs=pl.BlockSpec((tq, qw), lambda b, g, i: (b * nq + i, g)),
        out_shape=jax.ShapeDtypeStruct((batch * seq, ATTN_Q_W), BF16),
        scratch_shapes=[pltpu.VMEM((ATTN_HEAD_DIM, cols), BF16),
                        pltpu.VMEM((1, cols), F32),
                        pltpu.VMEM((vrows, cols), F32),
                        pltpu.VMEM((kc, cols), F32),
                        pltpu.VMEM((kc, cols), F32),
                        pltpu.VMEM((kc, cols), BF16),
                        pltpu.VMEM((kc, cols), BF16),
                        pltpu.VMEM((1, cols), F32),
                        pltpu.VMEM((1, cols), F32),
                        pltpu.VMEM((1, cols), F32),
                        pltpu.VMEM((1, cols), F32)],
        compiler_params=_params(3),
        name="attn",
    )(proj, gain, cos, sin, k_all, vt_all)


def _merge_kernel(x_ref, g1_ref, of_ref, ob_ref, gr_ref, ao_ref, gtr_ref, gta_ref,
                  wro_ref, wao_ref, wo_ref, o_ref):
    ro = of_ref[...].astype(F32) + ob_ref[...].astype(F32)
    gr = gr_ref[...].astype(F32)
    parts = []
    for h in range(RET_HEADS):
        sl = slice(h * RET_V_DIM, (h + 1) * RET_V_DIM)
        g = gr[:, sl]
        parts.append((g * jax.nn.sigmoid(g) * _rms(ro[:, sl])).astype(BF16))
    ret_in = jnp.concatenate(parts, axis=1)
    ret_branch = jnp.dot(ret_in, wro_ref[...], preferred_element_type=F32)
    attn_branch = jnp.dot(ao_ref[...], wao_ref[...], preferred_element_type=F32)
    y = (jax.nn.sigmoid(gtr_ref[...].astype(F32)) * ret_branch
         + jax.nn.sigmoid(gta_ref[...].astype(F32)) * attn_branch)
    y = jnp.dot(y.astype(BF16), wo_ref[...], preferred_element_type=F32)
    o_ref[...] = x_ref[...] + g1_ref[0] * y


def _merge(x2, g1, o_f, o_b, proj, attn_o, w_ret_o, w_attn_o, w_out, seq):
    t = x2.shape[0]
    tm = _largest_tile(seq, MERGE_ROWS, SUBLANES)
    per_b = seq // tm
    full = lambda i: (0, 0)
    return pl.pallas_call(
        _merge_kernel,
        grid=(t // tm,),
        in_specs=[pl.BlockSpec((tm, D_MODEL), lambda i: (i, 0)),
                  pl.BlockSpec((1, 1, D_MODEL), lambda i: (i // per_b, 0, 0)),
                  pl.BlockSpec((tm, RET_V_W), lambda i: (i, 0)),
                  pl.BlockSpec((tm, RET_V_W), lambda i: (i, 0)),
                  pl.BlockSpec((tm, RET_V_W), lambda i: (i, COL_GR // RET_V_W)),
                  pl.BlockSpec((tm, ATTN_Q_W), lambda i: (i, 0)),
                  pl.BlockSpec((tm, D_MODEL), lambda i: (i, COL_GATE_R // D_MODEL)),
                  pl.BlockSpec((tm, D_MODEL), lambda i: (i, COL_GATE_A // D_MODEL)),
                  pl.BlockSpec((RET_V_W, D_MODEL), full, pipeline_mode=pl.Buffered(1)),
                  pl.BlockSpec((ATTN_Q_W, D_MODEL), full, pipeline_mode=pl.Buffered(1)),
                  pl.BlockSpec((D_MODEL, D_MODEL), full, pipeline_mode=pl.Buffered(1))],
        out_specs=pl.BlockSpec((tm, D_MODEL), lambda i: (i, 0)),
        out_shape=jax.ShapeDtypeStruct((t, D_MODEL), F32),
        compiler_params=_params(1),
        name="merge",
    )(x2, g1, o_f, o_b, proj, attn_o, proj, proj, w_ret_o, w_attn_o, w_out)


def _router_kernel(x_ref, n_ref, sc_ref, sh_ref, wr_ref, br_ref, h_ref, ti_ref, tw_ref):
    h = _rms(x_ref[...]) * n_ref[...] * (1.0 + sc_ref[0]) + sh_ref[0]
    h_ref[...] = _pack_pairs(h)
    nt = (((1,), (1,)), ((), ()))
    w = wr_ref[...]
    h_hi = h.astype(BF16)
    h_lo = (h - h_hi.astype(F32)).astype(BF16)
    w_hi = w.astype(BF16)
    w_lo = (w - w_hi.astype(F32)).astype(BF16)
    logits = (lax.dot_general(w_hi, h_hi, nt, preferred_element_type=F32)
              + lax.dot_general(w_lo, h_hi, nt, preferred_element_type=F32)
              + lax.dot_general(w_hi, h_lo, nt, preferred_element_type=F32)) + br_ref[...]
    eid = lax.broadcasted_iota(jnp.int32, logits.shape, 0)
    vals = logits
    top_v = []
    top_i = []
    for _ in range(TOP_K):
        m = jnp.max(vals, axis=0, keepdims=True)
        idx = jnp.min(jnp.where(vals == m, eid, N_EXPERTS), axis=0, keepdims=True)
        top_v.append(m)
        top_i.append(idx)
        vals = jnp.where(eid == idx, -jnp.inf, vals)
    ex = [jnp.exp(v - top_v[0]) for v in top_v]
    denom = ex[0] + ex[1] + ex[2] + ex[3]
    ti_ref[...] = jnp.concatenate(top_i, axis=0)
    tw_ref[...] = jnp.concatenate([e / denom for e in ex], axis=0)


def _router(x1, norm, sc, sh, w_router_t, b_router, seq):
    t = x1.shape[0]
    tm = _largest_tile(seq, ROUTER_ROWS, LANES)
    per_b = seq // tm
    mod_map = lambda i: (i // per_b, 0, 0)
    return pl.pallas_call(
        _router_kernel,
        grid=(t // tm,),
        in_specs=[pl.BlockSpec((tm, D_MODEL), lambda i: (i, 0)),
                  pl.BlockSpec((1, D_MODEL), lambda i: (0, 0)),
                  pl.BlockSpec((1, 1, D_MODEL), mod_map),
                  pl.BlockSpec((1, 1, D_MODEL), mod_map),
                  pl.BlockSpec((N_EXPERTS, D_MODEL), lambda i: (0, 0)),
                  pl.BlockSpec((N_EXPERTS, 1), lambda i: (0, 0))],
        out_specs=[pl.BlockSpec((tm, PACK_W), lambda i: (i, 0)),
                   pl.BlockSpec((TOP_K, tm), lambda i: (0, i)),
                   pl.BlockSpec((TOP_K, tm), lambda i: (0, i))],
        out_shape=[jax.ShapeDtypeStruct((t, PACK_W), PACK_DTYPE),
                   jax.ShapeDtypeStruct((TOP_K, t), jnp.int32),
                   jax.ShapeDtypeStruct((TOP_K, t), F32)],
        compiler_params=_params(1),
        name="router",
    )(x1, norm, sc, sh, w_router_t, b_router)


def _gather_rows(src, idx):
    n, w = src.shape
    r = idx.shape[0]
    workers = SC_CORES * SC_SUBCORES
    per_w = r // workers
    n_win = per_w // SC_GATHER_ROWS
    assert per_w * workers == r and n_win * SC_GATHER_ROWS == per_w, (r, workers, SC_GATHER_ROWS)
    mesh = plsc.VectorSubcoreMesh(core_axis_name="c", subcore_axis_name="s")

    @functools.partial(
        pl.kernel, mesh=mesh, out_type=jax.ShapeDtypeStruct((r, w), src.dtype),
        scratch_types=[pltpu.VMEM((SC_GATHER_ROWS,), jnp.int32),
                       pltpu.VMEM((SC_GATHER_ROWS, w), src.dtype),
                       pltpu.SemaphoreType.DMA])
    def gather(src_hbm, idx_hbm, out_hbm, idx_v, rows_v, sem):
        wid = lax.axis_index("s") * SC_CORES + lax.axis_index("c")
        base = wid * per_w

        @pl.loop(0, n_win)
        def _(win):
            off = base + win * SC_GATHER_ROWS
            pltpu.sync_copy(idx_hbm.at[pl.ds(off, SC_GATHER_ROWS)], idx_v)
            pltpu.async_copy(src_hbm.at[idx_v], rows_v, sem).wait()
            pltpu.sync_copy(rows_v, out_hbm.at[pl.ds(off, SC_GATHER_ROWS)])

    return gather(src, idx)


def _scatter_rows(src, dest, n_out, first, group_rows):
    n, w = src.shape
    n_assign = dest.shape[0]
    workers = SC_CORES * SC_SUBCORES
    per_w = n_assign // workers
    n_win = per_w // SC_GATHER_ROWS
    assert per_w * workers == n_assign and n_win * SC_GATHER_ROWS == per_w and group_rows % per_w == 0
    mesh = plsc.VectorSubcoreMesh(core_axis_name="c", subcore_axis_name="s")

    @functools.partial(
        pl.kernel, mesh=mesh, out_type=jax.ShapeDtypeStruct((n_out, w), src.dtype),
        scratch_types=[pltpu.VMEM((SC_GATHER_ROWS,), jnp.int32),
                       pltpu.VMEM((SC_GATHER_ROWS, w), src.dtype),
                       pltpu.SemaphoreType.DMA])
    def scatter(src_hbm, dest_hbm, out_hbm, idx_v, rows_v, sem):
        wid = lax.axis_index("s") * SC_CORES + lax.axis_index("c")
        base = wid * per_w
        row_base = first + lax.rem(base, group_rows)

        @pl.loop(0, n_win)
        def _(win):
            pltpu.sync_copy(dest_hbm.at[pl.ds(base + win * SC_GATHER_ROWS, SC_GATHER_ROWS)], idx_v)
            pltpu.sync_copy(src_hbm.at[pl.ds(row_base + win * SC_GATHER_ROWS, SC_GATHER_ROWS)], rows_v)
            pltpu.async_copy(rows_v, out_hbm.at[idx_v], sem).wait()

    return scatter(src, dest)


def _ffn_kernel(te_ref, tf_ref, tr_ref, ts_ref, tn_ref, tp_ref, x_ref, w1_hbm, b1_ref, w2_hbm, b2_ref, o_ref,
                w1_sc, w2_sc, w1_st, w2_st, sem):
    j = pl.program_id(0)
    flag = tf_ref[j]
    n_rows = tr_ref[j]

    def weight_copies(expert, slot):
        return (pltpu.make_async_copy(w1_hbm.at[expert], w1_st.at[slot], sem.at[0, slot]),
                pltpu.make_async_copy(w2_hbm.at[expert], w2_st.at[slot], sem.at[1, slot]))

    @pl.when(flag == TILE_FIRST)
    def _():
        slot = ts_ref[j]

        @pl.when(tp_ref[j] != 0)
        def _():
            for cp in weight_copies(te_ref[j], slot):
                cp.start()

        for cp in weight_copies(te_ref[j], slot):
            cp.wait()
        w1_sc[...] = w1_st[slot].astype(BF16)
        w2_sc[...] = w2_st[slot].astype(BF16)

        @pl.when(tn_ref[j] >= 0)
        def _():
            for cp in weight_copies(tn_ref[j], 1 - slot):
                cp.start()

    @pl.when(flag == TILE_PAD)
    def _():
        o_ref[...] = jnp.zeros(o_ref.shape, o_ref.dtype)

    @pl.when(flag != TILE_PAD)
    def _():
        xw = x_ref[...]
        rid = lax.broadcasted_iota(jnp.int32, xw.shape, 0)
        lo, hi = _unpack_pairs(jnp.where(rid < n_rows, xw, jnp.zeros_like(xw)))
        x = jnp.concatenate([lo, hi], axis=1).astype(BF16)
        a = jnp.dot(x, w1_sc[...], preferred_element_type=F32) + b1_ref[0]
        gate = jnp.minimum(a[:, :EXPERT_FF], SWIGLU_LIMIT)
        up = jnp.clip(a[:, EXPERT_FF:], -SWIGLU_LIMIT, SWIGLU_LIMIT)
        act = gate * jax.nn.sigmoid(SWIGLU_ALPHA * gate) * (up + 1.0)
        y = jnp.dot(act.astype(BF16), w2_sc[...], preferred_element_type=F32) + b2_ref[0]
        o_ref[...] = _pack_pairs(y)


def _ffn(xs, tile_expert, tile_flag, tile_rows, w1, b1, w2, b2, tg):
    p = xs.shape[0]
    n_tiles = p // tg
    tile_slot, tile_next, tile_prime = _ffn_ring(tile_expert, tile_flag)
    row_map = lambda j, te, tf, tr, ts, tn, tp: (j, 0)
    bias_map = lambda j, te, tf, tr, ts, tn, tp: (te[j], 0, 0)
    grid_spec = pltpu.PrefetchScalarGridSpec(
        num_scalar_prefetch=6,
        grid=(n_tiles,),
        in_specs=[pl.BlockSpec((tg, PACK_W), row_map),
                  pl.BlockSpec(memory_space=pl.ANY),
                  pl.BlockSpec((1, 1, 2 * EXPERT_FF), bias_map),
                  pl.BlockSpec(memory_space=pl.ANY),
                  pl.BlockSpec((1, 1, D_MODEL), bias_map)],
        out_specs=pl.BlockSpec((tg, PACK_W), row_map),
        scratch_shapes=[pltpu.VMEM((D_MODEL, 2 * EXPERT_FF), BF16),
                        pltpu.VMEM((EXPERT_FF, D_MODEL), BF16),
                        pltpu.VMEM((2, D_MODEL, 2 * EXPERT_FF), F32),
                        pltpu.VMEM((2, EXPERT_FF, D_MODEL), F32),
                        pltpu.SemaphoreType.DMA((2, 2))],
    )
    return pl.pallas_call(
        _ffn_kernel,
        grid_spec=grid_spec,
        out_shape=jax.ShapeDtypeStruct((p, PACK_W), PACK_DTYPE),
        compiler_params=_params(1),
        name="ffn",
    )(tile_expert, tile_flag, tile_rows, tile_slot, tile_next, tile_prime, xs, w1, b1, w2, b2)


def _ffn_ring(tile_expert, tile_flag):
    n_tiles = tile_flag.shape[0]
    first = tile_flag == TILE_FIRST
    ordinal = jnp.cumsum(first.astype(jnp.int32)) - 1
    tile_slot = jnp.where(first, ordinal % 2, 0).astype(jnp.int32)
    tile_prime = (first & (ordinal == 0)).astype(jnp.int32)
    tiles = jnp.arange(n_tiles, dtype=jnp.int32)
    first_at_or_after = lax.cummin(jnp.where(first, tiles, n_tiles)[::-1], axis=0)[::-1]
    next_first = jnp.concatenate([first_at_or_after[1:], jnp.full((1,), n_tiles, jnp.int32)])
    tile_next = jnp.where(first & (next_first < n_tiles),
                          tile_expert[jnp.minimum(next_first, n_tiles - 1)], -1).astype(jnp.int32)
    return tile_slot, tile_next, tile_prime


def _combine_kernel(x_ref, g2_ref, w_ref, y_ref, o_ref):
    w = w_ref[...]
    acc_lo = None
    for k in range(TOP_K):
        lo, hi = _unpack_pairs(y_ref[k])
        wk = w[:, k:k + 1]
        acc_lo = wk * lo if acc_lo is None else acc_lo + wk * lo
        acc_hi = wk * hi if k == 0 else acc_hi + wk * hi
    acc = jnp.concatenate([acc_lo, acc_hi], axis=1)
    o_ref[...] = x_ref[...] + g2_ref[0] * acc


def _combine_into_kernel(x_ref, g2_ref, w_ref, y_ref, prev_ref, o_ref):
    del prev_ref
    _combine_kernel(x_ref, g2_ref, w_ref, y_ref, o_ref)


def _combine(x1, g2, w_tok, yk, seq, first_row, prev):
    t = x1.shape[0]
    rows = yk.shape[1]
    tm = _largest_tile(seq, COMBINE_ROWS, SUBLANES)
    per_b = seq // tm
    i0 = first_row // tm
    in_specs = [pl.BlockSpec((tm, D_MODEL), lambda i: (i0 + i, 0)),
                pl.BlockSpec((1, 1, D_MODEL), lambda i: ((i0 + i) // per_b, 0, 0)),
                pl.BlockSpec((tm, TOP_K), lambda i: (i0 + i, 0)),
                pl.BlockSpec((TOP_K, tm, PACK_W), lambda i: (0, i, 0))]
    args = [x1, g2, w_tok, yk]
    if prev is not None:
        in_specs.append(pl.BlockSpec(memory_space=pl.ANY))
        args.append(prev)
    return pl.pallas_call(
        _combine_kernel if prev is None else _combine_into_kernel,
        grid=(rows // tm,),
        in_specs=in_specs,
        out_specs=pl.BlockSpec((tm, D_MODEL), lambda i: (i0 + i, 0)),
        out_shape=jax.ShapeDtypeStruct((t, D_MODEL), F32),
        input_output_aliases={} if prev is None else {len(args) - 1: 0},
        compiler_params=_params(1),
        name="combine",
    )(*args)


def _count_kernel(ti_ref, cnt_ref):
    @pl.when(pl.program_id(0) == 0)
    def _():
        cnt_ref[...] = jnp.zeros(cnt_ref.shape, F32)

    bt = ti_ref.shape[1]
    eid = lax.broadcasted_iota(jnp.int32, (N_EXPERTS, bt), 0)
    acc = jnp.zeros(cnt_ref.shape, F32)
    for k in range(TOP_K):
        m = (eid == ti_ref[k:k + 1, :]).astype(F32)
        for c in range(bt // LANES):
            acc = acc + m[:, c * LANES:(c + 1) * LANES]
    cnt_ref[...] += acc


def _rank_kernel(ti_ref, off_ref, tri_ref, dest_ref, run_sc):
    @pl.when(pl.program_id(0) == 0)
    def _():
        run_sc[...] = off_ref[...] - 1.0

    bt = ti_ref.shape[1]
    eid = lax.broadcasted_iota(jnp.int32, (N_EXPERTS, bt), 0)
    run = run_sc[...]
    for k in range(TOP_K):
        m = eid == ti_ref[k:k + 1, :]
        pre = jnp.dot(jnp.where(m, 1.0, 0.0).astype(BF16), tri_ref[...], preferred_element_type=F32)
        slot = jnp.sum(jnp.where(m, pre + run, 0.0), axis=0, keepdims=True)
        dest_ref[k:k + 1, :] = slot.astype(jnp.int32)
        run = run + pre[:, bt - 1:bt]
    run_sc[...] = run


def _plan(top_i, tg):
    t = top_i.shape[1]
    n_assign = TOP_K * t
    p = n_assign + N_EXPERTS * tg
    n_tiles = p // tg
    bt = _largest_tile(t, PLAN_BLOCK, LANES)
    cnt = pl.pallas_call(
        _count_kernel,
        grid=(t // bt,),
        in_specs=[pl.BlockSpec((TOP_K, bt), lambda i: (0, i))],
        out_specs=pl.BlockSpec((N_EXPERTS, LANES), lambda i: (0, 0)),
        out_shape=jax.ShapeDtypeStruct((N_EXPERTS, LANES), F32),
        compiler_params=_params(1),
        name="count",
    )(top_i)
    counts = jnp.sum(cnt, axis=1).astype(jnp.int32)
    padded = ((counts + tg - 1) // tg) * tg
    off_end = jnp.cumsum(padded)
    off = off_end - padded
    tri = (jnp.arange(bt)[:, None] <= jnp.arange(bt)[None, :]).astype(BF16)
    dest = pl.pallas_call(
        _rank_kernel,
        grid=(t // bt,),
        in_specs=[pl.BlockSpec((TOP_K, bt), lambda i: (0, i)),
                  pl.BlockSpec((N_EXPERTS, 1), lambda i: (0, 0)),
                  pl.BlockSpec((bt, bt), lambda i: (0, 0))],
        out_specs=pl.BlockSpec((TOP_K, bt), lambda i: (0, i)),
        out_shape=jax.ShapeDtypeStruct((TOP_K, t), jnp.int32),
        scratch_shapes=[pltpu.VMEM((N_EXPERTS, 1), F32)],
        compiler_params=_params(1),
        name="rank",
    )(top_i, off.astype(F32).reshape(N_EXPERTS, 1), tri)
    tile_start = jnp.arange(n_tiles, dtype=jnp.int32) * tg
    tile_valid = tile_start < off_end[-1]
    te = jnp.sum((tile_start[:, None] >= off_end[None, :]).astype(jnp.int32), axis=1)
    last_e = jnp.sum(((off_end[-1] - 1) >= off_end).astype(jnp.int32))
    tile_expert = jnp.where(tile_valid, te, last_e)
    tile_rows = jnp.where(tile_valid, jnp.clip(counts[tile_expert] - (tile_start - off[tile_expert]), 0, tg), 0)
    changed = jnp.concatenate([jnp.ones((1,), jnp.bool_), tile_expert[1:] != tile_expert[:-1]])
    tile_flag = jnp.where(tile_valid, jnp.where(changed, TILE_FIRST, TILE_BODY), TILE_PAD).astype(jnp.int32)
    return dest.reshape(-1), tile_expert, tile_flag, tile_rows.astype(jnp.int32)


def kernel(x, c, ctx, c_ctx, norm1, norm2, w_ada, b_ada, w_in, ret_decay_f, ret_decay_b, attn_q_norm, attn_k_norm,
           w_ret_o, w_attn_o, w_out, w_router, b_router, w_exp_in, b_exp_in, w_exp_out, b_exp_out):
    assert w_in.shape[0] == 1, "single-layer block"
    b, seq, d = x.shape
    n_ctx = ctx.shape[1]
    t = b * seq
    rows = seq // GRID_W

    idx = np.cumsum(IN_SIZES)[:-1].tolist()
    wq_r, wk_r, wv_r, wg_r, wq_a, wk_a, wv_a, wgt_r, wgt_a = jnp.split(w_in[0], idx, axis=-1)
    w_in_p = jnp.concatenate([wq_r, wk_r, wv_r, wg_r, wq_a, wgt_r, wgt_a, wk_a, wv_a], axis=-1)
    w1 = w_exp_in[0]
    w2 = w_exp_out[0]
    b1 = b_exp_in[0].reshape(N_EXPERTS, 1, 2 * EXPERT_FF)
    b2 = b_exp_out[0].reshape(N_EXPERTS, 1, D_MODEL)

    pad = (-(b + 1)) % SUBLANES
    c_all = jnp.concatenate([c, c_ctx[None, :], jnp.zeros((pad, d), F32)], axis=0)
    mod = _ada(c_all, w_ada[0], b_ada[0])
    sh1, sc1, g1, sh2, sc2, g2 = [m.reshape(-1, 1, d) for m in jnp.split(mod, 6, axis=-1)]
    lat = lambda m: m[:b]
    cx = lambda m: m[b:b + 1]

    x2 = x.reshape(t, d)
    proj = _inproj(x2, norm1, lat(sc1), lat(sh1), w_in_p, seq)
    proj_c = _inproj(ctx.reshape(b * n_ctx, d), norm1, cx(sc1), cx(sh1), w_in_p, n_ctx)

    cos_r, sin_r = _rope_tables(rows, RET_QK_DIM)
    cos_a, sin_a = _rope_tables(rows, ATTN_HEAD_DIM)
    lg_f = -jax.nn.softplus(ret_decay_f[0].astype(F32))
    lg_b = -jax.nn.softplus(ret_decay_b[0].astype(F32))
    zero_state = jnp.zeros((2, b, RET_HEADS, RET_QK_DIM, RET_V_DIM), F32)
    cos_c = jnp.ones((n_ctx, RET_QK_DIM), F32)
    sin_c = jnp.zeros((n_ctx, RET_QK_DIM), F32)
    _, _, s_ctx = _retention(proj_c, cos_c, sin_c, lg_f, lg_b, zero_state, b, n_ctx, False)
    o_f, o_b, _ = _retention(proj, cos_r, sin_r, lg_f, lg_b, s_ctx, b, seq, True)

    k_all, vt_all = _kvprep(proj_c, proj, attn_k_norm, cos_a, sin_a, b, n_ctx, seq)
    attn_o = _attention(proj, attn_q_norm, cos_a, sin_a, k_all, vt_all, b, seq)

    x1 = _merge(x2, lat(g1), o_f, o_b, proj, attn_o, w_ret_o[0].astype(BF16), w_attn_o[0].astype(BF16),
                w_out[0].astype(BF16), seq)

    h2, top_i, top_w = _router(x1, norm2, lat(sc2), lat(sh2), w_router[0].T, b_router[0].reshape(N_EXPERTS, 1), seq)
    n_groups = MOE_GROUPS if b % MOE_GROUPS == 0 else 1
    tgrp = t // n_groups
    tg = _largest_tile(TOP_K * tgrp, FFN_ROWS, SUBLANES)
    w_tok = top_w.T
    out = None
    for grp in range(n_groups):
        first = grp * tgrp
        dest, tile_expert, tile_flag, tile_rows = _plan(top_i[:, first:first + tgrp], tg)
        xs = _scatter_rows(h2, dest, TOP_K * tgrp + N_EXPERTS * tg, first, tgrp)
        ys = _ffn(xs, tile_expert, tile_flag, tile_rows, w1, b1, w2, b2, tg)
        yk = _gather_rows(ys, dest).reshape(TOP_K, tgrp, PACK_W)
        out = _combine(x1, lat(g2), w_tok, yk, seq, first, out)
    return out.reshape(b, seq, d)
```

```python
import functools

import jax
import jax.numpy as jnp
import numpy as np
from jax import lax
from jax.experimental import pallas as pl
from jax.experimental.pallas import tpu as pltpu
from jax.experimental.pallas import tpu_sc as plsc

F32 = jnp.float32
BF16 = jnp.bfloat16

D_MODEL = 1024
GRID_W = 64
EPS = 1e-6
RET_HEADS = 4
RET_QK_DIM = 256
RET_V_DIM = 512
ATTN_HEADS = 8
ATTN_KV_HEADS = 2
ATTN_GROUP = ATTN_HEADS // ATTN_KV_HEADS
ATTN_HEAD_DIM = 128
ROPE_THETA = 10000.0
N_EXPERTS = 32
TOP_K = 4
EXPERT_FF = 1024
SWIGLU_LIMIT = 7.0
SWIGLU_ALPHA = 1.702

RET_QK_W = RET_HEADS * RET_QK_DIM
RET_V_W = RET_HEADS * RET_V_DIM
ATTN_Q_W = ATTN_HEADS * ATTN_HEAD_DIM
ATTN_KV_W = ATTN_KV_HEADS * ATTN_HEAD_DIM
IN_SIZES = (RET_QK_W, RET_QK_W, RET_V_W, RET_V_W, ATTN_Q_W, ATTN_KV_W, ATTN_KV_W, D_MODEL, D_MODEL)
IN_WIDTH = sum(IN_SIZES)
COL_QR = 0
COL_KR = COL_QR + RET_QK_W
COL_VR = COL_KR + RET_QK_W
COL_GR = COL_VR + RET_V_W
COL_QA = COL_GR + RET_V_W
COL_GATE_R = COL_QA + ATTN_Q_W
COL_GATE_A = COL_GATE_R + D_MODEL
COL_KA = COL_GATE_A + D_MODEL
COL_VA = COL_KA + ATTN_KV_W

RET_CHUNK = 256
RET_HEADS_PER_STEP = 4
LANES = 128
SUBLANES = 8
SC_CORES = 2
SC_SUBCORES = 16
SC_GATHER_ROWS = 128
MXU_DEPTH = 256
ADA_COLS = 1536
INPROJ_ROWS = 1024
INPROJ_COLS = 2560
MERGE_ROWS = 512
ROUTER_ROWS = 2048
COMBINE_ROWS = 1024
FFN_ROWS = 512
ATTN_KEY_CHUNK = 768
ATTN_Q_TILE = 1024
ATTN_EXP_SLAB = 32
ATTN_LAG_LIMIT = 64.0
ONES_ROWS = 16
KV_PREP_ROWS = 256
PACK_DTYPE = jnp.int32
PACK_W = D_MODEL // 2
MOE_GROUPS = 2
PLAN_BLOCK = 512
TILE_PAD, TILE_BODY, TILE_FIRST = 0, 1, 2
VMEM_LIMIT = 56 * 1024 * 1024

ARB = pltpu.ARBITRARY


def _params(n_axes, **kw):
    return pltpu.CompilerParams(dimension_semantics=(ARB,) * n_axes, vmem_limit_bytes=VMEM_LIMIT, **kw)


def _largest_tile(n, cap, mult):
    best = None
    for t in range(mult, min(n, cap) + 1, mult):
        if n % t == 0:
            best = t
    assert best is not None, (n, cap, mult)
    return best


def _rms(x):
    return x * lax.rsqrt(jnp.mean(x * x, axis=-1, keepdims=True) + EPS)


def _pack_pairs(x):
    half = x.shape[1] // 2
    lo = lax.bitcast_convert_type(x[:, :half].astype(BF16).astype(F32), jnp.int32)
    hi = lax.bitcast_convert_type(x[:, half:].astype(BF16).astype(F32), jnp.int32)
    return lax.bitwise_or(lax.bitwise_and(hi, jnp.int32(-65536)), lax.shift_right_logical(lo, jnp.int32(16)))


def _unpack_pairs(w):
    lo = lax.bitcast_convert_type(lax.shift_left(w, jnp.int32(16)), F32)
    hi = lax.bitcast_convert_type(lax.bitwise_and(w, jnp.int32(-65536)), F32)
    return lo, hi


def _rope_block(x, cos, sin, half):
    if 2 * half == LANES:
        swapped = pltpu.roll(x, half, 1)
    else:
        lane = lax.broadcasted_iota(jnp.int32, x.shape, 1)
        first = (lane % (2 * half)) < half
        swapped = jnp.where(first, pltpu.roll(x, LANES - half, 1), pltpu.roll(x, half, 1))
    return x * cos + swapped * sin


def _rope_tables(rows, head_dim):
    n_freq = head_dim // 4
    inv_freq = ROPE_THETA ** (-jnp.arange(n_freq, dtype=F32) / n_freq)
    ang_r = jnp.arange(rows, dtype=F32)[:, None] * inv_freq
    ang_c = jnp.arange(GRID_W, dtype=F32)[:, None] * inv_freq
    per_row = lambda a: jnp.broadcast_to(a[:, None, :], (rows, GRID_W, n_freq)).reshape(rows * GRID_W, n_freq)
    per_col = lambda a: jnp.broadcast_to(a[None, :, :], (rows, GRID_W, n_freq)).reshape(rows * GRID_W, n_freq)
    cos_r, sin_r = per_row(jnp.cos(ang_r)), per_row(jnp.sin(ang_r))
    cos_c, sin_c = per_col(jnp.cos(ang_c)), per_col(jnp.sin(ang_c))
    cos = jnp.concatenate([cos_r, cos_r, cos_c, cos_c], axis=1)
    sin = jnp.concatenate([-sin_r, sin_r, -sin_c, sin_c], axis=1)
    return cos, sin


def _ada_kernel(c_ref, w_ref, b_ref, o_ref):
    c = c_ref[...]
    s = c * jax.nn.sigmoid(c)
    o_ref[...] = jnp.dot(s, w_ref[...], preferred_element_type=F32,
                         precision=lax.Precision.HIGHEST) + b_ref[...]


def _ada(c_pad, w_ada, b_ada):
    rows = c_pad.shape[0]
    n = w_ada.shape[1]
    tn = _largest_tile(n, ADA_COLS, LANES)
    return pl.pallas_call(
        _ada_kernel,
        grid=(n // tn,),
        in_specs=[pl.BlockSpec((rows, D_MODEL), lambda j: (0, 0)),
                  pl.BlockSpec((D_MODEL, tn), lambda j: (0, j)),
                  pl.BlockSpec((1, tn), lambda j: (0, j))],
        out_specs=pl.BlockSpec((rows, tn), lambda j: (0, j)),
        out_shape=jax.ShapeDtypeStruct((rows, n), F32),
        compiler_params=_params(1),
        name="ada",
    )(c_pad, w_ada, b_ada.reshape(1, n))


def _inproj_kernel(x_ref, n_ref, sc_ref, sh_ref, w_ref, o_ref, w_sc):
    @pl.when(pl.program_id(1) == 0)
    def _():
        w_sc[...] = w_ref[...].astype(BF16)

    y = _rms(x_ref[...]) * n_ref[...]
    h = (y * (1.0 + sc_ref[0]) + sh_ref[0]).astype(BF16)
    o_ref[...] = jnp.dot(h, w_sc[...], preferred_element_type=F32).astype(o_ref.dtype)


def _inproj(x2, norm, sc, sh, w, rows_per_batch):
    t = x2.shape[0]
    n = w.shape[1]
    tm = _largest_tile(rows_per_batch, INPROJ_ROWS, SUBLANES)
    tn = _largest_tile(n, INPROJ_COLS, LANES)
    per_b = rows_per_batch // tm
    if sc.shape[0] == 1:
        mod_map = lambda j, i: (0, 0, 0)
    else:
        mod_map = lambda j, i: (i // per_b, 0, 0)
    return pl.pallas_call(
        _inproj_kernel,
        grid=(n // tn, t // tm),
        in_specs=[pl.BlockSpec((tm, D_MODEL), lambda j, i: (i, 0)),
                  pl.BlockSpec((1, D_MODEL), lambda j, i: (0, 0)),
                  pl.BlockSpec((1, 1, D_MODEL), mod_map),
                  pl.BlockSpec((1, 1, D_MODEL), mod_map),
                  pl.BlockSpec((D_MODEL, tn), lambda j, i: (0, j))],
        out_specs=pl.BlockSpec((tm, tn), lambda j, i: (i, j)),
        out_shape=jax.ShapeDtypeStruct((t, n), BF16),
        scratch_shapes=[pltpu.VMEM((D_MODEL, tn), BF16)],
        compiler_params=_params(2),
        name="inproj",
    )(x2, norm, sc, sh, w)


def _kvprep_kernel(kc_ref, vc_ref, kl_ref, vl_ref, g_ref, cos_ref, sin_ref, k_ref, vt_ref, *, ctx_blocks):
    j = pl.program_id(1)
    vrows = ATTN_HEAD_DIM + ONES_ROWS

    def emit(kin_ref, vin_ref, use_rope):
        k = kin_ref[...].astype(F32)
        v = vin_ref[...].astype(F32)
        for g in range(ATTN_KV_HEADS):
            sl = slice(g * ATTN_HEAD_DIM, (g + 1) * ATTN_HEAD_DIM)
            kh = _rms(k[:, sl]) * g_ref[...]
            if use_rope:
                kh = _rope_block(kh, cos_ref[...], sin_ref[...], ATTN_HEAD_DIM // 4)
            k_ref[0, g] = kh.astype(BF16)
            vt_ref[0, g * vrows:g * vrows + ATTN_HEAD_DIM, :] = v[:, sl].T.astype(BF16)
            vt_ref[0, g * vrows + ATTN_HEAD_DIM:(g + 1) * vrows, :] = jnp.ones((ONES_ROWS, v.shape[0]), BF16)

    @pl.when(j < ctx_blocks)
    def _():
        emit(kc_ref, vc_ref, False)

    @pl.when(j >= ctx_blocks)
    def _():
        emit(kl_ref, vl_ref, True)


def _kvprep(proj_c, proj, gain, cos, sin, batch, n_ctx, seq):
    tm = KV_PREP_ROWS
    assert n_ctx % tm == 0 and seq % tm == 0
    cb, lb = n_ctx // tm, seq // tm
    lk = n_ctx + seq
    ctx_row = lambda b, j: b * cb + jnp.minimum(j, cb - 1)
    lat_blk = lambda j: jnp.maximum(j - cb, 0)
    lat_row = lambda b, j: b * lb + lat_blk(j)
    vrows = ATTN_KV_HEADS * (ATTN_HEAD_DIM + ONES_ROWS)
    return pl.pallas_call(
        functools.partial(_kvprep_kernel, ctx_blocks=cb),
        grid=(batch, cb + lb),
        in_specs=[pl.BlockSpec((tm, ATTN_KV_W), lambda b, j: (ctx_row(b, j), COL_KA // ATTN_KV_W)),
                  pl.BlockSpec((tm, ATTN_KV_W), lambda b, j: (ctx_row(b, j), COL_VA // ATTN_KV_W)),
                  pl.BlockSpec((tm, ATTN_KV_W), lambda b, j: (lat_row(b, j), COL_KA // ATTN_KV_W)),
                  pl.BlockSpec((tm, ATTN_KV_W), lambda b, j: (lat_row(b, j), COL_VA // ATTN_KV_W)),
                  pl.BlockSpec((1, ATTN_HEAD_DIM), lambda b, j: (0, 0)),
                  pl.BlockSpec((tm, ATTN_HEAD_DIM), lambda b, j: (lat_blk(j), 0)),
                  pl.BlockSpec((tm, ATTN_HEAD_DIM), lambda b, j: (lat_blk(j), 0))],
        out_specs=[pl.BlockSpec((1, ATTN_KV_HEADS, tm, ATTN_HEAD_DIM), lambda b, j: (b, 0, j, 0)),
                   pl.BlockSpec((1, vrows, tm), lambda b, j: (b, 0, j))],
        out_shape=[jax.ShapeDtypeStruct((batch, ATTN_KV_HEADS, lk, ATTN_HEAD_DIM), BF16),
                   jax.ShapeDtypeStruct((batch, vrows, lk), BF16)],
        compiler_params=_params(2),
        name="kvprep",
    )(proj_c, proj_c, proj, proj, gain, cos, sin)


def _ret_kernel(dch_ref, qf_ref, kf_ref, vf_ref, cosf_ref, sinf_ref, qb_ref, kb_ref, vb_ref, cosb_ref, sinb_ref,
                dmat_ref, din_ref, dout_ref, s0_ref, of_ref, ob_ref, sfin_ref, s_sc, *, use_rope, n_chunks, heads):
    hb = pl.program_id(1)
    c = pl.program_id(2)

    @pl.when(c == 0)
    def _():
        s_sc[...] = s0_ref[:, 0]

    def rope(x, cos, sin):
        return jnp.concatenate([_rope_block(x[:, i * LANES:(i + 1) * LANES], cos[:, i * LANES:(i + 1) * LANES],
                                            sin[:, i * LANES:(i + 1) * LANES], RET_QK_DIM // 4)
                                for i in range(RET_QK_DIM // LANES)], axis=1)

    sides = ((qf_ref, kf_ref, vf_ref, cosf_ref, sinf_ref), (qb_ref, kb_ref, vb_ref, cosb_ref, sinb_ref))
    chains = [(d, j) for d in range(2) for j in range(heads)]
    qs, ks, vs, ss = {}, {}, {}, {}
    for d, j in chains:
        q_ref, k_ref, v_ref, cos_ref, sin_ref = sides[d]
        q = q_ref[:, j * RET_QK_DIM:(j + 1) * RET_QK_DIM].astype(F32)
        k = k_ref[:, j * RET_QK_DIM:(j + 1) * RET_QK_DIM].astype(F32)
        if use_rope:
            q = rope(q, cos_ref[...], sin_ref[...])
            k = rope(k, cos_ref[...], sin_ref[...])
        qs[d, j], ks[d, j] = q, k
        vs[d, j] = v_ref[:, j * RET_V_DIM:(j + 1) * RET_V_DIM]
        ss[d, j] = s_sc[d, j]
    scores = {ch: lax.dot_general(qs[ch].astype(BF16), ks[ch].astype(BF16), (((1,), (1,)), ((), ())),
                                  preferred_element_type=F32) * dmat_ref[ch[0], ch[1]] for ch in chains}
    inter = {ch: jnp.dot((qs[ch] * din_ref[ch[0], ch[1]]).astype(BF16), ss[ch].astype(BF16),
                         preferred_element_type=F32) for ch in chains}
    intra = {ch: jnp.dot(scores[ch].astype(BF16), vs[ch], preferred_element_type=F32) for ch in chains}
    o_refs = (of_ref, ob_ref)
    for d, j in chains:
        o_refs[d][:, j * RET_V_DIM:(j + 1) * RET_V_DIM] = (intra[d, j] + inter[d, j]).astype(of_ref.dtype)
    s_new = {ch: ss[ch] * dch_ref[ch[0] * RET_HEADS + hb * heads + ch[1]]
             + jnp.dot((ks[ch] * dout_ref[ch[0], ch[1]]).T.astype(BF16), vs[ch], preferred_element_type=F32)
             for ch in chains}
    for d, j in chains:
        s_sc[d, j] = s_new[d, j]

    @pl.when(c == n_chunks - 1)
    def _():
        for d, j in chains:
            sfin_ref[d, 0, j] = s_new[d, j]


def _ret_tables(log_gamma, chunk, reverse, k_scale):
    pos = jnp.arange(chunk, dtype=F32)
    diff = pos[:, None] - pos[None, :]
    if reverse:
        diff = -diff
        mask = diff > 0
        p_in = chunk - pos
        p_out = pos
    else:
        mask = diff >= 0
        p_in = pos + 1.0
        p_out = chunk - 1.0 - pos
    lg = log_gamma.astype(F32)
    dmat = jnp.where(mask[None], jnp.exp(lg[:, None, None] * jnp.maximum(diff, 0.0)[None]), 0.0) * k_scale
    d_in = jnp.exp(lg[:, None] * p_in)
    d_out = jnp.exp(lg[:, None] * p_out) * k_scale
    d_in = jnp.broadcast_to(d_in[:, :, None], (RET_HEADS, chunk, RET_QK_DIM))
    d_out = jnp.broadcast_to(d_out[:, :, None], (RET_HEADS, chunk, RET_QK_DIM))
    d_chunk = jnp.exp(lg * chunk)
    return d_chunk, dmat, d_in, d_out


def _retention(proj, cos, sin, lg_f, lg_b, s0, batch, seq, use_rope):
    chunk = min(RET_CHUNK, seq)
    nc = seq // chunk
    hs = RET_HEADS_PER_STEP
    qw, vw = hs * RET_QK_DIM, hs * RET_V_DIM
    k_scale = RET_QK_DIM ** -0.5
    tabs = [_ret_tables(lg_f, chunk, False, k_scale), _ret_tables(lg_b, chunk, True, k_scale)]
    d_chunk, dmat, d_in, d_out = [jnp.stack([tabs[0][i], tabs[1][i]]) for i in range(4)]
    fwd = lambda c: c
    bwd = lambda c: nc - 1 - c

    def side(cidx):
        row = lambda b, c: b * nc + cidx(c)
        return [pl.BlockSpec((chunk, qw), lambda b, h, c: (row(b, c), COL_QR // qw + h)),
                pl.BlockSpec((chunk, qw), lambda b, h, c: (row(b, c), COL_KR // qw + h)),
                pl.BlockSpec((chunk, vw), lambda b, h, c: (row(b, c), COL_VR // vw + h)),
                pl.BlockSpec((chunk, RET_QK_DIM), lambda b, h, c: (cidx(c), 0)),
                pl.BlockSpec((chunk, RET_QK_DIM), lambda b, h, c: (cidx(c), 0))]

    state_spec = pl.BlockSpec((2, 1, hs, RET_QK_DIM, RET_V_DIM), lambda b, h, c: (0, b, h, 0, 0))
    return pl.pallas_call(
        functools.partial(_ret_kernel, use_rope=use_rope, n_chunks=nc, heads=hs),
        grid=(batch, RET_HEADS // hs, nc),
        in_specs=[pl.BlockSpec(memory_space=pltpu.SMEM)] + side(fwd) + side(bwd) + [
            pl.BlockSpec((2, hs, chunk, chunk), lambda b, h, c: (0, h, 0, 0)),
            pl.BlockSpec((2, hs, chunk, RET_QK_DIM), lambda b, h, c: (0, h, 0, 0)),
            pl.BlockSpec((2, hs, chunk, RET_QK_DIM), lambda b, h, c: (0, h, 0, 0)),
            state_spec],
        out_specs=[pl.BlockSpec((chunk, vw), lambda b, h, c: (b * nc + c, h)),
                   pl.BlockSpec((chunk, vw), lambda b, h, c: (b * nc + nc - 1 - c, h)),
                   state_spec],
        out_shape=[jax.ShapeDtypeStruct((batch * seq, RET_V_W), BF16),
                   jax.ShapeDtypeStruct((batch * seq, RET_V_W), BF16),
                   jax.ShapeDtypeStruct((2, batch, RET_HEADS, RET_QK_DIM, RET_V_DIM), F32)],
        scratch_shapes=[pltpu.VMEM((2, hs, RET_QK_DIM, RET_V_DIM), F32)],
        compiler_params=_params(3),
        name="ret",
    )(d_chunk.reshape(-1), proj, proj, proj, cos, sin, proj, proj, proj, cos, sin, dmat, d_in, d_out, s0)


def _attn_kernel(q_ref, g_ref, cos_ref, sin_ref, k_ref, vt_ref, o_ref, qt_sc, m_sc, acc_sc, s_sc,
                 *ring, tq, kc, n_chunks):
    p_bufs, a_bufs, x_bufs, u_bufs, b_bufs = (ring[3 * i:3 * i + 3] for i in range(5))

    q = q_ref[...].astype(F32)
    scale = ATTN_HEAD_DIM ** -0.5 * np.log2(np.e)
    for g in range(ATTN_GROUP):
        qh = _rms(q[:, g * ATTN_HEAD_DIM:(g + 1) * ATTN_HEAD_DIM]) * g_ref[...]
        qh = _rope_block(qh, cos_ref[...], sin_ref[...], ATTN_HEAD_DIM // 4) * scale
        qt_sc[:, g * tq:(g + 1) * tq] = qh.T.astype(BF16)
    m_sc[...] = jnp.full(m_sc.shape, -jnp.inf, F32)
    acc_sc[...] = jnp.zeros(acc_sc.shape, F32)

    def scores(c):
        off = pl.multiple_of(c * kc, kc)
        return jnp.dot(k_ref[0, 0, pl.ds(off, kc), :], qt_sc[...], preferred_element_type=F32)

    def stage_s(c, slot):
        u = m_sc[...]
        u_bufs[slot][...] = u
        s_t = scores(c)
        x_bufs[slot][...] = jnp.max(s_t, axis=0, keepdims=True)
        p_bufs[slot][...] = jnp.exp2(s_t - u).astype(BF16)

    def stage_f(c, slot):
        m_prev = m_sc[...]
        x = x_bufs[slot][...]
        u = u_bufs[slot][...]
        m_new = jnp.maximum(m_prev, x)
        m_sc[...] = m_new
        a_bufs[slot][...] = jnp.exp2(m_prev - m_new)
        lagging = jnp.max(x - u) > ATTN_LAG_LIMIT

        @pl.when(lagging)
        def _():
            s_sc[...] = scores(c)
            for r in range(0, kc, ATTN_EXP_SLAB):
                p_bufs[slot][r:r + ATTN_EXP_SLAB, :] = jnp.exp2(s_sc[r:r + ATTN_EXP_SLAB, :] - m_new).astype(BF16)
            b_bufs[slot][...] = jnp.ones(m_new.shape, F32)

        @pl.when(jnp.logical_not(lagging))
        def _():
            b_bufs[slot][...] = jnp.exp2(u - m_new)

    def stage_a(c, slot):
        off = pl.multiple_of(c * kc, kc)
        acc_sc[...] = a_bufs[slot][...] * acc_sc[...] + b_bufs[slot][...] * jnp.dot(
            vt_ref[0, :, pl.ds(off, kc)], p_bufs[slot][...], preferred_element_type=F32)

    n = n_chunks

    def tick(t, slot, full=False):
        if full or 1 <= t <= n:
            stage_f(t - 1, (slot + 2) % 3)
        if full or t < n:
            stage_s(t, slot)
        if full or 2 <= t <= n + 1:
            stage_a(t - 2, (slot + 1) % 3)

    tick(0, 0)
    tick(1, 1)
    first = 2
    while first < n and (n - first) % 3 != 0:
        tick(first, first % 3)
        first += 1
    n_groups = (n - first) // 3 if n > first else 0
    if n_groups > 0:
        def group(i, carry):
            t = first + 3 * i
            for j in range(3):
                tick(t + j, (first + j) % 3, full=True)
            return carry
        lax.fori_loop(0, n_groups, group, 0)
    for t in range(max(n, 2), n + 2):
        tick(t, t % 3)

    acc = acc_sc[...]
    o = acc[:ATTN_HEAD_DIM] / acc[ATTN_HEAD_DIM:ATTN_HEAD_DIM + 1]
    for g in range(ATTN_GROUP):
        o_ref[:, g * ATTN_HEAD_DIM:(g + 1) * ATTN_HEAD_DIM] = o[:, g * tq:(g + 1) * tq].T.astype(o_ref.dtype)


def _attention(proj, gain, cos, sin, k_all, vt_all, batch, seq):
    lk = k_all.shape[2]
    kc = _largest_tile(lk, ATTN_KEY_CHUNK, MXU_DEPTH)
    tq = _largest_tile(seq, ATTN_Q_TILE, SUBLANES)
    nq = seq // tq
    qw = ATTN_GROUP * ATTN_HEAD_DIM
    cols = ATTN_GROUP * tq
    vrows = ATTN_HEAD_DIM + ONES_ROWS
    return pl.pallas_call(
        functools.partial(_attn_kernel, tq=tq, kc=kc, n_chunks=lk // kc),
        grid=(batch, ATTN_KV_HEADS, nq),
        in_specs=[pl.BlockSpec((tq, qw), lambda b, g, i: (b * nq + i, COL_QA // qw + g)),
                  pl.BlockSpec((1, ATTN_HEAD_DIM), lambda b, g, i: (0, 0)),
                  pl.BlockSpec((tq, ATTN_HEAD_DIM), lambda b, g, i: (i, 0)),
                  pl.BlockSpec((tq, ATTN_HEAD_DIM), lambda b, g, i: (i, 0)),
                  pl.BlockSpec((1, 1, lk, ATTN_HEAD_DIM), lambda b, g, i: (b, g, 0, 0), pipeline_mode=pl.Buffered(1)),
                  pl.BlockSpec((1, vrows, lk), lambda b, g, i: (b, g, 0), pipeline_mode=pl.Buffered(1))],
        out_specs=pl.BlockSpec((tq, qw), lambda b, g, i: (b * nq + i, g)),
        out_shape=jax.ShapeDtypeStruct((batch * seq, ATTN_Q_W), BF16),
        scratch_shapes=[pltpu.VMEM((ATTN_HEAD_DIM, cols), BF16),
                        pltpu.VMEM((1, cols), F32),
                        pltpu.VMEM((vrows, cols), F32),
                        pltpu.VMEM((kc, cols), F32)] + [pltpu.VMEM((kc, cols), BF16)] * 3
        + [pltpu.VMEM((1, cols), F32)] * 12,
        compiler_params=_params(3),
        name="attn",
    )(proj, gain, cos, sin, k_all, vt_all)


def _merge_kernel(x_ref, g1_ref, of_ref, ob_ref, gr_ref, ao_ref, gtr_ref, gta_ref,
                  wro_ref, wao_ref, wo_ref, o_ref):
    ro = of_ref[...].astype(F32) + ob_ref[...].astype(F32)
    gr = gr_ref[...].astype(F32)
    parts = []
    for h in range(RET_HEADS):
        sl = slice(h * RET_V_DIM, (h + 1) * RET_V_DIM)
        g = gr[:, sl]
        parts.append((g * jax.nn.sigmoid(g) * _rms(ro[:, sl])).astype(BF16))
    ret_in = jnp.concatenate(parts, axis=1)
    ret_branch = jnp.dot(ret_in, wro_ref[...], preferred_element_type=F32)
    attn_branch = jnp.dot(ao_ref[...], wao_ref[...], preferred_element_type=F32)
    y = (jax.nn.sigmoid(gtr_ref[...].astype(F32)) * ret_branch
         + jax.nn.sigmoid(gta_ref[...].astype(F32)) * attn_branch)
    y = jnp.dot(y.astype(BF16), wo_ref[...], preferred_element_type=F32)
    o_ref[...] = x_ref[...] + g1_ref[0] * y


def _merge(x2, g1, o_f, o_b, proj, attn_o, w_ret_o, w_attn_o, w_out, seq):
    t = x2.shape[0]
    tm = _largest_tile(seq, MERGE_ROWS, SUBLANES)
    per_b = seq // tm
    full = lambda i: (0, 0)
    return pl.pallas_call(
        _merge_kernel,
        grid=(t // tm,),
        in_specs=[pl.BlockSpec((tm, D_MODEL), lambda i: (i, 0)),
                  pl.BlockSpec((1, 1, D_MODEL), lambda i: (i // per_b, 0, 0)),
                  pl.BlockSpec((tm, RET_V_W), lambda i: (i, 0)),
                  pl.BlockSpec((tm, RET_V_W), lambda i: (i, 0)),
                  pl.BlockSpec((tm, RET_V_W), lambda i: (i, COL_GR // RET_V_W)),
                  pl.BlockSpec((tm, ATTN_Q_W), lambda i: (i, 0)),
                  pl.BlockSpec((tm, D_MODEL), lambda i: (i, COL_GATE_R // D_MODEL)),
                  pl.BlockSpec((tm, D_MODEL), lambda i: (i, COL_GATE_A // D_MODEL)),
                  pl.BlockSpec((RET_V_W, D_MODEL), full, pipeline_mode=pl.Buffered(1)),
                  pl.BlockSpec((ATTN_Q_W, D_MODEL), full, pipeline_mode=pl.Buffered(1)),
                  pl.BlockSpec((D_MODEL, D_MODEL), full, pipeline_mode=pl.Buffered(1))],
        out_specs=pl.BlockSpec((tm, D_MODEL), lambda i: (i, 0)),
        out_shape=jax.ShapeDtypeStruct((t, D_MODEL), F32),
        compiler_params=_params(1),
        name="merge",
    )(x2, g1, o_f, o_b, proj, attn_o, proj, proj, w_ret_o, w_attn_o, w_out)


def _router_kernel(x_ref, n_ref, sc_ref, sh_ref, wr_ref, br_ref, h_ref, ti_ref, tw_ref):
    h = _rms(x_ref[...]) * n_ref[...] * (1.0 + sc_ref[0]) + sh_ref[0]
    h_ref[...] = _pack_pairs(h)
    nt = (((1,), (1,)), ((), ()))
    w = wr_ref[...]
    h_hi = h.astype(BF16)
    h_lo = (h - h_hi.astype(F32)).astype(BF16)
    w_hi = w.astype(BF16)
    w_lo = (w - w_hi.astype(F32)).astype(BF16)
    logits = (lax.dot_general(w_hi, h_hi, nt, preferred_element_type=F32)
              + lax.dot_general(w_lo, h_hi, nt, preferred_element_type=F32)
              + lax.dot_general(w_hi, h_lo, nt, preferred_element_type=F32)) + br_ref[...]
    eid = lax.broadcasted_iota(jnp.int32, logits.shape, 0)
    vals = logits
    top_v = []
    top_i = []
    for _ in range(TOP_K):
        m = jnp.max(vals, axis=0, keepdims=True)
        idx = jnp.min(jnp.where(vals == m, eid, N_EXPERTS), axis=0, keepdims=True)
        top_v.append(m)
        top_i.append(idx)
        vals = jnp.where(eid == idx, -jnp.inf, vals)
    ex = [jnp.exp(v - top_v[0]) for v in top_v]
    denom = ex[0] + ex[1] + ex[2] + ex[3]
    ti_ref[...] = jnp.concatenate(top_i, axis=0)
    tw_ref[...] = jnp.concatenate([e / denom for e in ex], axis=0)


def _router(x1, norm, sc, sh, w_router_t, b_router, seq):
    t = x1.shape[0]
    tm = _largest_tile(seq, ROUTER_ROWS, LANES)
    per_b = seq // tm
    mod_map = lambda i: (i // per_b, 0, 0)
    return pl.pallas_call(
        _router_kernel,
        grid=(t // tm,),
        in_specs=[pl.BlockSpec((tm, D_MODEL), lambda i: (i, 0)),
                  pl.BlockSpec((1, D_MODEL), lambda i: (0, 0)),
                  pl.BlockSpec((1, 1, D_MODEL), mod_map),
                  pl.BlockSpec((1, 1, D_MODEL), mod_map),
                  pl.BlockSpec((N_EXPERTS, D_MODEL), lambda i: (0, 0)),
                  pl.BlockSpec((N_EXPERTS, 1), lambda i: (0, 0))],
        out_specs=[pl.BlockSpec((tm, PACK_W), lambda i: (i, 0)),
                   pl.BlockSpec((TOP_K, tm), lambda i: (0, i)),
                   pl.BlockSpec((TOP_K, tm), lambda i: (0, i))],
        out_shape=[jax.ShapeDtypeStruct((t, PACK_W), PACK_DTYPE),
                   jax.ShapeDtypeStruct((TOP_K, t), jnp.int32),
                   jax.ShapeDtypeStruct((TOP_K, t), F32)],
        compiler_params=_params(1),
        name="router",
    )(x1, norm, sc, sh, w_router_t, b_router)


def _gather_rows(src, idx):
    n, w = src.shape
    r = idx.shape[0]
    workers = SC_CORES * SC_SUBCORES
    per_w = r // workers
    n_win = per_w // SC_GATHER_ROWS
    assert per_w * workers == r and n_win * SC_GATHER_ROWS == per_w, (r, workers, SC_GATHER_ROWS)
    mesh = plsc.VectorSubcoreMesh(core_axis_name="c", subcore_axis_name="s")

    @functools.partial(
        pl.kernel, mesh=mesh, out_type=jax.ShapeDtypeStruct((r, w), src.dtype),
        scratch_types=[pltpu.VMEM((SC_GATHER_ROWS,), jnp.int32),
                       pltpu.VMEM((SC_GATHER_ROWS, w), src.dtype),
                       pltpu.SemaphoreType.DMA])
    def gather(src_hbm, idx_hbm, out_hbm, idx_v, rows_v, sem):
        wid = lax.axis_index("s") * SC_CORES + lax.axis_index("c")
        base = wid * per_w

        @pl.loop(0, n_win)
        def _(win):
            off = base + win * SC_GATHER_ROWS
            pltpu.sync_copy(idx_hbm.at[pl.ds(off, SC_GATHER_ROWS)], idx_v)
            pltpu.async_copy(src_hbm.at[idx_v], rows_v, sem).wait()
            pltpu.sync_copy(rows_v, out_hbm.at[pl.ds(off, SC_GATHER_ROWS)])

    return gather(src, idx)


def _scatter_rows(src, dest, n_out, first, group_rows):
    n, w = src.shape
    n_assign = dest.shape[0]
    workers = SC_CORES * SC_SUBCORES
    per_w = n_assign // workers
    n_win = per_w // SC_GATHER_ROWS
    assert per_w * workers == n_assign and n_win * SC_GATHER_ROWS == per_w and group_rows % per_w == 0
    mesh = plsc.VectorSubcoreMesh(core_axis_name="c", subcore_axis_name="s")

    @functools.partial(
        pl.kernel, mesh=mesh, out_type=jax.ShapeDtypeStruct((n_out, w), src.dtype),
        scratch_types=[pltpu.VMEM((SC_GATHER_ROWS,), jnp.int32),
                       pltpu.VMEM((SC_GATHER_ROWS, w), src.dtype),
                       pltpu.SemaphoreType.DMA])
    def scatter(src_hbm, dest_hbm, out_hbm, idx_v, rows_v, sem):
        wid = lax.axis_index("s") * SC_CORES + lax.axis_index("c")
        base = wid * per_w
        row_base = first + lax.rem(base, group_rows)

        @pl.loop(0, n_win)
        def _(win):
            pltpu.sync_copy(dest_hbm.at[pl.ds(base + win * SC_GATHER_ROWS, SC_GATHER_ROWS)], idx_v)
            pltpu.sync_copy(src_hbm.at[pl.ds(row_base + win * SC_GATHER_ROWS, SC_GATHER_ROWS)], rows_v)
            pltpu.async_copy(rows_v, out_hbm.at[idx_v], sem).wait()

    return scatter(src, dest)


def _ffn_kernel(te_ref, tf_ref, tr_ref, ts_ref, tn_ref, tp_ref, x_ref, w1_hbm, b1_ref, w2_hbm, b2_ref, o_ref,
                w1_sc, w2_sc, w1_st, w2_st, sem):
    j = pl.program_id(0)
    flag = tf_ref[j]
    n_rows = tr_ref[j]

    def weight_copies(expert, slot):
        return (pltpu.make_async_copy(w1_hbm.at[expert], w1_st.at[slot], sem.at[0, slot]),
                pltpu.make_async_copy(w2_hbm.at[expert], w2_st.at[slot], sem.at[1, slot]))

    @pl.when(flag == TILE_FIRST)
    def _():
        slot = ts_ref[j]

        @pl.when(tp_ref[j] != 0)
        def _():
            for cp in weight_copies(te_ref[j], slot):
                cp.start()

        for cp in weight_copies(te_ref[j], slot):
            cp.wait()
        w1_sc[...] = w1_st[slot].astype(BF16)
        w2_sc[...] = w2_st[slot].astype(BF16)

        @pl.when(tn_ref[j] >= 0)
        def _():
            for cp in weight_copies(tn_ref[j], 1 - slot):
                cp.start()

    @pl.when(flag == TILE_PAD)
    def _():
        o_ref[...] = jnp.zeros(o_ref.shape, o_ref.dtype)

    @pl.when(flag != TILE_PAD)
    def _():
        xw = x_ref[...]
        rid = lax.broadcasted_iota(jnp.int32, xw.shape, 0)
        lo, hi = _unpack_pairs(jnp.where(rid < n_rows, xw, jnp.zeros_like(xw)))
        x = jnp.concatenate([lo, hi], axis=1).astype(BF16)
        a = jnp.dot(x, w1_sc[...], preferred_element_type=F32) + b1_ref[0]
        gate = jnp.minimum(a[:, :EXPERT_FF], SWIGLU_LIMIT)
        up = jnp.clip(a[:, EXPERT_FF:], -SWIGLU_LIMIT, SWIGLU_LIMIT)
        act = gate * jax.nn.sigmoid(SWIGLU_ALPHA * gate) * (up + 1.0)
        y = jnp.dot(act.astype(BF16), w2_sc[...], preferred_element_type=F32) + b2_ref[0]
        o_ref[...] = _pack_pairs(y)


def _ffn(xs, tile_expert, tile_flag, tile_rows, w1, b1, w2, b2, tg):
    p = xs.shape[0]
    n_tiles = p // tg
    tile_slot, tile_next, tile_prime = _ffn_ring(tile_expert, tile_flag)
    row_map = lambda j, te, tf, tr, ts, tn, tp: (j, 0)
    bias_map = lambda j, te, tf, tr, ts, tn, tp: (te[j], 0, 0)
    grid_spec = pltpu.PrefetchScalarGridSpec(
        num_scalar_prefetch=6,
        grid=(n_tiles,),
        in_specs=[pl.BlockSpec((tg, PACK_W), row_map),
                  pl.BlockSpec(memory_space=pl.ANY),
                  pl.BlockSpec((1, 1, 2 * EXPERT_FF), bias_map),
                  pl.BlockSpec(memory_space=pl.ANY),
                  pl.BlockSpec((1, 1, D_MODEL), bias_map)],
        out_specs=pl.BlockSpec((tg, PACK_W), row_map),
        scratch_shapes=[pltpu.VMEM((D_MODEL, 2 * EXPERT_FF), BF16),
                        pltpu.VMEM((EXPERT_FF, D_MODEL), BF16),
                        pltpu.VMEM((2, D_MODEL, 2 * EXPERT_FF), F32),
                        pltpu.VMEM((2, EXPERT_FF, D_MODEL), F32),
                        pltpu.SemaphoreType.DMA((2, 2))],
    )
    return pl.pallas_call(
        _ffn_kernel,
        grid_spec=grid_spec,
        out_shape=jax.ShapeDtypeStruct((p, PACK_W), PACK_DTYPE),
        compiler_params=_params(1),
        name="ffn",
    )(tile_expert, tile_flag, tile_rows, tile_slot, tile_next, tile_prime, xs, w1, b1, w2, b2)


def _ffn_ring(tile_expert, tile_flag):
    n_tiles = tile_flag.shape[0]
    first = tile_flag == TILE_FIRST
    ordinal = jnp.cumsum(first.astype(jnp.int32)) - 1
    tile_slot = jnp.where(first, ordinal % 2, 0).astype(jnp.int32)
    tile_prime = (first & (ordinal == 0)).astype(jnp.int32)
    tiles = jnp.arange(n_tiles, dtype=jnp.int32)
    first_at_or_after = lax.cummin(jnp.where(first, tiles, n_tiles)[::-1], axis=0)[::-1]
    next_first = jnp.concatenate([first_at_or_after[1:], jnp.full((1,), n_tiles, jnp.int32)])
    tile_next = jnp.where(first & (next_first < n_tiles),
                          tile_expert[jnp.minimum(next_first, n_tiles - 1)], -1).astype(jnp.int32)
    return tile_slot, tile_next, tile_prime


def _combine_kernel(x_ref, g2_ref, w_ref, y_ref, o_ref):
    w = w_ref[...]
    acc_lo = None
    for k in range(TOP_K):
        lo, hi = _unpack_pairs(y_ref[k])
        wk = w[:, k:k + 1]
        acc_lo = wk * lo if acc_lo is None else acc_lo + wk * lo
        acc_hi = wk * hi if k == 0 else acc_hi + wk * hi
    acc = jnp.concatenate([acc_lo, acc_hi], axis=1)
    o_ref[...] = x_ref[...] + g2_ref[0] * acc


def _combine_into_kernel(x_ref, g2_ref, w_ref, y_ref, prev_ref, o_ref):
    del prev_ref
    _combine_kernel(x_ref, g2_ref, w_ref, y_ref, o_ref)


def _combine(x1, g2, w_tok, yk, seq, first_row, prev):
    t = x1.shape[0]
    rows = yk.shape[1]
    tm = _largest_tile(seq, COMBINE_ROWS, SUBLANES)
    per_b = seq // tm
    i0 = first_row // tm
    in_specs = [pl.BlockSpec((tm, D_MODEL), lambda i: (i0 + i, 0)),
                pl.BlockSpec((1, 1, D_MODEL), lambda i: ((i0 + i) // per_b, 0, 0)),
                pl.BlockSpec((tm, TOP_K), lambda i: (i0 + i, 0)),
                pl.BlockSpec((TOP_K, tm, PACK_W), lambda i: (0, i, 0))]
    args = [x1, g2, w_tok, yk]
    if prev is not None:
        in_specs.append(pl.BlockSpec(memory_space=pl.ANY))
        args.append(prev)
    return pl.pallas_call(
        _combine_kernel if prev is None else _combine_into_kernel,
        grid=(rows // tm,),
        in_specs=in_specs,
        out_specs=pl.BlockSpec((tm, D_MODEL), lambda i: (i0 + i, 0)),
        out_shape=jax.ShapeDtypeStruct((t, D_MODEL), F32),
        input_output_aliases={} if prev is None else {len(args) - 1: 0},
        compiler_params=_params(1),
        name="combine",
    )(*args)


def _count_kernel(ti_ref, cnt_ref):
    @pl.when(pl.program_id(0) == 0)
    def _():
        cnt_ref[...] = jnp.zeros(cnt_ref.shape, F32)

    bt = ti_ref.shape[1]
    eid = lax.broadcasted_iota(jnp.int32, (N_EXPERTS, bt), 0)
    acc = jnp.zeros(cnt_ref.shape, F32)
    for k in range(TOP_K):
        m = (eid == ti_ref[k:k + 1, :]).astype(F32)
        for c in range(bt // LANES):
            acc = acc + m[:, c * LANES:(c + 1) * LANES]
    cnt_ref[...] += acc


def _rank_kernel(ti_ref, off_ref, tri_ref, dest_ref, run_sc):
    @pl.when(pl.program_id(0) == 0)
    def _():
        run_sc[...] = off_ref[...] - 1.0

    bt = ti_ref.shape[1]
    eid = lax.broadcasted_iota(jnp.int32, (N_EXPERTS, bt), 0)
    run = run_sc[...]
    for k in range(TOP_K):
        m = eid == ti_ref[k:k + 1, :]
        pre = jnp.dot(jnp.where(m, 1.0, 0.0).astype(BF16), tri_ref[...], preferred_element_type=F32)
        slot = jnp.sum(jnp.where(m, pre + run, 0.0), axis=0, keepdims=True)
        dest_ref[k:k + 1, :] = slot.astype(jnp.int32)
        run = run + pre[:, bt - 1:bt]
    run_sc[...] = run


def _plan(top_i, tg):
    t = top_i.shape[1]
    n_assign = TOP_K * t
    p = n_assign + N_EXPERTS * tg
    n_tiles = p // tg
    bt = _largest_tile(t, PLAN_BLOCK, LANES)
    cnt = pl.pallas_call(
        _count_kernel,
        grid=(t // bt,),
        in_specs=[pl.BlockSpec((TOP_K, bt), lambda i: (0, i))],
        out_specs=pl.BlockSpec((N_EXPERTS, LANES), lambda i: (0, 0)),
        out_shape=jax.ShapeDtypeStruct((N_EXPERTS, LANES), F32),
        compiler_params=_params(1),
        name="count",
    )(top_i)
    counts = jnp.sum(cnt, axis=1).astype(jnp.int32)
    padded = ((counts + tg - 1) // tg) * tg
    off_end = jnp.cumsum(padded)
    off = off_end - padded
    tri = (jnp.arange(bt)[:, None] <= jnp.arange(bt)[None, :]).astype(BF16)
    dest = pl.pallas_call(
        _rank_kernel,
        grid=(t // bt,),
        in_specs=[pl.BlockSpec((TOP_K, bt), lambda i: (0, i)),
                  pl.BlockSpec((N_EXPERTS, 1), lambda i: (0, 0)),
                  pl.BlockSpec((bt, bt), lambda i: (0, 0))],
        out_specs=pl.BlockSpec((TOP_K, bt), lambda i: (0, i)),
        out_shape=jax.ShapeDtypeStruct((TOP_K, t), jnp.int32),
        scratch_shapes=[pltpu.VMEM((N_EXPERTS, 1), F32)],
        compiler_params=_params(1),
        name="rank",
    )(top_i, off.astype(F32).reshape(N_EXPERTS, 1), tri)
    tile_start = jnp.arange(n_tiles, dtype=jnp.int32) * tg
    tile_valid = tile_start < off_end[-1]
    te = jnp.sum((tile_start[:, None] >= off_end[None, :]).astype(jnp.int32), axis=1)
    last_e = jnp.sum(((off_end[-1] - 1) >= off_end).astype(jnp.int32))
    tile_expert = jnp.where(tile_valid, te, last_e)
    tile_rows = jnp.where(tile_valid, jnp.clip(counts[tile_expert] - (tile_start - off[tile_expert]), 0, tg), 0)
    changed = jnp.concatenate([jnp.ones((1,), jnp.bool_), tile_expert[1:] != tile_expert[:-1]])
    tile_flag = jnp.where(tile_valid, jnp.where(changed, TILE_FIRST, TILE_BODY), TILE_PAD).astype(jnp.int32)
    return dest.reshape(-1), tile_expert, tile_flag, tile_rows.astype(jnp.int32)


def kernel(x, c, ctx, c_ctx, norm1, norm2, w_ada, b_ada, w_in, ret_decay_f, ret_decay_b, attn_q_norm, attn_k_norm,
           w_ret_o, w_attn_o, w_out, w_router, b_router, w_exp_in, b_exp_in, w_exp_out, b_exp_out):
    assert w_in.shape[0] == 1, "single-layer block"
    b, seq, d = x.shape
    n_ctx = ctx.shape[1]
    t = b * seq
    rows = seq // GRID_W

    idx = np.cumsum(IN_SIZES)[:-1].tolist()
    wq_r, wk_r, wv_r, wg_r, wq_a, wk_a, wv_a, wgt_r, wgt_a = jnp.split(w_in[0], idx, axis=-1)
    w_in_p = jnp.concatenate([wq_r, wk_r, wv_r, wg_r, wq_a, wgt_r, wgt_a, wk_a, wv_a], axis=-1)
    w1 = w_exp_in[0]
    w2 = w_exp_out[0]
    b1 = b_exp_in[0].reshape(N_EXPERTS, 1, 2 * EXPERT_FF)
    b2 = b_exp_out[0].reshape(N_EXPERTS, 1, D_MODEL)

    pad = (-(b + 1)) % SUBLANES
    c_all = jnp.concatenate([c, c_ctx[None, :], jnp.zeros((pad, d), F32)], axis=0)
    mod = _ada(c_all, w_ada[0], b_ada[0])
    sh1, sc1, g1, sh2, sc2, g2 = [m.reshape(-1, 1, d) for m in jnp.split(mod, 6, axis=-1)]
    lat = lambda m: m[:b]
    cx = lambda m: m[b:b + 1]

    x2 = x.reshape(t, d)
    proj = _inproj(x2, norm1, lat(sc1), lat(sh1), w_in_p, seq)
    proj_c = _inproj(ctx.reshape(b * n_ctx, d), norm1, cx(sc1), cx(sh1), w_in_p, n_ctx)

    cos_r, sin_r = _rope_tables(rows, RET_QK_DIM)
    cos_a, sin_a = _rope_tables(rows, ATTN_HEAD_DIM)
    lg_f = -jax.nn.softplus(ret_decay_f[0].astype(F32))
    lg_b = -jax.nn.softplus(ret_decay_b[0].astype(F32))
    zero_state = jnp.zeros((2, b, RET_HEADS, RET_QK_DIM, RET_V_DIM), F32)
    cos_c = jnp.ones((n_ctx, RET_QK_DIM), F32)
    sin_c = jnp.zeros((n_ctx, RET_QK_DIM), F32)
    _, _, s_ctx = _retention(proj_c, cos_c, sin_c, lg_f, lg_b, zero_state, b, n_ctx, False)
    o_f, o_b, _ = _retention(proj, cos_r, sin_r, lg_f, lg_b, s_ctx, b, seq, True)

    k_all, vt_all = _kvprep(proj_c, proj, attn_k_norm, cos_a, sin_a, b, n_ctx, seq)
    attn_o = _attention(proj, attn_q_norm, cos_a, sin_a, k_all, vt_all, b, seq)

    x1 = _merge(x2, lat(g1), o_f, o_b, proj, attn_o, w_ret_o[0].astype(BF16), w_attn_o[0].astype(BF16),
                w_out[0].astype(BF16), seq)

    h2, top_i, top_w = _router(x1, norm2, lat(sc2), lat(sh2), w_router[0].T, b_router[0].reshape(N_EXPERTS, 1), seq)
    n_groups = MOE_GROUPS if b % MOE_GROUPS == 0 else 1
    tgrp = t // n_groups
    tg = _largest_tile(TOP_K * tgrp, FFN_ROWS, SUBLANES)
    w_tok = top_w.T
    out = None
    for grp in range(n_groups):
        first = grp * tgrp
        dest, tile_expert, tile_flag, tile_rows = _plan(top_i[:, first:first + tgrp], tg)
        xs = _scatter_rows(h2, dest, TOP_K * tgrp + N_EXPERTS * tg, first, tgrp)
        ys = _ffn(xs, tile_expert, tile_flag, tile_rows, w1, b1, w2, b2, tg)
        yk = _gather_rows(ys, dest).reshape(TOP_K, tgrp, PACK_W)
        out = _combine(x1, lat(g2), w_tok, yk, seq, first, out)
    return out.reshape(b, seq, d)
```

```python
import functools

import jax
import jax.numpy as jnp
import numpy as np
from jax import lax
from jax.experimental import pallas as pl
from jax.experimental.pallas import tpu as pltpu
from jax.experimental.pallas import tpu_sc as plsc

F32 = jnp.float32
BF16 = jnp.bfloat16

D_MODEL = 1024
GRID_W = 64
EPS = 1e-6
RET_HEADS = 4
RET_QK_DIM = 256
RET_V_DIM = 512
ATTN_HEADS = 8
ATTN_KV_HEADS = 2
ATTN_GROUP = ATTN_HEADS // ATTN_KV_HEADS
ATTN_HEAD_DIM = 128
ROPE_THETA = 10000.0
N_EXPERTS = 32
TOP_K = 4
EXPERT_FF = 1024
SWIGLU_LIMIT = 7.0
SWIGLU_ALPHA = 1.702

RET_QK_W = RET_HEADS * RET_QK_DIM
RET_V_W = RET_HEADS * RET_V_DIM
ATTN_Q_W = ATTN_HEADS * ATTN_HEAD_DIM
ATTN_KV_W = ATTN_KV_HEADS * ATTN_HEAD_DIM
IN_SIZES = (RET_QK_W, RET_QK_W, RET_V_W, RET_V_W, ATTN_Q_W, ATTN_KV_W, ATTN_KV_W, D_MODEL, D_MODEL)
IN_WIDTH = sum(IN_SIZES)
COL_QR = 0
COL_KR = COL_QR + RET_QK_W
COL_VR = COL_KR + RET_QK_W
COL_GR = COL_VR + RET_V_W
COL_QA = COL_GR + RET_V_W
COL_GATE_R = COL_QA + ATTN_Q_W
COL_GATE_A = COL_GATE_R + D_MODEL
COL_KA = COL_GATE_A + D_MODEL
COL_VA = COL_KA + ATTN_KV_W

RET_CHUNK = 256
RET_HEADS_PER_STEP = 4
LANES = 128
SUBLANES = 8
SC_CORES = 2
SC_SUBCORES = 16
SC_GATHER_ROWS = 128
MXU_DEPTH = 256
ADA_COLS = 1536
INPROJ_ROWS = 1024
INPROJ_COLS = 2560
MERGE_ROWS = 512
ROUTER_ROWS = 2048
COMBINE_ROWS = 1024
FFN_ROWS = 512
ATTN_KEY_CHUNK = 768
ATTN_Q_TILE = 1024
ATTN_EXP_SLAB = 32
ATTN_LAG_LIMIT = 64.0
ONES_ROWS = 16
KV_PREP_ROWS = 256
PACK_DTYPE = jnp.int32
PACK_W = D_MODEL // 2
MOE_GROUPS = 2
PLAN_BLOCK = 512
TILE_PAD, TILE_BODY, TILE_FIRST = 0, 1, 2
VMEM_LIMIT = 56 * 1024 * 1024

ARB = pltpu.ARBITRARY


def _params(n_axes, **kw):
    return pltpu.CompilerParams(dimension_semantics=(ARB,) * n_axes, vmem_limit_bytes=VMEM_LIMIT, **kw)


def _largest_tile(n, cap, mult):
    best = None
    for t in range(mult, min(n, cap) + 1, mult):
        if n % t == 0:
            best = t
    assert best is not None, (n, cap, mult)
    return best


def _rms(x):
    return x * lax.rsqrt(jnp.mean(x * x, axis=-1, keepdims=True) + EPS)


def _pack_pairs(x):
    half = x.shape[1] // 2
    lo = lax.bitcast_convert_type(x[:, :half].astype(BF16).astype(F32), jnp.int32)
    hi = lax.bitcast_convert_type(x[:, half:].astype(BF16).astype(F32), jnp.int32)
    return lax.bitwise_or(lax.bitwise_and(hi, jnp.int32(-65536)), lax.shift_right_logical(lo, jnp.int32(16)))


def _unpack_pairs(w):
    lo = lax.bitcast_convert_type(lax.shift_left(w, jnp.int32(16)), F32)
    hi = lax.bitcast_convert_type(lax.bitwise_and(w, jnp.int32(-65536)), F32)
    return lo, hi


def _rope_block(x, cos, sin, half):
    if 2 * half == LANES:
        swapped = pltpu.roll(x, half, 1)
    else:
        lane = lax.broadcasted_iota(jnp.int32, x.shape, 1)
        first = (lane % (2 * half)) < half
        swapped = jnp.where(first, pltpu.roll(x, LANES - half, 1), pltpu.roll(x, half, 1))
    return x * cos + swapped * sin


def _rope_tables(rows, head_dim):
    n_freq = head_dim // 4
    inv_freq = ROPE_THETA ** (-jnp.arange(n_freq, dtype=F32) / n_freq)
    ang_r = jnp.arange(rows, dtype=F32)[:, None] * inv_freq
    ang_c = jnp.arange(GRID_W, dtype=F32)[:, None] * inv_freq
    per_row = lambda a: jnp.broadcast_to(a[:, None, :], (rows, GRID_W, n_freq)).reshape(rows * GRID_W, n_freq)
    per_col = lambda a: jnp.broadcast_to(a[None, :, :], (rows, GRID_W, n_freq)).reshape(rows * GRID_W, n_freq)
    cos_r, sin_r = per_row(jnp.cos(ang_r)), per_row(jnp.sin(ang_r))
    cos_c, sin_c = per_col(jnp.cos(ang_c)), per_col(jnp.sin(ang_c))
    cos = jnp.concatenate([cos_r, cos_r, cos_c, cos_c], axis=1)
    sin = jnp.concatenate([-sin_r, sin_r, -sin_c, sin_c], axis=1)
    return cos, sin


def _ada_kernel(c_ref, w_ref, b_ref, o_ref):
    c = c_ref[...]
    s = c * jax.nn.sigmoid(c)
    o_ref[...] = jnp.dot(s, w_ref[...], preferred_element_type=F32,
                         precision=lax.Precision.HIGHEST) + b_ref[...]


def _ada(c_pad, w_ada, b_ada):
    rows = c_pad.shape[0]
    n = w_ada.shape[1]
    tn = _largest_tile(n, ADA_COLS, LANES)
    return pl.pallas_call(
        _ada_kernel,
        grid=(n // tn,),
        in_specs=[pl.BlockSpec((rows, D_MODEL), lambda j: (0, 0)),
                  pl.BlockSpec((D_MODEL, tn), lambda j: (0, j)),
                  pl.BlockSpec((1, tn), lambda j: (0, j))],
        out_specs=pl.BlockSpec((rows, tn), lambda j: (0, j)),
        out_shape=jax.ShapeDtypeStruct((rows, n), F32),
        compiler_params=_params(1),
        name="ada",
    )(c_pad, w_ada, b_ada.reshape(1, n))


def _inproj_kernel(x_ref, n_ref, sc_ref, sh_ref, w_ref, o_ref, w_sc):
    @pl.when(pl.program_id(1) == 0)
    def _():
        w_sc[...] = w_ref[...].astype(BF16)

    y = _rms(x_ref[...]) * n_ref[...]
    h = (y * (1.0 + sc_ref[0]) + sh_ref[0]).astype(BF16)
    o_ref[...] = jnp.dot(h, w_sc[...], preferred_element_type=F32).astype(o_ref.dtype)


def _inproj(x2, norm, sc, sh, w, rows_per_batch):
    t = x2.shape[0]
    n = w.shape[1]
    tm = _largest_tile(rows_per_batch, INPROJ_ROWS, SUBLANES)
    tn = _largest_tile(n, INPROJ_COLS, LANES)
    per_b = rows_per_batch // tm
    if sc.shape[0] == 1:
        mod_map = lambda j, i: (0, 0, 0)
    else:
        mod_map = lambda j, i: (i // per_b, 0, 0)
    return pl.pallas_call(
        _inproj_kernel,
        grid=(n // tn, t // tm),
        in_specs=[pl.BlockSpec((tm, D_MODEL), lambda j, i: (i, 0)),
                  pl.BlockSpec((1, D_MODEL), lambda j, i: (0, 0)),
                  pl.BlockSpec((1, 1, D_MODEL), mod_map),
                  pl.BlockSpec((1, 1, D_MODEL), mod_map),
                  pl.BlockSpec((D_MODEL, tn), lambda j, i: (0, j))],
        out_specs=pl.BlockSpec((tm, tn), lambda j, i: (i, j)),
        out_shape=jax.ShapeDtypeStruct((t, n), BF16),
        scratch_shapes=[pltpu.VMEM((D_MODEL, tn), BF16)],
        compiler_params=_params(2),
        name="inproj",
    )(x2, norm, sc, sh, w)


def _kvprep_kernel(kc_ref, vc_ref, kl_ref, vl_ref, g_ref, cos_ref, sin_ref, k_ref, vt_ref, *, ctx_blocks):
    j = pl.program_id(1)
    vrows = ATTN_HEAD_DIM + ONES_ROWS

    def emit(kin_ref, vin_ref, use_rope):
        k = kin_ref[...].astype(F32)
        v = vin_ref[...].astype(F32)
        for g in range(ATTN_KV_HEADS):
            sl = slice(g * ATTN_HEAD_DIM, (g + 1) * ATTN_HEAD_DIM)
            kh = _rms(k[:, sl]) * g_ref[...]
            if use_rope:
                kh = _rope_block(kh, cos_ref[...], sin_ref[...], ATTN_HEAD_DIM // 4)
            k_ref[0, g] = kh.astype(BF16)
            vt_ref[0, g * vrows:g * vrows + ATTN_HEAD_DIM, :] = v[:, sl].T.astype(BF16)
            vt_ref[0, g * vrows + ATTN_HEAD_DIM:(g + 1) * vrows, :] = jnp.ones((ONES_ROWS, v.shape[0]), BF16)

    @pl.when(j < ctx_blocks)
    def _():
        emit(kc_ref, vc_ref, False)

    @pl.when(j >= ctx_blocks)
    def _():
        emit(kl_ref, vl_ref, True)


def _kvprep(proj_c, proj, gain, cos, sin, batch, n_ctx, seq):
    tm = KV_PREP_ROWS
    assert n_ctx % tm == 0 and seq % tm == 0
    cb, lb = n_ctx // tm, seq // tm
    lk = n_ctx + seq
    ctx_row = lambda b, j: b * cb + jnp.minimum(j, cb - 1)
    lat_blk = lambda j: jnp.maximum(j - cb, 0)
    lat_row = lambda b, j: b * lb + lat_blk(j)
    vrows = ATTN_KV_HEADS * (ATTN_HEAD_DIM + ONES_ROWS)
    return pl.pallas_call(
        functools.partial(_kvprep_kernel, ctx_blocks=cb),
        grid=(batch, cb + lb),
        in_specs=[pl.BlockSpec((tm, ATTN_KV_W), lambda b, j: (ctx_row(b, j), COL_KA // ATTN_KV_W)),
                  pl.BlockSpec((tm, ATTN_KV_W), lambda b, j: (ctx_row(b, j), COL_VA // ATTN_KV_W)),
                  pl.BlockSpec((tm, ATTN_KV_W), lambda b, j: (lat_row(b, j), COL_KA // ATTN_KV_W)),
                  pl.BlockSpec((tm, ATTN_KV_W), lambda b, j: (lat_row(b, j), COL_VA // ATTN_KV_W)),
                  pl.BlockSpec((1, ATTN_HEAD_DIM), lambda b, j: (0, 0)),
                  pl.BlockSpec((tm, ATTN_HEAD_DIM), lambda b, j: (lat_blk(j), 0)),
                  pl.BlockSpec((tm, ATTN_HEAD_DIM), lambda b, j: (lat_blk(j), 0))],
        out_specs=[pl.BlockSpec((1, ATTN_KV_HEADS, tm, ATTN_HEAD_DIM), lambda b, j: (b, 0, j, 0)),
                   pl.BlockSpec((1, vrows, tm), lambda b, j: (b, 0, j))],
        out_shape=[jax.ShapeDtypeStruct((batch, ATTN_KV_HEADS, lk, ATTN_HEAD_DIM), BF16),
                   jax.ShapeDtypeStruct((batch, vrows, lk), BF16)],
        compiler_params=_params(2),
        name="kvprep",
    )(proj_c, proj_c, proj, proj, gain, cos, sin)


def _ret_kernel(dch_ref, qf_ref, kf_ref, vf_ref, cosf_ref, sinf_ref, qb_ref, kb_ref, vb_ref, cosb_ref, sinb_ref,
                dmat_ref, din_ref, dout_ref, s0_ref, of_ref, ob_ref, sfin_ref, s_sc, *, use_rope, n_chunks, heads):
    hb = pl.program_id(1)
    c = pl.program_id(2)

    @pl.when(c == 0)
    def _():
        s_sc[...] = s0_ref[:, 0]

    def rope(x, cos, sin):
        return jnp.concatenate([_rope_block(x[:, i * LANES:(i + 1) * LANES], cos[:, i * LANES:(i + 1) * LANES],
                                            sin[:, i * LANES:(i + 1) * LANES], RET_QK_DIM // 4)
                                for i in range(RET_QK_DIM // LANES)], axis=1)

    sides = ((qf_ref, kf_ref, vf_ref, cosf_ref, sinf_ref), (qb_ref, kb_ref, vb_ref, cosb_ref, sinb_ref))
    chains = [(d, j) for d in range(2) for j in range(heads)]
    qs, ks, vs, ss = {}, {}, {}, {}
    for d, j in chains:
        q_ref, k_ref, v_ref, cos_ref, sin_ref = sides[d]
        q = q_ref[:, j * RET_QK_DIM:(j + 1) * RET_QK_DIM].astype(F32)
        k = k_ref[:, j * RET_QK_DIM:(j + 1) * RET_QK_DIM].astype(F32)
        if use_rope:
            q = rope(q, cos_ref[...], sin_ref[...])
            k = rope(k, cos_ref[...], sin_ref[...])
        qs[d, j], ks[d, j] = q, k
        vs[d, j] = v_ref[:, j * RET_V_DIM:(j + 1) * RET_V_DIM]
        ss[d, j] = s_sc[d, j]
    scores = {ch: lax.dot_general(qs[ch].astype(BF16), ks[ch].astype(BF16), (((1,), (1,)), ((), ())),
                                  preferred_element_type=F32) * dmat_ref[ch[0], ch[1]] for ch in chains}
    inter = {ch: jnp.dot((qs[ch] * din_ref[ch[0], ch[1]]).astype(BF16), ss[ch].astype(BF16),
                         preferred_element_type=F32) for ch in chains}
    intra = {ch: jnp.dot(scores[ch].astype(BF16), vs[ch], preferred_element_type=F32) for ch in chains}
    o_refs = (of_ref, ob_ref)
    for d, j in chains:
        o_refs[d][:, j * RET_V_DIM:(j + 1) * RET_V_DIM] = (intra[d, j] + inter[d, j]).astype(of_ref.dtype)
    s_new = {ch: ss[ch] * dch_ref[ch[0] * RET_HEADS + hb * heads + ch[1]]
             + jnp.dot((ks[ch] * dout_ref[ch[0], ch[1]]).T.astype(BF16), vs[ch], preferred_element_type=F32)
             for ch in chains}
    for d, j in chains:
        s_sc[d, j] = s_new[d, j]

    @pl.when(c == n_chunks - 1)
    def _():
        for d, j in chains:
            sfin_ref[d, 0, j] = s_new[d, j]


def _ret_tables(log_gamma, chunk, reverse, k_scale):
    pos = jnp.arange(chunk, dtype=F32)
    diff = pos[:, None] - pos[None, :]
    if reverse:
        diff = -diff
        mask = diff > 0
        p_in = chunk - pos
        p_out = pos
    else:
        mask = diff >= 0
        p_in = pos + 1.0
        p_out = chunk - 1.0 - pos
    lg = log_gamma.astype(F32)
    dmat = jnp.where(mask[None], jnp.exp(lg[:, None, None] * jnp.maximum(diff, 0.0)[None]), 0.0) * k_scale
    d_in = jnp.exp(lg[:, None] * p_in)
    d_out = jnp.exp(lg[:, None] * p_out) * k_scale
    d_in = jnp.broadcast_to(d_in[:, :, None], (RET_HEADS, chunk, RET_QK_DIM))
    d_out = jnp.broadcast_to(d_out[:, :, None], (RET_HEADS, chunk, RET_QK_DIM))
    d_chunk = jnp.exp(lg * chunk)
    return d_chunk, dmat, d_in, d_out


def _retention(proj, cos, sin, lg_f, lg_b, s0, batch, seq, use_rope):
    chunk = min(RET_CHUNK, seq)
    nc = seq // chunk
    hs = RET_HEADS_PER_STEP
    qw, vw = hs * RET_QK_DIM, hs * RET_V_DIM
    k_scale = RET_QK_DIM ** -0.5
    tabs = [_ret_tables(lg_f, chunk, False, k_scale), _ret_tables(lg_b, chunk, True, k_scale)]
    d_chunk, dmat, d_in, d_out = [jnp.stack([tabs[0][i], tabs[1][i]]) for i in range(4)]
    fwd = lambda c: c
    bwd = lambda c: nc - 1 - c

    def side(cidx):
        row = lambda b, c: b * nc + cidx(c)
        return [pl.BlockSpec((chunk, qw), lambda b, h, c: (row(b, c), COL_QR // qw + h)),
                pl.BlockSpec((chunk, qw), lambda b, h, c: (row(b, c), COL_KR // qw + h)),
                pl.BlockSpec((chunk, vw), lambda b, h, c: (row(b, c), COL_VR // vw + h)),
                pl.BlockSpec((chunk, RET_QK_DIM), lambda b, h, c: (cidx(c), 0)),
                pl.BlockSpec((chunk, RET_QK_DIM), lambda b, h, c: (cidx(c), 0))]

    state_spec = pl.BlockSpec((2, 1, hs, RET_QK_DIM, RET_V_DIM), lambda b, h, c: (0, b, h, 0, 0))
    return pl.pallas_call(
        functools.partial(_ret_kernel, use_rope=use_rope, n_chunks=nc, heads=hs),
        grid=(batch, RET_HEADS // hs, nc),
        in_specs=[pl.BlockSpec(memory_space=pltpu.SMEM)] + side(fwd) + side(bwd) + [
            pl.BlockSpec((2, hs, chunk, chunk), lambda b, h, c: (0, h, 0, 0)),
            pl.BlockSpec((2, hs, chunk, RET_QK_DIM), lambda b, h, c: (0, h, 0, 0)),
            pl.BlockSpec((2, hs, chunk, RET_QK_DIM), lambda b, h, c: (0, h, 0, 0)),
            state_spec],
        out_specs=[pl.BlockSpec((chunk, vw), lambda b, h, c: (b * nc + c, h)),
                   pl.BlockSpec((chunk, vw), lambda b, h, c: (b * nc + nc - 1 - c, h)),
                   state_spec],
        out_shape=[jax.ShapeDtypeStruct((batch * seq, RET_V_W), BF16),
                   jax.ShapeDtypeStruct((batch * seq, RET_V_W), BF16),
                   jax.ShapeDtypeStruct((2, batch, RET_HEADS, RET_QK_DIM, RET_V_DIM), F32)],
        scratch_shapes=[pltpu.VMEM((2, hs, RET_QK_DIM, RET_V_DIM), F32)],
        compiler_params=_params(3),
        name="ret",
    )(d_chunk.reshape(-1), proj, proj, proj, cos, sin, proj, proj, proj, cos, sin, dmat, d_in, d_out, s0)


def _attn_kernel(q_ref, g_ref, cos_ref, sin_ref, k_ref, vt_ref, o_ref, qt_sc, m_sc, acc_sc, s_sc,
                 *ring, tq, kc, n_chunks):
    p_bufs, a_bufs, x_bufs, u_bufs, b_bufs = (ring[3 * i:3 * i + 3] for i in range(5))

    q = q_ref[...].astype(F32)
    scale = ATTN_HEAD_DIM ** -0.5 * np.log2(np.e)
    for g in range(ATTN_GROUP):
        qh = _rms(q[:, g * ATTN_HEAD_DIM:(g + 1) * ATTN_HEAD_DIM]) * g_ref[...]
        qh = _rope_block(qh, cos_ref[...], sin_ref[...], ATTN_HEAD_DIM // 4) * scale
        qt_sc[:, g * tq:(g + 1) * tq] = qh.T.astype(BF16)
    m_sc[...] = jnp.full(m_sc.shape, -jnp.inf, F32)
    acc_sc[...] = jnp.zeros(acc_sc.shape, F32)

    def scores(c):
        off = pl.multiple_of(c * kc, kc)
        return jnp.dot(k_ref[0, 0, pl.ds(off, kc), :], qt_sc[...], preferred_element_type=F32)

    def exact_probabilities(slot, m_new):
        for r in range(0, kc, ATTN_EXP_SLAB):
            p_bufs[slot][r:r + ATTN_EXP_SLAB, :] = jnp.exp2(s_sc[r:r + ATTN_EXP_SLAB, :] - m_new).astype(BF16)
        b_bufs[slot][...] = jnp.ones(m_new.shape, F32)

    def stage_s(c, slot):
        s_t = scores(c)
        x_bufs[slot][...] = jnp.max(s_t, axis=0, keepdims=True)
        if isinstance(c, int) and c == 0:
            s_sc[...] = s_t
            return
        u = m_sc[...]
        u_bufs[slot][...] = u
        p_bufs[slot][...] = jnp.exp2(s_t - u).astype(BF16)

    def stage_f(c, slot):
        m_prev = m_sc[...]
        x = x_bufs[slot][...]
        m_new = jnp.maximum(m_prev, x)
        m_sc[...] = m_new
        a_bufs[slot][...] = jnp.exp2(m_prev - m_new)
        if isinstance(c, int) and c == 0:
            exact_probabilities(slot, m_new)
            return
        u = u_bufs[slot][...]
        lagging = jnp.max(x - u) > ATTN_LAG_LIMIT

        @pl.when(lagging)
        def _():
            s_sc[...] = scores(c)
            exact_probabilities(slot, m_new)

        @pl.when(jnp.logical_not(lagging))
        def _():
            b_bufs[slot][...] = jnp.exp2(u - m_new)

    def stage_a(c, slot):
        off = pl.multiple_of(c * kc, kc)
        acc_sc[...] = a_bufs[slot][...] * acc_sc[...] + b_bufs[slot][...] * jnp.dot(
            vt_ref[0, :, pl.ds(off, kc)], p_bufs[slot][...], preferred_element_type=F32)

    n = n_chunks

    def tick(t, slot, full=False):
        if full or 1 <= t <= n:
            stage_f(t - 1, (slot + 2) % 3)
        if full or t < n:
            stage_s(t, slot)
        if full or 2 <= t <= n + 1:
            stage_a(t - 2, (slot + 1) % 3)

    tick(0, 0)
    tick(1, 1)
    first = 2
    while first < n and (n - first) % 3 != 0:
        tick(first, first % 3)
        first += 1
    n_groups = (n - first) // 3 if n > first else 0
    if n_groups > 0:
        def group(i, carry):
            t = first + 3 * i
            for j in range(3):
                tick(t + j, (first + j) % 3, full=True)
            return carry
        lax.fori_loop(0, n_groups, group, 0)
    for t in range(max(n, 2), n + 2):
        tick(t, t % 3)

    acc = acc_sc[...]
    o = acc[:ATTN_HEAD_DIM] / acc[ATTN_HEAD_DIM:ATTN_HEAD_DIM + 1]
    for g in range(ATTN_GROUP):
        o_ref[:, g * ATTN_HEAD_DIM:(g + 1) * ATTN_HEAD_DIM] = o[:, g * tq:(g + 1) * tq].T.astype(o_ref.dtype)


def _attention(proj, gain, cos, sin, k_all, vt_all, batch, seq):
    lk = k_all.shape[2]
    kc = _largest_tile(lk, ATTN_KEY_CHUNK, MXU_DEPTH)
    tq = _largest_tile(seq, ATTN_Q_TILE, SUBLANES)
    nq = seq // tq
    qw = ATTN_GROUP * ATTN_HEAD_DIM
    cols = ATTN_GROUP * tq
    vrows = ATTN_HEAD_DIM + ONES_ROWS
    return pl.pallas_call(
        functools.partial(_attn_kernel, tq=tq, kc=kc, n_chunks=lk // kc),
        grid=(batch, ATTN_KV_HEADS, nq),
        in_specs=[pl.BlockSpec((tq, qw), lambda b, g, i: (b * nq + i, COL_QA // qw + g)),
                  pl.BlockSpec((1, ATTN_HEAD_DIM), lambda b, g, i: (0, 0)),
                  pl.BlockSpec((tq, ATTN_HEAD_DIM), lambda b, g, i: (i, 0)),
                  pl.BlockSpec((tq, ATTN_HEAD_DIM), lambda b, g, i: (i, 0)),
                  pl.BlockSpec((1, 1, lk, ATTN_HEAD_DIM), lambda b, g, i: (b, g, 0, 0), pipeline_mode=pl.Buffered(1)),
                  pl.BlockSpec((1, vrows, lk), lambda b, g, i: (b, g, 0), pipeline_mode=pl.Buffered(1))],
        out_specs=pl.BlockSpec((tq, qw), lambda b, g, i: (b * nq + i, g)),
        out_shape=jax.ShapeDtypeStruct((batch * seq, ATTN_Q_W), BF16),
        scratch_shapes=[pltpu.VMEM((ATTN_HEAD_DIM, cols), BF16),
                        pltpu.VMEM((1, cols), F32),
                        pltpu.VMEM((vrows, cols), F32),
                        pltpu.VMEM((kc, cols), F32)] + [pltpu.VMEM((kc, cols), BF16)] * 3
        + [pltpu.VMEM((1, cols), F32)] * 12,
        compiler_params=_params(3),
        name="attn",
    )(proj, gain, cos, sin, k_all, vt_all)


def _merge_kernel(x_ref, g1_ref, of_ref, ob_ref, gr_ref, ao_ref, gtr_ref, gta_ref,
                  wro_ref, wao_ref, wo_ref, o_ref):
    ro = of_ref[...].astype(F32) + ob_ref[...].astype(F32)
    gr = gr_ref[...].astype(F32)
    parts = []
    for h in range(RET_HEADS):
        sl = slice(h * RET_V_DIM, (h + 1) * RET_V_DIM)
        g = gr[:, sl]
        parts.append((g * jax.nn.sigmoid(g) * _rms(ro[:, sl])).astype(BF16))
    ret_in = jnp.concatenate(parts, axis=1)
    ret_branch = jnp.dot(ret_in, wro_ref[...], preferred_element_type=F32)
    attn_branch = jnp.dot(ao_ref[...], wao_ref[...], preferred_element_type=F32)
    y = (jax.nn.sigmoid(gtr_ref[...].astype(F32)) * ret_branch
         + jax.nn.sigmoid(gta_ref[...].astype(F32)) * attn_branch)
    y = jnp.dot(y.astype(BF16), wo_ref[...], preferred_element_type=F32)
    o_ref[...] = x_ref[...] + g1_ref[0] * y


def _merge(x2, g1, o_f, o_b, proj, attn_o, w_ret_o, w_attn_o, w_out, seq):
    t = x2.shape[0]
    tm = _largest_tile(seq, MERGE_ROWS, SUBLANES)
    per_b = seq // tm
    full = lambda i: (0, 0)
    return pl.pallas_call(
        _merge_kernel,
        grid=(t // tm,),
        in_specs=[pl.BlockSpec((tm, D_MODEL), lambda i: (i, 0)),
                  pl.BlockSpec((1, 1, D_MODEL), lambda i: (i // per_b, 0, 0)),
                  pl.BlockSpec((tm, RET_V_W), lambda i: (i, 0)),
                  pl.BlockSpec((tm, RET_V_W), lambda i: (i, 0)),
                  pl.BlockSpec((tm, RET_V_W), lambda i: (i, COL_GR // RET_V_W)),
                  pl.BlockSpec((tm, ATTN_Q_W), lambda i: (i, 0)),
                  pl.BlockSpec((tm, D_MODEL), lambda i: (i, COL_GATE_R // D_MODEL)),
                  pl.BlockSpec((tm, D_MODEL), lambda i: (i, COL_GATE_A // D_MODEL)),
                  pl.BlockSpec((RET_V_W, D_MODEL), full, pipeline_mode=pl.Buffered(1)),
                  pl.BlockSpec((ATTN_Q_W, D_MODEL), full, pipeline_mode=pl.Buffered(1)),
                  pl.BlockSpec((D_MODEL, D_MODEL), full, pipeline_mode=pl.Buffered(1))],
        out_specs=pl.BlockSpec((tm, D_MODEL), lambda i: (i, 0)),
        out_shape=jax.ShapeDtypeStruct((t, D_MODEL), F32),
        compiler_params=_params(1),
        name="merge",
    )(x2, g1, o_f, o_b, proj, attn_o, proj, proj, w_ret_o, w_attn_o, w_out)


def _router_kernel(x_ref, n_ref, sc_ref, sh_ref, wr_ref, br_ref, h_ref, ti_ref, tw_ref):
    h = _rms(x_ref[...]) * n_ref[...] * (1.0 + sc_ref[0]) + sh_ref[0]
    h_ref[...] = _pack_pairs(h)
    nt = (((1,), (1,)), ((), ()))
    w = wr_ref[...]
    h_hi = h.astype(BF16)
    h_lo = (h - h_hi.astype(F32)).astype(BF16)
    w_hi = w.astype(BF16)
    w_lo = (w - w_hi.astype(F32)).astype(BF16)
    logits = (lax.dot_general(w_hi, h_hi, nt, preferred_element_type=F32)
              + lax.dot_general(w_lo, h_hi, nt, preferred_element_type=F32)
              + lax.dot_general(w_hi, h_lo, nt, preferred_element_type=F32)) + br_ref[...]
    eid = lax.broadcasted_iota(jnp.int32, logits.shape, 0)
    vals = logits
    top_v = []
    top_i = []
    for _ in range(TOP_K):
        m = jnp.max(vals, axis=0, keepdims=True)
        idx = jnp.min(jnp.where(vals == m, eid, N_EXPERTS), axis=0, keepdims=True)
        top_v.append(m)
        top_i.append(idx)
        vals = jnp.where(eid == idx, -jnp.inf, vals)
    ex = [jnp.exp(v - top_v[0]) for v in top_v]
    denom = ex[0] + ex[1] + ex[2] + ex[3]
    ti_ref[...] = jnp.concatenate(top_i, axis=0)
    tw_ref[...] = jnp.concatenate([e / denom for e in ex], axis=0)


def _router(x1, norm, sc, sh, w_router_t, b_router, seq):
    t = x1.shape[0]
    tm = _largest_tile(seq, ROUTER_ROWS, LANES)
    per_b = seq // tm
    mod_map = lambda i: (i // per_b, 0, 0)
    return pl.pallas_call(
        _router_kernel,
        grid=(t // tm,),
        in_specs=[pl.BlockSpec((tm, D_MODEL), lambda i: (i, 0)),
                  pl.BlockSpec((1, D_MODEL), lambda i: (0, 0)),
                  pl.BlockSpec((1, 1, D_MODEL), mod_map),
                  pl.BlockSpec((1, 1, D_MODEL), mod_map),
                  pl.BlockSpec((N_EXPERTS, D_MODEL), lambda i: (0, 0)),
                  pl.BlockSpec((N_EXPERTS, 1), lambda i: (0, 0))],
        out_specs=[pl.BlockSpec((tm, PACK_W), lambda i: (i, 0)),
                   pl.BlockSpec((TOP_K, tm), lambda i: (0, i)),
                   pl.BlockSpec((TOP_K, tm), lambda i: (0, i))],
        out_shape=[jax.ShapeDtypeStruct((t, PACK_W), PACK_DTYPE),
                   jax.ShapeDtypeStruct((TOP_K, t), jnp.int32),
                   jax.ShapeDtypeStruct((TOP_K, t), F32)],
        compiler_params=_params(1),
        name="router",
    )(x1, norm, sc, sh, w_router_t, b_router)


def _gather_rows(src, idx):
    n, w = src.shape
    r = idx.shape[0]
    workers = SC_CORES * SC_SUBCORES
    per_w = r // workers
    n_win = per_w // SC_GATHER_ROWS
    assert per_w * workers == r and n_win * SC_GATHER_ROWS == per_w, (r, workers, SC_GATHER_ROWS)
    mesh = plsc.VectorSubcoreMesh(core_axis_name="c", subcore_axis_name="s")

    @functools.partial(
        pl.kernel, mesh=mesh, out_type=jax.ShapeDtypeStruct((r, w), src.dtype),
        scratch_types=[pltpu.VMEM((SC_GATHER_ROWS,), jnp.int32),
                       pltpu.VMEM((SC_GATHER_ROWS, w), src.dtype),
                       pltpu.SemaphoreType.DMA])
    def gather(src_hbm, idx_hbm, out_hbm, idx_v, rows_v, sem):
        wid = lax.axis_index("s") * SC_CORES + lax.axis_index("c")
        base = wid * per_w

        @pl.loop(0, n_win)
        def _(win):
            off = base + win * SC_GATHER_ROWS
            pltpu.sync_copy(idx_hbm.at[pl.ds(off, SC_GATHER_ROWS)], idx_v)
            pltpu.async_copy(src_hbm.at[idx_v], rows_v, sem).wait()
            pltpu.sync_copy(rows_v, out_hbm.at[pl.ds(off, SC_GATHER_ROWS)])

    return gather(src, idx)


def _scatter_rows(src, dest, n_out, first, group_rows):
    n, w = src.shape
    n_assign = dest.shape[0]
    workers = SC_CORES * SC_SUBCORES
    per_w = n_assign // workers
    n_win = per_w // SC_GATHER_ROWS
    assert per_w * workers == n_assign and n_win * SC_GATHER_ROWS == per_w and group_rows % per_w == 0
    mesh = plsc.VectorSubcoreMesh(core_axis_name="c", subcore_axis_name="s")

    @functools.partial(
        pl.kernel, mesh=mesh, out_type=jax.ShapeDtypeStruct((n_out, w), src.dtype),
        scratch_types=[pltpu.VMEM((SC_GATHER_ROWS,), jnp.int32),
                       pltpu.VMEM((SC_GATHER_ROWS, w), src.dtype),
                       pltpu.SemaphoreType.DMA])
    def scatter(src_hbm, dest_hbm, out_hbm, idx_v, rows_v, sem):
        wid = lax.axis_index("s") * SC_CORES + lax.axis_index("c")
        base = wid * per_w
        row_base = first + lax.rem(base, group_rows)

        @pl.loop(0, n_win)
        def _(win):
            pltpu.sync_copy(dest_hbm.at[pl.ds(base + win * SC_GATHER_ROWS, SC_GATHER_ROWS)], idx_v)
            pltpu.sync_copy(src_hbm.at[pl.ds(row_base + win * SC_GATHER_ROWS, SC_GATHER_ROWS)], rows_v)
            pltpu.async_copy(rows_v, out_hbm.at[idx_v], sem).wait()

    return scatter(src, dest)


def _ffn_kernel(te_ref, tf_ref, tr_ref, ts_ref, tn_ref, tp_ref, x_ref, w1_hbm, b1_ref, w2_hbm, b2_ref, o_ref,
                w1_sc, w2_sc, w1_st, w2_st, sem):
    j = pl.program_id(0)
    flag = tf_ref[j]
    n_rows = tr_ref[j]

    def weight_copies(expert, slot):
        return (pltpu.make_async_copy(w1_hbm.at[expert], w1_st.at[slot], sem.at[0, slot]),
                pltpu.make_async_copy(w2_hbm.at[expert], w2_st.at[slot], sem.at[1, slot]))

    @pl.when(flag == TILE_FIRST)
    def _():
        slot = ts_ref[j]

        @pl.when(tp_ref[j] != 0)
        def _():
            for cp in weight_copies(te_ref[j], slot):
                cp.start()

        for cp in weight_copies(te_ref[j], slot):
            cp.wait()
        w1_sc[...] = w1_st[slot].astype(BF16)
        w2_sc[...] = w2_st[slot].astype(BF16)

        @pl.when(tn_ref[j] >= 0)
        def _():
            for cp in weight_copies(tn_ref[j], 1 - slot):
                cp.start()

    @pl.when(flag == TILE_PAD)
    def _():
        o_ref[...] = jnp.zeros(o_ref.shape, o_ref.dtype)

    @pl.when(flag != TILE_PAD)
    def _():
        xw = x_ref[...]
        rid = lax.broadcasted_iota(jnp.int32, xw.shape, 0)
        lo, hi = _unpack_pairs(jnp.where(rid < n_rows, xw, jnp.zeros_like(xw)))
        x = jnp.concatenate([lo, hi], axis=1).astype(BF16)
        a = jnp.dot(x, w1_sc[...], preferred_element_type=F32) + b1_ref[0]
        gate = jnp.minimum(a[:, :EXPERT_FF], SWIGLU_LIMIT)
        up = jnp.clip(a[:, EXPERT_FF:], -SWIGLU_LIMIT, SWIGLU_LIMIT)
        act = gate * jax.nn.sigmoid(SWIGLU_ALPHA * gate) * (up + 1.0)
        y = jnp.dot(act.astype(BF16), w2_sc[...], preferred_element_type=F32) + b2_ref[0]
        o_ref[...] = _pack_pairs(y)


def _ffn(xs, tile_expert, tile_flag, tile_rows, w1, b1, w2, b2, tg):
    p = xs.shape[0]
    n_tiles = p // tg
    tile_slot, tile_next, tile_prime = _ffn_ring(tile_expert, tile_flag)
    row_map = lambda j, te, tf, tr, ts, tn, tp: (j, 0)
    bias_map = lambda j, te, tf, tr, ts, tn, tp: (te[j], 0, 0)
    grid_spec = pltpu.PrefetchScalarGridSpec(
        num_scalar_prefetch=6,
        grid=(n_tiles,),
        in_specs=[pl.BlockSpec((tg, PACK_W), row_map),
                  pl.BlockSpec(memory_space=pl.ANY),
                  pl.BlockSpec((1, 1, 2 * EXPERT_FF), bias_map),
                  pl.BlockSpec(memory_space=pl.ANY),
                  pl.BlockSpec((1, 1, D_MODEL), bias_map)],
        out_specs=pl.BlockSpec((tg, PACK_W), row_map),
        scratch_shapes=[pltpu.VMEM((D_MODEL, 2 * EXPERT_FF), BF16),
                        pltpu.VMEM((EXPERT_FF, D_MODEL), BF16),
                        pltpu.VMEM((2, D_MODEL, 2 * EXPERT_FF), F32),
                        pltpu.VMEM((2, EXPERT_FF, D_MODEL), F32),
                        pltpu.SemaphoreType.DMA((2, 2))],
    )
    return pl.pallas_call(
        _ffn_kernel,
        grid_spec=grid_spec,
        out_shape=jax.ShapeDtypeStruct((p, PACK_W), PACK_DTYPE),
        compiler_params=_params(1),
        name="ffn",
    )(tile_expert, tile_flag, tile_rows, tile_slot, tile_next, tile_prime, xs, w1, b1, w2, b2)


def _ffn_ring(tile_expert, tile_flag):
    n_tiles = tile_flag.shape[0]
    first = tile_flag == TILE_FIRST
    ordinal = jnp.cumsum(first.astype(jnp.int32)) - 1
    tile_slot = jnp.where(first, ordinal % 2, 0).astype(jnp.int32)
    tile_prime = (first & (ordinal == 0)).astype(jnp.int32)
    tiles = jnp.arange(n_tiles, dtype=jnp.int32)
    first_at_or_after = lax.cummin(jnp.where(first, tiles, n_tiles)[::-1], axis=0)[::-1]
    next_first = jnp.concatenate([first_at_or_after[1:], jnp.full((1,), n_tiles, jnp.int32)])
    tile_next = jnp.where(first & (next_first < n_tiles),
                          tile_expert[jnp.minimum(next_first, n_tiles - 1)], -1).astype(jnp.int32)
    return tile_slot, tile_next, tile_prime


def _combine_kernel(x_ref, g2_ref, w_ref, y_ref, o_ref):
    w = w_ref[...]
    acc_lo = None
    for k in range(TOP_K):
        lo, hi = _unpack_pairs(y_ref[k])
        wk = w[:, k:k + 1]
        acc_lo = wk * lo if acc_lo is None else acc_lo + wk * lo
        acc_hi = wk * hi if k == 0 else acc_hi + wk * hi
    acc = jnp.concatenate([acc_lo, acc_hi], axis=1)
    o_ref[...] = x_ref[...] + g2_ref[0] * acc


def _combine_into_kernel(x_ref, g2_ref, w_ref, y_ref, prev_ref, o_ref):
    del prev_ref
    _combine_kernel(x_ref, g2_ref, w_ref, y_ref, o_ref)


def _combine(x1, g2, w_tok, yk, seq, first_row, prev):
    t = x1.shape[0]
    rows = yk.shape[1]
    tm = _largest_tile(seq, COMBINE_ROWS, SUBLANES)
    per_b = seq // tm
    i0 = first_row // tm
    in_specs = [pl.BlockSpec((tm, D_MODEL), lambda i: (i0 + i, 0)),
                pl.BlockSpec((1, 1, D_MODEL), lambda i: ((i0 + i) // per_b, 0, 0)),
                pl.BlockSpec((tm, TOP_K), lambda i: (i0 + i, 0)),
                pl.BlockSpec((TOP_K, tm, PACK_W), lambda i: (0, i, 0))]
    args = [x1, g2, w_tok, yk]
    if prev is not None:
        in_specs.append(pl.BlockSpec(memory_space=pl.ANY))
        args.append(prev)
    return pl.pallas_call(
        _combine_kernel if prev is None else _combine_into_kernel,
        grid=(rows // tm,),
        in_specs=in_specs,
        out_specs=pl.BlockSpec((tm, D_MODEL), lambda i: (i0 + i, 0)),
        out_shape=jax.ShapeDtypeStruct((t, D_MODEL), F32),
        input_output_aliases={} if prev is None else {len(args) - 1: 0},
        compiler_params=_params(1),
        name="combine",
    )(*args)


def _count_kernel(ti_ref, cnt_ref):
    @pl.when(pl.program_id(0) == 0)
    def _():
        cnt_ref[...] = jnp.zeros(cnt_ref.shape, F32)

    bt = ti_ref.shape[1]
    eid = lax.broadcasted_iota(jnp.int32, (N_EXPERTS, bt), 0)
    acc = jnp.zeros(cnt_ref.shape, F32)
    for k in range(TOP_K):
        m = (eid == ti_ref[k:k + 1, :]).astype(F32)
        for c in range(bt // LANES):
            acc = acc + m[:, c * LANES:(c + 1) * LANES]
    cnt_ref[...] += acc


def _rank_kernel(ti_ref, off_ref, tri_ref, dest_ref, run_sc):
    @pl.when(pl.program_id(0) == 0)
    def _():
        run_sc[...] = off_ref[...] - 1.0

    bt = ti_ref.shape[1]
    eid = lax.broadcasted_iota(jnp.int32, (N_EXPERTS, bt), 0)
    run = run_sc[...]
    for k in range(TOP_K):
        m = eid == ti_ref[k:k + 1, :]
        pre = jnp.dot(jnp.where(m, 1.0, 0.0).astype(BF16), tri_ref[...], preferred_element_type=F32)
        slot = jnp.sum(jnp.where(m, pre + run, 0.0), axis=0, keepdims=True)
        dest_ref[k:k + 1, :] = slot.astype(jnp.int32)
        run = run + pre[:, bt - 1:bt]
    run_sc[...] = run


def _plan(top_i, tg):
    t = top_i.shape[1]
    n_assign = TOP_K * t
    p = n_assign + N_EXPERTS * tg
    n_tiles = p // tg
    bt = _largest_tile(t, PLAN_BLOCK, LANES)
    cnt = pl.pallas_call(
        _count_kernel,
        grid=(t // bt,),
        in_specs=[pl.BlockSpec((TOP_K, bt), lambda i: (0, i))],
        out_specs=pl.BlockSpec((N_EXPERTS, LANES), lambda i: (0, 0)),
        out_shape=jax.ShapeDtypeStruct((N_EXPERTS, LANES), F32),
        compiler_params=_params(1),
        name="count",
    )(top_i)
    counts = jnp.sum(cnt, axis=1).astype(jnp.int32)
    padded = ((counts + tg - 1) // tg) * tg
    off_end = jnp.cumsum(padded)
    off = off_end - padded
    tri = (jnp.arange(bt)[:, None] <= jnp.arange(bt)[None, :]).astype(BF16)
    dest = pl.pallas_call(
        _rank_kernel,
        grid=(t // bt,),
        in_specs=[pl.BlockSpec((TOP_K, bt), lambda i: (0, i)),
                  pl.BlockSpec((N_EXPERTS, 1), lambda i: (0, 0)),
                  pl.BlockSpec((bt, bt), lambda i: (0, 0))],
        out_specs=pl.BlockSpec((TOP_K, bt), lambda i: (0, i)),
        out_shape=jax.ShapeDtypeStruct((TOP_K, t), jnp.int32),
        scratch_shapes=[pltpu.VMEM((N_EXPERTS, 1), F32)],
        compiler_params=_params(1),
        name="rank",
    )(top_i, off.astype(F32).reshape(N_EXPERTS, 1), tri)
    tile_start = jnp.arange(n_tiles, dtype=jnp.int32) * tg
    tile_valid = tile_start < off_end[-1]
    te = jnp.sum((tile_start[:, None] >= off_end[None, :]).astype(jnp.int32), axis=1)
    last_e = jnp.sum(((off_end[-1] - 1) >= off_end).astype(jnp.int32))
    tile_expert = jnp.where(tile_valid, te, last_e)
    tile_rows = jnp.where(tile_valid, jnp.clip(counts[tile_expert] - (tile_start - off[tile_expert]), 0, tg), 0)
    changed = jnp.concatenate([jnp.ones((1,), jnp.bool_), tile_expert[1:] != tile_expert[:-1]])
    tile_flag = jnp.where(tile_valid, jnp.where(changed, TILE_FIRST, TILE_BODY), TILE_PAD).astype(jnp.int32)
    return dest.reshape(-1), tile_expert, tile_flag, tile_rows.astype(jnp.int32)


def kernel(x, c, ctx, c_ctx, norm1, norm2, w_ada, b_ada, w_in, ret_decay_f, ret_decay_b, attn_q_norm, attn_k_norm,
           w_ret_o, w_attn_o, w_out, w_router, b_router, w_exp_in, b_exp_in, w_exp_out, b_exp_out):
    assert w_in.shape[0] == 1, "single-layer block"
    b, seq, d = x.shape
    n_ctx = ctx.shape[1]
    t = b * seq
    rows = seq // GRID_W

    idx = np.cumsum(IN_SIZES)[:-1].tolist()
    wq_r, wk_r, wv_r, wg_r, wq_a, wk_a, wv_a, wgt_r, wgt_a = jnp.split(w_in[0], idx, axis=-1)
    w_in_p = jnp.concatenate([wq_r, wk_r, wv_r, wg_r, wq_a, wgt_r, wgt_a, wk_a, wv_a], axis=-1)
    w1 = w_exp_in[0]
    w2 = w_exp_out[0]
    b1 = b_exp_in[0].reshape(N_EXPERTS, 1, 2 * EXPERT_FF)
    b2 = b_exp_out[0].reshape(N_EXPERTS, 1, D_MODEL)

    pad = (-(b + 1)) % SUBLANES
    c_all = jnp.concatenate([c, c_ctx[None, :], jnp.zeros((pad, d), F32)], axis=0)
    mod = _ada(c_all, w_ada[0], b_ada[0])
    sh1, sc1, g1, sh2, sc2, g2 = [m.reshape(-1, 1, d) for m in jnp.split(mod, 6, axis=-1)]
    lat = lambda m: m[:b]
    cx = lambda m: m[b:b + 1]

    x2 = x.reshape(t, d)
    proj = _inproj(x2, norm1, lat(sc1), lat(sh1), w_in_p, seq)
    proj_c = _inproj(ctx.reshape(b * n_ctx, d), norm1, cx(sc1), cx(sh1), w_in_p, n_ctx)

    cos_r, sin_r = _rope_tables(rows, RET_QK_DIM)
    cos_a, sin_a = _rope_tables(rows, ATTN_HEAD_DIM)
    lg_f = -jax.nn.softplus(ret_decay_f[0].astype(F32))
    lg_b = -jax.nn.softplus(ret_decay_b[0].astype(F32))
    zero_state = jnp.zeros((2, b, RET_HEADS, RET_QK_DIM, RET_V_DIM), F32)
    cos_c = jnp.ones((n_ctx, RET_QK_DIM), F32)
    sin_c = jnp.zeros((n_ctx, RET_QK_DIM), F32)
    _, _, s_ctx = _retention(proj_c, cos_c, sin_c, lg_f, lg_b, zero_state, b, n_ctx, False)
    o_f, o_b, _ = _retention(proj, cos_r, sin_r, lg_f, lg_b, s_ctx, b, seq, True)

    k_all, vt_all = _kvprep(proj_c, proj, attn_k_norm, cos_a, sin_a, b, n_ctx, seq)
    attn_o = _attention(proj, attn_q_norm, cos_a, sin_a, k_all, vt_all, b, seq)

    x1 = _merge(x2, lat(g1), o_f, o_b, proj, attn_o, w_ret_o[0].astype(BF16), w_attn_o[0].astype(BF16),
                w_out[0].astype(BF16), seq)

    h2, top_i, top_w = _router(x1, norm2, lat(sc2), lat(sh2), w_router[0].T, b_router[0].reshape(N_EXPERTS, 1), seq)
    n_groups = MOE_GROUPS if b % MOE_GROUPS == 0 else 1
    tgrp = t // n_groups
    tg = _largest_tile(TOP_K * tgrp, FFN_ROWS, SUBLANES)
    w_tok = top_w.T
    out = None
    for grp in range(n_groups):
        first = grp * tgrp
        dest, tile_expert, tile_flag, tile_rows = _plan(top_i[:, first:first + tgrp], tg)
        xs = _scatter_rows(h2, dest, TOP_K * tgrp + N_EXPERTS * tg, first, tgrp)
        ys = _ffn(xs, tile_expert, tile_flag, tile_rows, w1, b1, w2, b2, tg)
        yk = _gather_rows(ys, dest).reshape(TOP_K, tgrp, PACK_W)
        out = _combine(x1, lat(g2), w_tok, yk, seq, first, out)
    return out.reshape(b, seq, d)
```

```python
import functools

import jax
import jax.numpy as jnp
import numpy as np
from jax import lax
from jax.experimental import pallas as pl
from jax.experimental.pallas import tpu as pltpu
from jax.experimental.pallas import tpu_sc as plsc

F32 = jnp.float32
BF16 = jnp.bfloat16

D_MODEL = 1024
GRID_W = 64
EPS = 1e-6
RET_HEADS = 4
RET_QK_DIM = 256
RET_V_DIM = 512
ATTN_HEADS = 8
ATTN_KV_HEADS = 2
ATTN_GROUP = ATTN_HEADS // ATTN_KV_HEADS
ATTN_HEAD_DIM = 128
ROPE_THETA = 10000.0
N_EXPERTS = 32
TOP_K = 4
EXPERT_FF = 1024
SWIGLU_LIMIT = 7.0
SWIGLU_ALPHA = 1.702

RET_QK_W = RET_HEADS * RET_QK_DIM
RET_V_W = RET_HEADS * RET_V_DIM
ATTN_Q_W = ATTN_HEADS * ATTN_HEAD_DIM
ATTN_KV_W = ATTN_KV_HEADS * ATTN_HEAD_DIM
IN_SIZES = (RET_QK_W, RET_QK_W, RET_V_W, RET_V_W, ATTN_Q_W, ATTN_KV_W, ATTN_KV_W, D_MODEL, D_MODEL)
IN_WIDTH = sum(IN_SIZES)
COL_QR = 0
COL_KR = COL_QR + RET_QK_W
COL_VR = COL_KR + RET_QK_W
COL_GR = COL_VR + RET_V_W
COL_QA = COL_GR + RET_V_W
COL_GATE_R = COL_QA + ATTN_Q_W
COL_GATE_A = COL_GATE_R + D_MODEL
COL_KA = COL_GATE_A + D_MODEL
COL_VA = COL_KA + ATTN_KV_W

RET_CHUNK = 256
RET_HEADS_PER_STEP = 4
LANES = 128
SUBLANES = 8
SC_CORES = 2
SC_SUBCORES = 16
SC_GATHER_ROWS = 128
MXU_DEPTH = 256
ADA_COLS = 1536
INPROJ_ROWS = 1024
INPROJ_COLS = 2560
MERGE_ROWS = 512
ROUTER_ROWS = 2048
COMBINE_ROWS = 1024
FFN_ROWS = 512
ATTN_KEY_CHUNK = 768
ATTN_Q_TILE = 1024
ATTN_EXP_SLAB = 32
ATTN_LAG_LIMIT = 64.0
ONES_ROWS = 16
KV_PREP_ROWS = 256
PACK_DTYPE = jnp.int32
PACK_W = D_MODEL // 2
MOE_GROUPS = 2
PLAN_BLOCK = 512
TILE_PAD, TILE_BODY, TILE_FIRST = 0, 1, 2
VMEM_LIMIT = 56 * 1024 * 1024

ARB = pltpu.ARBITRARY


def _params(n_axes, **kw):
    return pltpu.CompilerParams(dimension_semantics=(ARB,) * n_axes, vmem_limit_bytes=VMEM_LIMIT, **kw)


def _largest_tile(n, cap, mult):
    best = None
    for t in range(mult, min(n, cap) + 1, mult):
        if n % t == 0:
            best = t
    assert best is not None, (n, cap, mult)
    return best


def _rms(x):
    return x * lax.rsqrt(jnp.mean(x * x, axis=-1, keepdims=True) + EPS)


def _pack_pairs(x):
    half = x.shape[1] // 2
    lo = lax.bitcast_convert_type(x[:, :half].astype(BF16).astype(F32), jnp.int32)
    hi = lax.bitcast_convert_type(x[:, half:].astype(BF16).astype(F32), jnp.int32)
    return lax.bitwise_or(lax.bitwise_and(hi, jnp.int32(-65536)), lax.shift_right_logical(lo, jnp.int32(16)))


def _unpack_pairs(w):
    lo = lax.bitcast_convert_type(lax.shift_left(w, jnp.int32(16)), F32)
    hi = lax.bitcast_convert_type(lax.bitwise_and(w, jnp.int32(-65536)), F32)
    return lo, hi


def _rope_block(x, cos, sin, half):
    if 2 * half == LANES:
        swapped = pltpu.roll(x, half, 1)
    else:
        lane = lax.broadcasted_iota(jnp.int32, x.shape, 1)
        first = (lane % (2 * half)) < half
        swapped = jnp.where(first, pltpu.roll(x, LANES - half, 1), pltpu.roll(x, half, 1))
    return x * cos + swapped * sin


def _rope_tables(rows, head_dim):
    n_freq = head_dim // 4
    inv_freq = ROPE_THETA ** (-jnp.arange(n_freq, dtype=F32) / n_freq)
    ang_r = jnp.arange(rows, dtype=F32)[:, None] * inv_freq
    ang_c = jnp.arange(GRID_W, dtype=F32)[:, None] * inv_freq
    per_row = lambda a: jnp.broadcast_to(a[:, None, :], (rows, GRID_W, n_freq)).reshape(rows * GRID_W, n_freq)
    per_col = lambda a: jnp.broadcast_to(a[None, :, :], (rows, GRID_W, n_freq)).reshape(rows * GRID_W, n_freq)
    cos_r, sin_r = per_row(jnp.cos(ang_r)), per_row(jnp.sin(ang_r))
    cos_c, sin_c = per_col(jnp.cos(ang_c)), per_col(jnp.sin(ang_c))
    cos = jnp.concatenate([cos_r, cos_r, cos_c, cos_c], axis=1)
    sin = jnp.concatenate([-sin_r, sin_r, -sin_c, sin_c], axis=1)
    return cos, sin


def _ada_kernel(c_ref, w_ref, b_ref, o_ref):
    c = c_ref[...]
    s = c * jax.nn.sigmoid(c)
    o_ref[...] = jnp.dot(s, w_ref[...], preferred_element_type=F32,
                         precision=lax.Precision.HIGHEST) + b_ref[...]


def _ada(c_pad, w_ada, b_ada):
    rows = c_pad.shape[0]
    n = w_ada.shape[1]
    tn = _largest_tile(n, ADA_COLS, LANES)
    return pl.pallas_call(
        _ada_kernel,
        grid=(n // tn,),
        in_specs=[pl.BlockSpec((rows, D_MODEL), lambda j: (0, 0)),
                  pl.BlockSpec((D_MODEL, tn), lambda j: (0, j)),
                  pl.BlockSpec((1, tn), lambda j: (0, j))],
        out_specs=pl.BlockSpec((rows, tn), lambda j: (0, j)),
        out_shape=jax.ShapeDtypeStruct((rows, n), F32),
        compiler_params=_params(1),
        name="ada",
    )(c_pad, w_ada, b_ada.reshape(1, n))


def _inproj_kernel(x_ref, n_ref, sc_ref, sh_ref, w_ref, o_ref, w_sc):
    @pl.when(pl.program_id(1) == 0)
    def _():
        w_sc[...] = w_ref[...].astype(BF16)

    y = _rms(x_ref[...]) * n_ref[...]
    h = (y * (1.0 + sc_ref[0]) + sh_ref[0]).astype(BF16)
    o_ref[...] = jnp.dot(h, w_sc[...], preferred_element_type=F32).astype(o_ref.dtype)


def _inproj(x2, norm, sc, sh, w, rows_per_batch):
    t = x2.shape[0]
    n = w.shape[1]
    tm = _largest_tile(rows_per_batch, INPROJ_ROWS, SUBLANES)
    tn = _largest_tile(n, INPROJ_COLS, LANES)
    per_b = rows_per_batch // tm
    if sc.shape[0] == 1:
        mod_map = lambda j, i: (0, 0, 0)
    else:
        mod_map = lambda j, i: (i // per_b, 0, 0)
    return pl.pallas_call(
        _inproj_kernel,
        grid=(n // tn, t // tm),
        in_specs=[pl.BlockSpec((tm, D_MODEL), lambda j, i: (i, 0)),
                  pl.BlockSpec((1, D_MODEL), lambda j, i: (0, 0)),
                  pl.BlockSpec((1, 1, D_MODEL), mod_map),
                  pl.BlockSpec((1, 1, D_MODEL), mod_map),
                  pl.BlockSpec((D_MODEL, tn), lambda j, i: (0, j))],
        out_specs=pl.BlockSpec((tm, tn), lambda j, i: (i, j)),
        out_shape=jax.ShapeDtypeStruct((t, n), BF16),
        scratch_shapes=[pltpu.VMEM((D_MODEL, tn), BF16)],
        compiler_params=_params(2),
        name="inproj",
    )(x2, norm, sc, sh, w)


def _kvprep_kernel(kc_ref, vc_ref, kl_ref, vl_ref, g_ref, cos_ref, sin_ref, k_ref, vt_ref, *, ctx_blocks):
    j = pl.program_id(1)
    vrows = ATTN_HEAD_DIM + ONES_ROWS

    def emit(kin_ref, vin_ref, use_rope):
        k = kin_ref[...].astype(F32)
        v = vin_ref[...].astype(F32)
        for g in range(ATTN_KV_HEADS):
            sl = slice(g * ATTN_HEAD_DIM, (g + 1) * ATTN_HEAD_DIM)
            kh = _rms(k[:, sl]) * g_ref[...]
            if use_rope:
                kh = _rope_block(kh, cos_ref[...], sin_ref[...], ATTN_HEAD_DIM // 4)
            k_ref[0, g] = kh.astype(BF16)
            vt_ref[0, g * vrows:g * vrows + ATTN_HEAD_DIM, :] = v[:, sl].T.astype(BF16)
            vt_ref[0, g * vrows + ATTN_HEAD_DIM:(g + 1) * vrows, :] = jnp.ones((ONES_ROWS, v.shape[0]), BF16)

    @pl.when(j < ctx_blocks)
    def _():
        emit(kc_ref, vc_ref, False)

    @pl.when(j >= ctx_blocks)
    def _():
        emit(kl_ref, vl_ref, True)


def _kvprep(proj_c, proj, gain, cos, sin, batch, n_ctx, seq):
    tm = KV_PREP_ROWS
    assert n_ctx % tm == 0 and seq % tm == 0
    cb, lb = n_ctx // tm, seq // tm
    lk = n_ctx + seq
    ctx_row = lambda b, j: b * cb + jnp.minimum(j, cb - 1)
    lat_blk = lambda j: jnp.maximum(j - cb, 0)
    lat_row = lambda b, j: b * lb + lat_blk(j)
    vrows = ATTN_KV_HEADS * (ATTN_HEAD_DIM + ONES_ROWS)
    return pl.pallas_call(
        functools.partial(_kvprep_kernel, ctx_blocks=cb),
        grid=(batch, cb + lb),
        in_specs=[pl.BlockSpec((tm, ATTN_KV_W), lambda b, j: (ctx_row(b, j), COL_KA // ATTN_KV_W)),
                  pl.BlockSpec((tm, ATTN_KV_W), lambda b, j: (ctx_row(b, j), COL_VA // ATTN_KV_W)),
                  pl.BlockSpec((tm, ATTN_KV_W), lambda b, j: (lat_row(b, j), COL_KA // ATTN_KV_W)),
                  pl.BlockSpec((tm, ATTN_KV_W), lambda b, j: (lat_row(b, j), COL_VA // ATTN_KV_W)),
                  pl.BlockSpec((1, ATTN_HEAD_DIM), lambda b, j: (0, 0)),
                  pl.BlockSpec((tm, ATTN_HEAD_DIM), lambda b, j: (lat_blk(j), 0)),
                  pl.BlockSpec((tm, ATTN_HEAD_DIM), lambda b, j: (lat_blk(j), 0))],
        out_specs=[pl.BlockSpec((1, ATTN_KV_HEADS, tm, ATTN_HEAD_DIM), lambda b, j: (b, 0, j, 0)),
                   pl.BlockSpec((1, vrows, tm), lambda b, j: (b, 0, j))],
        out_shape=[jax.ShapeDtypeStruct((batch, ATTN_KV_HEADS, lk, ATTN_HEAD_DIM), BF16),
                   jax.ShapeDtypeStruct((batch, vrows, lk), BF16)],
        compiler_params=_params(2),
        name="kvprep",
    )(proj_c, proj_c, proj, proj, gain, cos, sin)


def _ret_kernel(dch_ref, qf_ref, kf_ref, vf_ref, cosf_ref, sinf_ref, qb_ref, kb_ref, vb_ref, cosb_ref, sinb_ref,
                dmat_ref, din_ref, dout_ref, s0_ref, of_ref, ob_ref, sfin_ref, s_sc, *, use_rope, n_chunks, heads):
    hb = pl.program_id(1)
    c = pl.program_id(2)

    @pl.when(c == 0)
    def _():
        s_sc[...] = s0_ref[:, 0]

    def rope(x, cos, sin):
        return jnp.concatenate([_rope_block(x[:, i * LANES:(i + 1) * LANES], cos[:, i * LANES:(i + 1) * LANES],
                                            sin[:, i * LANES:(i + 1) * LANES], RET_QK_DIM // 4)
                                for i in range(RET_QK_DIM // LANES)], axis=1)

    sides = ((qf_ref, kf_ref, vf_ref, cosf_ref, sinf_ref), (qb_ref, kb_ref, vb_ref, cosb_ref, sinb_ref))
    chains = [(d, j) for d in range(2) for j in range(heads)]
    qs, ks, vs, ss = {}, {}, {}, {}
    for d, j in chains:
        q_ref, k_ref, v_ref, cos_ref, sin_ref = sides[d]
        q = q_ref[:, j * RET_QK_DIM:(j + 1) * RET_QK_DIM].astype(F32)
        k = k_ref[:, j * RET_QK_DIM:(j + 1) * RET_QK_DIM].astype(F32)
        if use_rope:
            q = rope(q, cos_ref[...], sin_ref[...])
            k = rope(k, cos_ref[...], sin_ref[...])
        qs[d, j], ks[d, j] = q, k
        vs[d, j] = v_ref[:, j * RET_V_DIM:(j + 1) * RET_V_DIM]
        ss[d, j] = s_sc[d, j]
    scores = {ch: lax.dot_general(qs[ch].astype(BF16), ks[ch].astype(BF16), (((1,), (1,)), ((), ())),
                                  preferred_element_type=F32) * dmat_ref[ch[0], ch[1]] for ch in chains}
    inter = {ch: jnp.dot((qs[ch] * din_ref[ch[0], ch[1]]).astype(BF16), ss[ch].astype(BF16),
                         preferred_element_type=F32) for ch in chains}
    intra = {ch: jnp.dot(scores[ch].astype(BF16), vs[ch], preferred_element_type=F32) for ch in chains}
    o_refs = (of_ref, ob_ref)
    for d, j in chains:
        o_refs[d][:, j * RET_V_DIM:(j + 1) * RET_V_DIM] = (intra[d, j] + inter[d, j]).astype(of_ref.dtype)
    s_new = {ch: ss[ch] * dch_ref[ch[0] * RET_HEADS + hb * heads + ch[1]]
             + jnp.dot((ks[ch] * dout_ref[ch[0], ch[1]]).T.astype(BF16), vs[ch], preferred_element_type=F32)
             for ch in chains}
    for d, j in chains:
        s_sc[d, j] = s_new[d, j]

    @pl.when(c == n_chunks - 1)
    def _():
        for d, j in chains:
            sfin_ref[d, 0, j] = s_new[d, j]


def _ret_tables(log_gamma, chunk, reverse, k_scale):
    pos = jnp.arange(chunk, dtype=F32)
    diff = pos[:, None] - pos[None, :]
    if reverse:
        diff = -diff
        mask = diff > 0
        p_in = chunk - pos
        p_out = pos
    else:
        mask = diff >= 0
        p_in = pos + 1.0
        p_out = chunk - 1.0 - pos
    lg = log_gamma.astype(F32)
    dmat = jnp.where(mask[None], jnp.exp(lg[:, None, None] * jnp.maximum(diff, 0.0)[None]), 0.0) * k_scale
    d_in = jnp.exp(lg[:, None] * p_in)
    d_out = jnp.exp(lg[:, None] * p_out) * k_scale
    d_in = jnp.broadcast_to(d_in[:, :, None], (RET_HEADS, chunk, RET_QK_DIM))
    d_out = jnp.broadcast_to(d_out[:, :, None], (RET_HEADS, chunk, RET_QK_DIM))
    d_chunk = jnp.exp(lg * chunk)
    return d_chunk, dmat, d_in, d_out


def _retention(proj, cos, sin, lg_f, lg_b, s0, batch, seq, use_rope):
    chunk = min(RET_CHUNK, seq)
    nc = seq // chunk
    hs = RET_HEADS_PER_STEP
    qw, vw = hs * RET_QK_DIM, hs * RET_V_DIM
    k_scale = RET_QK_DIM ** -0.5
    tabs = [_ret_tables(lg_f, chunk, False, k_scale), _ret_tables(lg_b, chunk, True, k_scale)]
    d_chunk, dmat, d_in, d_out = [jnp.stack([tabs[0][i], tabs[1][i]]) for i in range(4)]
    fwd = lambda c: c
    bwd = lambda c: nc - 1 - c

    def side(cidx):
        row = lambda b, c: b * nc + cidx(c)
        return [pl.BlockSpec((chunk, qw), lambda b, h, c: (row(b, c), COL_QR // qw + h)),
                pl.BlockSpec((chunk, qw), lambda b, h, c: (row(b, c), COL_KR // qw + h)),
                pl.BlockSpec((chunk, vw), lambda b, h, c: (row(b, c), COL_VR // vw + h)),
                pl.BlockSpec((chunk, RET_QK_DIM), lambda b, h, c: (cidx(c), 0)),
                pl.BlockSpec((chunk, RET_QK_DIM), lambda b, h, c: (cidx(c), 0))]

    state_spec = pl.BlockSpec((2, 1, hs, RET_QK_DIM, RET_V_DIM), lambda b, h, c: (0, b, h, 0, 0))
    return pl.pallas_call(
        functools.partial(_ret_kernel, use_rope=use_rope, n_chunks=nc, heads=hs),
        grid=(batch, RET_HEADS // hs, nc),
        in_specs=[pl.BlockSpec(memory_space=pltpu.SMEM)] + side(fwd) + side(bwd) + [
            pl.BlockSpec((2, hs, chunk, chunk), lambda b, h, c: (0, h, 0, 0)),
            pl.BlockSpec((2, hs, chunk, RET_QK_DIM), lambda b, h, c: (0, h, 0, 0)),
            pl.BlockSpec((2, hs, chunk, RET_QK_DIM), lambda b, h, c: (0, h, 0, 0)),
            state_spec],
        out_specs=[pl.BlockSpec((chunk, vw), lambda b, h, c: (b * nc + c, h)),
                   pl.BlockSpec((chunk, vw), lambda b, h, c: (b * nc + nc - 1 - c, h)),
                   state_spec],
        out_shape=[jax.ShapeDtypeStruct((batch * seq, RET_V_W), BF16),
                   jax.ShapeDtypeStruct((batch * seq, RET_V_W), BF16),
                   jax.ShapeDtypeStruct((2, batch, RET_HEADS, RET_QK_DIM, RET_V_DIM), F32)],
        scratch_shapes=[pltpu.VMEM((2, hs, RET_QK_DIM, RET_V_DIM), F32)],
        compiler_params=_params(3),
        name="ret",
    )(d_chunk.reshape(-1), proj, proj, proj, cos, sin, proj, proj, proj, cos, sin, dmat, d_in, d_out, s0)


def _attn_kernel(q_ref, g_ref, cos_ref, sin_ref, k_ref, vt_ref, o_ref, qt_sc, m_sc, acc_sc, s_sc,
                 *ring, tq, kc, n_chunks):
    p_bufs, a_bufs, x_bufs, u_bufs, b_bufs = (ring[3 * i:3 * i + 3] for i in range(5))

    q = q_ref[...].astype(F32)
    scale = ATTN_HEAD_DIM ** -0.5 * np.log2(np.e)
    for g in range(ATTN_GROUP):
        qh = _rms(q[:, g * ATTN_HEAD_DIM:(g + 1) * ATTN_HEAD_DIM]) * g_ref[...]
        qh = _rope_block(qh, cos_ref[...], sin_ref[...], ATTN_HEAD_DIM // 4) * scale
        qt_sc[:, g * tq:(g + 1) * tq] = qh.T.astype(BF16)
    m_sc[...] = jnp.full(m_sc.shape, -jnp.inf, F32)
    acc_sc[...] = jnp.zeros(acc_sc.shape, F32)

    def scores(c):
        off = pl.multiple_of(c * kc, kc)
        return jnp.dot(k_ref[0, 0, pl.ds(off, kc), :], qt_sc[...], preferred_element_type=F32)

    def exact_probabilities(slot, m_new):
        for r in range(0, kc, ATTN_EXP_SLAB):
            p_bufs[slot][r:r + ATTN_EXP_SLAB, :] = jnp.exp2(s_sc[r:r + ATTN_EXP_SLAB, :] - m_new).astype(BF16)
        b_bufs[slot][...] = jnp.ones(m_new.shape, F32)

    def stage_s(c, slot):
        s_t = scores(c)
        x_bufs[slot][...] = jnp.max(s_t, axis=0, keepdims=True)
        if isinstance(c, int) and c == 0:
            s_sc[...] = s_t
            return
        u = m_sc[...]
        u_bufs[slot][...] = u
        p_bufs[slot][...] = jnp.exp2(s_t - u).astype(BF16)

    def stage_f(c, slot):
        m_prev = m_sc[...]
        x = x_bufs[slot][...]
        m_new = jnp.maximum(m_prev, x)
        m_sc[...] = m_new
        a_bufs[slot][...] = jnp.exp2(m_prev - m_new)
        if isinstance(c, int) and c == 0:
            exact_probabilities(slot, m_new)
            return
        u = u_bufs[slot][...]
        lagging = jnp.max(x - u) > ATTN_LAG_LIMIT

        @pl.when(lagging)
        def _():
            s_sc[...] = scores(c)
            exact_probabilities(slot, m_new)

        @pl.when(jnp.logical_not(lagging))
        def _():
            b_bufs[slot][...] = jnp.exp2(u - m_new)

    def stage_a(c, slot):
        off = pl.multiple_of(c * kc, kc)
        acc_sc[...] = a_bufs[slot][...] * acc_sc[...] + b_bufs[slot][...] * jnp.dot(
            vt_ref[0, :, pl.ds(off, kc)], p_bufs[slot][...], preferred_element_type=F32)

    n = n_chunks

    def tick(t, slot, full=False):
        if full or 1 <= t <= n:
            stage_f(t - 1, (slot + 2) % 3)
        if full or t < n:
            stage_s(t, slot)
        if full or 2 <= t <= n + 1:
            stage_a(t - 2, (slot + 1) % 3)

    tick(0, 0)
    tick(1, 1)
    first = 2
    while first < n and (n - first) % 3 != 0:
        tick(first, first % 3)
        first += 1
    n_groups = (n - first) // 3 if n > first else 0
    if n_groups > 0:
        def group(i, carry):
            t = first + 3 * i
            for j in range(3):
                tick(t + j, (first + j) % 3, full=True)
            return carry
        lax.fori_loop(0, n_groups, group, 0)
    for t in range(max(n, 2), n + 2):
        tick(t, t % 3)

    acc = acc_sc[...]
    o = acc[:ATTN_HEAD_DIM] / acc[ATTN_HEAD_DIM:ATTN_HEAD_DIM + 1]
    for g in range(ATTN_GROUP):
        o_ref[:, g * ATTN_HEAD_DIM:(g + 1) * ATTN_HEAD_DIM] = o[:, g * tq:(g + 1) * tq].T.astype(o_ref.dtype)


def _attention(proj, gain, cos, sin, k_all, vt_all, batch, seq):
    lk = k_all.shape[2]
    kc = _largest_tile(lk, ATTN_KEY_CHUNK, MXU_DEPTH)
    tq = _largest_tile(seq, ATTN_Q_TILE, SUBLANES)
    nq = seq // tq
    qw = ATTN_GROUP * ATTN_HEAD_DIM
    cols = ATTN_GROUP * tq
    vrows = ATTN_HEAD_DIM + ONES_ROWS
    return pl.pallas_call(
        functools.partial(_attn_kernel, tq=tq, kc=kc, n_chunks=lk // kc),
        grid=(batch, ATTN_KV_HEADS, nq),
        in_specs=[pl.BlockSpec((tq, qw), lambda b, g, i: (b * nq + i, COL_QA // qw + g)),
                  pl.BlockSpec((1, ATTN_HEAD_DIM), lambda b, g, i: (0, 0)),
                  pl.BlockSpec((tq, ATTN_HEAD_DIM), lambda b, g, i: (i, 0)),
                  pl.BlockSpec((tq, ATTN_HEAD_DIM), lambda b, g, i: (i, 0)),
                  pl.BlockSpec((1, 1, lk, ATTN_HEAD_DIM), lambda b, g, i: (b, g, 0, 0), pipeline_mode=pl.Buffered(1)),
                  pl.BlockSpec((1, vrows, lk), lambda b, g, i: (b, g, 0), pipeline_mode=pl.Buffered(1))],
        out_specs=pl.BlockSpec((tq, qw), lambda b, g, i: (b * nq + i, g)),
        out_shape=jax.ShapeDtypeStruct((batch * seq, ATTN_Q_W), BF16),
        scratch_shapes=[pltpu.VMEM((ATTN_HEAD_DIM, cols), BF16),
                        pltpu.VMEM((1, cols), F32),
                        pltpu.VMEM((vrows, cols), F32),
                        pltpu.VMEM((kc, cols), F32)] + [pltpu.VMEM((kc, cols), BF16)] * 3
        + [pltpu.VMEM((1, cols), F32)] * 12,
        compiler_params=_params(3),
        name="attn",
    )(proj, gain, cos, sin, k_all, vt_all)


def _merge_kernel(x_ref, g1_ref, of_ref, ob_ref, gr_ref, ao_ref, gtr_ref, gta_ref,
                  wro_ref, wao_ref, wo_ref, o_ref):
    ro = of_ref[...].astype(F32) + ob_ref[...].astype(F32)
    gr = gr_ref[...].astype(F32)
    parts = []
    for h in range(RET_HEADS):
        sl = slice(h * RET_V_DIM, (h + 1) * RET_V_DIM)
        g = gr[:, sl]
        parts.append((g * jax.nn.sigmoid(g) * _rms(ro[:, sl])).astype(BF16))
    ret_in = jnp.concatenate(parts, axis=1)
    ret_branch = jnp.dot(ret_in, wro_ref[...], preferred_element_type=F32)
    attn_branch = jnp.dot(ao_ref[...], wao_ref[...], preferred_element_type=F32)
    y = (jax.nn.sigmoid(gtr_ref[...].astype(F32)) * ret_branch
         + jax.nn.sigmoid(gta_ref[...].astype(F32)) * attn_branch)
    y = jnp.dot(y.astype(BF16), wo_ref[...], preferred_element_type=F32)
    o_ref[...] = x_ref[...] + g1_ref[0] * y


def _merge(x2, g1, o_f, o_b, proj, attn_o, w_ret_o, w_attn_o, w_out, seq):
    t = x2.shape[0]
    tm = _largest_tile(seq, MERGE_ROWS, SUBLANES)
    per_b = seq // tm
    full = lambda i: (0, 0)
    return pl.pallas_call(
        _merge_kernel,
        grid=(t // tm,),
        in_specs=[pl.BlockSpec((tm, D_MODEL), lambda i: (i, 0)),
                  pl.BlockSpec((1, 1, D_MODEL), lambda i: (i // per_b, 0, 0)),
                  pl.BlockSpec((tm, RET_V_W), lambda i: (i, 0)),
                  pl.BlockSpec((tm, RET_V_W), lambda i: (i, 0)),
                  pl.BlockSpec((tm, RET_V_W), lambda i: (i, COL_GR // RET_V_W)),
                  pl.BlockSpec((tm, ATTN_Q_W), lambda i: (i, 0)),
                  pl.BlockSpec((tm, D_MODEL), lambda i: (i, COL_GATE_R // D_MODEL)),
                  pl.BlockSpec((tm, D_MODEL), lambda i: (i, COL_GATE_A // D_MODEL)),
                  pl.BlockSpec((RET_V_W, D_MODEL), full, pipeline_mode=pl.Buffered(1)),
                  pl.BlockSpec((ATTN_Q_W, D_MODEL), full, pipeline_mode=pl.Buffered(1)),
                  pl.BlockSpec((D_MODEL, D_MODEL), full, pipeline_mode=pl.Buffered(1))],
        out_specs=pl.BlockSpec((tm, D_MODEL), lambda i: (i, 0)),
        out_shape=jax.ShapeDtypeStruct((t, D_MODEL), F32),
        compiler_params=_params(1),
        name="merge",
    )(x2, g1, o_f, o_b, proj, attn_o, proj, proj, w_ret_o, w_attn_o, w_out)


def _router_kernel(x_ref, n_ref, sc_ref, sh_ref, wr_ref, br_ref, h_ref, ti_ref, tw_ref, cnt_ref):
    h = _rms(x_ref[...]) * n_ref[...] * (1.0 + sc_ref[0]) + sh_ref[0]
    h_ref[...] = _pack_pairs(h)
    nt = (((1,), (1,)), ((), ()))
    w = wr_ref[...]
    h_hi = h.astype(BF16)
    h_lo = (h - h_hi.astype(F32)).astype(BF16)
    w_hi = w.astype(BF16)
    w_lo = (w - w_hi.astype(F32)).astype(BF16)
    logits = (lax.dot_general(w_hi, h_hi, nt, preferred_element_type=F32)
              + lax.dot_general(w_lo, h_hi, nt, preferred_element_type=F32)
              + lax.dot_general(w_hi, h_lo, nt, preferred_element_type=F32)) + br_ref[...]
    eid = lax.broadcasted_iota(jnp.int32, logits.shape, 0)
    vals = logits
    top_v = []
    top_i = []
    for _ in range(TOP_K):
        m = jnp.max(vals, axis=0, keepdims=True)
        idx = jnp.min(jnp.where(vals == m, eid, N_EXPERTS), axis=0, keepdims=True)
        top_v.append(m)
        top_i.append(idx)
        vals = jnp.where(eid == idx, -jnp.inf, vals)
    ex = [jnp.exp(v - top_v[0]) for v in top_v]
    denom = ex[0] + ex[1] + ex[2] + ex[3]
    ti_ref[...] = jnp.concatenate(top_i, axis=0)
    tw_ref[...] = jnp.concatenate([e / denom for e in ex], axis=0)
    cnt = jnp.zeros(cnt_ref.shape[1:], F32)
    for idx in top_i:
        hit = jnp.where(eid == idx, 1.0, 0.0)
        for c0 in range(0, hit.shape[1], LANES):
            cnt = cnt + hit[:, c0:c0 + LANES]
    cnt_ref[0] = cnt


def _router(x1, norm, sc, sh, w_router_t, b_router, seq):
    t = x1.shape[0]
    tm = _largest_tile(seq, ROUTER_ROWS, LANES)
    per_b = seq // tm
    mod_map = lambda i: (i // per_b, 0, 0)
    return pl.pallas_call(
        _router_kernel,
        grid=(t // tm,),
        in_specs=[pl.BlockSpec((tm, D_MODEL), lambda i: (i, 0)),
                  pl.BlockSpec((1, D_MODEL), lambda i: (0, 0)),
                  pl.BlockSpec((1, 1, D_MODEL), mod_map),
                  pl.BlockSpec((1, 1, D_MODEL), mod_map),
                  pl.BlockSpec((N_EXPERTS, D_MODEL), lambda i: (0, 0)),
                  pl.BlockSpec((N_EXPERTS, 1), lambda i: (0, 0))],
        out_specs=[pl.BlockSpec((tm, PACK_W), lambda i: (i, 0)),
                   pl.BlockSpec((TOP_K, tm), lambda i: (0, i)),
                   pl.BlockSpec((TOP_K, tm), lambda i: (0, i)),
                   pl.BlockSpec((1, N_EXPERTS, LANES), lambda i: (i, 0, 0))],
        out_shape=[jax.ShapeDtypeStruct((t, PACK_W), PACK_DTYPE),
                   jax.ShapeDtypeStruct((TOP_K, t), jnp.int32),
                   jax.ShapeDtypeStruct((TOP_K, t), F32),
                   jax.ShapeDtypeStruct((t // tm, N_EXPERTS, LANES), F32)],
        compiler_params=_params(1),
        name="router",
    )(x1, norm, sc, sh, w_router_t, b_router)


def _gather_rows(src, idx):
    n, w = src.shape
    r = idx.shape[0]
    workers = SC_CORES * SC_SUBCORES
    per_w = r // workers
    n_win = per_w // SC_GATHER_ROWS
    assert per_w * workers == r and n_win * SC_GATHER_ROWS == per_w, (r, workers, SC_GATHER_ROWS)
    mesh = plsc.VectorSubcoreMesh(core_axis_name="c", subcore_axis_name="s")

    @functools.partial(
        pl.kernel, mesh=mesh, out_type=jax.ShapeDtypeStruct((r, w), src.dtype),
        scratch_types=[pltpu.VMEM((SC_GATHER_ROWS,), jnp.int32),
                       pltpu.VMEM((SC_GATHER_ROWS, w), src.dtype),
                       pltpu.SemaphoreType.DMA])
    def gather(src_hbm, idx_hbm, out_hbm, idx_v, rows_v, sem):
        wid = lax.axis_index("s") * SC_CORES + lax.axis_index("c")
        base = wid * per_w

        @pl.loop(0, n_win)
        def _(win):
            off = base + win * SC_GATHER_ROWS
            pltpu.sync_copy(idx_hbm.at[pl.ds(off, SC_GATHER_ROWS)], idx_v)
            pltpu.async_copy(src_hbm.at[idx_v], rows_v, sem).wait()
            pltpu.sync_copy(rows_v, out_hbm.at[pl.ds(off, SC_GATHER_ROWS)])

    return gather(src, idx)


def _scatter_rows(src, dest, n_out, first, group_rows):
    n, w = src.shape
    n_assign = dest.shape[0]
    workers = SC_CORES * SC_SUBCORES
    per_w = n_assign // workers
    n_win = per_w // SC_GATHER_ROWS
    assert per_w * workers == n_assign and n_win * SC_GATHER_ROWS == per_w and group_rows % per_w == 0
    mesh = plsc.VectorSubcoreMesh(core_axis_name="c", subcore_axis_name="s")

    @functools.partial(
        pl.kernel, mesh=mesh, out_type=jax.ShapeDtypeStruct((n_out, w), src.dtype),
        scratch_types=[pltpu.VMEM((SC_GATHER_ROWS,), jnp.int32),
                       pltpu.VMEM((SC_GATHER_ROWS, w), src.dtype),
                       pltpu.SemaphoreType.DMA])
    def scatter(src_hbm, dest_hbm, out_hbm, idx_v, rows_v, sem):
        wid = lax.axis_index("s") * SC_CORES + lax.axis_index("c")
        base = wid * per_w
        row_base = first + lax.rem(base, group_rows)

        @pl.loop(0, n_win)
        def _(win):
            pltpu.sync_copy(dest_hbm.at[pl.ds(base + win * SC_GATHER_ROWS, SC_GATHER_ROWS)], idx_v)
            pltpu.sync_copy(src_hbm.at[pl.ds(row_base + win * SC_GATHER_ROWS, SC_GATHER_ROWS)], rows_v)
            pltpu.async_copy(rows_v, out_hbm.at[idx_v], sem).wait()

    return scatter(src, dest)


def _ffn_kernel(te_ref, tf_ref, tr_ref, ts_ref, tn_ref, tp_ref, x_ref, w1_hbm, b1_ref, w2_hbm, b2_ref, o_ref,
                w1_sc, w2_sc, w1_st, w2_st, sem):
    j = pl.program_id(0)
    flag = tf_ref[j]
    n_rows = tr_ref[j]

    def weight_copies(expert, slot):
        return (pltpu.make_async_copy(w1_hbm.at[expert], w1_st.at[slot], sem.at[0, slot]),
                pltpu.make_async_copy(w2_hbm.at[expert], w2_st.at[slot], sem.at[1, slot]))

    @pl.when(flag == TILE_FIRST)
    def _():
        slot = ts_ref[j]

        @pl.when(tp_ref[j] != 0)
        def _():
            for cp in weight_copies(te_ref[j], slot):
                cp.start()

        for cp in weight_copies(te_ref[j], slot):
            cp.wait()
        w1_sc[...] = w1_st[slot].astype(BF16)
        w2_sc[...] = w2_st[slot].astype(BF16)

        @pl.when(tn_ref[j] >= 0)
        def _():
            for cp in weight_copies(tn_ref[j], 1 - slot):
                cp.start()

    @pl.when(flag == TILE_PAD)
    def _():
        o_ref[...] = jnp.zeros(o_ref.shape, o_ref.dtype)

    @pl.when(flag != TILE_PAD)
    def _():
        xw = x_ref[...]
        rid = lax.broadcasted_iota(jnp.int32, xw.shape, 0)
        lo, hi = _unpack_pairs(jnp.where(rid < n_rows, xw, jnp.zeros_like(xw)))
        x = jnp.concatenate([lo, hi], axis=1).astype(BF16)
        a = jnp.dot(x, w1_sc[...], preferred_element_type=F32) + b1_ref[0]
        gate = jnp.minimum(a[:, :EXPERT_FF], SWIGLU_LIMIT)
        up = jnp.clip(a[:, EXPERT_FF:], -SWIGLU_LIMIT, SWIGLU_LIMIT)
        act = gate * jax.nn.sigmoid(SWIGLU_ALPHA * gate) * (up + 1.0)
        y = jnp.dot(act.astype(BF16), w2_sc[...], preferred_element_type=F32) + b2_ref[0]
        o_ref[...] = _pack_pairs(y)


def _ffn(xs, tile_expert, tile_flag, tile_rows, w1, b1, w2, b2, tg):
    p = xs.shape[0]
    n_tiles = p // tg
    tile_slot, tile_next, tile_prime = _ffn_ring(tile_expert, tile_flag)
    row_map = lambda j, te, tf, tr, ts, tn, tp: (j, 0)
    bias_map = lambda j, te, tf, tr, ts, tn, tp: (te[j], 0, 0)
    grid_spec = pltpu.PrefetchScalarGridSpec(
        num_scalar_prefetch=6,
        grid=(n_tiles,),
        in_specs=[pl.BlockSpec((tg, PACK_W), row_map),
                  pl.BlockSpec(memory_space=pl.ANY),
                  pl.BlockSpec((1, 1, 2 * EXPERT_FF), bias_map),
                  pl.BlockSpec(memory_space=pl.ANY),
                  pl.BlockSpec((1, 1, D_MODEL), bias_map)],
        out_specs=pl.BlockSpec((tg, PACK_W), row_map),
        scratch_shapes=[pltpu.VMEM((D_MODEL, 2 * EXPERT_FF), BF16),
                        pltpu.VMEM((EXPERT_FF, D_MODEL), BF16),
                        pltpu.VMEM((2, D_MODEL, 2 * EXPERT_FF), F32),
                        pltpu.VMEM((2, EXPERT_FF, D_MODEL), F32),
                        pltpu.SemaphoreType.DMA((2, 2))],
    )
    return pl.pallas_call(
        _ffn_kernel,
        grid_spec=grid_spec,
        out_shape=jax.ShapeDtypeStruct((p, PACK_W), PACK_DTYPE),
        compiler_params=_params(1),
        name="ffn",
    )(tile_expert, tile_flag, tile_rows, tile_slot, tile_next, tile_prime, xs, w1, b1, w2, b2)


def _ffn_ring(tile_expert, tile_flag):
    n_tiles = tile_flag.shape[0]
    first = tile_flag == TILE_FIRST
    ordinal = jnp.cumsum(first.astype(jnp.int32)) - 1
    tile_slot = jnp.where(first, ordinal % 2, 0).astype(jnp.int32)
    tile_prime = (first & (ordinal == 0)).astype(jnp.int32)
    tiles = jnp.arange(n_tiles, dtype=jnp.int32)
    first_at_or_after = lax.cummin(jnp.where(first, tiles, n_tiles)[::-1], axis=0)[::-1]
    next_first = jnp.concatenate([first_at_or_after[1:], jnp.full((1,), n_tiles, jnp.int32)])
    tile_next = jnp.where(first & (next_first < n_tiles),
                          tile_expert[jnp.minimum(next_first, n_tiles - 1)], -1).astype(jnp.int32)
    return tile_slot, tile_next, tile_prime


def _combine_kernel(x_ref, g2_ref, w_ref, y_ref, o_ref):
    w = w_ref[...]
    acc_lo = None
    for k in range(TOP_K):
        lo, hi = _unpack_pairs(y_ref[k])
        wk = w[:, k:k + 1]
        acc_lo = wk * lo if acc_lo is None else acc_lo + wk * lo
        acc_hi = wk * hi if k == 0 else acc_hi + wk * hi
    acc = jnp.concatenate([acc_lo, acc_hi], axis=1)
    o_ref[...] = x_ref[...] + g2_ref[0] * acc


def _combine_into_kernel(x_ref, g2_ref, w_ref, y_ref, prev_ref, o_ref):
    del prev_ref
    _combine_kernel(x_ref, g2_ref, w_ref, y_ref, o_ref)


def _combine(x1, g2, w_tok, yk, seq, first_row, prev):
    t = x1.shape[0]
    rows = yk.shape[1]
    tm = _largest_tile(seq, COMBINE_ROWS, SUBLANES)
    per_b = seq // tm
    i0 = first_row // tm
    in_specs = [pl.BlockSpec((tm, D_MODEL), lambda i: (i0 + i, 0)),
                pl.BlockSpec((1, 1, D_MODEL), lambda i: ((i0 + i) // per_b, 0, 0)),
                pl.BlockSpec((tm, TOP_K), lambda i: (i0 + i, 0)),
                pl.BlockSpec((TOP_K, tm, PACK_W), lambda i: (0, i, 0))]
    args = [x1, g2, w_tok, yk]
    if prev is not None:
        in_specs.append(pl.BlockSpec(memory_space=pl.ANY))
        args.append(prev)
    return pl.pallas_call(
        _combine_kernel if prev is None else _combine_into_kernel,
        grid=(rows // tm,),
        in_specs=in_specs,
        out_specs=pl.BlockSpec((tm, D_MODEL), lambda i: (i0 + i, 0)),
        out_shape=jax.ShapeDtypeStruct((t, D_MODEL), F32),
        input_output_aliases={} if prev is None else {len(args) - 1: 0},
        compiler_params=_params(1),
        name="combine",
    )(*args)


def _rank_kernel(ti_ref, off_ref, tri_ref, dest_ref, run_sc):
    @pl.when(pl.program_id(0) == 0)
    def _():
        run_sc[...] = off_ref[...] - 1.0

    bt = ti_ref.shape[1]
    eid = lax.broadcasted_iota(jnp.int32, (N_EXPERTS, bt), 0)
    run = run_sc[...]
    for k in range(TOP_K):
        m = eid == ti_ref[k:k + 1, :]
        pre = jnp.dot(jnp.where(m, 1.0, 0.0).astype(BF16), tri_ref[...], preferred_element_type=F32)
        slot = jnp.sum(jnp.where(m, pre + run, 0.0), axis=0, keepdims=True)
        dest_ref[k:k + 1, :] = slot.astype(jnp.int32)
        run = run + pre[:, bt - 1:bt]
    run_sc[...] = run


def _plan(top_i, cnt, tg):
    t = top_i.shape[1]
    n_assign = TOP_K * t
    p = n_assign + N_EXPERTS * tg
    n_tiles = p // tg
    bt = _largest_tile(t, PLAN_BLOCK, LANES)
    counts = jnp.sum(cnt, axis=(0, 2)).astype(jnp.int32)
    padded = ((counts + tg - 1) // tg) * tg
    off_end = jnp.cumsum(padded)
    off = off_end - padded
    tri = (jnp.arange(bt)[:, None] <= jnp.arange(bt)[None, :]).astype(BF16)
    dest = pl.pallas_call(
        _rank_kernel,
        grid=(t // bt,),
        in_specs=[pl.BlockSpec((TOP_K, bt), lambda i: (0, i)),
                  pl.BlockSpec((N_EXPERTS, 1), lambda i: (0, 0)),
                  pl.BlockSpec((bt, bt), lambda i: (0, 0))],
        out_specs=pl.BlockSpec((TOP_K, bt), lambda i: (0, i)),
        out_shape=jax.ShapeDtypeStruct((TOP_K, t), jnp.int32),
        scratch_shapes=[pltpu.VMEM((N_EXPERTS, 1), F32)],
        compiler_params=_params(1),
        name="rank",
    )(top_i, off.astype(F32).reshape(N_EXPERTS, 1), tri)
    tile_start = jnp.arange(n_tiles, dtype=jnp.int32) * tg
    tile_valid = tile_start < off_end[-1]
    te = jnp.sum((tile_start[:, None] >= off_end[None, :]).astype(jnp.int32), axis=1)
    last_e = jnp.sum(((off_end[-1] - 1) >= off_end).astype(jnp.int32))
    tile_expert = jnp.where(tile_valid, te, last_e)
    tile_rows = jnp.where(tile_valid, jnp.clip(counts[tile_expert] - (tile_start - off[tile_expert]), 0, tg), 0)
    changed = jnp.concatenate([jnp.ones((1,), jnp.bool_), tile_expert[1:] != tile_expert[:-1]])
    tile_flag = jnp.where(tile_valid, jnp.where(changed, TILE_FIRST, TILE_BODY), TILE_PAD).astype(jnp.int32)
    return dest.reshape(-1), tile_expert, tile_flag, tile_rows.astype(jnp.int32)


def kernel(x, c, ctx, c_ctx, norm1, norm2, w_ada, b_ada, w_in, ret_decay_f, ret_decay_b, attn_q_norm, attn_k_norm,
           w_ret_o, w_attn_o, w_out, w_router, b_router, w_exp_in, b_exp_in, w_exp_out, b_exp_out):
    assert w_in.shape[0] == 1, "single-layer block"
    b, seq, d = x.shape
    n_ctx = ctx.shape[1]
    t = b * seq
    rows = seq // GRID_W

    idx = np.cumsum(IN_SIZES)[:-1].tolist()
    wq_r, wk_r, wv_r, wg_r, wq_a, wk_a, wv_a, wgt_r, wgt_a = jnp.split(w_in[0], idx, axis=-1)
    w_in_p = jnp.concatenate([wq_r, wk_r, wv_r, wg_r, wq_a, wgt_r, wgt_a, wk_a, wv_a], axis=-1)
    w1 = w_exp_in[0]
    w2 = w_exp_out[0]
    b1 = b_exp_in[0].reshape(N_EXPERTS, 1, 2 * EXPERT_FF)
    b2 = b_exp_out[0].reshape(N_EXPERTS, 1, D_MODEL)

    pad = (-(b + 1)) % SUBLANES
    c_all = jnp.concatenate([c, c_ctx[None, :], jnp.zeros((pad, d), F32)], axis=0)
    mod = _ada(c_all, w_ada[0], b_ada[0])
    sh1, sc1, g1, sh2, sc2, g2 = [m.reshape(-1, 1, d) for m in jnp.split(mod, 6, axis=-1)]
    lat = lambda m: m[:b]
    cx = lambda m: m[b:b + 1]

    x2 = x.reshape(t, d)
    proj = _inproj(x2, norm1, lat(sc1), lat(sh1), w_in_p, seq)
    proj_c = _inproj(ctx.reshape(b * n_ctx, d), norm1, cx(sc1), cx(sh1), w_in_p, n_ctx)

    cos_r, sin_r = _rope_tables(rows, RET_QK_DIM)
    cos_a, sin_a = _rope_tables(rows, ATTN_HEAD_DIM)
    lg_f = -jax.nn.softplus(ret_decay_f[0].astype(F32))
    lg_b = -jax.nn.softplus(ret_decay_b[0].astype(F32))
    zero_state = jnp.zeros((2, b, RET_HEADS, RET_QK_DIM, RET_V_DIM), F32)
    cos_c = jnp.ones((n_ctx, RET_QK_DIM), F32)
    sin_c = jnp.zeros((n_ctx, RET_QK_DIM), F32)
    _, _, s_ctx = _retention(proj_c, cos_c, sin_c, lg_f, lg_b, zero_state, b, n_ctx, False)
    o_f, o_b, _ = _retention(proj, cos_r, sin_r, lg_f, lg_b, s_ctx, b, seq, True)

    k_all, vt_all = _kvprep(proj_c, proj, attn_k_norm, cos_a, sin_a, b, n_ctx, seq)
    attn_o = _attention(proj, attn_q_norm, cos_a, sin_a, k_all, vt_all, b, seq)

    x1 = _merge(x2, lat(g1), o_f, o_b, proj, attn_o, w_ret_o[0].astype(BF16), w_attn_o[0].astype(BF16),
                w_out[0].astype(BF16), seq)

    h2, top_i, top_w, cnt = _router(x1, norm2, lat(sc2), lat(sh2), w_router[0].T,
                                    b_router[0].reshape(N_EXPERTS, 1), seq)
    n_groups = MOE_GROUPS if b % MOE_GROUPS == 0 else 1
    tgrp = t // n_groups
    tg = _largest_tile(TOP_K * tgrp, FFN_ROWS, SUBLANES)
    w_tok = top_w.T
    assert cnt.shape[0] % n_groups == 0
    cnt_per_grp = cnt.shape[0] // n_groups
    out = None
    for grp in range(n_groups):
        first = grp * tgrp
        dest, tile_expert, tile_flag, tile_rows = _plan(
            top_i[:, first:first + tgrp], cnt[grp * cnt_per_grp:(grp + 1) * cnt_per_grp], tg)
        xs = _scatter_rows(h2, dest, TOP_K * tgrp + N_EXPERTS * tg, first, tgrp)
        ys = _ffn(xs, tile_expert, tile_flag, tile_rows, w1, b1, w2, b2, tg)
        yk = _gather_rows(ys, dest).reshape(TOP_K, tgrp, PACK_W)
        out = _combine(x1, lat(g2), w_tok, yk, seq, first, out)
    return out.reshape(b, seq, d)
```
